```python
import jax, jax.numpy as jnp
from jax import lax
import numpy as np

D_MODEL = 1024
BATCH = 8
SEQ = 2048
DEPTH = 1
DEC_BATCH = 32
DEC_SEQ = 32
PAST_LEN = 4096

CHUNK = 64
Q_BLOCK = 128
A_HEADS = 8
A_KV_HEADS = 2
A_HEAD_DIM = 64
A_Q = A_HEADS * A_HEAD_DIM
A_KV = A_KV_HEADS * A_HEAD_DIM
IDX_HEADS = 16
IDX_DIM = 64
IDX_Q = IDX_HEADS * IDX_DIM
TOPK_MAX = 256
B_HEADS = 8
B_HEAD_DIM = 64
B_WIDTH = B_HEADS * B_HEAD_DIM
DECAY_LORA = 64
AAA_LORA = 64
GATE_LORA = 128
RWKV_SPLITS = (B_WIDTH, B_WIDTH, B_WIDTH, DECAY_LORA, AAA_LORA, GATE_LORA)
SHIFT_W = 3 * B_WIDTH + DECAY_LORA + AAA_LORA + GATE_LORA
IN_SPLITS = (A_Q, A_KV, A_KV, IDX_Q, IDX_DIM, IDX_HEADS, SHIFT_W, D_MODEL, D_MODEL)
IN_WIDTH = A_Q + 2 * A_KV + IDX_Q + IDX_DIM + IDX_HEADS + SHIFT_W + 2 * D_MODEL
D_FF = -(-8 * D_MODEL // (3 * 256)) * 256
RMS_EPS = 1e-6
LN_EPS = 1e-6
GN_EPS = 64e-5

kernel_name = 'chunk_stream_dsa_rwkv7_hybrid'


def _split(z, sizes):
    return jnp.split(z, np.cumsum(sizes)[:-1].tolist(), axis=-1)


def _rms_norm(x, g):
    xf = x.astype(jnp.float32)
    y = xf * lax.rsqrt(jnp.mean(xf * xf, axis=-1, keepdims=True) + RMS_EPS)
    return (y * g.astype(jnp.float32)).astype(x.dtype)


def _layer_norm(x, g, b):
    xf = x.astype(jnp.float32)
    mu = jnp.mean(xf, axis=-1, keepdims=True)
    var = jnp.mean(jnp.square(xf - mu), axis=-1, keepdims=True)
    y = (xf - mu) * lax.rsqrt(var + LN_EPS)
    return (y * g.astype(jnp.float32) + b.astype(jnp.float32)).astype(x.dtype)


def _dsa_attend(q, iq, iw, k_all, v_all, ik_all, adm, n_sel):
    f32 = jnp.float32
    B, T = q.shape[0], q.shape[1]
    iqh = iq.reshape(B, T, IDX_HEADS, IDX_DIM).astype(f32)
    logits = jnp.einsum('bthd,bsd->bths', iqh, ik_all.astype(f32)) * (IDX_DIM ** -0.5)
    score = jnp.einsum('bths,bth->bts', jax.nn.relu(logits), iw.astype(f32) * (IDX_HEADS ** -0.5))
    score = jnp.where(adm[None], score, -jnp.inf)
    top_val, top_idx = lax.top_k(score, n_sel)
    valid = jnp.isfinite(top_val)
    gather = jax.vmap(lambda rows, idx: rows[idx])
    k_sel = gather(k_all, top_idx).astype(f32)
    v_sel = gather(v_all, top_idx).astype(f32)
    qh = q.reshape(B, T, A_KV_HEADS, A_HEADS // A_KV_HEADS, A_HEAD_DIM).astype(f32)
    s = jnp.einsum('btkgd,btskd->btkgs', qh, k_sel) * (A_HEAD_DIM ** -0.5)
    s = jnp.where(valid[:, :, None, None, :], s, -jnp.inf)
    p = jax.nn.softmax(s, axis=-1)
    o = jnp.einsum('btkgs,btskd->btkgd', p, v_sel)
    return o.reshape(B, T, A_Q).astype(q.dtype)


def _prompt_attention(q, iq, iw, k, v, ik):
    B, S = q.shape[0], q.shape[1]
    n_sel = min(TOPK_MAX, S // 4)
    key_pos = jnp.arange(S)

    def one_block(i):
        t0 = i * Q_BLOCK
        sl = lambda a: lax.dynamic_slice_in_dim(a, t0, Q_BLOCK, axis=1)
        q_pos = t0 + jnp.arange(Q_BLOCK)
        adm = key_pos[None, :] < (q_pos[:, None] // CHUNK + 1) * CHUNK
        return _dsa_attend(sl(q), sl(iq), sl(iw), k, v, ik, adm, n_sel)

    o = lax.map(one_block, jnp.arange(S // Q_BLOCK))
    return jnp.moveaxis(o, 0, 1).reshape(B, S, A_Q)


def _sample_attention(cache_k, cache_v, cache_ik):
    def attend(q, iq, iw, k, v, ik):
        k_all = jnp.concatenate([cache_k.astype(k.dtype), k], axis=1)
        v_all = jnp.concatenate([cache_v.astype(v.dtype), v], axis=1)
        ik_all = jnp.concatenate([cache_ik.astype(ik.dtype), ik], axis=1)
        T, L = q.shape[1], k_all.shape[1]
        adm = jnp.ones((T, L), dtype=bool)
        return _dsa_attend(q, iq, iw, k_all, v_all, ik_all, adm, min(TOPK_MAX, L // 4))
    return attend


def _rwkv_time_mix(u, shift_prev, wkv0, p):
    f32 = jnp.float32
    B, T = u.shape[0], u.shape[1]
    uf = u.astype(f32)
    u_prev = jnp.concatenate([shift_prev.astype(f32), uf[:, :-1]], axis=1)
    m = uf + (u_prev - uf) * p['shift_mu'].astype(f32)
    r, k, v, wl, al, gl = _split(m, RWKV_SPLITS)
    w_log = -jax.nn.softplus(-(p['w0'].astype(f32) + jnp.tanh(wl) @ p['w2'].astype(f32))) - 0.5
    decay = jnp.exp(-jnp.exp(w_log))
    a = jax.nn.sigmoid(p['a0'].astype(f32) + al @ p['a2'].astype(f32))
    g = jax.nn.sigmoid(gl) @ p['g2'].astype(f32)
    heads = lambda z: z.reshape(B, T, B_HEADS, B_HEAD_DIM)
    kk = heads(k * p['k_k'].astype(f32))
    kk = kk / jnp.maximum(jnp.sqrt(jnp.sum(kk * kk, axis=-1, keepdims=True)), 1e-12)
    k = k * (1.0 + (a - 1.0) * p['k_a'].astype(f32))
    rh, wh, kh, vh, ah = heads(r), heads(decay), heads(k), heads(v), heads(a)
    a_vec, b_vec = -kk, kk * ah

    def step(S, inp):
        r_t, w_t, k_t, v_t, a_t, b_t = inp
        sa = jnp.einsum('bhij,bhj->bhi', S, a_t)
        S = S * w_t[:, :, None, :] + sa[..., None] * b_t[:, :, None, :] + v_t[..., None] * k_t[:, :, None, :]
        return S, jnp.einsum('bhij,bhj->bhi', S, r_t)

    tm = lambda z: jnp.moveaxis(z, 1, 0)
    s_fin, y = lax.scan(step, wkv0.astype(f32), (tm(rh), tm(wh), tm(kh), tm(vh), tm(a_vec), tm(b_vec)))
    y = jnp.moveaxis(y, 0, 1)
    mu = jnp.mean(y, axis=-1, keepdims=True)
    var = jnp.mean(jnp.square(y - mu), axis=-1, keepdims=True)
    y = ((y - mu) * lax.rsqrt(var + GN_EPS)).reshape(B, T, B_WIDTH)
    y = y * p['gn_w'].astype(f32) + p['gn_b'].astype(f32)
    bonus = jnp.sum(rh * kh * p['r_k'].astype(f32).reshape(B_HEADS, B_HEAD_DIM), axis=-1, keepdims=True) * vh
    y = (y + bonus.reshape(B, T, B_WIDTH)) * g
    return y.astype(u.dtype), u[:, -1:], s_fin.astype(u.dtype)


def _layer(x, attend, shift_prev, wkv0, p):
    B, T = x.shape[0], x.shape[1]
    xn = _rms_norm(x, p['norm1'])
    q, k, v, iq, ik, iw, u, g_a, g_b = _split(xn @ p['w_in'], IN_SPLITS)
    k = k.reshape(B, T, A_KV_HEADS, A_HEAD_DIM)
    v = v.reshape(B, T, A_KV_HEADS, A_HEAD_DIM)
    ik = _layer_norm(ik, p['idx_k_g'], p['idx_k_b'])
    o_a = attend(q, iq, iw, k, v, ik)
    y_b, shift_new, wkv_new = _rwkv_time_mix(u, shift_prev, wkv0, p)
    merged = jax.nn.sigmoid(g_a) * (o_a @ p['w_oa']) + jax.nn.sigmoid(g_b) * (y_b @ p['w_ob'])
    h = x + merged @ p['w_out']
    hn = _rms_norm(h, p['norm2'])
    h = h + (jax.nn.silu(hn @ p['w_gate']) * (hn @ p['w_up'])) @ p['w_down']
    return h, k, v, ik, wkv_new, shift_new


def setup_inputs(seed: int = 0) -> dict:
    key = jax.random.key(seed)
    ks = jax.random.split(key, 32)
    nrm = lambda kk, shape, scale: scale * jax.random.normal(kk, shape, jnp.float32)
    L = DEPTH
    return {
        'x_prompt': nrm(ks[0], (BATCH, SEQ, D_MODEL), 1.0),
        'x_sample': nrm(ks[1], (DEC_BATCH, DEC_SEQ, D_MODEL), 1.0),
        'cache_k': nrm(ks[2], (L, DEC_BATCH, PAST_LEN, A_KV_HEADS, A_HEAD_DIM), 1.0),
        'cache_v': nrm(ks[3], (L, DEC_BATCH, PAST_LEN, A_KV_HEADS, A_HEAD_DIM), 1.0),
        'cache_idx_k': nrm(ks[4], (L, DEC_BATCH, PAST_LEN, IDX_DIM), 1.0),
        'state_wkv': nrm(ks[5], (L, DEC_BATCH, B_HEADS, B_HEAD_DIM, B_HEAD_DIM), 0.3),
        'state_shift': nrm(ks[6], (L, DEC_BATCH, 1, SHIFT_W), 1.0),
        'norm1': 1.0 + nrm(ks[7], (L, D_MODEL), 0.02),
        'w_in': nrm(ks[8], (L, D_MODEL, IN_WIDTH), D_MODEL ** -0.5),
        'idx_k_g': 1.0 + nrm(ks[9], (L, IDX_DIM), 0.02),
        'idx_k_b': nrm(ks[10], (L, IDX_DIM), 0.02),
        'shift_mu': jax.random.uniform(ks[11], (L, SHIFT_W), jnp.float32),
        'w0': nrm(ks[12], (L, B_WIDTH), 0.5),
        'w2': nrm(ks[13], (L, DECAY_LORA, B_WIDTH), DECAY_LORA ** -0.5),
        'a0': nrm(ks[14], (L, B_WIDTH), 0.5),
        'a2': nrm(ks[15], (L, AAA_LORA, B_WIDTH), AAA_LORA ** -0.5),
        'g2': nrm(ks[16], (L, GATE_LORA, B_WIDTH), GATE_LORA ** -0.5),
        'k_k': 0.85 + nrm(ks[17], (L, B_WIDTH), 0.02),
        'k_a': 1.0 + nrm(ks[18], (L, B_WIDTH), 0.02),
        'r_k': nrm(ks[19], (L, B_WIDTH), 0.1),
        'gn_w': 1.0 + nrm(ks[20], (L, B_WIDTH), 0.02),
        'gn_b': nrm(ks[21], (L, B_WIDTH), 0.02),
        'w_oa': nrm(ks[22], (L, A_Q, D_MODEL), A_Q ** -0.5),
        'w_ob': nrm(ks[23], (L, B_WIDTH, D_MODEL), B_WIDTH ** -0.5),
        'w_out': nrm(ks[24], (L, D_MODEL, D_MODEL), D_MODEL ** -0.5),
        'norm2': 1.0 + nrm(ks[25], (L, D_MODEL), 0.02),
        'w_gate': nrm(ks[26], (L, D_MODEL, D_FF), D_MODEL ** -0.5),
        'w_up': nrm(ks[27], (L, D_MODEL, D_FF), D_MODEL ** -0.5),
        'w_down': nrm(ks[28], (L, D_FF, D_MODEL), D_FF ** -0.5),
        'norm_f': 1.0 + nrm(ks[29], (D_MODEL,), 0.02),
    }


def reference(x_prompt, x_sample, cache_k, cache_v, cache_idx_k, state_wkv, state_shift,
              norm1, w_in, idx_k_g, idx_k_b, shift_mu, w0, w2, a0, a2, g2, k_k, k_a, r_k,
              gn_w, gn_b, w_oa, w_ob, w_out, norm2, w_gate, w_up, w_down, norm_f):
    params = {'norm1': norm1, 'w_in': w_in, 'idx_k_g': idx_k_g, 'idx_k_b': idx_k_b,
              'shift_mu': shift_mu, 'w0': w0, 'w2': w2, 'a0': a0, 'a2': a2, 'g2': g2,
              'k_k': k_k, 'k_a': k_a, 'r_k': r_k, 'gn_w': gn_w, 'gn_b': gn_b,
              'w_oa': w_oa, 'w_ob': w_ob, 'w_out': w_out, 'norm2': norm2,
              'w_gate': w_gate, 'w_up': w_up, 'w_down': w_down}
    n_prompt = x_prompt.shape[0]
    hp, hs = x_prompt, x_sample
    outs_p, outs_s = [], []
    for l in range(DEPTH):
        p = {name: arr[l] for name, arr in params.items()}
        shift0 = jnp.zeros((n_prompt, 1, SHIFT_W), x_prompt.dtype)
        wkv_zero = jnp.zeros((n_prompt, B_HEADS, B_HEAD_DIM, B_HEAD_DIM), jnp.float32)
        hp, *st_p = _layer(hp, _prompt_attention, shift0, wkv_zero, p)
        hs, *st_s = _layer(hs, _sample_attention(cache_k[l], cache_v[l], cache_idx_k[l]),
                           state_shift[l], state_wkv[l], p)
        outs_p.append(st_p)
        outs_s.append(st_s)
    stk = lambda outs, i: jnp.stack([o[i] for o in outs], axis=0)
    y_prompt = _rms_norm(hp, norm_f)
    y_sample = _rms_norm(hs, norm_f)
    k_p, v_p, ik_p, wkv_p, shift_p = stk(outs_p, 0), stk(outs_p, 1), stk(outs_p, 2), stk(outs_p, 3), stk(outs_p, 4)
    k_s, v_s, ik_s, wkv_s, shift_s = stk(outs_s, 0), stk(outs_s, 1), stk(outs_s, 2), stk(outs_s, 3), stk(outs_s, 4)
    return (y_prompt, y_sample, k_p, v_p, ik_p, wkv_p, shift_p, k_s, v_s, ik_s, wkv_s, shift_s)
```

```python
import functools

import jax
import jax.numpy as jnp
import numpy as np
from jax import lax
from jax.experimental import pallas as pl
from jax.experimental.pallas import tpu as pltpu

F32 = jnp.float32
BF16 = jnp.bfloat16

D_MODEL = 1024
CHUNK = 64
A_HEADS = 8
A_KV_HEADS = 2
A_GROUP = A_HEADS // A_KV_HEADS
A_HEAD_DIM = 64
A_Q = A_HEADS * A_HEAD_DIM
A_KV = A_KV_HEADS * A_HEAD_DIM
IDX_HEADS = 16
IDX_DIM = 64
IDX_Q = IDX_HEADS * IDX_DIM
TOPK_MAX = 256
B_HEADS = 8
B_HEAD_DIM = 64
B_WIDTH = B_HEADS * B_HEAD_DIM
DECAY_LORA = 64
AAA_LORA = 64
GATE_LORA = 128
SHIFT_W = 3 * B_WIDTH + DECAY_LORA + AAA_LORA + GATE_LORA
D_FF = 2816
RMS_EPS = 1e-6
LN_EPS = 1e-6
GN_EPS = 64e-5

LANES = 128
VMEM_LIMIT = 48 * 1024 * 1024
NEG = -1e30

C_IQ = 0
C_GA = C_IQ + IDX_Q
C_GB = C_GA + D_MODEL
C_Q = C_GB + D_MODEL
C_U = C_Q + A_Q
C_K = C_U + SHIFT_W
C_V = C_K + A_KV
C_IKW = C_V + A_KV
P_WIDTH = C_IKW + LANES


def _dot(a, b, trans_b=False):
    dn = (((1,), (1 if trans_b else 0,)), ((), ()))
    return lax.dot_general(a, b, dn, preferred_element_type=F32)


def _split2(x):
    hi = x.astype(BF16)
    lo = (x - hi.astype(F32)).astype(BF16)
    return hi, lo


def _dot3(a, b, trans_b=False):
    ah, al = _split2(a)
    bh, bl = _split2(b)
    return _dot(ah, bh, trans_b) + (_dot(ah, bl, trans_b) + _dot(al, bh, trans_b))


def _dot_exact_rhs(a, b_bf16, terms=2):
    out = None
    rem = a
    for _ in range(terms):
        part = rem.astype(BF16)
        rem = rem - part.astype(F32)
        d = _dot(part, b_bf16)
        out = d if out is None else out + d
    return out


def _dot_exact_lhs(a_bf16, b, terms=3):
    out = None
    rem = b
    for _ in range(terms):
        part = rem.astype(BF16)
        rem = rem - part.astype(F32)
        d = _dot(a_bf16, part)
        out = d if out is None else out + d
    return out


def _sigmoid(x):
    return 1.0 / (1.0 + jnp.exp(-x))


def _rms(x, g):
    ms = jnp.mean(x * x, axis=-1, keepdims=True)
    return x * lax.rsqrt(ms + RMS_EPS) * g


def _in_proj_kernel(x_ref, g_ref, w_ref, o_ref, xn_sc):
    @pl.when(pl.program_id(1) == 0)
    def _():
        xn_sc[...] = _rms(x_ref[...], g_ref[...]).astype(BF16)

    o_ref[...] = _dot(xn_sc[...], w_ref[...])


def _in_proj(x2, norm1, w_in_p):
    n = x2.shape[0]
    tm = min(n, 1024)
    tn = 640
    return pl.pallas_call(
        _in_proj_kernel,
        grid=(n // tm, P_WIDTH // tn),
        in_specs=[
            pl.BlockSpec((tm, D_MODEL), lambda i, j: (i, 0)),
            pl.BlockSpec((1, D_MODEL), lambda i, j: (0, 0)),
            pl.BlockSpec((D_MODEL, tn), lambda i, j: (0, j)),
        ],
        out_specs=pl.BlockSpec((tm, tn), lambda i, j: (i, j)),
        out_shape=jax.ShapeDtypeStruct((n, P_WIDTH), F32),
        scratch_shapes=[pltpu.VMEM((tm, D_MODEL), BF16)],
        compiler_params=pltpu.CompilerParams(
            dimension_semantics=("arbitrary", "arbitrary"), vmem_limit_bytes=VMEM_LIMIT),
        name="in_proj",
    )(x2, norm1, w_in_p)


def _layer_norm(x, g, b):
    mu = jnp.mean(x, axis=-1, keepdims=True)
    d = x - mu
    var = jnp.mean(d * d, axis=-1, keepdims=True)
    return d * lax.rsqrt(var + LN_EPS) * g + b


def _stage_queries(iq, q, iw, iqs, wb, qg, rows):
    for h in range(IDX_HEADS):
        iqs[h * rows:(h + 1) * rows, :] = (iq[:, h * IDX_DIM:(h + 1) * IDX_DIM] * (IDX_DIM ** -0.5)).astype(BF16)
        wb[h] = jnp.broadcast_to(iw[:, h:h + 1] * (IDX_HEADS ** -0.5), (rows, LANES))
    for g in range(A_KV_HEADS):
        for hh in range(A_GROUP):
            h = g * A_GROUP + hh
            qg[g, hh * rows:(hh + 1) * rows, :] = (
                q[:, h * A_HEAD_DIM:(h + 1) * A_HEAD_DIM] * (A_HEAD_DIM ** -0.5)).astype(BF16)


def _index_scores(iqs, wb, ikb, sc, nt, adm_fn, rows):
    def body(j, carry):
        off = pl.multiple_of(j * LANES, LANES)
        lg = _dot(iqs[...], ikb[pl.ds(off, LANES), :], trans_b=True)
        acc = jnp.maximum(lg[0:rows], 0.0) * wb[0]
        for h in range(1, IDX_HEADS):
            acc = acc + jnp.maximum(lg[h * rows:(h + 1) * rows], 0.0) * wb[h]
        sc[:, pl.ds(off, LANES)] = jnp.where(adm_fn(j), acc, -jnp.inf)
        return carry

    lax.fori_loop(0, nt, body, 0)


def _select_topk(sc, tri_ref, nt, rows):
    shape = (rows, LANES)
    inf = jnp.float32(jnp.inf)

    def tile(j):
        return sc[:, pl.ds(pl.multiple_of(j * LANES, LANES), LANES)]

    def rowred(x, red):
        return jnp.broadcast_to(red(x, axis=1, keepdims=True), shape)

    def count_ge(t):
        acc = lax.fori_loop(
            0, nt, lambda j, a: a + jnp.where(tile(j) >= t, 1.0, 0.0), jnp.zeros(shape, F32))
        return rowred(acc, jnp.sum)

    def stats(j, c):
        mn, mx, na = c
        s = tile(j)
        fin = s > -inf
        return (jnp.minimum(mn, jnp.where(fin, s, inf)), jnp.maximum(mx, s), na + jnp.where(fin, 1.0, 0.0))

    mn, mx, na = lax.fori_loop(
        0, nt, stats, (jnp.full(shape, inf, F32), jnp.full(shape, -inf, F32), jnp.zeros(shape, F32)))
    lo0 = rowred(mn, jnp.min)
    mx = rowred(mx, jnp.max)
    n_adm = rowred(na, jnp.sum)
    kq = jnp.minimum(n_adm, float(TOPK_MAX))
    hi0 = mx + (jnp.abs(mx) * 1e-6 + 1e-30)

    def cond(c):
        return jnp.logical_and(c[0] < 400, c[5] > 0.5)

    def body(c):
        it, lo, hi, c_lo, c_hi, _ = c
        mid = 0.5 * (lo + hi)
        c_mid = count_ge(mid)
        ge = c_mid >= kq
        lo = jnp.where(ge, mid, lo)
        c_lo = jnp.where(ge, c_mid, c_lo)
        hi = jnp.where(ge, hi, mid)
        c_hi = jnp.where(ge, c_hi, c_mid)
        nxt = 0.5 * (lo + hi)
        active = jnp.where(c_lo - c_hi > 1.5, jnp.where(nxt > lo, jnp.where(nxt < hi, 1.0, 0.0), 0.0), 0.0)
        return it + 1, lo, hi, c_lo, c_hi, jnp.max(active)

    _, lo, _, c_lo, c_hi, _ = lax.while_loop(
        cond, body, (jnp.int32(0), lo0, hi0, n_adm, jnp.zeros(shape, F32), jnp.float32(1.0)))

    thr = rowred(
        lax.fori_loop(0, nt, lambda j, a: jnp.minimum(a, jnp.where(tile(j) >= lo, tile(j), inf)),
                      jnp.full(shape, inf, F32)),
        jnp.min)
    take = kq - c_hi
    has_tie = jnp.max(c_lo - kq) > 0.5

    @pl.when(jnp.logical_not(has_tie))
    def _():
        def wr(j, carry):
            off = pl.multiple_of(j * LANES, LANES)
            sc[:, pl.ds(off, LANES)] = jnp.where(sc[:, pl.ds(off, LANES)] >= thr, 0.0, NEG)
            return carry
        lax.fori_loop(0, nt, wr, 0)

    @pl.when(has_tie)
    def _():
        def wr(j, seen):
            off = pl.multiple_of(j * LANES, LANES)
            s = sc[:, pl.ds(off, LANES)]
            tie = jnp.where(s == thr, 1.0, 0.0)
            rank = _dot(tie.astype(BF16), tri_ref[...]) + seen
            keep_tie = jnp.where(s == thr, jnp.where(rank <= take, 0.0, NEG), NEG)
            sc[:, pl.ds(off, LANES)] = jnp.where(s > thr, 0.0, keep_tie)
            return seen + rowred(tie, jnp.sum)
        lax.fori_loop(0, nt, wr, jnp.zeros(shape, F32))


def _attend(sc, qg, kb, vb, m_sc, l_sc, acc_sc, nt, rows):
    m_sc[...] = jnp.full(m_sc.shape, NEG, F32)
    l_sc[...] = jnp.zeros(l_sc.shape, F32)
    acc_sc[...] = jnp.zeros(acc_sc.shape, F32)

    def body(j, carry):
        off = pl.multiple_of(j * LANES, LANES)
        bias = sc[:, pl.ds(off, LANES)]
        bias_g = jnp.concatenate([bias] * A_GROUP, axis=0)
        for g in range(A_KV_HEADS):
            s = _dot(qg[g], kb[g, pl.ds(off, LANES), :], trans_b=True) + bias_g
            m_old = m_sc[g]
            m_new = jnp.maximum(m_old, jnp.max(s, axis=1, keepdims=True))
            alpha = jnp.exp(m_old - m_new)
            p = jnp.exp(s - m_new)
            l_sc[g] = l_sc[g] * alpha + jnp.sum(p, axis=1, keepdims=True)
            acc_sc[g] = acc_sc[g] * alpha + _dot(p.astype(BF16), vb[g, pl.ds(off, LANES), :])
            m_sc[g] = m_new
        return carry

    lax.fori_loop(0, nt, body, 0)


def _write_heads(o_ref, l_sc, acc_sc, rows):
    for g in range(A_KV_HEADS):
        o = acc_sc[g] * (1.0 / l_sc[g])
        for hh in range(A_GROUP):
            h = g * A_GROUP + hh
            o_ref[0, :, h * A_HEAD_DIM:(h + 1) * A_HEAD_DIM] = o[hh * rows:(hh + 1) * rows].astype(o_ref.dtype)


_ATTN_SCRATCH = lambda rows, keys: [
    pltpu.VMEM((A_KV_HEADS, keys, A_HEAD_DIM), BF16),
    pltpu.VMEM((A_KV_HEADS, keys, A_HEAD_DIM), BF16),
    pltpu.VMEM((keys, IDX_DIM), BF16),
    pltpu.VMEM((IDX_HEADS * rows, IDX_DIM), BF16),
    pltpu.VMEM((IDX_HEADS, rows, LANES), F32),
    pltpu.VMEM((A_KV_HEADS, A_GROUP * rows, A_HEAD_DIM), BF16),
    pltpu.VMEM((rows, keys), F32),
    pltpu.VMEM((A_KV_HEADS, A_GROUP * rows, 1), F32),
    pltpu.VMEM((A_KV_HEADS, A_GROUP * rows, 1), F32),
    pltpu.VMEM((A_KV_HEADS, A_GROUP * rows, A_HEAD_DIM), F32),
]


def _attn_prompt_kernel(iq_ref, q_ref, k_ref, v_ref, ikw_ref, ikwq_ref, lng_ref, lnb_ref, tri_ref,
                        o_ref, ikln_ref, kb, vb, ikb, iqs, wb, qg, sc, m_sc, l_sc, acc_sc, *, rows):
    i = pl.program_id(1)

    @pl.when(i == 0)
    def _():
        kf = k_ref[0]
        vf = v_ref[0]
        for g in range(A_KV_HEADS):
            kb[g] = kf[:, g * A_HEAD_DIM:(g + 1) * A_HEAD_DIM].astype(BF16)
            vb[g] = vf[:, g * A_HEAD_DIM:(g + 1) * A_HEAD_DIM].astype(BF16)
        ln = _layer_norm(ikw_ref[0][:, :IDX_DIM], lng_ref[...], lnb_ref[...])
        ikln_ref[0] = ln
        ikb[...] = ln.astype(BF16)

    _stage_queries(iq_ref[0], q_ref[0], ikwq_ref[0][:, IDX_DIM:IDX_DIM + IDX_HEADS], iqs, wb, qg, rows)

    nt = (i + 1) * (rows // LANES)
    q_pos = i * rows + lax.broadcasted_iota(jnp.int32, (rows, LANES), 0)
    q_end = (q_pos // CHUNK + 1) * CHUNK

    def adm(j):
        return j * LANES + lax.broadcasted_iota(jnp.int32, (rows, LANES), 1) < q_end

    _index_scores(iqs, wb, ikb, sc, nt, adm, rows)
    _select_topk(sc, tri_ref, nt, rows)
    _attend(sc, qg, kb, vb, m_sc, l_sc, acc_sc, nt, rows)
    _write_heads(o_ref, l_sc, acc_sc, rows)


def _attn_prompt(proj3, idx_k_g, idx_k_b, tri):
    b, s, _ = proj3.shape
    rows = 128
    kernel = functools.partial(_attn_prompt_kernel, rows=rows)
    return pl.pallas_call(
        kernel,
        grid=(b, s // rows),
        in_specs=[
            pl.BlockSpec((1, rows, IDX_Q), lambda bi, i: (bi, i, C_IQ // IDX_Q)),
            pl.BlockSpec((1, rows, A_Q), lambda bi, i: (bi, i, C_Q // A_Q)),
            pl.BlockSpec((1, s, A_KV), lambda bi, i: (bi, 0, C_K // A_KV)),
            pl.BlockSpec((1, s, A_KV), lambda bi, i: (bi, 0, C_V // A_KV)),
            pl.BlockSpec((1, s, LANES), lambda bi, i: (bi, 0, C_IKW // LANES)),
            pl.BlockSpec((1, rows, LANES), lambda bi, i: (bi, i, C_IKW // LANES)),
            pl.BlockSpec((1, IDX_DIM), lambda bi, i: (0, 0)),
            pl.BlockSpec((1, IDX_DIM), lambda bi, i: (0, 0)),
            pl.BlockSpec((LANES, LANES), lambda bi, i: (0, 0)),
        ],
        out_specs=[
            pl.BlockSpec((1, rows, A_Q), lambda bi, i: (bi, i, 0)),
            pl.BlockSpec((1, s, IDX_DIM), lambda bi, i: (bi, 0, 0)),
        ],
        out_shape=[
            jax.ShapeDtypeStruct((b, s, A_Q), BF16),
            jax.ShapeDtypeStruct((b, s, IDX_DIM), F32),
        ],
        scratch_shapes=_ATTN_SCRATCH(rows, s),
        compiler_params=pltpu.CompilerParams(
            dimension_semantics=("arbitrary", "arbitrary"), vmem_limit_bytes=VMEM_LIMIT),
        name="attn_prompt",
    )(proj3, proj3, proj3, proj3, proj3, proj3, idx_k_g, idx_k_b, tri)


def _attn_sample_kernel(iq_ref, q_ref, k_ref, v_ref, ikw_ref, ck_ref, cv_ref, cik_ref, lng_ref, lnb_ref, tri_ref,
                        o_ref, ikln_ref, kb, vb, ikb, iqs, wb, qg, sc, m_sc, l_sc, acc_sc, *, rows, past, keys):
    new = rows
    pad = keys - past - new
    ck = ck_ref[0]
    cv = cv_ref[0]
    kf = k_ref[0]
    vf = v_ref[0]
    for g in range(A_KV_HEADS):
        sl = slice(g * A_HEAD_DIM, (g + 1) * A_HEAD_DIM)
        kb[g, 0:past, :] = ck[:, sl].astype(BF16)
        vb[g, 0:past, :] = cv[:, sl].astype(BF16)
        kb[g, past:past + new, :] = kf[:, sl].astype(BF16)
        vb[g, past:past + new, :] = vf[:, sl].astype(BF16)
        kb[g, past + new:keys, :] = jnp.zeros((pad, A_HEAD_DIM), BF16)
        vb[g, past + new:keys, :] = jnp.zeros((pad, A_HEAD_DIM), BF16)
    ikw = ikw_ref[0]
    ln = _layer_norm(ikw[:, :IDX_DIM], lng_ref[...], lnb_ref[...])
    ikln_ref[0] = ln
    ikb[0:past, :] = cik_ref[0].astype(BF16)
    ikb[past:past + new, :] = ln.astype(BF16)
    ikb[past + new:keys, :] = jnp.zeros((pad, IDX_DIM), BF16)

    _stage_queries(iq_ref[0], q_ref[0], ikw[:, IDX_DIM:IDX_DIM + IDX_HEADS], iqs, wb, qg, rows)

    nt = keys // LANES

    def adm(j):
        return j * LANES + lax.broadcasted_iota(jnp.int32, (rows, LANES), 1) < past + new

    _index_scores(iqs, wb, ikb, sc, nt, adm, rows)
    _select_topk(sc, tri_ref, nt, rows)
    _attend(sc, qg, kb, vb, m_sc, l_sc, acc_sc, nt, rows)
    _write_heads(o_ref, l_sc, acc_sc, rows)


def _attn_sample(proj3, cache_k, cache_v, cache_ik, idx_k_g, idx_k_b, tri):
    b, t, _ = proj3.shape
    past = cache_k.shape[1]
    keys = -(-(past + t) // LANES) * LANES
    kernel = functools.partial(_attn_sample_kernel, rows=t, past=past, keys=keys)
    return pl.pallas_call(
        kernel,
        grid=(b,),
        in_specs=[
            pl.BlockSpec((1, t, IDX_Q), lambda bi: (bi, 0, C_IQ // IDX_Q)),
            pl.BlockSpec((1, t, A_Q), lambda bi: (bi, 0, C_Q // A_Q)),
            pl.BlockSpec((1, t, A_KV), lambda bi: (bi, 0, C_K // A_KV)),
            pl.BlockSpec((1, t, A_KV), lambda bi: (bi, 0, C_V // A_KV)),
            pl.BlockSpec((1, t, LANES), lambda bi: (bi, 0, C_IKW // LANES)),
            pl.BlockSpec((1, past, A_KV), lambda bi: (bi, 0, 0)),
            pl.BlockSpec((1, past, A_KV), lambda bi: (bi, 0, 0)),
            pl.BlockSpec((1, past, IDX_DIM), lambda bi: (bi, 0, 0)),
            pl.BlockSpec((1, IDX_DIM), lambda bi: (0, 0)),
            pl.BlockSpec((1, IDX_DIM), lambda bi: (0, 0)),
            pl.BlockSpec((LANES, LANES), lambda bi: (0, 0)),
        ],
        out_specs=[
            pl.BlockSpec((1, t, A_Q), lambda bi: (bi, 0, 0)),
            pl.BlockSpec((1, t, IDX_DIM), lambda bi: (bi, 0, 0)),
        ],
        out_shape=[
            jax.ShapeDtypeStruct((b, t, A_Q), BF16),
            jax.ShapeDtypeStruct((b, t, IDX_DIM), F32),
        ],
        scratch_shapes=_ATTN_SCRATCH(t, keys),
        compiler_params=pltpu.CompilerParams(
            dimension_semantics=("arbitrary",), vmem_limit_bytes=VMEM_LIMIT),
        name="attn_sample",
    )(proj3, proj3, proj3, proj3, proj3, cache_k, cache_v, cache_ik, idx_k_g, idx_k_b, tri)


def _rwkv_prep_kernel(u_ref, shift_ref, mu_ref, w0_ref, a0_ref, kk_ref, ka_ref, rk_ref, w2_ref, a2_ref, g2_ref,
                      bd_ref, r_o, lw_o, k_o, v_o, kk_o, a_o, g_o, bonus_o, carry):
    t = pl.program_id(1)
    u = u_ref[0]
    tb = u.shape[0]

    @pl.when(t == 0)
    def _():
        carry[...] = shift_ref[0]

    row = lax.broadcasted_iota(jnp.int32, u.shape, 0)
    u_prev = jnp.where(row == 0, carry[...], pltpu.roll(u, 1, 0))
    carry[...] = u[tb - 1:tb, :]
    m = u + (u_prev - u) * mu_ref[...]

    r = m[:, 0:B_WIDTH]
    k = m[:, B_WIDTH:2 * B_WIDTH]
    v = m[:, 2 * B_WIDTH:3 * B_WIDTH]
    lora = m[:, 3 * B_WIDTH:3 * B_WIDTH + LANES]
    gl = m[:, 3 * B_WIDTH + LANES:]
    lane = lax.broadcasted_iota(jnp.int32, lora.shape, 1)
    lora = jnp.where(lane < DECAY_LORA, jnp.tanh(lora), lora).astype(BF16)
    z = w0_ref[...] + _dot(lora, w2_ref[...])
    softplus = jnp.maximum(-z, 0.0) + jnp.log(1.0 + jnp.exp(-jnp.abs(z)))
    lw = -jnp.exp(-softplus - 0.5)
    a = _sigmoid(a0_ref[...] + _dot(lora, a2_ref[...]))
    g = _dot(_sigmoid(gl).astype(BF16), g2_ref[...])

    kk = k * kk_ref[...]
    ss = _dot_exact_rhs(kk * kk, bd_ref[...])
    kk = kk / jnp.maximum(jnp.sqrt(ss), 1e-12)
    k2 = k * (1.0 + (a - 1.0) * ka_ref[...])
    bonus = _dot_exact_rhs(r * k2 * rk_ref[...], bd_ref[...]) * v

    r_o[0] = r
    lw_o[0] = lw
    k_o[0] = k2
    v_o[0] = v
    kk_o[0] = kk
    a_o[0] = a
    g_o[0] = g
    bonus_o[0] = bonus


def _rwkv_prep(proj3, shift_prev, p):
    b, s, _ = proj3.shape
    tb = min(s, 256)
    vec = lambda w: pl.BlockSpec((1, w), lambda bi, t: (0, 0))
    mat = lambda r, c: pl.BlockSpec((r, c), lambda bi, t: (0, 0))
    out_spec = pl.BlockSpec((1, tb, B_WIDTH), lambda bi, t: (bi, t, 0))
    return pl.pallas_call(
        _rwkv_prep_kernel,
        grid=(b, s // tb),
        in_specs=[
            pl.BlockSpec((1, tb, SHIFT_W), lambda bi, t: (bi, t, C_U // SHIFT_W)),
            pl.BlockSpec((1, 1, SHIFT_W), lambda bi, t: (bi, 0, 0)),
            vec(SHIFT_W), vec(B_WIDTH), vec(B_WIDTH), vec(B_WIDTH), vec(B_WIDTH), vec(B_WIDTH),
            mat(LANES, B_WIDTH), mat(LANES, B_WIDTH), mat(GATE_LORA, B_WIDTH), mat(B_WIDTH, B_WIDTH),
        ],
        out_specs=[out_spec] * 8,
        out_shape=[jax.ShapeDtypeStruct((b, s, B_WIDTH), F32)] * 8,
        scratch_shapes=[pltpu.VMEM((1, SHIFT_W), F32)],
        compiler_params=pltpu.CompilerParams(
            dimension_semantics=("arbitrary", "arbitrary"), vmem_limit_bytes=VMEM_LIMIT),
        name="rwkv_prep",
    )(proj3, shift_prev, p['shift_mu'], p['w0'], p['a0'], p['k_k'], p['k_a'], p['r_k'],
      p['w2p'], p['a2p'], p['g2'], p['bd'])


def _rwkv_chunk_kernel(r_ref, lw_ref, k_ref, v_ref, kk_ref, a_ref, s0_ref, tri_ref, y_ref, s_out_ref, state):
    c = pl.program_id(1)
    tc = r_ref.shape[1]
    n = B_HEAD_DIM

    @pl.when(c == 0)
    def _():
        state[...] = s0_ref[0]

    lw = lw_ref[0]
    cum = _dot_exact_lhs(tri_ref[...], lw)
    g_in = jnp.exp(cum)
    g_ex = jnp.exp(cum - lw)
    g_inv = jnp.exp(-cum)
    g_end = g_in[tc - 1:tc, :]
    kk = kk_ref[0]
    a_t = -kk * g_ex
    b_t = kk * a_ref[0] * g_inv
    k_t = k_ref[0] * g_inv
    r_t = r_ref[0] * g_in
    v = v_ref[0]
    b_e = b_t * g_end
    k_e = k_t * g_end

    ri = lax.broadcasted_iota(jnp.int32, (tc, tc), 0)
    ci = lax.broadcasted_iota(jnp.int32, (tc, tc), 1)
    strict = ci < ri
    incl = ci <= ri
    eye = jnp.where(ci == ri, 1.0, 0.0)

    for h in range(B_HEADS):
        sl = slice(h * n, (h + 1) * n)
        ar = jnp.concatenate([a_t[:, sl], r_t[:, sl]], axis=0)
        bk = jnp.concatenate([b_t[:, sl], k_t[:, sl]], axis=0)
        cross = _dot3(ar, bk, trans_b=True)
        l_ab = jnp.where(strict, cross[0:tc, 0:tc], 0.0)
        l_ak = jnp.where(strict, cross[0:tc, tc:2 * tc], 0.0)
        m_rb = jnp.where(incl, cross[tc:2 * tc, 0:tc], 0.0)
        m_rk = jnp.where(incl, cross[tc:2 * tc, tc:2 * tc], 0.0)

        inv = eye + l_ab
        pw = l_ab
        span = 2
        while span < tc:
            pw = _dot3(pw, pw)
            inv = inv + _dot3(inv, pw)
            span *= 2

        s_h = state[h]
        vh = v[:, sl]
        from_state = _dot3(ar, s_h, trans_b=True)
        z = _dot3(inv, from_state[0:tc] + _dot3(l_ak, vh))
        zv = jnp.concatenate([z, vh], axis=0)
        y = from_state[tc:2 * tc] + _dot3(jnp.concatenate([m_rb, m_rk], axis=1), zv)
        y_ref[0, :, sl] = y
        upd = _dot3(zv.T, jnp.concatenate([b_e[:, sl], k_e[:, sl]], axis=0))
        state[h] = s_h * g_end[:, sl] + upd

    @pl.when(c == pl.num_programs(1) - 1)
    def _():
        s_out_ref[0] = state[...]


def _rwkv_chunk(r, lw, k2, v, kk, a, s0, tc):
    b, s, _ = r.shape
    tri = jnp.asarray(np.tril(np.ones((tc, tc), np.float32)), BF16)
    blk = pl.BlockSpec((1, tc, B_WIDTH), lambda bi, c: (bi, c, 0))
    st = pl.BlockSpec((1, B_HEADS, B_HEAD_DIM, B_HEAD_DIM), lambda bi, c: (bi, 0, 0, 0))
    return pl.pallas_call(
        _rwkv_chunk_kernel,
        grid=(b, s // tc),
        in_specs=[blk] * 6 + [st, pl.BlockSpec((tc, tc), lambda bi, c: (0, 0))],
        out_specs=[blk, st],
        out_shape=[jax.ShapeDtypeStruct((b, s, B_WIDTH), F32),
                   jax.ShapeDtypeStruct((b, B_HEADS, B_HEAD_DIM, B_HEAD_DIM), F32)],
        scratch_shapes=[pltpu.VMEM((B_HEADS, B_HEAD_DIM, B_HEAD_DIM), F32)],
        compiler_params=pltpu.CompilerParams(
            dimension_semantics=("arbitrary", "arbitrary"), vmem_limit_bytes=VMEM_LIMIT),
        name="rwkv_chunk",
    )(r, lw, k2, v, kk, a, s0, tri)


def _merge_kernel(oa_ref, y_ref, bonus_ref, g_ref, ga_ref, gb_ref, x_ref, gnw_ref, gnb_ref, bd_ref,
                  woa_ref, wob_ref, wout_ref, h_ref):
    y = y_ref[...]
    mean = _dot_exact_rhs(y, bd_ref[...]) * (1.0 / B_HEAD_DIM)
    d = y - mean
    var = _dot_exact_rhs(d * d, bd_ref[...]) * (1.0 / B_HEAD_DIM)
    yn = d * lax.rsqrt(var + GN_EPS) * gnw_ref[...] + gnb_ref[...]
    yb = ((yn + bonus_ref[...]) * g_ref[...]).astype(BF16)
    merged = (_sigmoid(ga_ref[...]) * _dot(oa_ref[...], woa_ref[...])
              + _sigmoid(gb_ref[...]) * _dot(yb, wob_ref[...]))
    h_ref[...] = x_ref[...] + _dot(merged.astype(BF16), wout_ref[...])


def _merge(o_a, y, bonus, g, proj, x2, p):
    n = x2.shape[0]
    tm = min(n, 512)
    row = lambda w, j=0: pl.BlockSpec((tm, w), lambda i: (i, j))
    full = lambda r, c: pl.BlockSpec((r, c), lambda i: (0, 0))
    return pl.pallas_call(
        _merge_kernel,
        grid=(n // tm,),
        in_specs=[
            row(A_Q), row(B_WIDTH), row(B_WIDTH), row(B_WIDTH),
            row(D_MODEL, C_GA // D_MODEL), row(D_MODEL, C_GB // D_MODEL), row(D_MODEL),
            full(1, B_WIDTH), full(1, B_WIDTH), full(B_WIDTH, B_WIDTH),
            full(A_Q, D_MODEL), full(B_WIDTH, D_MODEL), full(D_MODEL, D_MODEL),
        ],
        out_specs=row(D_MODEL),
        out_shape=jax.ShapeDtypeStruct((n, D_MODEL), F32),
        compiler_params=pltpu.CompilerParams(
            dimension_semantics=("arbitrary",), vmem_limit_bytes=VMEM_LIMIT),
        name="merge_out_proj",
    )(o_a, y, bonus, g, proj, proj, x2, p['gn_w'], p['gn_b'], p['bd'], p['w_oa'], p['w_ob'], p['w_out'])


def _ffn_kernel(h_ref, n2_ref, nf_ref, wg_ref, wu_ref, wd_ref, o_ref, hn_sc, acc_sc):
    j = pl.program_id(1)

    @pl.when(j == 0)
    def _():
        hn_sc[...] = _rms(h_ref[...], n2_ref[...]).astype(BF16)
        acc_sc[...] = jnp.zeros(acc_sc.shape, F32)

    hn = hn_sc[...]
    gate = _dot(hn, wg_ref[...])
    up = _dot(hn, wu_ref[...])
    act = (gate * _sigmoid(gate) * up).astype(BF16)
    acc_sc[...] += _dot(act, wd_ref[...])

    @pl.when(j == pl.num_programs(1) - 1)
    def _():
        o_ref[...] = _rms(h_ref[...] + acc_sc[...], nf_ref[...])


def _ffn(h, p, norm_f):
    n = h.shape[0]
    tm = min(n, 512)
    tf = D_FF // 2
    return pl.pallas_call(
        _ffn_kernel,
        grid=(n // tm, D_FF // tf),
        in_specs=[
            pl.BlockSpec((tm, D_MODEL), lambda i, j: (i, 0)),
            pl.BlockSpec((1, D_MODEL), lambda i, j: (0, 0)),
            pl.BlockSpec((1, D_MODEL), lambda i, j: (0, 0)),
            pl.BlockSpec((D_MODEL, tf), lambda i, j: (0, j)),
            pl.BlockSpec((D_MODEL, tf), lambda i, j: (0, j)),
            pl.BlockSpec((tf, D_MODEL), lambda i, j: (j, 0)),
        ],
        out_specs=pl.BlockSpec((tm, D_MODEL), lambda i, j: (i, 0)),
        out_shape=jax.ShapeDtypeStruct((n, D_MODEL), F32),
        scratch_shapes=[pltpu.VMEM((tm, D_MODEL), BF16), pltpu.VMEM((tm, D_MODEL), F32)],
        compiler_params=pltpu.CompilerParams(
            dimension_semantics=("arbitrary", "arbitrary"), vmem_limit_bytes=VMEM_LIMIT),
        name="ffn_final_norm",
    )(h, p['norm2'], norm_f, p['w_gate'], p['w_up'], p['w_down'])


def _prepare_params(l, norm1, w_in, idx_k_g, idx_k_b, shift_mu, w0, w2, a0, a2, g2, k_k, k_a, r_k,
                    gn_w, gn_b, w_oa, w_ob, w_out, norm2, w_gate, w_up, w_down):
    w = w_in[l]
    o = np.cumsum([0, A_Q, A_KV, A_KV, IDX_Q, IDX_DIM, IDX_HEADS, SHIFT_W, D_MODEL, D_MODEL])
    seg = lambda i: w[:, o[i]:o[i + 1]]
    pad = jnp.zeros((D_MODEL, LANES - IDX_DIM - IDX_HEADS), w.dtype)
    w_in_p = jnp.concatenate([seg(3), seg(7), seg(8), seg(0), seg(6), seg(1), seg(2), seg(4), seg(5), pad],
                             axis=1).astype(BF16)
    zeros = jnp.zeros((LANES - DECAY_LORA, B_WIDTH), F32)
    head = np.arange(B_WIDTH) // B_HEAD_DIM
    row = lambda x: x[l].reshape(1, -1)
    return {
        'norm1': row(norm1), 'w_in_p': w_in_p,
        'idx_k_g': row(idx_k_g), 'idx_k_b': row(idx_k_b),
        'shift_mu': row(shift_mu), 'w0': row(w0), 'a0': row(a0),
        'k_k': row(k_k), 'k_a': row(k_a), 'r_k': row(r_k), 'gn_w': row(gn_w), 'gn_b': row(gn_b),
        'w2p': jnp.concatenate([w2[l], zeros], axis=0).astype(BF16),
        'a2p': jnp.concatenate([zeros, a2[l]], axis=0).astype(BF16),
        'g2': g2[l].astype(BF16),
        'bd': jnp.asarray(head[:, None] == head[None, :], BF16),
        'w_oa': w_oa[l].astype(BF16), 'w_ob': w_ob[l].astype(BF16), 'w_out': w_out[l].astype(BF16),
        'norm2': row(norm2),
        'w_gate': w_gate[l].astype(BF16), 'w_up': w_up[l].astype(BF16), 'w_down': w_down[l].astype(BF16),
    }


def _layer(x, attend, shift_prev, wkv0, p, norm_f, chunk):
    b, t, _ = x.shape
    x2 = x.reshape(b * t, D_MODEL)
    proj = _in_proj(x2, p['norm1'], p['w_in_p'])
    proj3 = proj.reshape(b, t, P_WIDTH)
    o_a, ik_ln = attend(proj3)
    r, lw, k2, v, kk, a, g, bonus = _rwkv_prep(proj3, shift_prev, p)
    y, wkv_new = _rwkv_chunk(r, lw, k2, v, kk, a, wkv0, chunk)
    flat = lambda z: z.reshape(b * t, z.shape[-1])
    h = _merge(flat(o_a), flat(y), flat(bonus), flat(g), proj, x2, p)
    out = _ffn(h, p, norm_f).reshape(b, t, D_MODEL)
    k_new = proj3[:, :, C_K:C_K + A_KV].reshape(b, t, A_KV_HEADS, A_HEAD_DIM)
    v_new = proj3[:, :, C_V:C_V + A_KV].reshape(b, t, A_KV_HEADS, A_HEAD_DIM)
    shift_new = proj3[:, t - 1:t, C_U:C_U + SHIFT_W]
    return out, k_new, v_new, ik_ln, wkv_new, shift_new


def kernel(x_prompt, x_sample, cache_k, cache_v, cache_idx_k, state_wkv, state_shift,
           norm1, w_in, idx_k_g, idx_k_b, shift_mu, w0, w2, a0, a2, g2, k_k, k_a, r_k,
           gn_w, gn_b, w_oa, w_ob, w_out, norm2, w_gate, w_up, w_down, norm_f):
    assert w_in.shape[0] == 1, "single-layer kernel"
    l = 0
    p = _prepare_params(l, norm1, w_in, idx_k_g, idx_k_b, shift_mu, w0, w2, a0, a2, g2, k_k, k_a, r_k,
                        gn_w, gn_b, w_oa, w_ob, w_out, norm2, w_gate, w_up, w_down)
    nf = norm_f.reshape(1, -1)
    tri = jnp.asarray(np.triu(np.ones((LANES, LANES), np.float32)), BF16)

    n_p = x_prompt.shape[0]
    shift0 = jnp.zeros((n_p, 1, SHIFT_W), F32)
    wkv_zero = jnp.zeros((n_p, B_HEADS, B_HEAD_DIM, B_HEAD_DIM), F32)
    attend_p = lambda proj3: _attn_prompt(proj3, p['idx_k_g'], p['idx_k_b'], tri)
    y_p, k_p, v_p, ik_p, wkv_p, shift_p = _layer(x_prompt, attend_p, shift0, wkv_zero, p, nf, 64)

    n_s, t_s = x_sample.shape[0], x_sample.shape[1]
    past = cache_k.shape[2]
    ck = cache_k[l].reshape(n_s, past, A_KV)
    cv = cache_v[l].reshape(n_s, past, A_KV)
    attend_s = lambda proj3: _attn_sample(proj3, ck, cv, cache_idx_k[l], p['idx_k_g'], p['idx_k_b'], tri)
    y_s, k_s, v_s, ik_s, wkv_s, shift_s = _layer(x_sample, attend_s, state_shift[l], state_wkv[l], p, nf, t_s)

    lead = lambda z: z[None]
    return (y_p, y_s, lead(k_p), lead(v_p), lead(ik_p), lead(wkv_p), lead(shift_p),
            lead(k_s), lead(v_s), lead(ik_s), lead(wkv_s), lead(shift_s))
```

```python
import functools

import jax
import jax.numpy as jnp
import numpy as np
from jax import lax
from jax.experimental import pallas as pl
from jax.experimental.pallas import tpu as pltpu

F32 = jnp.float32
BF16 = jnp.bfloat16

D_MODEL = 1024
CHUNK = 64
A_HEADS = 8
A_KV_HEADS = 2
A_GROUP = A_HEADS // A_KV_HEADS
A_HEAD_DIM = 64
A_Q = A_HEADS * A_HEAD_DIM
A_KV = A_KV_HEADS * A_HEAD_DIM
IDX_HEADS = 16
IDX_DIM = 64
IDX_Q = IDX_HEADS * IDX_DIM
TOPK_MAX = 256
B_HEADS = 8
B_HEAD_DIM = 64
B_WIDTH = B_HEADS * B_HEAD_DIM
DECAY_LORA = 64
AAA_LORA = 64
GATE_LORA = 128
SHIFT_W = 3 * B_WIDTH + DECAY_LORA + AAA_LORA + GATE_LORA
D_FF = 2816
RMS_EPS = 1e-6
LN_EPS = 1e-6
GN_EPS = 64e-5

LANES = 128
VMEM_LIMIT = 48 * 1024 * 1024
NEG = -1e30

C_IQ = 0
C_GA = C_IQ + IDX_Q
C_GB = C_GA + D_MODEL
C_Q = C_GB + D_MODEL
C_U = C_Q + A_Q
C_K = C_U + SHIFT_W
C_V = C_K + A_KV
C_IKW = C_V + A_KV
P_WIDTH = C_IKW + LANES


def _dot(a, b, trans_b=False):
    dn = (((1,), (1 if trans_b else 0,)), ((), ()))
    return lax.dot_general(a, b, dn, preferred_element_type=F32)


def _split2(x):
    hi = x.astype(BF16)
    lo = (x - hi.astype(F32)).astype(BF16)
    return hi, lo


def _dot3s(a_split, b_split, trans_b=False):
    ah, al = a_split
    bh, bl = b_split
    return _dot(ah, bh, trans_b) + (_dot(ah, bl, trans_b) + _dot(al, bh, trans_b))


def _dot3(a, b, trans_b=False):
    return _dot3s(_split2(a), _split2(b), trans_b)


def _dot_exact_rhs(a, b_bf16, terms=2):
    out = None
    rem = a
    for _ in range(terms):
        part = rem.astype(BF16)
        rem = rem - part.astype(F32)
        d = _dot(part, b_bf16)
        out = d if out is None else out + d
    return out


def _dot_exact_lhs(a_bf16, b, terms=3):
    out = None
    rem = b
    for _ in range(terms):
        part = rem.astype(BF16)
        rem = rem - part.astype(F32)
        d = _dot(a_bf16, part)
        out = d if out is None else out + d
    return out


def _sigmoid(x):
    return 1.0 / (1.0 + jnp.exp(-x))


def _rms(x, g):
    ms = jnp.mean(x * x, axis=-1, keepdims=True)
    return x * lax.rsqrt(ms + RMS_EPS) * g


def _in_proj_kernel(x_ref, g_ref, w_ref, o_ref, xn_sc):
    @pl.when(pl.program_id(1) == 0)
    def _():
        xn_sc[...] = _rms(x_ref[...], g_ref[...]).astype(BF16)

    o_ref[...] = _dot(xn_sc[...], w_ref[...])


def _in_proj(x2, norm1, w_in_p):
    n = x2.shape[0]
    tm = min(n, 1024)
    tn = 640
    return pl.pallas_call(
        _in_proj_kernel,
        grid=(n // tm, P_WIDTH // tn),
        in_specs=[
            pl.BlockSpec((tm, D_MODEL), lambda i, j: (i, 0)),
            pl.BlockSpec((1, D_MODEL), lambda i, j: (0, 0)),
            pl.BlockSpec((D_MODEL, tn), lambda i, j: (0, j)),
        ],
        out_specs=pl.BlockSpec((tm, tn), lambda i, j: (i, j)),
        out_shape=jax.ShapeDtypeStruct((n, P_WIDTH), F32),
        scratch_shapes=[pltpu.VMEM((tm, D_MODEL), BF16)],
        compiler_params=pltpu.CompilerParams(
            dimension_semantics=("arbitrary", "arbitrary"), vmem_limit_bytes=VMEM_LIMIT),
        name="in_proj",
    )(x2, norm1, w_in_p)


def _layer_norm(x, g, b):
    mu = jnp.mean(x, axis=-1, keepdims=True)
    d = x - mu
    var = jnp.mean(d * d, axis=-1, keepdims=True)
    return d * lax.rsqrt(var + LN_EPS) * g + b


def _stage_queries(iq, q, iw, iqs, wb, qg, rows):
    for h in range(IDX_HEADS):
        iqs[h * rows:(h + 1) * rows, :] = (iq[:, h * IDX_DIM:(h + 1) * IDX_DIM] * (IDX_DIM ** -0.5)).astype(BF16)
        wb[h] = jnp.broadcast_to(iw[:, h:h + 1] * (IDX_HEADS ** -0.5), (rows, LANES))
    for g in range(A_KV_HEADS):
        for hh in range(A_GROUP):
            h = g * A_GROUP + hh
            qg[g, hh * rows:(hh + 1) * rows, :] = (
                q[:, h * A_HEAD_DIM:(h + 1) * A_HEAD_DIM] * (A_HEAD_DIM ** -0.5)).astype(BF16)


def _index_scores(iqs, wb, ikb, sc, nt, adm_fn, rows):
    def body(j, carry):
        off = pl.multiple_of(j * LANES, LANES)
        lg = _dot(iqs[...], ikb[pl.ds(off, LANES), :], trans_b=True)
        acc = jnp.maximum(lg[0:rows], 0.0) * wb[0]
        for h in range(1, IDX_HEADS):
            acc = acc + jnp.maximum(lg[h * rows:(h + 1) * rows], 0.0) * wb[h]
        sc[:, pl.ds(off, LANES)] = jnp.where(adm_fn(j), acc, -jnp.inf)
        return carry

    lax.fori_loop(0, nt, body, 0)


def _select_topk(sc, tri_ref, nt, rows):
    shape = (rows, LANES)
    inf = jnp.float32(jnp.inf)

    def tile(j):
        return sc[:, pl.ds(pl.multiple_of(j * LANES, LANES), LANES)]

    def rowred(x, red):
        return jnp.broadcast_to(red(x, axis=1, keepdims=True), shape)

    def count_ge(t):
        acc = lax.fori_loop(
            0, nt, lambda j, a: a + jnp.where(tile(j) >= t, 1.0, 0.0), jnp.zeros(shape, F32))
        return rowred(acc, jnp.sum)

    def stats(j, c):
        mn, mx, na = c
        s = tile(j)
        fin = s > -inf
        return (jnp.minimum(mn, jnp.where(fin, s, inf)), jnp.maximum(mx, s), na + jnp.where(fin, 1.0, 0.0))

    mn, mx, na = lax.fori_loop(
        0, nt, stats, (jnp.full(shape, inf, F32), jnp.full(shape, -inf, F32), jnp.zeros(shape, F32)))
    lo0 = rowred(mn, jnp.min)
    mx = rowred(mx, jnp.max)
    n_adm = rowred(na, jnp.sum)
    kq = jnp.minimum(n_adm, float(TOPK_MAX))
    hi0 = mx + (jnp.abs(mx) * 1e-6 + 1e-30)

    def cond(c):
        return jnp.logical_and(c[0] < 400, c[5] > 0.5)

    def body(c):
        it, lo, hi, c_lo, c_hi, _ = c
        mid = 0.5 * (lo + hi)
        c_mid = count_ge(mid)
        ge = c_mid >= kq
        lo = jnp.where(ge, mid, lo)
        c_lo = jnp.where(ge, c_mid, c_lo)
        hi = jnp.where(ge, hi, mid)
        c_hi = jnp.where(ge, c_hi, c_mid)
        nxt = 0.5 * (lo + hi)
        active = jnp.where(c_lo - c_hi > 1.5, jnp.where(nxt > lo, jnp.where(nxt < hi, 1.0, 0.0), 0.0), 0.0)
        return it + 1, lo, hi, c_lo, c_hi, jnp.max(active)

    _, lo, _, c_lo, c_hi, _ = lax.while_loop(
        cond, body, (jnp.int32(0), lo0, hi0, n_adm, jnp.zeros(shape, F32), jnp.float32(1.0)))

    thr = rowred(
        lax.fori_loop(0, nt, lambda j, a: jnp.minimum(a, jnp.where(tile(j) >= lo, tile(j), inf)),
                      jnp.full(shape, inf, F32)),
        jnp.min)
    take = kq - c_hi
    has_tie = jnp.max(c_lo - kq) > 0.5

    @pl.when(jnp.logical_not(has_tie))
    def _():
        def wr(j, carry):
            off = pl.multiple_of(j * LANES, LANES)
            sc[:, pl.ds(off, LANES)] = jnp.where(sc[:, pl.ds(off, LANES)] >= thr, 0.0, NEG)
            return carry
        lax.fori_loop(0, nt, wr, 0)

    @pl.when(has_tie)
    def _():
        def wr(j, seen):
            off = pl.multiple_of(j * LANES, LANES)
            s = sc[:, pl.ds(off, LANES)]
            tie = jnp.where(s == thr, 1.0, 0.0)
            rank = _dot(tie.astype(BF16), tri_ref[...]) + seen
            keep_tie = jnp.where(s == thr, jnp.where(rank <= take, 0.0, NEG), NEG)
            sc[:, pl.ds(off, LANES)] = jnp.where(s > thr, 0.0, keep_tie)
            return seen + rowred(tie, jnp.sum)
        lax.fori_loop(0, nt, wr, jnp.zeros(shape, F32))


def _attend(sc, qg, kb, vb, m_sc, l_sc, acc_sc, nt, rows):
    m_sc[...] = jnp.full(m_sc.shape, NEG, F32)
    l_sc[...] = jnp.zeros(l_sc.shape, F32)
    acc_sc[...] = jnp.zeros(acc_sc.shape, F32)

    def body(j, carry):
        off = pl.multiple_of(j * LANES, LANES)
        bias = sc[:, pl.ds(off, LANES)]
        bias_g = jnp.concatenate([bias] * A_GROUP, axis=0)
        for g in range(A_KV_HEADS):
            s = _dot(qg[g], kb[g, pl.ds(off, LANES), :], trans_b=True) + bias_g
            m_old = m_sc[g]
            m_new = jnp.maximum(m_old, jnp.max(s, axis=1, keepdims=True))
            alpha = jnp.exp(m_old - m_new)
            p = jnp.exp(s - m_new)
            l_sc[g] = l_sc[g] * alpha + jnp.sum(p, axis=1, keepdims=True)
            acc_sc[g] = acc_sc[g] * alpha + _dot(p.astype(BF16), vb[g, pl.ds(off, LANES), :])
            m_sc[g] = m_new
        return carry

    lax.fori_loop(0, nt, body, 0)


def _write_heads(o_ref, l_sc, acc_sc, rows):
    for g in range(A_KV_HEADS):
        o = acc_sc[g] * (1.0 / l_sc[g])
        for hh in range(A_GROUP):
            h = g * A_GROUP + hh
            o_ref[0, :, h * A_HEAD_DIM:(h + 1) * A_HEAD_DIM] = o[hh * rows:(hh + 1) * rows].astype(o_ref.dtype)


_ATTN_SCRATCH = lambda rows, keys: [
    pltpu.VMEM((A_KV_HEADS, keys, A_HEAD_DIM), BF16),
    pltpu.VMEM((A_KV_HEADS, keys, A_HEAD_DIM), BF16),
    pltpu.VMEM((keys, IDX_DIM), BF16),
    pltpu.VMEM((IDX_HEADS * rows, IDX_DIM), BF16),
    pltpu.VMEM((IDX_HEADS, rows, LANES), F32),
    pltpu.VMEM((A_KV_HEADS, A_GROUP * rows, A_HEAD_DIM), BF16),
    pltpu.VMEM((rows, keys), F32),
    pltpu.VMEM((A_KV_HEADS, A_GROUP * rows, 1), F32),
    pltpu.VMEM((A_KV_HEADS, A_GROUP * rows, 1), F32),
    pltpu.VMEM((A_KV_HEADS, A_GROUP * rows, A_HEAD_DIM), F32),
]


def _attn_prompt_kernel(iq_ref, q_ref, k_ref, v_ref, ikw_ref, ikwq_ref, lng_ref, lnb_ref, tri_ref,
                        o_ref, ikln_ref, kb, vb, ikb, iqs, wb, qg, sc, m_sc, l_sc, acc_sc, *, rows):
    i = pl.program_id(1)

    @pl.when(i == 0)
    def _():
        kf = k_ref[0]
        vf = v_ref[0]
        for g in range(A_KV_HEADS):
            kb[g] = kf[:, g * A_HEAD_DIM:(g + 1) * A_HEAD_DIM].astype(BF16)
            vb[g] = vf[:, g * A_HEAD_DIM:(g + 1) * A_HEAD_DIM].astype(BF16)
        ln = _layer_norm(ikw_ref[0][:, :IDX_DIM], lng_ref[...], lnb_ref[...])
        ikln_ref[0] = ln
        ikb[...] = ln.astype(BF16)

    _stage_queries(iq_ref[0], q_ref[0], ikwq_ref[0][:, IDX_DIM:IDX_DIM + IDX_HEADS], iqs, wb, qg, rows)

    nt = (i + 1) * (rows // LANES)
    q_pos = i * rows + lax.broadcasted_iota(jnp.int32, (rows, LANES), 0)
    q_end = (q_pos // CHUNK + 1) * CHUNK

    def adm(j):
        return j * LANES + lax.broadcasted_iota(jnp.int32, (rows, LANES), 1) < q_end

    _index_scores(iqs, wb, ikb, sc, nt, adm, rows)
    _select_topk(sc, tri_ref, nt, rows)
    _attend(sc, qg, kb, vb, m_sc, l_sc, acc_sc, nt, rows)
    _write_heads(o_ref, l_sc, acc_sc, rows)


def _attn_prompt(proj3, idx_k_g, idx_k_b, tri):
    b, s, _ = proj3.shape
    rows = 128
    kernel = functools.partial(_attn_prompt_kernel, rows=rows)
    return pl.pallas_call(
        kernel,
        grid=(b, s // rows),
        in_specs=[
            pl.BlockSpec((1, rows, IDX_Q), lambda bi, i: (bi, i, C_IQ // IDX_Q)),
            pl.BlockSpec((1, rows, A_Q), lambda bi, i: (bi, i, C_Q // A_Q)),
            pl.BlockSpec((1, s, A_KV), lambda bi, i: (bi, 0, C_K // A_KV)),
            pl.BlockSpec((1, s, A_KV), lambda bi, i: (bi, 0, C_V // A_KV)),
            pl.BlockSpec((1, s, LANES), lambda bi, i: (bi, 0, C_IKW // LANES)),
            pl.BlockSpec((1, rows, LANES), lambda bi, i: (bi, i, C_IKW // LANES)),
            pl.BlockSpec((1, IDX_DIM), lambda bi, i: (0, 0)),
            pl.BlockSpec((1, IDX_DIM), lambda bi, i: (0, 0)),
            pl.BlockSpec((LANES, LANES), lambda bi, i: (0, 0)),
        ],
        out_specs=[
            pl.BlockSpec((1, rows, A_Q), lambda bi, i: (bi, i, 0)),
            pl.BlockSpec((1, s, IDX_DIM), lambda bi, i: (bi, 0, 0)),
        ],
        out_shape=[
            jax.ShapeDtypeStruct((b, s, A_Q), BF16),
            jax.ShapeDtypeStruct((b, s, IDX_DIM), F32),
        ],
        scratch_shapes=_ATTN_SCRATCH(rows, s),
        compiler_params=pltpu.CompilerParams(
            dimension_semantics=("arbitrary", "arbitrary"), vmem_limit_bytes=VMEM_LIMIT),
        name="attn_prompt",
    )(proj3, proj3, proj3, proj3, proj3, proj3, idx_k_g, idx_k_b, tri)


def _attn_sample_kernel(iq_ref, q_ref, k_ref, v_ref, ikw_ref, ck_ref, cv_ref, cik_ref, lng_ref, lnb_ref, tri_ref,
                        o_ref, ikln_ref, kb, vb, ikb, iqs, wb, qg, sc, m_sc, l_sc, acc_sc, *, rows, past, keys):
    new = rows
    pad = keys - past - new
    ck = ck_ref[0]
    cv = cv_ref[0]
    kf = k_ref[0]
    vf = v_ref[0]
    for g in range(A_KV_HEADS):
        sl = slice(g * A_HEAD_DIM, (g + 1) * A_HEAD_DIM)
        kb[g, 0:past, :] = ck[:, sl].astype(BF16)
        vb[g, 0:past, :] = cv[:, sl].astype(BF16)
        kb[g, past:past + new, :] = kf[:, sl].astype(BF16)
        vb[g, past:past + new, :] = vf[:, sl].astype(BF16)
        kb[g, past + new:keys, :] = jnp.zeros((pad, A_HEAD_DIM), BF16)
        vb[g, past + new:keys, :] = jnp.zeros((pad, A_HEAD_DIM), BF16)
    ikw = ikw_ref[0]
    ln = _layer_norm(ikw[:, :IDX_DIM], lng_ref[...], lnb_ref[...])
    ikln_ref[0] = ln
    ikb[0:past, :] = cik_ref[0].astype(BF16)
    ikb[past:past + new, :] = ln.astype(BF16)
    ikb[past + new:keys, :] = jnp.zeros((pad, IDX_DIM), BF16)

    _stage_queries(iq_ref[0], q_ref[0], ikw[:, IDX_DIM:IDX_DIM + IDX_HEADS], iqs, wb, qg, rows)

    nt = keys // LANES

    def adm(j):
        return j * LANES + lax.broadcasted_iota(jnp.int32, (rows, LANES), 1) < past + new

    _index_scores(iqs, wb, ikb, sc, nt, adm, rows)
    _select_topk(sc, tri_ref, nt, rows)
    _attend(sc, qg, kb, vb, m_sc, l_sc, acc_sc, nt, rows)
    _write_heads(o_ref, l_sc, acc_sc, rows)


def _attn_sample(proj3, cache_k, cache_v, cache_ik, idx_k_g, idx_k_b, tri):
    b, t, _ = proj3.shape
    past = cache_k.shape[1]
    keys = -(-(past + t) // LANES) * LANES
    kernel = functools.partial(_attn_sample_kernel, rows=t, past=past, keys=keys)
    return pl.pallas_call(
        kernel,
        grid=(b,),
        in_specs=[
            pl.BlockSpec((1, t, IDX_Q), lambda bi: (bi, 0, C_IQ // IDX_Q)),
            pl.BlockSpec((1, t, A_Q), lambda bi: (bi, 0, C_Q // A_Q)),
            pl.BlockSpec((1, t, A_KV), lambda bi: (bi, 0, C_K // A_KV)),
            pl.BlockSpec((1, t, A_KV), lambda bi: (bi, 0, C_V // A_KV)),
            pl.BlockSpec((1, t, LANES), lambda bi: (bi, 0, C_IKW // LANES)),
            pl.BlockSpec((1, past, A_KV), lambda bi: (bi, 0, 0)),
            pl.BlockSpec((1, past, A_KV), lambda bi: (bi, 0, 0)),
            pl.BlockSpec((1, past, IDX_DIM), lambda bi: (bi, 0, 0)),
            pl.BlockSpec((1, IDX_DIM), lambda bi: (0, 0)),
            pl.BlockSpec((1, IDX_DIM), lambda bi: (0, 0)),
            pl.BlockSpec((LANES, LANES), lambda bi: (0, 0)),
        ],
        out_specs=[
            pl.BlockSpec((1, t, A_Q), lambda bi: (bi, 0, 0)),
            pl.BlockSpec((1, t, IDX_DIM), lambda bi: (bi, 0, 0)),
        ],
        out_shape=[
            jax.ShapeDtypeStruct((b, t, A_Q), BF16),
            jax.ShapeDtypeStruct((b, t, IDX_DIM), F32),
        ],
        scratch_shapes=_ATTN_SCRATCH(t, keys),
        compiler_params=pltpu.CompilerParams(
            dimension_semantics=("arbitrary",), vmem_limit_bytes=VMEM_LIMIT),
        name="attn_sample",
    )(proj3, proj3, proj3, proj3, proj3, cache_k, cache_v, cache_ik, idx_k_g, idx_k_b, tri)


def _rwkv_prep_kernel(u_ref, shift_ref, mu_ref, w0_ref, a0_ref, kk_ref, ka_ref, rk_ref, w2_ref, a2_ref, g2_ref,
                      bd_ref, r_o, lw_o, k_o, v_o, kk_o, a_o, g_o, bonus_o, carry):
    t = pl.program_id(1)
    u = u_ref[0]
    tb = u.shape[0]

    @pl.when(t == 0)
    def _():
        carry[...] = shift_ref[0]

    row = lax.broadcasted_iota(jnp.int32, u.shape, 0)
    u_prev = jnp.where(row == 0, carry[...], pltpu.roll(u, 1, 0))
    carry[...] = u[tb - 1:tb, :]
    m = u + (u_prev - u) * mu_ref[...]

    r = m[:, 0:B_WIDTH]
    k = m[:, B_WIDTH:2 * B_WIDTH]
    v = m[:, 2 * B_WIDTH:3 * B_WIDTH]
    lora = m[:, 3 * B_WIDTH:3 * B_WIDTH + LANES]
    gl = m[:, 3 * B_WIDTH + LANES:]
    lane = lax.broadcasted_iota(jnp.int32, lora.shape, 1)
    lora = jnp.where(lane < DECAY_LORA, jnp.tanh(lora), lora).astype(BF16)
    z = w0_ref[...] + _dot(lora, w2_ref[...])
    softplus = jnp.maximum(-z, 0.0) + jnp.log(1.0 + jnp.exp(-jnp.abs(z)))
    lw = -jnp.exp(-softplus - 0.5)
    a = _sigmoid(a0_ref[...] + _dot(lora, a2_ref[...]))
    g = _dot(_sigmoid(gl).astype(BF16), g2_ref[...])

    kk = k * kk_ref[...]
    ss = _dot_exact_rhs(kk * kk, bd_ref[...])
    kk = kk / jnp.maximum(jnp.sqrt(ss), 1e-12)
    k2 = k * (1.0 + (a - 1.0) * ka_ref[...])
    bonus = _dot_exact_rhs(r * k2 * rk_ref[...], bd_ref[...]) * v

    r_o[0] = r
    lw_o[0] = lw
    k_o[0] = k2
    v_o[0] = v
    kk_o[0] = kk
    a_o[0] = a
    g_o[0] = g
    bonus_o[0] = bonus


def _rwkv_prep(proj3, shift_prev, p):
    b, s, _ = proj3.shape
    tb = min(s, 256)
    vec = lambda w: pl.BlockSpec((1, w), lambda bi, t: (0, 0))
    mat = lambda r, c: pl.BlockSpec((r, c), lambda bi, t: (0, 0))
    out_spec = pl.BlockSpec((1, tb, B_WIDTH), lambda bi, t: (bi, t, 0))
    return pl.pallas_call(
        _rwkv_prep_kernel,
        grid=(b, s // tb),
        in_specs=[
            pl.BlockSpec((1, tb, SHIFT_W), lambda bi, t: (bi, t, C_U // SHIFT_W)),
            pl.BlockSpec((1, 1, SHIFT_W), lambda bi, t: (bi, 0, 0)),
            vec(SHIFT_W), vec(B_WIDTH), vec(B_WIDTH), vec(B_WIDTH), vec(B_WIDTH), vec(B_WIDTH),
            mat(LANES, B_WIDTH), mat(LANES, B_WIDTH), mat(GATE_LORA, B_WIDTH), mat(B_WIDTH, B_WIDTH),
        ],
        out_specs=[out_spec] * 8,
        out_shape=[jax.ShapeDtypeStruct((b, s, B_WIDTH), F32)] * 8,
        scratch_shapes=[pltpu.VMEM((1, SHIFT_W), F32)],
        compiler_params=pltpu.CompilerParams(
            dimension_semantics=("arbitrary", "arbitrary"), vmem_limit_bytes=VMEM_LIMIT),
        name="rwkv_prep",
    )(proj3, shift_prev, p['shift_mu'], p['w0'], p['a0'], p['k_k'], p['k_a'], p['r_k'],
      p['w2p'], p['a2p'], p['g2'], p['bd'])


def _rwkv_chunk_kernel(r_ref, lw_ref, k_ref, v_ref, kk_ref, a_ref, s0_ref, tri_ref, y_ref, s_out_ref, state):
    c = pl.program_id(1)
    tc = r_ref.shape[1]
    n = B_HEAD_DIM

    @pl.when(c == 0)
    def _():
        state[...] = s0_ref[0]

    lw = lw_ref[0]
    cum = _dot_exact_lhs(tri_ref[...], lw)
    g_in = jnp.exp(cum)
    g_ex = jnp.exp(cum - lw)
    g_inv = jnp.exp(-cum)
    g_end = g_in[tc - 1:tc, :]
    kk = kk_ref[0]
    a_t = -kk * g_ex
    b_t = kk * a_ref[0] * g_inv
    k_t = k_ref[0] * g_inv
    r_t = r_ref[0] * g_in
    v = v_ref[0]
    b_e = b_t * g_end
    k_e = k_t * g_end

    ri = lax.broadcasted_iota(jnp.int32, (tc, tc), 0)
    ci = lax.broadcasted_iota(jnp.int32, (tc, tc), 1)
    strict = ci < ri
    incl = ci <= ri
    eye = jnp.where(ci == ri, 1.0, 0.0)

    heads = range(B_HEADS)
    sls = [slice(h * n, (h + 1) * n) for h in heads]
    s_old = [state[h] for h in heads]
    vh = [v[:, sl] for sl in sls]
    ar = [_split2(jnp.concatenate([a_t[:, sl], r_t[:, sl]], axis=0)) for sl in sls]
    bk = [_split2(jnp.concatenate([b_t[:, sl], k_t[:, sl]], axis=0)) for sl in sls]
    cross = [_dot3s(ar[h], bk[h], trans_b=True) for h in heads]
    from_state = [_dot3s(ar[h], _split2(s_old[h]), trans_b=True) for h in heads]
    l_ab = [jnp.where(strict, cross[h][0:tc, 0:tc], 0.0) for h in heads]
    l_ak = [jnp.where(strict, cross[h][0:tc, tc:2 * tc], 0.0) for h in heads]
    m_rbk = [jnp.concatenate([jnp.where(incl, cross[h][tc:2 * tc, 0:tc], 0.0),
                              jnp.where(incl, cross[h][tc:2 * tc, tc:2 * tc], 0.0)], axis=1) for h in heads]
    rhs = [from_state[h][0:tc] + _dot3(l_ak[h], vh[h]) for h in heads]

    inv = [eye + l_ab[h] for h in heads]
    pw = l_ab
    span = 2
    while span < tc:
        pws = [_split2(pw[h]) for h in heads]
        pw = [_dot3s(pws[h], pws[h]) for h in heads]
        pws = [_split2(pw[h]) for h in heads]
        inv = [inv[h] + _dot3s(_split2(inv[h]), pws[h]) for h in heads]
        span *= 2

    z = [_dot3(inv[h], rhs[h]) for h in heads]
    zv = [jnp.concatenate([z[h], vh[h]], axis=0) for h in heads]
    y = [from_state[h][tc:2 * tc] + _dot3(m_rbk[h], zv[h]) for h in heads]
    upd = [_dot3(zv[h].T, jnp.concatenate([b_e[:, sls[h]], k_e[:, sls[h]]], axis=0)) for h in heads]
    for h in heads:
        y_ref[0, :, sls[h]] = y[h]
        state[h] = s_old[h] * g_end[:, sls[h]] + upd[h]

    @pl.when(c == pl.num_programs(1) - 1)
    def _():
        s_out_ref[0] = state[...]


def _rwkv_chunk(r, lw, k2, v, kk, a, s0, tc):
    b, s, _ = r.shape
    tri = jnp.asarray(np.tril(np.ones((tc, tc), np.float32)), BF16)
    blk = pl.BlockSpec((1, tc, B_WIDTH), lambda bi, c: (bi, c, 0))
    st = pl.BlockSpec((1, B_HEADS, B_HEAD_DIM, B_HEAD_DIM), lambda bi, c: (bi, 0, 0, 0))
    return pl.pallas_call(
        _rwkv_chunk_kernel,
        grid=(b, s // tc),
        in_specs=[blk] * 6 + [st, pl.BlockSpec((tc, tc), lambda bi, c: (0, 0))],
        out_specs=[blk, st],
        out_shape=[jax.ShapeDtypeStruct((b, s, B_WIDTH), F32),
                   jax.ShapeDtypeStruct((b, B_HEADS, B_HEAD_DIM, B_HEAD_DIM), F32)],
        scratch_shapes=[pltpu.VMEM((B_HEADS, B_HEAD_DIM, B_HEAD_DIM), F32)],
        compiler_params=pltpu.CompilerParams(
            dimension_semantics=("arbitrary", "arbitrary"), vmem_limit_bytes=VMEM_LIMIT),
        name="rwkv_chunk",
    )(r, lw, k2, v, kk, a, s0, tri)


def _merge_kernel(oa_ref, y_ref, bonus_ref, g_ref, ga_ref, gb_ref, x_ref, gnw_ref, gnb_ref, bd_ref,
                  woa_ref, wob_ref, wout_ref, h_ref):
    y = y_ref[...]
    mean = _dot_exact_rhs(y, bd_ref[...]) * (1.0 / B_HEAD_DIM)
    d = y - mean
    var = _dot_exact_rhs(d * d, bd_ref[...]) * (1.0 / B_HEAD_DIM)
    yn = d * lax.rsqrt(var + GN_EPS) * gnw_ref[...] + gnb_ref[...]
    yb = ((yn + bonus_ref[...]) * g_ref[...]).astype(BF16)
    merged = (_sigmoid(ga_ref[...]) * _dot(oa_ref[...], woa_ref[...])
              + _sigmoid(gb_ref[...]) * _dot(yb, wob_ref[...]))
    h_ref[...] = x_ref[...] + _dot(merged.astype(BF16), wout_ref[...])


def _merge(o_a, y, bonus, g, proj, x2, p):
    n = x2.shape[0]
    tm = min(n, 512)
    row = lambda w, j=0: pl.BlockSpec((tm, w), lambda i: (i, j))
    full = lambda r, c: pl.BlockSpec((r, c), lambda i: (0, 0))
    return pl.pallas_call(
        _merge_kernel,
        grid=(n // tm,),
        in_specs=[
            row(A_Q), row(B_WIDTH), row(B_WIDTH), row(B_WIDTH),
            row(D_MODEL, C_GA // D_MODEL), row(D_MODEL, C_GB // D_MODEL), row(D_MODEL),
            full(1, B_WIDTH), full(1, B_WIDTH), full(B_WIDTH, B_WIDTH),
            full(A_Q, D_MODEL), full(B_WIDTH, D_MODEL), full(D_MODEL, D_MODEL),
        ],
        out_specs=row(D_MODEL),
        out_shape=jax.ShapeDtypeStruct((n, D_MODEL), F32),
        compiler_params=pltpu.CompilerParams(
            dimension_semantics=("arbitrary",), vmem_limit_bytes=VMEM_LIMIT),
        name="merge_out_proj",
    )(o_a, y, bonus, g, proj, proj, x2, p['gn_w'], p['gn_b'], p['bd'], p['w_oa'], p['w_ob'], p['w_out'])


def _ffn_kernel(h_ref, n2_ref, nf_ref, wg_ref, wu_ref, wd_ref, o_ref, hn_sc, acc_sc):
    j = pl.program_id(1)

    @pl.when(j == 0)
    def _():
        hn_sc[...] = _rms(h_ref[...], n2_ref[...]).astype(BF16)
        acc_sc[...] = jnp.zeros(acc_sc.shape, F32)

    hn = hn_sc[...]
    gate = _dot(hn, wg_ref[...])
    up = _dot(hn, wu_ref[...])
    act = (gate * _sigmoid(gate) * up).astype(BF16)
    acc_sc[...] += _dot(act, wd_ref[...])

    @pl.when(j == pl.num_programs(1) - 1)
    def _():
        o_ref[...] = _rms(h_ref[...] + acc_sc[...], nf_ref[...])


def _ffn(h, p, norm_f):
    n = h.shape[0]
    tm = min(n, 512)
    tf = D_FF // 2
    return pl.pallas_call(
        _ffn_kernel,
        grid=(n // tm, D_FF // tf),
        in_specs=[
            pl.BlockSpec((tm, D_MODEL), lambda i, j: (i, 0)),
            pl.BlockSpec((1, D_MODEL), lambda i, j: (0, 0)),
            pl.BlockSpec((1, D_MODEL), lambda i, j: (0, 0)),
            pl.BlockSpec((D_MODEL, tf), lambda i, j: (0, j)),
            pl.BlockSpec((D_MODEL, tf), lambda i, j: (0, j)),
            pl.BlockSpec((tf, D_MODEL), lambda i, j: (j, 0)),
        ],
        out_specs=pl.BlockSpec((tm, D_MODEL), lambda i, j: (i, 0)),
        out_shape=jax.ShapeDtypeStruct((n, D_MODEL), F32),
        scratch_shapes=[pltpu.VMEM((tm, D_MODEL), BF16), pltpu.VMEM((tm, D_MODEL), F32)],
        compiler_params=pltpu.CompilerParams(
            dimension_semantics=("arbitrary", "arbitrary"), vmem_limit_bytes=VMEM_LIMIT),
        name="ffn_final_norm",
    )(h, p['norm2'], norm_f, p['w_gate'], p['w_up'], p['w_down'])


def _prepare_params(l, norm1, w_in, idx_k_g, idx_k_b, shift_mu, w0, w2, a0, a2, g2, k_k, k_a, r_k,
                    gn_w, gn_b, w_oa, w_ob, w_out, norm2, w_gate, w_up, w_down):
    w = w_in[l]
    o = np.cumsum([0, A_Q, A_KV, A_KV, IDX_Q, IDX_DIM, IDX_HEADS, SHIFT_W, D_MODEL, D_MODEL])
    seg = lambda i: w[:, o[i]:o[i + 1]]
    pad = jnp.zeros((D_MODEL, LANES - IDX_DIM - IDX_HEADS), w.dtype)
    w_in_p = jnp.concatenate([seg(3), seg(7), seg(8), seg(0), seg(6), seg(1), seg(2), seg(4), seg(5), pad],
                             axis=1).astype(BF16)
    zeros = jnp.zeros((LANES - DECAY_LORA, B_WIDTH), F32)
    head = np.arange(B_WIDTH) // B_HEAD_DIM
    row = lambda x: x[l].reshape(1, -1)
    return {
        'norm1': row(norm1), 'w_in_p': w_in_p,
        'idx_k_g': row(idx_k_g), 'idx_k_b': row(idx_k_b),
        'shift_mu': row(shift_mu), 'w0': row(w0), 'a0': row(a0),
        'k_k': row(k_k), 'k_a': row(k_a), 'r_k': row(r_k), 'gn_w': row(gn_w), 'gn_b': row(gn_b),
        'w2p': jnp.concatenate([w2[l], zeros], axis=0).astype(BF16),
        'a2p': jnp.concatenate([zeros, a2[l]], axis=0).astype(BF16),
        'g2': g2[l].astype(BF16),
        'bd': jnp.asarray(head[:, None] == head[None, :], BF16),
        'w_oa': w_oa[l].astype(BF16), 'w_ob': w_ob[l].astype(BF16), 'w_out': w_out[l].astype(BF16),
        'norm2': row(norm2),
        'w_gate': w_gate[l].astype(BF16), 'w_up': w_up[l].astype(BF16), 'w_down': w_down[l].astype(BF16),
    }


def _layer(x, attend, shift_prev, wkv0, p, norm_f, chunk):
    b, t, _ = x.shape
    x2 = x.reshape(b * t, D_MODEL)
    proj = _in_proj(x2, p['norm1'], p['w_in_p'])
    proj3 = proj.reshape(b, t, P_WIDTH)
    o_a, ik_ln = attend(proj3)
    r, lw, k2, v, kk, a, g, bonus = _rwkv_prep(proj3, shift_prev, p)
    y, wkv_new = _rwkv_chunk(r, lw, k2, v, kk, a, wkv0, chunk)
    flat = lambda z: z.reshape(b * t, z.shape[-1])
    h = _merge(flat(o_a), flat(y), flat(bonus), flat(g), proj, x2, p)
    out = _ffn(h, p, norm_f).reshape(b, t, D_MODEL)
    k_new = proj3[:, :, C_K:C_K + A_KV].reshape(b, t, A_KV_HEADS, A_HEAD_DIM)
    v_new = proj3[:, :, C_V:C_V + A_KV].reshape(b, t, A_KV_HEADS, A_HEAD_DIM)
    shift_new = proj3[:, t - 1:t, C_U:C_U + SHIFT_W]
    return out, k_new, v_new, ik_ln, wkv_new, shift_new


def kernel(x_prompt, x_sample, cache_k, cache_v, cache_idx_k, state_wkv, state_shift,
           norm1, w_in, idx_k_g, idx_k_b, shift_mu, w0, w2, a0, a2, g2, k_k, k_a, r_k,
           gn_w, gn_b, w_oa, w_ob, w_out, norm2, w_gate, w_up, w_down, norm_f):
    assert w_in.shape[0] == 1, "single-layer kernel"
    l = 0
    p = _prepare_params(l, norm1, w_in, idx_k_g, idx_k_b, shift_mu, w0, w2, a0, a2, g2, k_k, k_a, r_k,
                        gn_w, gn_b, w_oa, w_ob, w_out, norm2, w_gate, w_up, w_down)
    nf = norm_f.reshape(1, -1)
    tri = jnp.asarray(np.triu(np.ones((LANES, LANES), np.float32)), BF16)

    n_p = x_prompt.shape[0]
    shift0 = jnp.zeros((n_p, 1, SHIFT_W), F32)
    wkv_zero = jnp.zeros((n_p, B_HEADS, B_HEAD_DIM, B_HEAD_DIM), F32)
    attend_p = lambda proj3: _attn_prompt(proj3, p['idx_k_g'], p['idx_k_b'], tri)
    y_p, k_p, v_p, ik_p, wkv_p, shift_p = _layer(x_prompt, attend_p, shift0, wkv_zero, p, nf, 64)

    n_s, t_s = x_sample.shape[0], x_sample.shape[1]
    past = cache_k.shape[2]
    ck = cache_k[l].reshape(n_s, past, A_KV)
    cv = cache_v[l].reshape(n_s, past, A_KV)
    attend_s = lambda proj3: _attn_sample(proj3, ck, cv, cache_idx_k[l], p['idx_k_g'], p['idx_k_b'], tri)
    y_s, k_s, v_s, ik_s, wkv_s, shift_s = _layer(x_sample, attend_s, state_shift[l], state_wkv[l], p, nf, t_s)

    lead = lambda z: z[None]
    return (y_p, y_s, lead(k_p), lead(v_p), lead(ik_p), lead(wkv_p), lead(shift_p),
            lead(k_s), lead(v_s), lead(ik_s), lead(wkv_s), lead(shift_s))
```

```python
import functools

import jax
import jax.numpy as jnp
import numpy as np
from jax import lax
from jax.experimental import pallas as pl
from jax.experimental.pallas import tpu as pltpu

F32 = jnp.float32
BF16 = jnp.bfloat16

D_MODEL = 1024
CHUNK = 64
A_HEADS = 8
A_KV_HEADS = 2
A_GROUP = A_HEADS // A_KV_HEADS
A_HEAD_DIM = 64
A_Q = A_HEADS * A_HEAD_DIM
A_KV = A_KV_HEADS * A_HEAD_DIM
IDX_HEADS = 16
IDX_DIM = 64
IDX_Q = IDX_HEADS * IDX_DIM
TOPK_MAX = 256
B_HEADS = 8
B_HEAD_DIM = 64
B_WIDTH = B_HEADS * B_HEAD_DIM
DECAY_LORA = 64
AAA_LORA = 64
GATE_LORA = 128
SHIFT_W = 3 * B_WIDTH + DECAY_LORA + AAA_LORA + GATE_LORA
D_FF = 2816
RMS_EPS = 1e-6
LN_EPS = 1e-6
GN_EPS = 64e-5

LANES = 128
SUB = 8
VMEM_LIMIT = 48 * 1024 * 1024
NEG = -1e30
KEY_TILE = 256

C_IQ = 0
C_GA = C_IQ + IDX_Q
C_GB = C_GA + D_MODEL
C_Q = C_GB + D_MODEL
C_U = C_Q + A_Q
C_K = C_U + SHIFT_W
C_V = C_K + A_KV
C_IKW = C_V + A_KV
P_WIDTH = C_IKW + LANES


def _dot(a, b, trans_b=False):
    dn = (((1,), (1 if trans_b else 0,)), ((), ()))
    return lax.dot_general(a, b, dn, preferred_element_type=F32)


def _split2(x):
    hi = x.astype(BF16)
    lo = (x - hi.astype(F32)).astype(BF16)
    return hi, lo


def _dot3s(a_split, b_split, trans_b=False):
    ah, al = a_split
    bh, bl = b_split
    return _dot(ah, bh, trans_b) + (_dot(ah, bl, trans_b) + _dot(al, bh, trans_b))


def _dot3(a, b, trans_b=False):
    return _dot3s(_split2(a), _split2(b), trans_b)


def _dot_exact_rhs(a, b_bf16, terms=2):
    out = None
    rem = a
    for _ in range(terms):
        part = rem.astype(BF16)
        rem = rem - part.astype(F32)
        d = _dot(part, b_bf16)
        out = d if out is None else out + d
    return out


def _dot_exact_lhs(a_bf16, b, terms=3, trans_b=False):
    out = None
    rem = b
    for _ in range(terms):
        part = rem.astype(BF16)
        rem = rem - part.astype(F32)
        d = _dot(a_bf16, part, trans_b)
        out = d if out is None else out + d
    return out


def _sigmoid(x):
    return 1.0 / (1.0 + jnp.exp(-x))


def _rms(x, g):
    ms = jnp.mean(x * x, axis=-1, keepdims=True)
    return x * lax.rsqrt(ms + RMS_EPS) * g


def _in_proj_kernel(x_ref, g_ref, w_ref, o_ref, xn_sc):
    @pl.when(pl.program_id(1) == 0)
    def _():
        xn_sc[...] = _rms(x_ref[...], g_ref[...]).astype(BF16)

    o_ref[...] = _dot(xn_sc[...], w_ref[...])


def _in_proj(x2, norm1, w_in_p):
    n = x2.shape[0]
    tm = min(n, 1024)
    tn = 640
    return pl.pallas_call(
        _in_proj_kernel,
        grid=(n // tm, P_WIDTH // tn),
        in_specs=[
            pl.BlockSpec((tm, D_MODEL), lambda i, j: (i, 0)),
            pl.BlockSpec((1, D_MODEL), lambda i, j: (0, 0)),
            pl.BlockSpec((D_MODEL, tn), lambda i, j: (0, j)),
        ],
        out_specs=pl.BlockSpec((tm, tn), lambda i, j: (i, j)),
        out_shape=jax.ShapeDtypeStruct((n, P_WIDTH), F32),
        scratch_shapes=[pltpu.VMEM((tm, D_MODEL), BF16)],
        compiler_params=pltpu.CompilerParams(
            dimension_semantics=("arbitrary", "arbitrary"), vmem_limit_bytes=VMEM_LIMIT),
        name="in_proj",
    )(x2, norm1, w_in_p)


def _layer_norm(x, g, b):
    mu = jnp.mean(x, axis=-1, keepdims=True)
    d = x - mu
    var = jnp.mean(d * d, axis=-1, keepdims=True)
    return d * lax.rsqrt(var + LN_EPS) * g + b


def _transpose_bf16(x, eye):
    return _dot(eye, x, trans_b=True)


def _fold(x, op):
    x = x.reshape(x.shape[0] // SUB, SUB, x.shape[1])
    while x.shape[0] > 1:
        half = x.shape[0] // 2
        x = op(x[:half], x[half:])
    return x[0]


def _stage_queries(iq, q, iw, eye_ref, iq_t, w_full, q_t, rows):
    eye = eye_ref[0:IDX_DIM, 0:IDX_DIM]
    for h in range(IDX_HEADS):
        x = (iq[:, h * IDX_DIM:(h + 1) * IDX_DIM] * (IDX_DIM ** -0.5)).astype(BF16)
        iq_t[:, h * rows:(h + 1) * rows] = _transpose_bf16(x, eye).astype(BF16)
    w_t = _dot_exact_lhs(eye_ref[0:IDX_HEADS, 0:IDX_HEADS], iw, trans_b=True)
    for h in range(IDX_HEADS):
        w_full[:, h * rows:(h + 1) * rows] = w_t[h:h + 1, :] * (IDX_HEADS ** -0.5)
    for g in range(A_KV_HEADS):
        for hh in range(A_GROUP):
            h = g * A_GROUP + hh
            x = (q[:, h * A_HEAD_DIM:(h + 1) * A_HEAD_DIM] * (A_HEAD_DIM ** -0.5)).astype(BF16)
            q_t[g, :, hh * rows:(hh + 1) * rows] = _transpose_bf16(x, eye).astype(BF16)


def _index_scores(iq_t, w_full, ikb, sc, nt, adm_fn, rows):
    nblk = IDX_HEADS * rows // LANES

    def body(j, carry):
        off = pl.multiple_of(j * KEY_TILE, KEY_TILE)
        lg = _dot(ikb[pl.ds(off, KEY_TILE), :], iq_t[...])
        acc = None
        for c in range(nblk):
            x = jnp.maximum(lg[:, c * LANES:(c + 1) * LANES], 0.0) * w_full[:, c * LANES:(c + 1) * LANES]
            acc = x if acc is None else acc + x
        shift = LANES // 2
        while shift >= rows:
            acc = acc + pltpu.roll(acc, shift, 1)
            shift //= 2
        sc[pl.ds(off, KEY_TILE), :] = jnp.where(adm_fn(j), acc, -jnp.inf)
        return carry

    lax.fori_loop(0, nt, body, 0)


def _select_topk(sc, tril_ref, nt):
    shape = (SUB, LANES)
    inf = jnp.float32(jnp.inf)

    def tile(j):
        return sc[pl.ds(pl.multiple_of(j * KEY_TILE, KEY_TILE), KEY_TILE), :]

    def allsub(x, red):
        return jnp.broadcast_to(red(x, axis=0, keepdims=True), shape)

    def count_ge(t):
        t1 = t[0:1, :]
        acc = lax.fori_loop(
            0, nt, lambda j, a: a + _fold(jnp.where(tile(j) >= t1, 1.0, 0.0), jnp.add), jnp.zeros(shape, F32))
        return allsub(acc, jnp.sum)

    def stats(j, c):
        mn, mx, na = c
        s = tile(j)
        fin = s > -inf
        return (jnp.minimum(mn, _fold(jnp.where(fin, s, inf), jnp.minimum)), jnp.maximum(mx, _fold(s, jnp.maximum)),
                na + _fold(jnp.where(fin, 1.0, 0.0), jnp.add))

    mn, mx, na = lax.fori_loop(
        0, nt, stats, (jnp.full(shape, inf, F32), jnp.full(shape, -inf, F32), jnp.zeros(shape, F32)))
    lo0 = allsub(mn, jnp.min)
    mx = allsub(mx, jnp.max)
    n_adm = allsub(na, jnp.sum)
    kq = jnp.minimum(n_adm, float(TOPK_MAX))
    hi0 = mx + (jnp.abs(mx) * 1e-6 + 1e-30)

    def cond(c):
        return jnp.logical_and(c[0] < 400, c[5] > 0.5)

    def probe(x, lo, hi, c_lo, c_hi):
        c_x = count_ge(x)
        ge = c_x >= kq
        return jnp.where(ge, x, lo), jnp.where(ge, hi, x), jnp.where(ge, c_x, c_lo), jnp.where(ge, c_hi, c_x)

    def body(c):
        it, lo, hi, c_lo, c_hi, _ = c
        for _ in range(2):
            lo, hi, c_lo, c_hi = probe(0.5 * (lo + hi), lo, hi, c_lo, c_hi)
        nxt = 0.5 * (lo + hi)
        active = jnp.where(c_lo - c_hi > 1.5, jnp.where(nxt > lo, jnp.where(nxt < hi, 1.0, 0.0), 0.0), 0.0)
        active = jnp.where(n_adm > kq, active, 0.0)
        return it + 1, lo, hi, c_lo, c_hi, jnp.max(active)

    _, lo, _, c_lo, c_hi, _ = lax.while_loop(
        cond, body, (jnp.int32(0), lo0, hi0, n_adm, jnp.zeros(shape, F32), jnp.float32(1.0)))

    lo1 = lo[0:1, :]
    thr = allsub(
        lax.fori_loop(0, nt, lambda j, a: jnp.minimum(a, _fold(jnp.where(tile(j) >= lo1, tile(j), inf), jnp.minimum)),
                      jnp.full(shape, inf, F32)),
        jnp.min)[0:1, :]
    take = (kq - c_hi)[0:1, :]
    has_tie = jnp.max(c_lo - kq) > 0.5

    @pl.when(jnp.logical_not(has_tie))
    def _():
        def wr(j, carry):
            off = pl.multiple_of(j * KEY_TILE, KEY_TILE)
            sc[pl.ds(off, KEY_TILE), :] = jnp.where(sc[pl.ds(off, KEY_TILE), :] >= thr, 0.0, NEG)
            return carry
        lax.fori_loop(0, nt, wr, 0)

    @pl.when(has_tie)
    def _():
        def wr(j, seen):
            off = pl.multiple_of(j * KEY_TILE, KEY_TILE)
            s = sc[pl.ds(off, KEY_TILE), :]
            tie = jnp.where(s == thr, 1.0, 0.0)
            rank = _dot(tril_ref[...], tie.astype(BF16)) + seen
            keep_tie = jnp.where(s == thr, jnp.where(rank <= take, 0.0, NEG), NEG)
            sc[pl.ds(off, KEY_TILE), :] = jnp.where(s > thr, 0.0, keep_tie)
            return seen + jnp.sum(tie, axis=0, keepdims=True)
        lax.fori_loop(0, nt, wr, jnp.zeros((1, LANES), F32))


def _attend(sc, q_t, kb, v_t, s_sc, acc_sc, nt):
    lanes_g = q_t.shape[2]
    rep = lanes_g // LANES
    groups = range(A_KV_HEADS)

    def scores(j, macc):
        off = pl.multiple_of(j * KEY_TILE, KEY_TILE)
        bias = sc[pl.ds(off, KEY_TILE), :]
        bias = bias if rep == 1 else jnp.concatenate([bias] * rep, axis=1)
        out = []
        for g in groups:
            s = _dot(kb[g, pl.ds(off, KEY_TILE), :], q_t[g]) + bias
            s_sc[g, pl.ds(off, KEY_TILE), :] = s
            out.append(jnp.maximum(macc[g], _fold(s, jnp.maximum)))
        return tuple(out)

    macc = lax.fori_loop(0, nt, scores, tuple(jnp.full((SUB, lanes_g), NEG, F32) for _ in groups))
    m = [jnp.max(macc[g], axis=0, keepdims=True) for g in groups]
    acc_sc[...] = jnp.zeros(acc_sc.shape, F32)

    def weighted(j, lacc):
        off = pl.multiple_of(j * KEY_TILE, KEY_TILE)
        out = []
        for g in groups:
            p = jnp.exp(s_sc[g, pl.ds(off, KEY_TILE), :] - m[g])
            acc_sc[g] += _dot(v_t[g, :, pl.ds(off, KEY_TILE)], p.astype(BF16))
            out.append(lacc[g] + _fold(p, jnp.add))
        return tuple(out)

    lacc = lax.fori_loop(0, nt, weighted, tuple(jnp.zeros((SUB, lanes_g), F32) for _ in groups))
    return [jnp.sum(lacc[g], axis=0, keepdims=True) for g in groups]


def _write_heads(o_ref, l, acc_sc, eye_ref, rows):
    eye = eye_ref[0:rows, 0:rows]
    for g in range(A_KV_HEADS):
        o_t = (acc_sc[g] * (1.0 / l[g])).astype(BF16)
        for hh in range(A_GROUP):
            h = g * A_GROUP + hh
            o_ref[0, :, h * A_HEAD_DIM:(h + 1) * A_HEAD_DIM] = _transpose_bf16(
                o_t[:, hh * rows:(hh + 1) * rows], eye).astype(o_ref.dtype)


_ATTN_SCRATCH = lambda rows, keys: [
    pltpu.VMEM((A_KV_HEADS, keys, A_HEAD_DIM), BF16),
    pltpu.VMEM((A_KV_HEADS, A_HEAD_DIM, keys), BF16),
    pltpu.VMEM((keys, IDX_DIM), BF16),
    pltpu.VMEM((IDX_DIM, IDX_HEADS * rows), BF16),
    pltpu.VMEM((1, IDX_HEADS * rows), F32),
    pltpu.VMEM((A_KV_HEADS, A_HEAD_DIM, A_GROUP * rows), BF16),
    pltpu.VMEM((keys, LANES), F32),
    pltpu.VMEM((A_KV_HEADS, keys, A_GROUP * rows), F32),
    pltpu.VMEM((A_KV_HEADS, A_HEAD_DIM, A_GROUP * rows), F32),
]


def _attn_prompt_kernel(iq_ref, q_ref, k_ref, v_ref, ikw_ref, ikwq_ref, lng_ref, lnb_ref, tril_ref, eye_ref,
                        o_ref, ikln_ref, kb, v_t, ikb, iq_t, w_full, q_t, sc, s_sc, acc_sc, *, rows):
    i = pl.program_id(1)

    @pl.when(i == 0)
    def _():
        kf = k_ref[0]
        vf = v_ref[0]
        eye = eye_ref[0:A_HEAD_DIM, 0:A_HEAD_DIM]
        for g in range(A_KV_HEADS):
            sl = slice(g * A_HEAD_DIM, (g + 1) * A_HEAD_DIM)
            kb[g] = kf[:, sl].astype(BF16)
            v_t[g] = _transpose_bf16(vf[:, sl].astype(BF16), eye).astype(BF16)
        ln = _layer_norm(ikw_ref[0][:, :IDX_DIM], lng_ref[...], lnb_ref[...])
        ikln_ref[0] = ln
        ikb[...] = ln.astype(BF16)

    _stage_queries(iq_ref[0], q_ref[0], ikwq_ref[0][:, IDX_DIM:IDX_DIM + IDX_HEADS], eye_ref, iq_t, w_full, q_t, rows)

    nt = ((i + 1) * rows + KEY_TILE - 1) // KEY_TILE
    q_pos = i * rows + lax.broadcasted_iota(jnp.int32, (KEY_TILE, LANES), 1)
    q_end = (q_pos // CHUNK + 1) * CHUNK

    def adm(j):
        return j * KEY_TILE + lax.broadcasted_iota(jnp.int32, (KEY_TILE, LANES), 0) < q_end

    _index_scores(iq_t, w_full, ikb, sc, nt, adm, rows)
    _select_topk(sc, tril_ref, nt)
    l = _attend(sc, q_t, kb, v_t, s_sc, acc_sc, nt)
    _write_heads(o_ref, l, acc_sc, eye_ref, rows)


def _attn_prompt(proj3, idx_k_g, idx_k_b, tril, eye):
    b, s, _ = proj3.shape
    rows = LANES
    kernel = functools.partial(_attn_prompt_kernel, rows=rows)
    return pl.pallas_call(
        kernel,
        grid=(b, s // rows),
        in_specs=[
            pl.BlockSpec((1, rows, IDX_Q), lambda bi, i: (bi, i, C_IQ // IDX_Q)),
            pl.BlockSpec((1, rows, A_Q), lambda bi, i: (bi, i, C_Q // A_Q)),
            pl.BlockSpec((1, s, A_KV), lambda bi, i: (bi, 0, C_K // A_KV)),
            pl.BlockSpec((1, s, A_KV), lambda bi, i: (bi, 0, C_V // A_KV)),
            pl.BlockSpec((1, s, LANES), lambda bi, i: (bi, 0, C_IKW // LANES)),
            pl.BlockSpec((1, rows, LANES), lambda bi, i: (bi, i, C_IKW // LANES)),
            pl.BlockSpec((1, IDX_DIM), lambda bi, i: (0, 0)),
            pl.BlockSpec((1, IDX_DIM), lambda bi, i: (0, 0)),
            pl.BlockSpec((KEY_TILE, KEY_TILE), lambda bi, i: (0, 0)),
            pl.BlockSpec((LANES, LANES), lambda bi, i: (0, 0)),
        ],
        out_specs=[
            pl.BlockSpec((1, rows, A_Q), lambda bi, i: (bi, i, 0)),
            pl.BlockSpec((1, s, IDX_DIM), lambda bi, i: (bi, 0, 0)),
        ],
        out_shape=[
            jax.ShapeDtypeStruct((b, s, A_Q), BF16),
            jax.ShapeDtypeStruct((b, s, IDX_DIM), F32),
        ],
        scratch_shapes=_ATTN_SCRATCH(rows, s),
        compiler_params=pltpu.CompilerParams(
            dimension_semantics=("arbitrary", "arbitrary"), vmem_limit_bytes=VMEM_LIMIT),
        name="attn_prompt",
    )(proj3, proj3, proj3, proj3, proj3, proj3, idx_k_g, idx_k_b, tril, eye)


def _attn_sample_kernel(iq_ref, q_ref, k_ref, v_ref, ikw_ref, ck_ref, cv_ref, cik_ref, lng_ref, lnb_ref, tril_ref,
                        eye_ref, o_ref, ikln_ref, kb, v_t, ikb, iq_t, w_full, q_t, sc, s_sc, acc_sc,
                        *, rows, past, keys):
    new = rows
    pad = keys - past - new
    ck = ck_ref[0]
    cv = cv_ref[0]
    kf = k_ref[0]
    vf = v_ref[0]
    eye = eye_ref[0:A_HEAD_DIM, 0:A_HEAD_DIM]
    for g in range(A_KV_HEADS):
        sl = slice(g * A_HEAD_DIM, (g + 1) * A_HEAD_DIM)
        kb[g, 0:past, :] = ck[:, sl].astype(BF16)
        kb[g, past:past + new, :] = kf[:, sl].astype(BF16)
        kb[g, past + new:keys, :] = jnp.zeros((pad, A_HEAD_DIM), BF16)
        v_t[g, :, 0:past] = _transpose_bf16(cv[:, sl].astype(BF16), eye).astype(BF16)
        v_t[g, :, past:past + new] = _transpose_bf16(vf[:, sl].astype(BF16), eye).astype(BF16)
        v_t[g, :, past + new:keys] = jnp.zeros((A_HEAD_DIM, pad), BF16)
    ikw = ikw_ref[0]
    ln = _layer_norm(ikw[:, :IDX_DIM], lng_ref[...], lnb_ref[...])
    ikln_ref[0] = ln
    ikb[0:past, :] = cik_ref[0].astype(BF16)
    ikb[past:past + new, :] = ln.astype(BF16)
    ikb[past + new:keys, :] = jnp.zeros((pad, IDX_DIM), BF16)

    _stage_queries(iq_ref[0], q_ref[0], ikw[:, IDX_DIM:IDX_DIM + IDX_HEADS], eye_ref, iq_t, w_full, q_t, rows)

    nt = keys // KEY_TILE

    def adm(j):
        return j * KEY_TILE + lax.broadcasted_iota(jnp.int32, (KEY_TILE, LANES), 0) < past + new

    _index_scores(iq_t, w_full, ikb, sc, nt, adm, rows)
    _select_topk(sc, tril_ref, nt)
    l = _attend(sc, q_t, kb, v_t, s_sc, acc_sc, nt)
    _write_heads(o_ref, l, acc_sc, eye_ref, rows)


def _attn_sample(proj3, cache_k, cache_v, cache_ik, idx_k_g, idx_k_b, tril, eye):
    b, t, _ = proj3.shape
    past = cache_k.shape[1]
    keys = -(-(past + t) // KEY_TILE) * KEY_TILE
    kernel = functools.partial(_attn_sample_kernel, rows=t, past=past, keys=keys)
    return pl.pallas_call(
        kernel,
        grid=(b,),
        in_specs=[
            pl.BlockSpec((1, t, IDX_Q), lambda bi: (bi, 0, C_IQ // IDX_Q)),
            pl.BlockSpec((1, t, A_Q), lambda bi: (bi, 0, C_Q // A_Q)),
            pl.BlockSpec((1, t, A_KV), lambda bi: (bi, 0, C_K // A_KV)),
            pl.BlockSpec((1, t, A_KV), lambda bi: (bi, 0, C_V // A_KV)),
            pl.BlockSpec((1, t, LANES), lambda bi: (bi, 0, C_IKW // LANES)),
            pl.BlockSpec((1, past, A_KV), lambda bi: (bi, 0, 0)),
            pl.BlockSpec((1, past, A_KV), lambda bi: (bi, 0, 0)),
            pl.BlockSpec((1, past, IDX_DIM), lambda bi: (bi, 0, 0)),
            pl.BlockSpec((1, IDX_DIM), lambda bi: (0, 0)),
            pl.BlockSpec((1, IDX_DIM), lambda bi: (0, 0)),
            pl.BlockSpec((KEY_TILE, KEY_TILE), lambda bi: (0, 0)),
            pl.BlockSpec((LANES, LANES), lambda bi: (0, 0)),
        ],
        out_specs=[
            pl.BlockSpec((1, t, A_Q), lambda bi: (bi, 0, 0)),
            pl.BlockSpec((1, t, IDX_DIM), lambda bi: (bi, 0, 0)),
        ],
        out_shape=[
            jax.ShapeDtypeStruct((b, t, A_Q), BF16),
            jax.ShapeDtypeStruct((b, t, IDX_DIM), F32),
        ],
        scratch_shapes=_ATTN_SCRATCH(t, keys),
        compiler_params=pltpu.CompilerParams(
            dimension_semantics=("arbitrary",), vmem_limit_bytes=VMEM_LIMIT),
        name="attn_sample",
    )(proj3, proj3, proj3, proj3, proj3, cache_k, cache_v, cache_ik, idx_k_g, idx_k_b, tril, eye)


def _rwkv_prep_kernel(u_ref, shift_ref, mu_ref, w0_ref, a0_ref, kk_ref, ka_ref, rk_ref, w2_ref, a2_ref, g2_ref,
                      bd_ref, r_o, lw_o, k_o, v_o, kk_o, a_o, g_o, bonus_o, carry):
    t = pl.program_id(1)
    u = u_ref[0]
    tb = u.shape[0]

    @pl.when(t == 0)
    def _():
        carry[...] = shift_ref[0]

    row = lax.broadcasted_iota(jnp.int32, u.shape, 0)
    u_prev = jnp.where(row == 0, carry[...], pltpu.roll(u, 1, 0))
    carry[...] = u[tb - 1:tb, :]
    m = u + (u_prev - u) * mu_ref[...]

    r = m[:, 0:B_WIDTH]
    k = m[:, B_WIDTH:2 * B_WIDTH]
    v = m[:, 2 * B_WIDTH:3 * B_WIDTH]
    lora = m[:, 3 * B_WIDTH:3 * B_WIDTH + LANES]
    gl = m[:, 3 * B_WIDTH + LANES:]
    lane = lax.broadcasted_iota(jnp.int32, lora.shape, 1)
    lora = jnp.where(lane < DECAY_LORA, jnp.tanh(lora), lora).astype(BF16)
    z = w0_ref[...] + _dot(lora, w2_ref[...])
    softplus = jnp.maximum(-z, 0.0) + jnp.log(1.0 + jnp.exp(-jnp.abs(z)))
    lw = -jnp.exp(-softplus - 0.5)
    a = _sigmoid(a0_ref[...] + _dot(lora, a2_ref[...]))
    g = _dot(_sigmoid(gl).astype(BF16), g2_ref[...])

    kk = k * kk_ref[...]
    ss = _dot_exact_rhs(kk * kk, bd_ref[...])
    kk = kk / jnp.maximum(jnp.sqrt(ss), 1e-12)
    k2 = k * (1.0 + (a - 1.0) * ka_ref[...])
    bonus = _dot_exact_rhs(r * k2 * rk_ref[...], bd_ref[...]) * v

    r_o[0] = r
    lw_o[0] = lw
    k_o[0] = k2
    v_o[0] = v
    kk_o[0] = kk
    a_o[0] = a
    g_o[0] = g
    bonus_o[0] = bonus


def _rwkv_prep(proj3, shift_prev, p):
    b, s, _ = proj3.shape
    tb = min(s, 256)
    vec = lambda w: pl.BlockSpec((1, w), lambda bi, t: (0, 0))
    mat = lambda r, c: pl.BlockSpec((r, c), lambda bi, t: (0, 0))
    out_spec = pl.BlockSpec((1, tb, B_WIDTH), lambda bi, t: (bi, t, 0))
    return pl.pallas_call(
        _rwkv_prep_kernel,
        grid=(b, s // tb),
        in_specs=[
            pl.BlockSpec((1, tb, SHIFT_W), lambda bi, t: (bi, t, C_U // SHIFT_W)),
            pl.BlockSpec((1, 1, SHIFT_W), lambda bi, t: (bi, 0, 0)),
            vec(SHIFT_W), vec(B_WIDTH), vec(B_WIDTH), vec(B_WIDTH), vec(B_WIDTH), vec(B_WIDTH),
            mat(LANES, B_WIDTH), mat(LANES, B_WIDTH), mat(GATE_LORA, B_WIDTH), mat(B_WIDTH, B_WIDTH),
        ],
        out_specs=[out_spec] * 8,
        out_shape=[jax.ShapeDtypeStruct((b, s, B_WIDTH), F32)] * 8,
        scratch_shapes=[pltpu.VMEM((1, SHIFT_W), F32)],
        compiler_params=pltpu.CompilerParams(
            dimension_semantics=("arbitrary", "arbitrary"), vmem_limit_bytes=VMEM_LIMIT),
        name="rwkv_prep",
    )(proj3, shift_prev, p['shift_mu'], p['w0'], p['a0'], p['k_k'], p['k_a'], p['r_k'],
      p['w2p'], p['a2p'], p['g2'], p['bd'])


def _rwkv_chunk_kernel(r_ref, lw_ref, k_ref, v_ref, kk_ref, a_ref, s0_ref, tri_ref, y_ref, s_out_ref, state):
    c = pl.program_id(1)
    tc = r_ref.shape[1]
    n = B_HEAD_DIM

    @pl.when(c == 0)
    def _():
        state[...] = s0_ref[0]

    lw = lw_ref[0]
    cum = _dot_exact_lhs(tri_ref[...], lw)
    g_in = jnp.exp(cum)
    g_ex = jnp.exp(cum - lw)
    g_inv = jnp.exp(-cum)
    g_end = g_in[tc - 1:tc, :]
    kk = kk_ref[0]
    a_t = -kk * g_ex
    b_t = kk * a_ref[0] * g_inv
    k_t = k_ref[0] * g_inv
    r_t = r_ref[0] * g_in
    v = v_ref[0]
    b_e = b_t * g_end
    k_e = k_t * g_end

    ri = lax.broadcasted_iota(jnp.int32, (tc, tc), 0)
    ci = lax.broadcasted_iota(jnp.int32, (tc, tc), 1)
    strict = ci < ri
    incl = ci <= ri
    eye = jnp.where(ci == ri, 1.0, 0.0)

    heads = range(B_HEADS)
    sls = [slice(h * n, (h + 1) * n) for h in heads]
    s_old = [state[h] for h in heads]
    vh = [v[:, sl] for sl in sls]
    bf = lambda x: x.astype(BF16)
    ar = [_split2(jnp.concatenate([a_t[:, sl], r_t[:, sl]], axis=0)) for sl in sls]
    bk = [bf(jnp.concatenate([b_t[:, sl], k_t[:, sl]], axis=0)) for sl in sls]
    cross = [_dot(ar[h][0], bk[h], trans_b=True) for h in heads]
    from_state = [_dot3s(ar[h], _split2(s_old[h]), trans_b=True) for h in heads]
    l_ab = [jnp.where(strict, cross[h][0:tc, 0:tc], 0.0) for h in heads]
    l_ak = [jnp.where(strict, cross[h][0:tc, tc:2 * tc], 0.0) for h in heads]
    m_rbk = [jnp.concatenate([jnp.where(incl, cross[h][tc:2 * tc, 0:tc], 0.0),
                              jnp.where(incl, cross[h][tc:2 * tc, tc:2 * tc], 0.0)], axis=1) for h in heads]
    rhs = [from_state[h][0:tc] + _dot(bf(l_ak[h]), bf(vh[h])) for h in heads]

    inv = [eye + l_ab[h] for h in heads]
    pw = l_ab
    span = 2
    while span < tc:
        pwb = [bf(pw[h]) for h in heads]
        pw = [_dot(pwb[h], pwb[h]) for h in heads]
        inv = [inv[h] + _dot(bf(inv[h]), bf(pw[h])) for h in heads]
        span *= 2

    z = [_dot(bf(inv[h]), bf(rhs[h])) for h in heads]
    zv = [jnp.concatenate([z[h], vh[h]], axis=0) for h in heads]
    y = [from_state[h][tc:2 * tc] + _dot(bf(m_rbk[h]), bf(zv[h])) for h in heads]
    upd = [_dot3(zv[h].T, jnp.concatenate([b_e[:, sls[h]], k_e[:, sls[h]]], axis=0)) for h in heads]
    for h in heads:
        y_ref[0, :, sls[h]] = y[h]
        state[h] = s_old[h] * g_end[:, sls[h]] + upd[h]

    @pl.when(c == pl.num_programs(1) - 1)
    def _():
        s_out_ref[0] = state[...]


def _rwkv_chunk(r, lw, k2, v, kk, a, s0, tc):
    b, s, _ = r.shape
    tri = jnp.asarray(np.tril(np.ones((tc, tc), np.float32)), BF16)
    blk = pl.BlockSpec((1, tc, B_WIDTH), lambda bi, c: (bi, c, 0))
    st = pl.BlockSpec((1, B_HEADS, B_HEAD_DIM, B_HEAD_DIM), lambda bi, c: (bi, 0, 0, 0))
    return pl.pallas_call(
        _rwkv_chunk_kernel,
        grid=(b, s // tc),
        in_specs=[blk] * 6 + [st, pl.BlockSpec((tc, tc), lambda bi, c: (0, 0))],
        out_specs=[blk, st],
        out_shape=[jax.ShapeDtypeStruct((b, s, B_WIDTH), F32),
                   jax.ShapeDtypeStruct((b, B_HEADS, B_HEAD_DIM, B_HEAD_DIM), F32)],
        scratch_shapes=[pltpu.VMEM((B_HEADS, B_HEAD_DIM, B_HEAD_DIM), F32)],
        compiler_params=pltpu.CompilerParams(
            dimension_semantics=("arbitrary", "arbitrary"), vmem_limit_bytes=VMEM_LIMIT),
        name="rwkv_chunk",
    )(r, lw, k2, v, kk, a, s0, tri)


def _merge_kernel(oa_ref, y_ref, bonus_ref, g_ref, ga_ref, gb_ref, x_ref, gnw_ref, gnb_ref, bd_ref,
                  woa_ref, wob_ref, wout_ref, h_ref):
    y = y_ref[...]
    mean = _dot_exact_rhs(y, bd_ref[...]) * (1.0 / B_HEAD_DIM)
    d = y - mean
    var = _dot_exact_rhs(d * d, bd_ref[...]) * (1.0 / B_HEAD_DIM)
    yn = d * lax.rsqrt(var + GN_EPS) * gnw_ref[...] + gnb_ref[...]
    yb = ((yn + bonus_ref[...]) * g_ref[...]).astype(BF16)
    merged = (_sigmoid(ga_ref[...]) * _dot(oa_ref[...], woa_ref[...])
              + _sigmoid(gb_ref[...]) * _dot(yb, wob_ref[...]))
    h_ref[...] = x_ref[...] + _dot(merged.astype(BF16), wout_ref[...])


def _merge(o_a, y, bonus, g, proj, x2, p):
    n = x2.shape[0]
    tm = min(n, 512)
    row = lambda w, j=0: pl.BlockSpec((tm, w), lambda i: (i, j))
    full = lambda r, c: pl.BlockSpec((r, c), lambda i: (0, 0))
    return pl.pallas_call(
        _merge_kernel,
        grid=(n // tm,),
        in_specs=[
            row(A_Q), row(B_WIDTH), row(B_WIDTH), row(B_WIDTH),
            row(D_MODEL, C_GA // D_MODEL), row(D_MODEL, C_GB // D_MODEL), row(D_MODEL),
            full(1, B_WIDTH), full(1, B_WIDTH), full(B_WIDTH, B_WIDTH),
            full(A_Q, D_MODEL), full(B_WIDTH, D_MODEL), full(D_MODEL, D_MODEL),
        ],
        out_specs=row(D_MODEL),
        out_shape=jax.ShapeDtypeStruct((n, D_MODEL), F32),
        compiler_params=pltpu.CompilerParams(
            dimension_semantics=("arbitrary",), vmem_limit_bytes=VMEM_LIMIT),
        name="merge_out_proj",
    )(o_a, y, bonus, g, proj, proj, x2, p['gn_w'], p['gn_b'], p['bd'], p['w_oa'], p['w_ob'], p['w_out'])


def _ffn_kernel(h_ref, n2_ref, nf_ref, wg_ref, wu_ref, wd_ref, o_ref, hn_sc, acc_sc):
    j = pl.program_id(1)

    @pl.when(j == 0)
    def _():
        hn_sc[...] = _rms(h_ref[...], n2_ref[...]).astype(BF16)
        acc_sc[...] = jnp.zeros(acc_sc.shape, F32)

    hn = hn_sc[...]
    gate = _dot(hn, wg_ref[...])
    up = _dot(hn, wu_ref[...])
    act = (gate * _sigmoid(gate) * up).astype(BF16)
    acc_sc[...] += _dot(act, wd_ref[...])

    @pl.when(j == pl.num_programs(1) - 1)
    def _():
        o_ref[...] = _rms(h_ref[...] + acc_sc[...], nf_ref[...])


def _ffn(h, p, norm_f):
    n = h.shape[0]
    tm = min(n, 512)
    tf = D_FF // 2
    return pl.pallas_call(
        _ffn_kernel,
        grid=(n // tm, D_FF // tf),
        in_specs=[
            pl.BlockSpec((tm, D_MODEL), lambda i, j: (i, 0)),
            pl.BlockSpec((1, D_MODEL), lambda i, j: (0, 0)),
            pl.BlockSpec((1, D_MODEL), lambda i, j: (0, 0)),
            pl.BlockSpec((D_MODEL, tf), lambda i, j: (0, j)),
            pl.BlockSpec((D_MODEL, tf), lambda i, j: (0, j)),
            pl.BlockSpec((tf, D_MODEL), lambda i, j: (j, 0)),
        ],
        out_specs=pl.BlockSpec((tm, D_MODEL), lambda i, j: (i, 0)),
        out_shape=jax.ShapeDtypeStruct((n, D_MODEL), F32),
        scratch_shapes=[pltpu.VMEM((tm, D_MODEL), BF16), pltpu.VMEM((tm, D_MODEL), F32)],
        compiler_params=pltpu.CompilerParams(
            dimension_semantics=("arbitrary", "arbitrary"), vmem_limit_bytes=VMEM_LIMIT),
        name="ffn_final_norm",
    )(h, p['norm2'], norm_f, p['w_gate'], p['w_up'], p['w_down'])


def _prepare_params(l, norm1, w_in, idx_k_g, idx_k_b, shift_mu, w0, w2, a0, a2, g2, k_k, k_a, r_k,
                    gn_w, gn_b, w_oa, w_ob, w_out, norm2, w_gate, w_up, w_down):
    w = w_in[l]
    o = np.cumsum([0, A_Q, A_KV, A_KV, IDX_Q, IDX_DIM, IDX_HEADS, SHIFT_W, D_MODEL, D_MODEL])
    seg = lambda i: w[:, o[i]:o[i + 1]]
    pad = jnp.zeros((D_MODEL, LANES - IDX_DIM - IDX_HEADS), w.dtype)
    w_in_p = jnp.concatenate([seg(3), seg(7), seg(8), seg(0), seg(6), seg(1), seg(2), seg(4), seg(5), pad],
                             axis=1).astype(BF16)
    zeros = jnp.zeros((LANES - DECAY_LORA, B_WIDTH), F32)
    head = np.arange(B_WIDTH) // B_HEAD_DIM
    row = lambda x: x[l].reshape(1, -1)
    return {
        'norm1': row(norm1), 'w_in_p': w_in_p,
        'idx_k_g': row(idx_k_g), 'idx_k_b': row(idx_k_b),
        'shift_mu': row(shift_mu), 'w0': row(w0), 'a0': row(a0),
        'k_k': row(k_k), 'k_a': row(k_a), 'r_k': row(r_k), 'gn_w': row(gn_w), 'gn_b': row(gn_b),
        'w2p': jnp.concatenate([w2[l], zeros], axis=0).astype(BF16),
        'a2p': jnp.concatenate([zeros, a2[l]], axis=0).astype(BF16),
        'g2': g2[l].astype(BF16),
        'bd': jnp.asarray(head[:, None] == head[None, :], BF16),
        'w_oa': w_oa[l].astype(BF16), 'w_ob': w_ob[l].astype(BF16), 'w_out': w_out[l].astype(BF16),
        'norm2': row(norm2),
        'w_gate': w_gate[l].astype(BF16), 'w_up': w_up[l].astype(BF16), 'w_down': w_down[l].astype(BF16),
    }


def _layer(x, attend, shift_prev, wkv0, p, norm_f, chunk):
    b, t, _ = x.shape
    x2 = x.reshape(b * t, D_MODEL)
    proj = _in_proj(x2, p['norm1'], p['w_in_p'])
    proj3 = proj.reshape(b, t, P_WIDTH)
    o_a, ik_ln = attend(proj3)
    r, lw, k2, v, kk, a, g, bonus = _rwkv_prep(proj3, shift_prev, p)
    y, wkv_new = _rwkv_chunk(r, lw, k2, v, kk, a, wkv0, chunk)
    flat = lambda z: z.reshape(b * t, z.shape[-1])
    h = _merge(flat(o_a), flat(y), flat(bonus), flat(g), proj, x2, p)
    out = _ffn(h, p, norm_f).reshape(b, t, D_MODEL)
    k_new = proj3[:, :, C_K:C_K + A_KV].reshape(b, t, A_KV_HEADS, A_HEAD_DIM)
    v_new = proj3[:, :, C_V:C_V + A_KV].reshape(b, t, A_KV_HEADS, A_HEAD_DIM)
    shift_new = proj3[:, t - 1:t, C_U:C_U + SHIFT_W]
    return out, k_new, v_new, ik_ln, wkv_new, shift_new


def kernel(x_prompt, x_sample, cache_k, cache_v, cache_idx_k, state_wkv, state_shift,
           norm1, w_in, idx_k_g, idx_k_b, shift_mu, w0, w2, a0, a2, g2, k_k, k_a, r_k,
           gn_w, gn_b, w_oa, w_ob, w_out, norm2, w_gate, w_up, w_down, norm_f):
    assert w_in.shape[0] == 1, "single-layer kernel"
    l = 0
    p = _prepare_params(l, norm1, w_in, idx_k_g, idx_k_b, shift_mu, w0, w2, a0, a2, g2, k_k, k_a, r_k,
                        gn_w, gn_b, w_oa, w_ob, w_out, norm2, w_gate, w_up, w_down)
    nf = norm_f.reshape(1, -1)
    tril = jnp.asarray(np.tril(np.ones((KEY_TILE, KEY_TILE), np.float32)), BF16)
    eye = jnp.asarray(np.eye(LANES, dtype=np.float32), BF16)

    n_p = x_prompt.shape[0]
    shift0 = jnp.zeros((n_p, 1, SHIFT_W), F32)
    wkv_zero = jnp.zeros((n_p, B_HEADS, B_HEAD_DIM, B_HEAD_DIM), F32)
    attend_p = lambda proj3: _attn_prompt(proj3, p['idx_k_g'], p['idx_k_b'], tril, eye)
    y_p, k_p, v_p, ik_p, wkv_p, shift_p = _layer(x_prompt, attend_p, shift0, wkv_zero, p, nf, 64)

    n_s, t_s = x_sample.shape[0], x_sample.shape[1]
    past = cache_k.shape[2]
    ck = cache_k[l].reshape(n_s, past, A_KV)
    cv = cache_v[l].reshape(n_s, past, A_KV)
    attend_s = lambda proj3: _attn_sample(proj3, ck, cv, cache_idx_k[l], p['idx_k_g'], p['idx_k_b'], tril, eye)
    y_s, k_s, v_s, ik_s, wkv_s, shift_s = _layer(x_sample, attend_s, state_shift[l], state_wkv[l], p, nf, t_s)

    lead = lambda z: z[None]
    return (y_p, y_s, lead(k_p), lead(v_p), lead(ik_p), lead(wkv_p), lead(shift_p),
            lead(k_s), lead(v_s), lead(ik_s), lead(wkv_s), lead(shift_s))
```

```python
import functools

import jax
import jax.numpy as jnp
import numpy as np
from jax import lax
from jax.experimental import pallas as pl
from jax.experimental.pallas import tpu as pltpu

F32 = jnp.float32
BF16 = jnp.bfloat16

D_MODEL = 1024
CHUNK = 64
A_HEADS = 8
A_KV_HEADS = 2
A_GROUP = A_HEADS // A_KV_HEADS
A_HEAD_DIM = 64
A_Q = A_HEADS * A_HEAD_DIM
A_KV = A_KV_HEADS * A_HEAD_DIM
IDX_HEADS = 16
IDX_DIM = 64
IDX_Q = IDX_HEADS * IDX_DIM
TOPK_MAX = 256
B_HEADS = 8
B_HEAD_DIM = 64
B_WIDTH = B_HEADS * B_HEAD_DIM
DECAY_LORA = 64
AAA_LORA = 64
GATE_LORA = 128
SHIFT_W = 3 * B_WIDTH + DECAY_LORA + AAA_LORA + GATE_LORA
D_FF = 2816
RMS_EPS = 1e-6
LN_EPS = 1e-6
GN_EPS = 64e-5

LANES = 128
SUB = 8
VMEM_LIMIT = 48 * 1024 * 1024
NEG = -1e30
KEY_TILE = 256

CA_IQ = 0
CA_GA = CA_IQ + IDX_Q
CA_GB = CA_GA + D_MODEL
CA_Q = CA_GB + D_MODEL
PA_WIDTH = CA_Q + A_Q
CB_U = 0
CB_K = CB_U + SHIFT_W
CB_V = CB_K + A_KV
CB_IKW = CB_V + A_KV
PB_WIDTH = CB_IKW + LANES
MXU_COLS = 256


def _dot(a, b, trans_b=False):
    dn = (((1,), (1 if trans_b else 0,)), ((), ()))
    return lax.dot_general(a, b, dn, preferred_element_type=F32)


def _split2(x):
    hi = x.astype(BF16)
    lo = (x - hi.astype(F32)).astype(BF16)
    return hi, lo


def _dot3s(a_split, b_split, trans_b=False):
    ah, al = a_split
    bh, bl = b_split
    return _dot(ah, bh, trans_b) + (_dot(ah, bl, trans_b) + _dot(al, bh, trans_b))


def _dot3(a, b, trans_b=False):
    return _dot3s(_split2(a), _split2(b), trans_b)


def _dot_exact_rhs(a, b_bf16, terms=2):
    out = None
    rem = a
    for _ in range(terms):
        part = rem.astype(BF16)
        rem = rem - part.astype(F32)
        d = _dot(part, b_bf16)
        out = d if out is None else out + d
    return out


def _dot_exact_lhs(a_bf16, b, terms=3, trans_b=False):
    out = None
    rem = b
    for _ in range(terms):
        part = rem.astype(BF16)
        rem = rem - part.astype(F32)
        d = _dot(a_bf16, part, trans_b)
        out = d if out is None else out + d
    return out


def _sigmoid(x):
    return 1.0 / (1.0 + jnp.exp(-x))


def _rms(x, g):
    ms = jnp.mean(x * x, axis=-1, keepdims=True)
    return x * lax.rsqrt(ms + RMS_EPS) * g


def _col_chunks(width, chunk):
    return [(c, min(chunk, width - c)) for c in range(0, width, chunk)]


def _in_proj_kernel(x_ref, g_ref, w_ref, pa_ref, pb_ref):
    xn = _rms(x_ref[...], g_ref[...]).astype(BF16)
    for c, n in _col_chunks(PA_WIDTH, 2 * MXU_COLS):
        pa_ref[:, c:c + n] = _dot(xn, w_ref[:, c:c + n]).astype(BF16)
    for c, n in _col_chunks(PB_WIDTH, 2 * MXU_COLS):
        pb_ref[:, c:c + n] = _dot(xn, w_ref[:, PA_WIDTH + c:PA_WIDTH + c + n])


def _in_proj(x2, norm1, w_in_p):
    n = x2.shape[0]
    tm = min(n, 512)
    return pl.pallas_call(
        _in_proj_kernel,
        grid=(n // tm,),
        in_specs=[
            pl.BlockSpec((tm, D_MODEL), lambda i: (i, 0)),
            pl.BlockSpec((1, D_MODEL), lambda i: (0, 0)),
            pl.BlockSpec((D_MODEL, PA_WIDTH + PB_WIDTH), lambda i: (0, 0), pipeline_mode=pl.Buffered(1)),
        ],
        out_specs=[pl.BlockSpec((tm, PA_WIDTH), lambda i: (i, 0)), pl.BlockSpec((tm, PB_WIDTH), lambda i: (i, 0))],
        out_shape=[jax.ShapeDtypeStruct((n, PA_WIDTH), BF16), jax.ShapeDtypeStruct((n, PB_WIDTH), F32)],
        compiler_params=pltpu.CompilerParams(
            dimension_semantics=("arbitrary",), vmem_limit_bytes=VMEM_LIMIT),
        name="in_proj",
    )(x2, norm1, w_in_p)


def _layer_norm(x, g, b):
    mu = jnp.mean(x, axis=-1, keepdims=True)
    d = x - mu
    var = jnp.mean(d * d, axis=-1, keepdims=True)
    return d * lax.rsqrt(var + LN_EPS) * g + b


def _transpose_bf16(x, eye):
    return _dot(eye, x, trans_b=True)


def _key_loop(nt, body, init):
    if isinstance(nt, int):
        carry = init
        for j in range(nt):
            carry = body(j, carry)
        return carry
    return lax.fori_loop(0, nt, body, init)


def _key_off(j):
    return j * KEY_TILE if isinstance(j, int) else pl.multiple_of(j * KEY_TILE, KEY_TILE)


def _fold(x, op):
    x = x.reshape(x.shape[0] // SUB, SUB, x.shape[1])
    while x.shape[0] > 1:
        half = x.shape[0] // 2
        x = op(x[:half], x[half:])
    return x[0]


def _stage_queries(iq, q, iw, eye_ref, iq_t, w_full, q_t, rows):
    eye = eye_ref[0:IDX_DIM, 0:IDX_DIM]
    for h in range(IDX_HEADS):
        x = (iq[:, h * IDX_DIM:(h + 1) * IDX_DIM] * (IDX_DIM ** -0.5)).astype(BF16)
        iq_t[:, h * rows:(h + 1) * rows] = _transpose_bf16(x, eye).astype(BF16)
    w_t = _dot_exact_lhs(eye_ref[0:IDX_HEADS, 0:IDX_HEADS], iw, trans_b=True)
    for h in range(IDX_HEADS):
        w_full[:, h * rows:(h + 1) * rows] = w_t[h:h + 1, :] * (IDX_HEADS ** -0.5)
    for g in range(A_KV_HEADS):
        for hh in range(A_GROUP):
            h = g * A_GROUP + hh
            x = (q[:, h * A_HEAD_DIM:(h + 1) * A_HEAD_DIM] * (A_HEAD_DIM ** -0.5)).astype(BF16)
            q_t[g, :, hh * rows:(hh + 1) * rows] = _transpose_bf16(x, eye).astype(BF16)


PACK_ORDER = (0, 2, 1, 3)


def _pack_quarters(acc):
    n = acc.shape[0] // 4
    a, b, c, d = (acc[i * n:(i + 1) * n] for i in range(4))
    lane = lax.broadcasted_iota(jnp.int32, (n, LANES), 1)
    low64 = lane < LANES // 2
    low32 = (lane & (LANES // 4)) == 0
    ab = jnp.where(low64, a, b) + pltpu.roll(jnp.where(low64, b, a), LANES // 2, 1)
    cd = jnp.where(low64, c, d) + pltpu.roll(jnp.where(low64, d, c), LANES // 2, 1)
    t1 = ab + pltpu.roll(ab, 3 * LANES // 4, 1)
    t2 = cd + pltpu.roll(cd, LANES // 4, 1)
    return jnp.where(low32, t1, t2)


def _index_scores(iq_t, w_full, ikb, sc, sc_pk, nt, adm_fn, adm_pk_fn, rows):
    nblk = IDX_HEADS * rows // LANES

    def body(j, carry):
        off = _key_off(j)
        lg = _dot(ikb[pl.ds(off, KEY_TILE), :], iq_t[...])
        acc = None
        for c in range(nblk):
            x = jnp.maximum(lg[:, c * LANES:(c + 1) * LANES], 0.0) * w_full[:, c * LANES:(c + 1) * LANES]
            acc = x if acc is None else acc + x
        if sc_pk is not None:
            assert LANES // rows == 4
            pk = KEY_TILE // 4
            sc_pk[pl.ds(j * pk, pk), :] = jnp.where(adm_pk_fn(j), _pack_quarters(acc), -jnp.inf)
        shift = LANES // 2
        while shift >= rows:
            acc = acc + pltpu.roll(acc, shift, 1)
            shift //= 2
        sc[pl.ds(off, KEY_TILE), :] = jnp.where(adm_fn(j), acc, -jnp.inf)
        return carry

    _key_loop(nt, body, 0)


def _select_topk(sc, sc_pk, tril_ref, nt):
    shape = (SUB, LANES)
    inf = jnp.float32(jnp.inf)
    src = sc if sc_pk is None else sc_pk
    rows_t = KEY_TILE if sc_pk is None else KEY_TILE // 4

    def tile(j):
        start = _key_off(j) if sc_pk is None else j * rows_t
        return src[pl.ds(start, rows_t), :]

    def allsub(x, red):
        op = {jnp.sum: jnp.add, jnp.min: jnp.minimum, jnp.max: jnp.maximum}[red]
        if sc_pk is not None:
            x = op(op(x, pltpu.roll(x, LANES // 4, 1)),
                   op(pltpu.roll(x, LANES // 2, 1), pltpu.roll(x, 3 * LANES // 4, 1)))
        return jnp.broadcast_to(red(x, axis=0, keepdims=True), shape)

    def count_ge(t):
        t1 = t[0:1, :]
        acc = _key_loop(
            nt, lambda j, a: a + _fold(jnp.where(tile(j) >= t1, 1.0, 0.0), jnp.add), jnp.zeros(shape, F32))
        return allsub(acc, jnp.sum)

    def stats(j, c):
        mn, mx, na = c
        s = tile(j)
        fin = s > -inf
        return (jnp.minimum(mn, _fold(jnp.where(fin, s, inf), jnp.minimum)), jnp.maximum(mx, _fold(s, jnp.maximum)),
                na + _fold(jnp.where(fin, 1.0, 0.0), jnp.add))

    mn, mx, na = _key_loop(
        nt, stats, (jnp.full(shape, inf, F32), jnp.full(shape, -inf, F32), jnp.zeros(shape, F32)))
    lo0 = allsub(mn, jnp.min)
    mx = allsub(mx, jnp.max)
    n_adm = allsub(na, jnp.sum)
    kq = jnp.minimum(n_adm, float(TOPK_MAX))
    hi0 = mx + (jnp.abs(mx) * 1e-6 + 1e-30)

    def cond(c):
        return jnp.logical_and(c[0] < 400, c[5] > 0.5)

    def probe(x, lo, hi, c_lo, c_hi):
        c_x = count_ge(x)
        ge = c_x >= kq
        return jnp.where(ge, x, lo), jnp.where(ge, hi, x), jnp.where(ge, c_x, c_lo), jnp.where(ge, c_hi, c_x)

    def body(c):
        it, lo, hi, c_lo, c_hi, _ = c
        for _ in range(2):
            lo, hi, c_lo, c_hi = probe(0.5 * (lo + hi), lo, hi, c_lo, c_hi)
        nxt = 0.5 * (lo + hi)
        active = jnp.where(c_lo - c_hi > 1.5, jnp.where(nxt > lo, jnp.where(nxt < hi, 1.0, 0.0), 0.0), 0.0)
        active = jnp.where(n_adm > kq, active, 0.0)
        return it + 1, lo, hi, c_lo, c_hi, jnp.max(active)

    _, lo, _, c_lo, c_hi, _ = lax.while_loop(
        cond, body, (jnp.int32(0), lo0, hi0, n_adm, jnp.zeros(shape, F32), jnp.float32(1.0)))

    lo1 = lo[0:1, :]
    thr = allsub(
        _key_loop(nt, lambda j, a: jnp.minimum(a, _fold(jnp.where(tile(j) >= lo1, tile(j), inf), jnp.minimum)),
                  jnp.full(shape, inf, F32)),
        jnp.min)[0:1, :]
    take = (kq - c_hi)[0:1, :]
    has_tie = jnp.max(c_lo - kq) > 0.5

    @pl.when(jnp.logical_not(has_tie))
    def _():
        def wr(j, carry):
            off = _key_off(j)
            sc[pl.ds(off, KEY_TILE), :] = jnp.where(sc[pl.ds(off, KEY_TILE), :] >= thr, 0.0, NEG)
            return carry
        _key_loop(nt, wr, 0)

    @pl.when(has_tie)
    def _():
        def wr(j, seen):
            off = _key_off(j)
            s = sc[pl.ds(off, KEY_TILE), :]
            tie = jnp.where(s == thr, 1.0, 0.0)
            rank = _dot(tril_ref[...], tie.astype(BF16)) + seen
            keep_tie = jnp.where(s == thr, jnp.where(rank <= take, 0.0, NEG), NEG)
            sc[pl.ds(off, KEY_TILE), :] = jnp.where(s > thr, 0.0, keep_tie)
            return seen + jnp.sum(tie, axis=0, keepdims=True)
        lax.fori_loop(0, nt, wr, jnp.zeros((1, LANES), F32))


def _attend(sc, q_t, kb, v_t, s_sc, acc_sc, nt):
    lanes_g = q_t.shape[2]
    rep = lanes_g // LANES
    groups = range(A_KV_HEADS)

    def scores(j, macc):
        off = _key_off(j)
        bias = sc[pl.ds(off, KEY_TILE), :]
        bias = bias if rep == 1 else jnp.concatenate([bias] * rep, axis=1)
        out = []
        for g in groups:
            s = _dot(kb[g, pl.ds(off, KEY_TILE), :], q_t[g]) + bias
            s_sc[g, pl.ds(off, KEY_TILE), :] = s
            out.append(jnp.maximum(macc[g], _fold(s, jnp.maximum)))
        return tuple(out)

    macc = _key_loop(nt, scores, tuple(jnp.full((SUB, lanes_g), NEG, F32) for _ in groups))
    m = [jnp.max(macc[g], axis=0, keepdims=True) for g in groups]
    acc_sc[...] = jnp.zeros(acc_sc.shape, F32)

    def weighted(j, lacc):
        off = _key_off(j)
        out = []
        for g in groups:
            p = jnp.exp(s_sc[g, pl.ds(off, KEY_TILE), :] - m[g])
            acc_sc[g] += _dot(v_t[g, :, pl.ds(off, KEY_TILE)], p.astype(BF16))
            out.append(lacc[g] + _fold(p, jnp.add))
        return tuple(out)

    lacc = _key_loop(nt, weighted, tuple(jnp.zeros((SUB, lanes_g), F32) for _ in groups))
    return [jnp.sum(lacc[g], axis=0, keepdims=True) for g in groups]


def _write_heads(o_ref, l, acc_sc, eye_ref, rows):
    eye = eye_ref[0:rows, 0:rows]
    for g in range(A_KV_HEADS):
        o_t = (acc_sc[g] * (1.0 / l[g])).astype(BF16)
        for hh in range(A_GROUP):
            h = g * A_GROUP + hh
            o_ref[0, :, h * A_HEAD_DIM:(h + 1) * A_HEAD_DIM] = _transpose_bf16(
                o_t[:, hh * rows:(hh + 1) * rows], eye).astype(o_ref.dtype)


_ATTN_SCRATCH = lambda rows, keys: [
    pltpu.VMEM((A_KV_HEADS, keys, A_HEAD_DIM), BF16),
    pltpu.VMEM((A_KV_HEADS, A_HEAD_DIM, keys), BF16),
    pltpu.VMEM((keys, IDX_DIM), BF16),
    pltpu.VMEM((IDX_DIM, IDX_HEADS * rows), BF16),
    pltpu.VMEM((1, IDX_HEADS * rows), F32),
    pltpu.VMEM((A_KV_HEADS, A_HEAD_DIM, A_GROUP * rows), BF16),
    pltpu.VMEM((keys, LANES), F32),
    pltpu.VMEM((A_KV_HEADS, keys, A_GROUP * rows), F32),
    pltpu.VMEM((A_KV_HEADS, A_HEAD_DIM, A_GROUP * rows), F32),
] + ([pltpu.VMEM((keys // (LANES // rows), LANES), F32)] if rows < LANES else [])


def _attn_prompt_kernel(iq_ref, q_ref, k_ref, v_ref, ikw_ref, ikwq_ref, lng_ref, lnb_ref, tril_ref, eye_ref,
                        o_ref, ikln_ref, kb, v_t, ikb, iq_t, w_full, q_t, sc, s_sc, acc_sc, *, rows):
    i = pl.program_id(1)

    @pl.when(i == 0)
    def _():
        kf = k_ref[0]
        vf = v_ref[0]
        eye = eye_ref[0:A_HEAD_DIM, 0:A_HEAD_DIM]
        for g in range(A_KV_HEADS):
            sl = slice(g * A_HEAD_DIM, (g + 1) * A_HEAD_DIM)
            kb[g] = kf[:, sl].astype(BF16)
            v_t[g] = _transpose_bf16(vf[:, sl].astype(BF16), eye).astype(BF16)
        ln = _layer_norm(ikw_ref[0][:, :IDX_DIM], lng_ref[...], lnb_ref[...])
        ikln_ref[0] = ln
        ikb[...] = ln.astype(BF16)

    _stage_queries(iq_ref[0], q_ref[0], ikwq_ref[0][:, IDX_DIM:IDX_DIM + IDX_HEADS], eye_ref, iq_t, w_full, q_t, rows)

    nt = ((i + 1) * rows + KEY_TILE - 1) // KEY_TILE
    q_pos = i * rows + lax.broadcasted_iota(jnp.int32, (KEY_TILE, LANES), 1)
    q_end = (q_pos // CHUNK + 1) * CHUNK

    def adm(j):
        return j * KEY_TILE + lax.broadcasted_iota(jnp.int32, (KEY_TILE, LANES), 0) < q_end

    _index_scores(iq_t, w_full, ikb, sc, None, nt, adm, None, rows)
    _select_topk(sc, None, tril_ref, nt)
    l = _attend(sc, q_t, kb, v_t, s_sc, acc_sc, nt)
    _write_heads(o_ref, l, acc_sc, eye_ref, rows)


def _attn_prompt(pa3, pb3, idx_k_g, idx_k_b, tril, eye):
    b, s, _ = pa3.shape
    rows = LANES
    kernel = functools.partial(_attn_prompt_kernel, rows=rows)
    return pl.pallas_call(
        kernel,
        grid=(b, s // rows),
        in_specs=[
            pl.BlockSpec((1, rows, IDX_Q), lambda bi, i: (bi, i, CA_IQ // IDX_Q)),
            pl.BlockSpec((1, rows, A_Q), lambda bi, i: (bi, i, CA_Q // A_Q)),
            pl.BlockSpec((1, s, A_KV), lambda bi, i: (bi, 0, CB_K // A_KV)),
            pl.BlockSpec((1, s, A_KV), lambda bi, i: (bi, 0, CB_V // A_KV)),
            pl.BlockSpec((1, s, LANES), lambda bi, i: (bi, 0, CB_IKW // LANES)),
            pl.BlockSpec((1, rows, LANES), lambda bi, i: (bi, i, CB_IKW // LANES)),
            pl.BlockSpec((1, IDX_DIM), lambda bi, i: (0, 0)),
            pl.BlockSpec((1, IDX_DIM), lambda bi, i: (0, 0)),
            pl.BlockSpec((KEY_TILE, KEY_TILE), lambda bi, i: (0, 0)),
            pl.BlockSpec((LANES, LANES), lambda bi, i: (0, 0)),
        ],
        out_specs=[
            pl.BlockSpec((1, rows, A_Q), lambda bi, i: (bi, i, 0)),
            pl.BlockSpec((1, s, IDX_DIM), lambda bi, i: (bi, 0, 0)),
        ],
        out_shape=[
            jax.ShapeDtypeStruct((b, s, A_Q), BF16),
            jax.ShapeDtypeStruct((b, s, IDX_DIM), F32),
        ],
        scratch_shapes=_ATTN_SCRATCH(rows, s),
        compiler_params=pltpu.CompilerParams(
            dimension_semantics=("arbitrary", "arbitrary"), vmem_limit_bytes=VMEM_LIMIT),
        name="attn_prompt",
    )(pa3, pa3, pb3, pb3, pb3, pb3, idx_k_g, idx_k_b, tril, eye)


def _attn_sample_kernel(iq_ref, q_ref, k_ref, v_ref, ikw_ref, ck_ref, cv_ref, cik_ref, lng_ref, lnb_ref, tril_ref,
                        eye_ref, o_ref, ikln_ref, kb, v_t, ikb, iq_t, w_full, q_t, sc, s_sc, acc_sc, sc_pk,
                        *, rows, past, keys):
    new = rows
    pad = keys - past - new
    ck = ck_ref[0]
    cv = cv_ref[0]
    kf = k_ref[0]
    vf = v_ref[0]
    eye = eye_ref[0:A_HEAD_DIM, 0:A_HEAD_DIM]
    for g in range(A_KV_HEADS):
        sl = slice(g * A_HEAD_DIM, (g + 1) * A_HEAD_DIM)
        kb[g, 0:past, :] = ck[:, sl].astype(BF16)
        kb[g, past:past + new, :] = kf[:, sl].astype(BF16)
        kb[g, past + new:keys, :] = jnp.zeros((pad, A_HEAD_DIM), BF16)
        v_t[g, :, 0:past] = _transpose_bf16(cv[:, sl].astype(BF16), eye).astype(BF16)
        v_t[g, :, past:past + new] = _transpose_bf16(vf[:, sl].astype(BF16), eye).astype(BF16)
        v_t[g, :, past + new:keys] = jnp.zeros((A_HEAD_DIM, pad), BF16)
    ikw = ikw_ref[0]
    ln = _layer_norm(ikw[:, :IDX_DIM], lng_ref[...], lnb_ref[...])
    ikln_ref[0] = ln
    ikb[0:past, :] = cik_ref[0].astype(BF16)
    ikb[past:past + new, :] = ln.astype(BF16)
    ikb[past + new:keys, :] = jnp.zeros((pad, IDX_DIM), BF16)

    _stage_queries(iq_ref[0], q_ref[0], ikw[:, IDX_DIM:IDX_DIM + IDX_HEADS], eye_ref, iq_t, w_full, q_t, rows)

    nt = keys // KEY_TILE

    def adm(j):
        return j * KEY_TILE + lax.broadcasted_iota(jnp.int32, (KEY_TILE, LANES), 0) < past + new

    pk = KEY_TILE // 4
    group = lax.broadcasted_iota(jnp.int32, (pk, LANES), 1) // rows
    sub_block = jnp.zeros((pk, LANES), jnp.int32)
    for c, blk in enumerate(PACK_ORDER):
        sub_block = jnp.where(group == c, blk, sub_block)
    key_in_tile = sub_block * pk + lax.broadcasted_iota(jnp.int32, (pk, LANES), 0)

    def adm_pk(j):
        return j * KEY_TILE + key_in_tile < past + new

    _index_scores(iq_t, w_full, ikb, sc, sc_pk, nt, adm, adm_pk, rows)
    _select_topk(sc, sc_pk, tril_ref, nt)
    l = _attend(sc, q_t, kb, v_t, s_sc, acc_sc, nt)
    _write_heads(o_ref, l, acc_sc, eye_ref, rows)


def _attn_sample(pa3, pb3, cache_k, cache_v, cache_ik, idx_k_g, idx_k_b, tril, eye):
    b, t, _ = pa3.shape
    past = cache_k.shape[1]
    keys = -(-(past + t) // KEY_TILE) * KEY_TILE
    kernel = functools.partial(_attn_sample_kernel, rows=t, past=past, keys=keys)
    return pl.pallas_call(
        kernel,
        grid=(b,),
        in_specs=[
            pl.BlockSpec((1, t, IDX_Q), lambda bi: (bi, 0, CA_IQ // IDX_Q)),
            pl.BlockSpec((1, t, A_Q), lambda bi: (bi, 0, CA_Q // A_Q)),
            pl.BlockSpec((1, t, A_KV), lambda bi: (bi, 0, CB_K // A_KV)),
            pl.BlockSpec((1, t, A_KV), lambda bi: (bi, 0, CB_V // A_KV)),
            pl.BlockSpec((1, t, LANES), lambda bi: (bi, 0, CB_IKW // LANES)),
            pl.BlockSpec((1, past, A_KV), lambda bi: (bi, 0, 0)),
            pl.BlockSpec((1, past, A_KV), lambda bi: (bi, 0, 0)),
            pl.BlockSpec((1, past, IDX_DIM), lambda bi: (bi, 0, 0)),
            pl.BlockSpec((1, IDX_DIM), lambda bi: (0, 0)),
            pl.BlockSpec((1, IDX_DIM), lambda bi: (0, 0)),
            pl.BlockSpec((KEY_TILE, KEY_TILE), lambda bi: (0, 0)),
            pl.BlockSpec((LANES, LANES), lambda bi: (0, 0)),
        ],
        out_specs=[
            pl.BlockSpec((1, t, A_Q), lambda bi: (bi, 0, 0)),
            pl.BlockSpec((1, t, IDX_DIM), lambda bi: (bi, 0, 0)),
        ],
        out_shape=[
            jax.ShapeDtypeStruct((b, t, A_Q), BF16),
            jax.ShapeDtypeStruct((b, t, IDX_DIM), F32),
        ],
        scratch_shapes=_ATTN_SCRATCH(t, keys),
        compiler_params=pltpu.CompilerParams(
            dimension_semantics=("arbitrary",), vmem_limit_bytes=VMEM_LIMIT),
        name="attn_sample",
    )(pa3, pa3, pb3, pb3, pb3, cache_k, cache_v, cache_ik, idx_k_g, idx_k_b, tril, eye)


def _rwkv_prep_kernel(u_ref, shift_ref, mu_ref, w0_ref, a0_ref, kk_ref, ka_ref, rk_ref, w2_ref, a2_ref, g2_ref,
                      bd_ref, r_o, lw_o, k_o, v_o, kk_o, a_o, g_o, bonus_o, carry):
    t = pl.program_id(1)
    u = u_ref[0]
    tb = u.shape[0]

    @pl.when(t == 0)
    def _():
        carry[...] = shift_ref[0]

    row = lax.broadcasted_iota(jnp.int32, u.shape, 0)
    u_prev = jnp.where(row == 0, carry[...], pltpu.roll(u, 1, 0))
    carry[...] = u[tb - 1:tb, :]
    m = u + (u_prev - u) * mu_ref[...]

    r = m[:, 0:B_WIDTH]
    k = m[:, B_WIDTH:2 * B_WIDTH]
    v = m[:, 2 * B_WIDTH:3 * B_WIDTH]
    lora = m[:, 3 * B_WIDTH:3 * B_WIDTH + LANES]
    gl = m[:, 3 * B_WIDTH + LANES:]
    lane = lax.broadcasted_iota(jnp.int32, lora.shape, 1)
    lora = jnp.where(lane < DECAY_LORA, jnp.tanh(lora), lora).astype(BF16)
    z = w0_ref[...] + _dot(lora, w2_ref[...])
    softplus = jnp.maximum(-z, 0.0) + jnp.log(1.0 + jnp.exp(-jnp.abs(z)))
    lw = -jnp.exp(-softplus - 0.5)
    a = _sigmoid(a0_ref[...] + _dot(lora, a2_ref[...]))
    g = _dot(_sigmoid(gl).astype(BF16), g2_ref[...])

    kk = k * kk_ref[...]
    ss = _dot_exact_rhs(kk * kk, bd_ref[...])
    kk = kk / jnp.maximum(jnp.sqrt(ss), 1e-12)
    k2 = k * (1.0 + (a - 1.0) * ka_ref[...])
    bonus = _dot_exact_rhs(r * k2 * rk_ref[...], bd_ref[...]) * v

    r_o[0] = r
    lw_o[0] = lw
    k_o[0] = k2
    v_o[0] = v
    kk_o[0] = kk
    a_o[0] = a
    g_o[0] = g
    bonus_o[0] = bonus


def _rwkv_prep(pb3, shift_prev, p):
    b, s, _ = pb3.shape
    tb = min(s, 256)
    vec = lambda w: pl.BlockSpec((1, w), lambda bi, t: (0, 0))
    mat = lambda r, c: pl.BlockSpec((r, c), lambda bi, t: (0, 0))
    out_spec = pl.BlockSpec((1, tb, B_WIDTH), lambda bi, t: (bi, t, 0))
    return pl.pallas_call(
        _rwkv_prep_kernel,
        grid=(b, s // tb),
        in_specs=[
            pl.BlockSpec((1, tb, SHIFT_W), lambda bi, t: (bi, t, CB_U // SHIFT_W)),
            pl.BlockSpec((1, 1, SHIFT_W), lambda bi, t: (bi, 0, 0)),
            vec(SHIFT_W), vec(B_WIDTH), vec(B_WIDTH), vec(B_WIDTH), vec(B_WIDTH), vec(B_WIDTH),
            mat(LANES, B_WIDTH), mat(LANES, B_WIDTH), mat(GATE_LORA, B_WIDTH), mat(B_WIDTH, B_WIDTH),
        ],
        out_specs=[out_spec] * 8,
        out_shape=[jax.ShapeDtypeStruct((b, s, B_WIDTH), F32)] * 8,
        scratch_shapes=[pltpu.VMEM((1, SHIFT_W), F32)],
        compiler_params=pltpu.CompilerParams(
            dimension_semantics=("arbitrary", "arbitrary"), vmem_limit_bytes=VMEM_LIMIT),
        name="rwkv_prep",
    )(pb3, shift_prev, p['shift_mu'], p['w0'], p['a0'], p['k_k'], p['k_a'], p['r_k'],
      p['w2p'], p['a2p'], p['g2'], p['bd'])


def _rwkv_chunk_kernel(r_ref, lw_ref, k_ref, v_ref, kk_ref, a_ref, s0_ref, tri_ref, y_ref, s_out_ref, state):
    c = pl.program_id(1)
    tc = r_ref.shape[1]
    n = B_HEAD_DIM

    @pl.when(c == 0)
    def _():
        state[...] = s0_ref[0]

    lw = lw_ref[0]
    cum = _dot_exact_lhs(tri_ref[...], lw)
    g_in = jnp.exp(cum)
    g_ex = jnp.exp(cum - lw)
    g_inv = jnp.exp(-cum)
    g_end = g_in[tc - 1:tc, :]
    kk = kk_ref[0]
    a_t = -kk * g_ex
    b_t = kk * a_ref[0] * g_inv
    k_t = k_ref[0] * g_inv
    r_t = r_ref[0] * g_in
    v = v_ref[0]
    b_e = b_t * g_end
    k_e = k_t * g_end

    ri = lax.broadcasted_iota(jnp.int32, (tc, tc), 0)
    ci = lax.broadcasted_iota(jnp.int32, (tc, tc), 1)
    strict = ci < ri
    incl = ci <= ri
    eye = jnp.where(ci == ri, 1.0, 0.0)

    heads = range(B_HEADS)
    sls = [slice(h * n, (h + 1) * n) for h in heads]
    s_old = [state[h] for h in heads]
    vh = [v[:, sl] for sl in sls]
    bf = lambda x: x.astype(BF16)
    ar = [_split2(jnp.concatenate([a_t[:, sl], r_t[:, sl]], axis=0)) for sl in sls]
    bk = [bf(jnp.concatenate([b_t[:, sl], k_t[:, sl]], axis=0)) for sl in sls]
    cross = [_dot(ar[h][0], bk[h], trans_b=True) for h in heads]
    from_state = [_dot3s(ar[h], _split2(s_old[h]), trans_b=True) for h in heads]
    l_ab = [jnp.where(strict, cross[h][0:tc, 0:tc], 0.0) for h in heads]
    l_ak = [jnp.where(strict, cross[h][0:tc, tc:2 * tc], 0.0) for h in heads]
    m_rbk = [jnp.concatenate([jnp.where(incl, cross[h][tc:2 * tc, 0:tc], 0.0),
                              jnp.where(incl, cross[h][tc:2 * tc, tc:2 * tc], 0.0)], axis=1) for h in heads]
    rhs = [from_state[h][0:tc] + _dot(bf(l_ak[h]), bf(vh[h])) for h in heads]

    inv = [eye + l_ab[h] for h in heads]
    pw = l_ab
    span = 2
    while span < tc:
        pwb = [bf(pw[h]) for h in heads]
        pw = [_dot(pwb[h], pwb[h]) for h in heads]
        inv = [inv[h] + _dot(bf(inv[h]), bf(pw[h])) for h in heads]
        span *= 2

    z = [_dot(bf(inv[h]), bf(rhs[h])) for h in heads]
    zv = [jnp.concatenate([z[h], vh[h]], axis=0) for h in heads]
    y = [from_state[h][tc:2 * tc] + _dot(bf(m_rbk[h]), bf(zv[h])) for h in heads]
    upd = [_dot3(zv[h].T, jnp.concatenate([b_e[:, sls[h]], k_e[:, sls[h]]], axis=0)) for h in heads]
    for h in heads:
        y_ref[0, :, sls[h]] = y[h]
        state[h] = s_old[h] * g_end[:, sls[h]] + upd[h]

    @pl.when(c == pl.num_programs(1) - 1)
    def _():
        s_out_ref[0] = state[...]


def _rwkv_chunk(r, lw, k2, v, kk, a, s0, tc):
    b, s, _ = r.shape
    tri = jnp.asarray(np.tril(np.ones((tc, tc), np.float32)), BF16)
    blk = pl.BlockSpec((1, tc, B_WIDTH), lambda bi, c: (bi, c, 0))
    st = pl.BlockSpec((1, B_HEADS, B_HEAD_DIM, B_HEAD_DIM), lambda bi, c: (bi, 0, 0, 0))
    return pl.pallas_call(
        _rwkv_chunk_kernel,
        grid=(b, s // tc),
        in_specs=[blk] * 6 + [st, pl.BlockSpec((tc, tc), lambda bi, c: (0, 0))],
        out_specs=[blk, st],
        out_shape=[jax.ShapeDtypeStruct((b, s, B_WIDTH), F32),
                   jax.ShapeDtypeStruct((b, B_HEADS, B_HEAD_DIM, B_HEAD_DIM), F32)],
        scratch_shapes=[pltpu.VMEM((B_HEADS, B_HEAD_DIM, B_HEAD_DIM), F32)],
        compiler_params=pltpu.CompilerParams(
            dimension_semantics=("arbitrary", "arbitrary"), vmem_limit_bytes=VMEM_LIMIT),
        name="rwkv_chunk",
    )(r, lw, k2, v, kk, a, s0, tri)


def _merge_kernel(oa_ref, y_ref, bonus_ref, g_ref, ga_ref, gb_ref, x_ref, gnw_ref, gnb_ref, bd_ref,
                  woa_ref, wob_ref, wout_ref, h_ref):
    y = y_ref[...]
    mean = _dot_exact_rhs(y, bd_ref[...]) * (1.0 / B_HEAD_DIM)
    d = y - mean
    var = _dot_exact_rhs(d * d, bd_ref[...]) * (1.0 / B_HEAD_DIM)
    yn = d * lax.rsqrt(var + GN_EPS) * gnw_ref[...] + gnb_ref[...]
    yb = ((yn + bonus_ref[...]) * g_ref[...]).astype(BF16)
    merged = (_sigmoid(ga_ref[...].astype(F32)) * _dot(oa_ref[...], woa_ref[...])
              + _sigmoid(gb_ref[...].astype(F32)) * _dot(yb, wob_ref[...]))
    h_ref[...] = x_ref[...] + _dot(merged.astype(BF16), wout_ref[...])


def _merge(o_a, y, bonus, g, pa, x2, p):
    n = x2.shape[0]
    tm = min(n, 512)
    row = lambda w, j=0: pl.BlockSpec((tm, w), lambda i: (i, j))
    full = lambda r, c: pl.BlockSpec((r, c), lambda i: (0, 0))
    return pl.pallas_call(
        _merge_kernel,
        grid=(n // tm,),
        in_specs=[
            row(A_Q), row(B_WIDTH), row(B_WIDTH), row(B_WIDTH),
            row(D_MODEL, CA_GA // D_MODEL), row(D_MODEL, CA_GB // D_MODEL), row(D_MODEL),
            full(1, B_WIDTH), full(1, B_WIDTH), full(B_WIDTH, B_WIDTH),
            full(A_Q, D_MODEL), full(B_WIDTH, D_MODEL), full(D_MODEL, D_MODEL),
        ],
        out_specs=row(D_MODEL),
        out_shape=jax.ShapeDtypeStruct((n, D_MODEL), F32),
        compiler_params=pltpu.CompilerParams(
            dimension_semantics=("arbitrary",), vmem_limit_bytes=VMEM_LIMIT),
        name="merge_out_proj",
    )(o_a, y, bonus, g, pa, pa, x2, p['gn_w'], p['gn_b'], p['bd'], p['w_oa'], p['w_ob'], p['w_out'])


def _ffn_kernel(h_ref, n2_ref, nf_ref, wg_ref, wu_ref, wd_ref, o_ref):
    h = h_ref[...]
    hn = _rms(h, n2_ref[...]).astype(BF16)
    out = h
    for c, n in _col_chunks(D_FF, 4 * MXU_COLS):
        gate = _dot(hn, wg_ref[:, c:c + n])
        up = _dot(hn, wu_ref[:, c:c + n])
        act = (gate * _sigmoid(gate) * up).astype(BF16)
        out = out + _dot(act, wd_ref[c:c + n, :])
    o_ref[...] = _rms(out, nf_ref[...])


def _ffn(h, p, norm_f):
    n = h.shape[0]
    tm = min(n, 512)
    resident = lambda r, c: pl.BlockSpec((r, c), lambda i: (0, 0), pipeline_mode=pl.Buffered(1))
    return pl.pallas_call(
        _ffn_kernel,
        grid=(n // tm,),
        in_specs=[
            pl.BlockSpec((tm, D_MODEL), lambda i: (i, 0)),
            pl.BlockSpec((1, D_MODEL), lambda i: (0, 0)),
            pl.BlockSpec((1, D_MODEL), lambda i: (0, 0)),
            resident(D_MODEL, D_FF), resident(D_MODEL, D_FF), resident(D_FF, D_MODEL),
        ],
        out_specs=pl.BlockSpec((tm, D_MODEL), lambda i: (i, 0)),
        out_shape=jax.ShapeDtypeStruct((n, D_MODEL), F32),
        compiler_params=pltpu.CompilerParams(
            dimension_semantics=("arbitrary",), vmem_limit_bytes=VMEM_LIMIT),
        name="ffn_final_norm",
    )(h, p['norm2'], norm_f, p['w_gate'], p['w_up'], p['w_down'])


def _prepare_params(l, norm1, w_in, idx_k_g, idx_k_b, shift_mu, w0, w2, a0, a2, g2, k_k, k_a, r_k,
                    gn_w, gn_b, w_oa, w_ob, w_out, norm2, w_gate, w_up, w_down):
    w = w_in[l]
    o = np.cumsum([0, A_Q, A_KV, A_KV, IDX_Q, IDX_DIM, IDX_HEADS, SHIFT_W, D_MODEL, D_MODEL])
    seg = lambda i: w[:, o[i]:o[i + 1]]
    pad = jnp.zeros((D_MODEL, LANES - IDX_DIM - IDX_HEADS), w.dtype)
    w_in_p = jnp.concatenate([seg(3), seg(7), seg(8), seg(0), seg(6), seg(1), seg(2), seg(4), seg(5), pad],
                             axis=1).astype(BF16)
    zeros = jnp.zeros((LANES - DECAY_LORA, B_WIDTH), F32)
    head = np.arange(B_WIDTH) // B_HEAD_DIM
    row = lambda x: x[l].reshape(1, -1)
    return {
        'norm1': row(norm1), 'w_in_p': w_in_p,
        'idx_k_g': row(idx_k_g), 'idx_k_b': row(idx_k_b),
        'shift_mu': row(shift_mu), 'w0': row(w0), 'a0': row(a0),
        'k_k': row(k_k), 'k_a': row(k_a), 'r_k': row(r_k), 'gn_w': row(gn_w), 'gn_b': row(gn_b),
        'w2p': jnp.concatenate([w2[l], zeros], axis=0).astype(BF16),
        'a2p': jnp.concatenate([zeros, a2[l]], axis=0).astype(BF16),
        'g2': g2[l].astype(BF16),
        'bd': jnp.asarray(head[:, None] == head[None, :], BF16),
        'w_oa': w_oa[l].astype(BF16), 'w_ob': w_ob[l].astype(BF16), 'w_out': w_out[l].astype(BF16),
        'norm2': row(norm2),
        'w_gate': w_gate[l].astype(BF16), 'w_up': w_up[l].astype(BF16), 'w_down': w_down[l].astype(BF16),
    }


def _layer(x, attend, shift_prev, wkv0, p, norm_f, chunk):
    b, t, _ = x.shape
    x2 = x.reshape(b * t, D_MODEL)
    pa, pb = _in_proj(x2, p['norm1'], p['w_in_p'])
    pa3 = pa.reshape(b, t, PA_WIDTH)
    pb3 = pb.reshape(b, t, PB_WIDTH)
    o_a, ik_ln = attend(pa3, pb3)
    r, lw, k2, v, kk, a, g, bonus = _rwkv_prep(pb3, shift_prev, p)
    y, wkv_new = _rwkv_chunk(r, lw, k2, v, kk, a, wkv0, chunk)
    flat = lambda z: z.reshape(b * t, z.shape[-1])
    h = _merge(flat(o_a), flat(y), flat(bonus), flat(g), pa, x2, p)
    out = _ffn(h, p, norm_f).reshape(b, t, D_MODEL)
    k_new = pb3[:, :, CB_K:CB_K + A_KV].reshape(b, t, A_KV_HEADS, A_HEAD_DIM)
    v_new = pb3[:, :, CB_V:CB_V + A_KV].reshape(b, t, A_KV_HEADS, A_HEAD_DIM)
    shift_new = pb3[:, t - 1:t, CB_U:CB_U + SHIFT_W]
    return out, k_new, v_new, ik_ln, wkv_new, shift_new


def kernel(x_prompt, x_sample, cache_k, cache_v, cache_idx_k, state_wkv, state_shift,
           norm1, w_in, idx_k_g, idx_k_b, shift_mu, w0, w2, a0, a2, g2, k_k, k_a, r_k,
           gn_w, gn_b, w_oa, w_ob, w_out, norm2, w_gate, w_up, w_down, norm_f):
    assert w_in.shape[0] == 1, "single-layer kernel"
    l = 0
    p = _prepare_params(l, norm1, w_in, idx_k_g, idx_k_b, shift_mu, w0, w2, a0, a2, g2, k_k, k_a, r_k,
                        gn_w, gn_b, w_oa, w_ob, w_out, norm2, w_gate, w_up, w_down)
    nf = norm_f.reshape(1, -1)
    tril = jnp.asarray(np.tril(np.ones((KEY_TILE, KEY_TILE), np.float32)), BF16)
    eye = jnp.asarray(np.eye(LANES, dtype=np.float32), BF16)

    n_p = x_prompt.shape[0]
    shift0 = jnp.zeros((n_p, 1, SHIFT_W), F32)
    wkv_zero = jnp.zeros((n_p, B_HEADS, B_HEAD_DIM, B_HEAD_DIM), F32)
    attend_p = lambda pa3, pb3: _attn_prompt(pa3, pb3, p['idx_k_g'], p['idx_k_b'], tril, eye)
    y_p, k_p, v_p, ik_p, wkv_p, shift_p = _layer(x_prompt, attend_p, shift0, wkv_zero, p, nf, 64)

    n_s, t_s = x_sample.shape[0], x_sample.shape[1]
    past = cache_k.shape[2]
    ck = cache_k[l].reshape(n_s, past, A_KV).astype(BF16)
    cv = cache_v[l].reshape(n_s, past, A_KV).astype(BF16)
    cik = cache_idx_k[l].astype(BF16)
    attend_s = lambda pa3, pb3: _attn_sample(pa3, pb3, ck, cv, cik, p['idx_k_g'], p['idx_k_b'], tril, eye)
    y_s, k_s, v_s, ik_s, wkv_s, shift_s = _layer(x_sample, attend_s, state_shift[l], state_wkv[l], p, nf, t_s)

    lead = lambda z: z[None]
    return (y_p, y_s, lead(k_p), lead(v_p), lead(ik_p), lead(wkv_p), lead(shift_p),
            lead(k_s), lead(v_s), lead(ik_s), lead(wkv_s), lead(shift_s))
```

```python
import functools

import jax
import jax.numpy as jnp
import numpy as np
from jax import lax
from jax.experimental import pallas as pl
from jax.experimental.pallas import tpu as pltpu

F32 = jnp.float32
BF16 = jnp.bfloat16

D_MODEL = 1024
CHUNK = 64
A_HEADS = 8
A_KV_HEADS = 2
A_GROUP = A_HEADS // A_KV_HEADS
A_HEAD_DIM = 64
A_Q = A_HEADS * A_HEAD_DIM
A_KV = A_KV_HEADS * A_HEAD_DIM
IDX_HEADS = 16
IDX_DIM = 64
IDX_Q = IDX_HEADS * IDX_DIM
TOPK_MAX = 256
B_HEADS = 8
B_HEAD_DIM = 64
B_WIDTH = B_HEADS * B_HEAD_DIM
DECAY_LORA = 64
AAA_LORA = 64
GATE_LORA = 128
SHIFT_W = 3 * B_WIDTH + DECAY_LORA + AAA_LORA + GATE_LORA
D_FF = 2816
RMS_EPS = 1e-6
LN_EPS = 1e-6
GN_EPS = 64e-5

LANES = 128
SUB = 8
VMEM_LIMIT = 48 * 1024 * 1024
NEG = -1e30
KEY_TILE = 256

CA_IQ = 0
CA_GA = CA_IQ + IDX_Q
CA_GB = CA_GA + D_MODEL
CA_Q = CA_GB + D_MODEL
PA_WIDTH = CA_Q + A_Q
CB_U = 0
CB_K = CB_U + SHIFT_W
CB_V = CB_K + A_KV
CB_IKW = CB_V + A_KV
PB_WIDTH = CB_IKW + LANES
MXU_COLS = 256


def _dot(a, b, trans_b=False):
    dn = (((1,), (1 if trans_b else 0,)), ((), ()))
    return lax.dot_general(a, b, dn, preferred_element_type=F32)


def _split2(x):
    hi = x.astype(BF16)
    lo = (x - hi.astype(F32)).astype(BF16)
    return hi, lo


def _dot3s(a_split, b_split, trans_b=False):
    ah, al = a_split
    bh, bl = b_split
    return _dot(ah, bh, trans_b) + (_dot(ah, bl, trans_b) + _dot(al, bh, trans_b))


def _dot3(a, b, trans_b=False):
    return _dot3s(_split2(a), _split2(b), trans_b)


def _dot_exact_rhs(a, b_bf16, terms=2):
    out = None
    rem = a
    for _ in range(terms):
        part = rem.astype(BF16)
        rem = rem - part.astype(F32)
        d = _dot(part, b_bf16)
        out = d if out is None else out + d
    return out


def _dot_exact_lhs(a_bf16, b, terms=3, trans_b=False):
    out = None
    rem = b
    for _ in range(terms):
        part = rem.astype(BF16)
        rem = rem - part.astype(F32)
        d = _dot(a_bf16, part, trans_b)
        out = d if out is None else out + d
    return out


def _sigmoid(x):
    return 1.0 / (1.0 + jnp.exp(-x))


def _rms(x, g):
    ms = jnp.mean(x * x, axis=-1, keepdims=True)
    return x * lax.rsqrt(ms + RMS_EPS) * g


def _col_chunks(width, chunk):
    return [(c, min(chunk, width - c)) for c in range(0, width, chunk)]


def _in_proj_kernel(x_ref, g_ref, w_ref, pa_ref, pb_ref):
    xn = _rms(x_ref[...], g_ref[...]).astype(BF16)
    for c, n in _col_chunks(PA_WIDTH, 2 * MXU_COLS):
        pa_ref[:, c:c + n] = _dot(xn, w_ref[:, c:c + n]).astype(BF16)
    for c, n in _col_chunks(PB_WIDTH, 2 * MXU_COLS):
        pb_ref[:, c:c + n] = _dot(xn, w_ref[:, PA_WIDTH + c:PA_WIDTH + c + n])


def _in_proj(x2, norm1, w_in_p):
    n = x2.shape[0]
    tm = min(n, 512)
    return pl.pallas_call(
        _in_proj_kernel,
        grid=(n // tm,),
        in_specs=[
            pl.BlockSpec((tm, D_MODEL), lambda i: (i, 0)),
            pl.BlockSpec((1, D_MODEL), lambda i: (0, 0)),
            pl.BlockSpec((D_MODEL, PA_WIDTH + PB_WIDTH), lambda i: (0, 0), pipeline_mode=pl.Buffered(1)),
        ],
        out_specs=[pl.BlockSpec((tm, PA_WIDTH), lambda i: (i, 0)), pl.BlockSpec((tm, PB_WIDTH), lambda i: (i, 0))],
        out_shape=[jax.ShapeDtypeStruct((n, PA_WIDTH), BF16), jax.ShapeDtypeStruct((n, PB_WIDTH), F32)],
        compiler_params=pltpu.CompilerParams(
            dimension_semantics=("arbitrary",), vmem_limit_bytes=VMEM_LIMIT),
        name="in_proj",
    )(x2, norm1, w_in_p)


def _layer_norm(x, g, b):
    mu = jnp.mean(x, axis=-1, keepdims=True)
    d = x - mu
    var = jnp.mean(d * d, axis=-1, keepdims=True)
    return d * lax.rsqrt(var + LN_EPS) * g + b


def _transpose_bf16(x, eye):
    return _dot(eye, x, trans_b=True)


def _key_loop(nt, body, init):
    if isinstance(nt, int):
        carry = init
        for j in range(nt):
            carry = body(j, carry)
        return carry
    return lax.fori_loop(0, nt, body, init)


def _key_off(j):
    return j * KEY_TILE if isinstance(j, int) else pl.multiple_of(j * KEY_TILE, KEY_TILE)


def _fold(x, op):
    x = x.reshape(x.shape[0] // SUB, SUB, x.shape[1])
    while x.shape[0] > 1:
        half = x.shape[0] // 2
        x = op(x[:half], x[half:])
    return x[0]


def _stage_queries(iq, q, iw, eye_ref, iq_t, w_full, q_t, rows):
    eye = eye_ref[0:IDX_DIM, 0:IDX_DIM]
    for h in range(IDX_HEADS):
        x = (iq[:, h * IDX_DIM:(h + 1) * IDX_DIM] * (IDX_DIM ** -0.5)).astype(BF16)
        iq_t[:, h * rows:(h + 1) * rows] = _transpose_bf16(x, eye).astype(BF16)
    w_t = _dot_exact_lhs(eye_ref[0:IDX_HEADS, 0:IDX_HEADS], iw, trans_b=True)
    for h in range(IDX_HEADS):
        w_full[:, h * rows:(h + 1) * rows] = w_t[h:h + 1, :] * (IDX_HEADS ** -0.5)
    for g in range(A_KV_HEADS):
        for hh in range(A_GROUP):
            h = g * A_GROUP + hh
            x = (q[:, h * A_HEAD_DIM:(h + 1) * A_HEAD_DIM] * (A_HEAD_DIM ** -0.5)).astype(BF16)
            q_t[g, :, hh * rows:(hh + 1) * rows] = _transpose_bf16(x, eye).astype(BF16)


PACK_ORDER = (0, 2, 1, 3)


def _pack_quarters(acc):
    n = acc.shape[0] // 4
    a, b, c, d = (acc[i * n:(i + 1) * n] for i in range(4))
    lane = lax.broadcasted_iota(jnp.int32, (n, LANES), 1)
    low64 = lane < LANES // 2
    low32 = (lane & (LANES // 4)) == 0
    ab = jnp.where(low64, a, b) + pltpu.roll(jnp.where(low64, b, a), LANES // 2, 1)
    cd = jnp.where(low64, c, d) + pltpu.roll(jnp.where(low64, d, c), LANES // 2, 1)
    t1 = ab + pltpu.roll(ab, 3 * LANES // 4, 1)
    t2 = cd + pltpu.roll(cd, LANES // 4, 1)
    return jnp.where(low32, t1, t2)


def _index_scores(iq_t, w_full, ikb, sc, sc_pk, nt, adm_fn, adm_pk_fn, rows):
    width = sc.shape[1]

    def body(j, carry):
        off = _key_off(j)
        lg = _dot(ikb[pl.ds(off, KEY_TILE), :], iq_t[...])
        acc = None
        for c in range(IDX_HEADS * rows // width):
            x = jnp.maximum(lg[:, c * width:(c + 1) * width], 0.0) * w_full[:, c * width:(c + 1) * width]
            acc = x if acc is None else acc + x
        if sc_pk is not None:
            assert LANES // rows == 4
            pk = KEY_TILE // 4
            sc_pk[pl.ds(j * pk, pk), :] = jnp.where(adm_pk_fn(j), _pack_quarters(acc), -jnp.inf)
        shift = LANES // 2
        while shift >= rows:
            acc = acc + pltpu.roll(acc, shift, 1)
            shift //= 2
        sc[pl.ds(off, KEY_TILE), :] = jnp.where(adm_fn(j), acc, -jnp.inf)
        return carry

    _key_loop(nt, body, 0)


def _select_topk(sc, sc_pk, tril_ref, nt):
    shape = (SUB, sc.shape[1])
    inf = jnp.float32(jnp.inf)
    src = sc if sc_pk is None else sc_pk
    rows_t = KEY_TILE if sc_pk is None else KEY_TILE // 4

    def tile(j):
        start = _key_off(j) if sc_pk is None else j * rows_t
        return src[pl.ds(start, rows_t), :]

    def allsub(x, red):
        op = {jnp.sum: jnp.add, jnp.min: jnp.minimum, jnp.max: jnp.maximum}[red]
        if sc_pk is not None:
            x = op(op(x, pltpu.roll(x, LANES // 4, 1)),
                   op(pltpu.roll(x, LANES // 2, 1), pltpu.roll(x, 3 * LANES // 4, 1)))
        return jnp.broadcast_to(red(x, axis=0, keepdims=True), shape)

    def count_ge(t):
        t1 = t[0:1, :]
        acc = _key_loop(
            nt, lambda j, a: a + _fold(jnp.where(tile(j) >= t1, 1.0, 0.0), jnp.add), jnp.zeros(shape, F32))
        return allsub(acc, jnp.sum)

    def stats(j, c):
        mn, mx, na = c
        s = tile(j)
        fin = s > -inf
        return (jnp.minimum(mn, _fold(jnp.where(fin, s, inf), jnp.minimum)), jnp.maximum(mx, _fold(s, jnp.maximum)),
                na + _fold(jnp.where(fin, 1.0, 0.0), jnp.add))

    mn, mx, na = _key_loop(
        nt, stats, (jnp.full(shape, inf, F32), jnp.full(shape, -inf, F32), jnp.zeros(shape, F32)))
    lo0 = allsub(mn, jnp.min)
    mx = allsub(mx, jnp.max)
    n_adm = allsub(na, jnp.sum)
    kq = jnp.minimum(n_adm, float(TOPK_MAX))
    hi0 = mx + (jnp.abs(mx) * 1e-6 + 1e-30)

    def cond(c):
        return jnp.logical_and(c[0] < 400, c[5] > 0.5)

    def probe(x, lo, hi, c_lo, c_hi):
        c_x = count_ge(x)
        ge = c_x >= kq
        return jnp.where(ge, x, lo), jnp.where(ge, hi, x), jnp.where(ge, c_x, c_lo), jnp.where(ge, c_hi, c_x)

    def body(c):
        it, lo, hi, c_lo, c_hi, _ = c
        for _ in range(2):
            lo, hi, c_lo, c_hi = probe(0.5 * (lo + hi), lo, hi, c_lo, c_hi)
        nxt = 0.5 * (lo + hi)
        active = jnp.where(c_lo - c_hi > 1.5, jnp.where(nxt > lo, jnp.where(nxt < hi, 1.0, 0.0), 0.0), 0.0)
        active = jnp.where(n_adm > kq, active, 0.0)
        return it + 1, lo, hi, c_lo, c_hi, jnp.max(active)

    _, lo, _, c_lo, c_hi, _ = lax.while_loop(
        cond, body, (jnp.int32(0), lo0, hi0, n_adm, jnp.zeros(shape, F32), jnp.float32(1.0)))

    lo1 = lo[0:1, :]
    thr = allsub(
        _key_loop(nt, lambda j, a: jnp.minimum(a, _fold(jnp.where(tile(j) >= lo1, tile(j), inf), jnp.minimum)),
                  jnp.full(shape, inf, F32)),
        jnp.min)[0:1, :]
    take = (kq - c_hi)[0:1, :]
    has_tie = jnp.max(c_lo - kq) > 0.5

    @pl.when(jnp.logical_not(has_tie))
    def _():
        def wr(j, carry):
            off = _key_off(j)
            sc[pl.ds(off, KEY_TILE), :] = jnp.where(sc[pl.ds(off, KEY_TILE), :] >= thr, 0.0, NEG)
            return carry
        _key_loop(nt, wr, 0)

    @pl.when(has_tie)
    def _():
        def wr(j, seen):
            off = _key_off(j)
            s = sc[pl.ds(off, KEY_TILE), :]
            tie = jnp.where(s == thr, 1.0, 0.0)
            rank = _dot(tril_ref[...], tie.astype(BF16)) + seen
            keep_tie = jnp.where(s == thr, jnp.where(rank <= take, 0.0, NEG), NEG)
            sc[pl.ds(off, KEY_TILE), :] = jnp.where(s > thr, 0.0, keep_tie)
            return seen + jnp.sum(tie, axis=0, keepdims=True)
        lax.fori_loop(0, nt, wr, jnp.zeros((1, sc.shape[1]), F32))


def _attend(sc, q_t, kb, v_t, s_sc, acc_sc, nt):
    lanes_g = q_t.shape[2]
    rep = lanes_g // sc.shape[1]
    groups = range(A_KV_HEADS)

    def scores(j, macc):
        off = _key_off(j)
        bias = sc[pl.ds(off, KEY_TILE), :]
        bias = bias if rep == 1 else jnp.concatenate([bias] * rep, axis=1)
        out = []
        for g in groups:
            s = _dot(kb[g, pl.ds(off, KEY_TILE), :], q_t[g]) + bias
            s_sc[g, pl.ds(off, KEY_TILE), :] = s
            out.append(jnp.maximum(macc[g], _fold(s, jnp.maximum)))
        return tuple(out)

    macc = _key_loop(nt, scores, tuple(jnp.full((SUB, lanes_g), NEG, F32) for _ in groups))
    m = [jnp.max(macc[g], axis=0, keepdims=True) for g in groups]
    acc_sc[...] = jnp.zeros(acc_sc.shape, F32)

    def weighted(j, lacc):
        off = _key_off(j)
        out = []
        for g in groups:
            p = jnp.exp(s_sc[g, pl.ds(off, KEY_TILE), :] - m[g])
            acc_sc[g] += _dot(v_t[g, :, pl.ds(off, KEY_TILE)], p.astype(BF16))
            out.append(lacc[g] + _fold(p, jnp.add))
        return tuple(out)

    lacc = _key_loop(nt, weighted, tuple(jnp.zeros((SUB, lanes_g), F32) for _ in groups))
    return [jnp.sum(lacc[g], axis=0, keepdims=True) for g in groups]


def _write_heads(o_ref, l, acc_sc, eye_ref, rows):
    eye = eye_ref[0:rows, 0:rows]
    for g in range(A_KV_HEADS):
        o_t = (acc_sc[g] * (1.0 / l[g])).astype(BF16)
        for hh in range(A_GROUP):
            h = g * A_GROUP + hh
            o_ref[0, :, h * A_HEAD_DIM:(h + 1) * A_HEAD_DIM] = _transpose_bf16(
                o_t[:, hh * rows:(hh + 1) * rows], eye).astype(o_ref.dtype)


_ATTN_SCRATCH = lambda rows, keys: [
    pltpu.VMEM((A_KV_HEADS, keys, A_HEAD_DIM), BF16),
    pltpu.VMEM((A_KV_HEADS, A_HEAD_DIM, keys), BF16),
    pltpu.VMEM((keys, IDX_DIM), BF16),
    pltpu.VMEM((IDX_DIM, IDX_HEADS * rows), BF16),
    pltpu.VMEM((1, IDX_HEADS * rows), F32),
    pltpu.VMEM((A_KV_HEADS, A_HEAD_DIM, A_GROUP * rows), BF16),
    pltpu.VMEM((keys, max(rows, LANES)), F32),
    pltpu.VMEM((A_KV_HEADS, keys, A_GROUP * rows), F32),
    pltpu.VMEM((A_KV_HEADS, A_HEAD_DIM, A_GROUP * rows), F32),
] + ([pltpu.VMEM((keys // (LANES // rows), LANES), F32)] if rows < LANES else [])


def _attn_prompt_kernel(iq_ref, q_ref, k_ref, v_ref, ikw_ref, ikwq_ref, lng_ref, lnb_ref, tril_ref, eye_ref,
                        o_ref, ikln_ref, kb, v_t, ikb, iq_t, w_full, q_t, sc, s_sc, acc_sc, *, rows):
    i = pl.program_id(1)

    @pl.when(i == 0)
    def _():
        kf = k_ref[0]
        vf = v_ref[0]
        eye = eye_ref[0:A_HEAD_DIM, 0:A_HEAD_DIM]
        for g in range(A_KV_HEADS):
            sl = slice(g * A_HEAD_DIM, (g + 1) * A_HEAD_DIM)
            kb[g] = kf[:, sl].astype(BF16)
            v_t[g] = _transpose_bf16(vf[:, sl].astype(BF16), eye).astype(BF16)
        ln = _layer_norm(ikw_ref[0][:, :IDX_DIM], lng_ref[...], lnb_ref[...])
        ikln_ref[0] = ln
        ikb[...] = ln.astype(BF16)

    _stage_queries(iq_ref[0], q_ref[0], ikwq_ref[0][:, IDX_DIM:IDX_DIM + IDX_HEADS], eye_ref, iq_t, w_full, q_t, rows)

    nt = ((i + 1) * rows + KEY_TILE - 1) // KEY_TILE
    q_pos = i * rows + lax.broadcasted_iota(jnp.int32, (KEY_TILE, rows), 1)
    q_end = (q_pos // CHUNK + 1) * CHUNK

    def adm(j):
        return j * KEY_TILE + lax.broadcasted_iota(jnp.int32, (KEY_TILE, rows), 0) < q_end

    _index_scores(iq_t, w_full, ikb, sc, None, nt, adm, None, rows)
    _select_topk(sc, None, tril_ref, nt)
    l = _attend(sc, q_t, kb, v_t, s_sc, acc_sc, nt)
    _write_heads(o_ref, l, acc_sc, eye_ref, rows)


def _attn_prompt(pa3, pb3, idx_k_g, idx_k_b, tril, eye):
    b, s, _ = pa3.shape
    rows = 2 * LANES
    kernel = functools.partial(_attn_prompt_kernel, rows=rows)
    return pl.pallas_call(
        kernel,
        grid=(b, s // rows),
        in_specs=[
            pl.BlockSpec((1, rows, IDX_Q), lambda bi, i: (bi, i, CA_IQ // IDX_Q)),
            pl.BlockSpec((1, rows, A_Q), lambda bi, i: (bi, i, CA_Q // A_Q)),
            pl.BlockSpec((1, s, A_KV), lambda bi, i: (bi, 0, CB_K // A_KV)),
            pl.BlockSpec((1, s, A_KV), lambda bi, i: (bi, 0, CB_V // A_KV)),
            pl.BlockSpec((1, s, LANES), lambda bi, i: (bi, 0, CB_IKW // LANES)),
            pl.BlockSpec((1, rows, LANES), lambda bi, i: (bi, i, CB_IKW // LANES)),
            pl.BlockSpec((1, IDX_DIM), lambda bi, i: (0, 0)),
            pl.BlockSpec((1, IDX_DIM), lambda bi, i: (0, 0)),
            pl.BlockSpec((KEY_TILE, KEY_TILE), lambda bi, i: (0, 0)),
            pl.BlockSpec((KEY_TILE, KEY_TILE), lambda bi, i: (0, 0)),
        ],
        out_specs=[
            pl.BlockSpec((1, rows, A_Q), lambda bi, i: (bi, i, 0)),
            pl.BlockSpec((1, s, IDX_DIM), lambda bi, i: (bi, 0, 0)),
        ],
        out_shape=[
            jax.ShapeDtypeStruct((b, s, A_Q), BF16),
            jax.ShapeDtypeStruct((b, s, IDX_DIM), F32),
        ],
        scratch_shapes=_ATTN_SCRATCH(rows, s),
        compiler_params=pltpu.CompilerParams(
            dimension_semantics=("arbitrary", "arbitrary"), vmem_limit_bytes=VMEM_LIMIT),
        name="attn_prompt",
    )(pa3, pa3, pb3, pb3, pb3, pb3, idx_k_g, idx_k_b, tril, eye)


def _attn_sample_kernel(iq_ref, q_ref, k_ref, v_ref, ikw_ref, ck_ref, cv_ref, cik_ref, lng_ref, lnb_ref, tril_ref,
                        eye_ref, o_ref, ikln_ref, kb, v_t, ikb, iq_t, w_full, q_t, sc, s_sc, acc_sc, sc_pk,
                        *, rows, past, keys):
    new = rows
    pad = keys - past - new
    ck = ck_ref[0]
    cv = cv_ref[0]
    kf = k_ref[0]
    vf = v_ref[0]
    eye = eye_ref[0:A_HEAD_DIM, 0:A_HEAD_DIM]
    for g in range(A_KV_HEADS):
        sl = slice(g * A_HEAD_DIM, (g + 1) * A_HEAD_DIM)
        kb[g, 0:past, :] = ck[:, sl].astype(BF16)
        kb[g, past:past + new, :] = kf[:, sl].astype(BF16)
        kb[g, past + new:keys, :] = jnp.zeros((pad, A_HEAD_DIM), BF16)
        v_t[g, :, 0:past] = _transpose_bf16(cv[:, sl].astype(BF16), eye).astype(BF16)
        v_t[g, :, past:past + new] = _transpose_bf16(vf[:, sl].astype(BF16), eye).astype(BF16)
        v_t[g, :, past + new:keys] = jnp.zeros((A_HEAD_DIM, pad), BF16)
    ikw = ikw_ref[0]
    ln = _layer_norm(ikw[:, :IDX_DIM], lng_ref[...], lnb_ref[...])
    ikln_ref[0] = ln
    ikb[0:past, :] = cik_ref[0].astype(BF16)
    ikb[past:past + new, :] = ln.astype(BF16)
    ikb[past + new:keys, :] = jnp.zeros((pad, IDX_DIM), BF16)

    _stage_queries(iq_ref[0], q_ref[0], ikw[:, IDX_DIM:IDX_DIM + IDX_HEADS], eye_ref, iq_t, w_full, q_t, rows)

    nt = keys // KEY_TILE

    def adm(j):
        return j * KEY_TILE + lax.broadcasted_iota(jnp.int32, (KEY_TILE, LANES), 0) < past + new

    pk = KEY_TILE // 4
    group = lax.broadcasted_iota(jnp.int32, (pk, LANES), 1) // rows
    sub_block = jnp.zeros((pk, LANES), jnp.int32)
    for c, blk in enumerate(PACK_ORDER):
        sub_block = jnp.where(group == c, blk, sub_block)
    key_in_tile = sub_block * pk + lax.broadcasted_iota(jnp.int32, (pk, LANES), 0)

    def adm_pk(j):
        return j * KEY_TILE + key_in_tile < past + new

    _index_scores(iq_t, w_full, ikb, sc, sc_pk, nt, adm, adm_pk, rows)
    _select_topk(sc, sc_pk, tril_ref, nt)
    l = _attend(sc, q_t, kb, v_t, s_sc, acc_sc, nt)
    _write_heads(o_ref, l, acc_sc, eye_ref, rows)


def _attn_sample(pa3, pb3, cache_k, cache_v, cache_ik, idx_k_g, idx_k_b, tril, eye):
    b, t, _ = pa3.shape
    past = cache_k.shape[1]
    keys = -(-(past + t) // KEY_TILE) * KEY_TILE
    kernel = functools.partial(_attn_sample_kernel, rows=t, past=past, keys=keys)
    return pl.pallas_call(
        kernel,
        grid=(b,),
        in_specs=[
            pl.BlockSpec((1, t, IDX_Q), lambda bi: (bi, 0, CA_IQ // IDX_Q)),
            pl.BlockSpec((1, t, A_Q), lambda bi: (bi, 0, CA_Q // A_Q)),
            pl.BlockSpec((1, t, A_KV), lambda bi: (bi, 0, CB_K // A_KV)),
            pl.BlockSpec((1, t, A_KV), lambda bi: (bi, 0, CB_V // A_KV)),
            pl.BlockSpec((1, t, LANES), lambda bi: (bi, 0, CB_IKW // LANES)),
            pl.BlockSpec((1, past, A_KV), lambda bi: (bi, 0, 0)),
            pl.BlockSpec((1, past, A_KV), lambda bi: (bi, 0, 0)),
            pl.BlockSpec((1, past, IDX_DIM), lambda bi: (bi, 0, 0)),
            pl.BlockSpec((1, IDX_DIM), lambda bi: (0, 0)),
            pl.BlockSpec((1, IDX_DIM), lambda bi: (0, 0)),
            pl.BlockSpec((KEY_TILE, KEY_TILE), lambda bi: (0, 0)),
            pl.BlockSpec((KEY_TILE, KEY_TILE), lambda bi: (0, 0)),
        ],
        out_specs=[
            pl.BlockSpec((1, t, A_Q), lambda bi: (bi, 0, 0)),
            pl.BlockSpec((1, t, IDX_DIM), lambda bi: (bi, 0, 0)),
        ],
        out_shape=[
            jax.ShapeDtypeStruct((b, t, A_Q), BF16),
            jax.ShapeDtypeStruct((b, t, IDX_DIM), F32),
        ],
        scratch_shapes=_ATTN_SCRATCH(t, keys),
        compiler_params=pltpu.CompilerParams(
            dimension_semantics=("arbitrary",), vmem_limit_bytes=VMEM_LIMIT),
        name="attn_sample",
    )(pa3, pa3, pb3, pb3, pb3, cache_k, cache_v, cache_ik, idx_k_g, idx_k_b, tril, eye)


def _rwkv_kernel(u_ref, shift_ref, s0_ref, mu_ref, w0_ref, a0_ref, kk_ref, ka_ref, rk_ref, gnw_ref, gnb_ref,
                 w2_ref, a2_ref, g2_ref, bd_ref, tri_ref, yb_ref, s_out_ref, carry, state, y_sc, *, tc):
    t = pl.program_id(1)
    u = u_ref[0]
    tb = u.shape[0]
    n = B_HEAD_DIM

    @pl.when(t == 0)
    def _():
        carry[...] = shift_ref[0]
        state[...] = s0_ref[0]

    row = lax.broadcasted_iota(jnp.int32, u.shape, 0)
    u_prev = jnp.where(row == 0, carry[...], pltpu.roll(u, 1, 0))
    carry[...] = u[tb - 1:tb, :]
    m = u + (u_prev - u) * mu_ref[...]

    r = m[:, 0:B_WIDTH]
    k = m[:, B_WIDTH:2 * B_WIDTH]
    v = m[:, 2 * B_WIDTH:3 * B_WIDTH]
    lora = m[:, 3 * B_WIDTH:3 * B_WIDTH + LANES]
    gl = m[:, 3 * B_WIDTH + LANES:]
    lane = lax.broadcasted_iota(jnp.int32, lora.shape, 1)
    lora = jnp.where(lane < DECAY_LORA, jnp.tanh(lora), lora).astype(BF16)
    z = w0_ref[...] + _dot(lora, w2_ref[...])
    softplus = jnp.maximum(-z, 0.0) + jnp.log(1.0 + jnp.exp(-jnp.abs(z)))
    lw = -jnp.exp(-softplus - 0.5)
    a = _sigmoid(a0_ref[...] + _dot(lora, a2_ref[...]))
    g = _dot(_sigmoid(gl).astype(BF16), g2_ref[...])

    kk = k * kk_ref[...]
    ss = _dot_exact_rhs(kk * kk, bd_ref[...])
    kk = kk / jnp.maximum(jnp.sqrt(ss), 1e-12)
    k2 = k * (1.0 + (a - 1.0) * ka_ref[...])
    bonus = _dot_exact_rhs(r * k2 * rk_ref[...], bd_ref[...]) * v

    cum = _dot_exact_lhs(tri_ref[...], lw)
    g_in = jnp.exp(cum)
    g_ex = jnp.exp(cum - lw)
    g_inv = jnp.exp(-cum)
    a_t = -kk * g_ex
    b_t = kk * a * g_inv
    k_t = k2 * g_inv
    r_t = r * g_in

    ri = lax.broadcasted_iota(jnp.int32, (tc, tc), 0)
    ci = lax.broadcasted_iota(jnp.int32, (tc, tc), 1)
    strict = ci < ri
    incl = ci <= ri
    eye = jnp.where(ci == ri, 1.0, 0.0)

    heads = range(B_HEADS)
    chunks = range(tb // tc)
    pairs = [(c, h) for c in chunks for h in heads]
    lanes = [slice(h * n, (h + 1) * n) for h in heads]
    toks = [slice(c * tc, (c + 1) * tc) for c in chunks]
    bf = lambda x: x.astype(BF16)
    g_end = [g_in[(c + 1) * tc - 1:(c + 1) * tc, :] for c in chunks]
    vh = {(c, h): v[toks[c], lanes[h]] for c, h in pairs}
    ar = {(c, h): _split2(jnp.concatenate([a_t[toks[c], lanes[h]], r_t[toks[c], lanes[h]]], axis=0))
          for c, h in pairs}
    bk = {(c, h): jnp.concatenate([b_t[toks[c], lanes[h]], k_t[toks[c], lanes[h]]], axis=0) for c, h in pairs}
    cross = {p: _dot(ar[p][0], bf(bk[p]), trans_b=True) for p in pairs}
    l_ab = {p: jnp.where(strict, cross[p][0:tc, 0:tc], 0.0) for p in pairs}
    l_akv = {p: _dot(bf(jnp.where(strict, cross[p][0:tc, tc:2 * tc], 0.0)), bf(vh[p])) for p in pairs}
    m_rbk = {p: bf(jnp.concatenate([jnp.where(incl, cross[p][tc:2 * tc, 0:tc], 0.0),
                                    jnp.where(incl, cross[p][tc:2 * tc, tc:2 * tc], 0.0)], axis=1)) for p in pairs}

    inv = {p: eye + l_ab[p] for p in pairs}
    pw = l_ab
    span = 2
    while span < tc:
        pwb = {p: bf(pw[p]) for p in pairs}
        pw = {p: _dot(pwb[p], pwb[p]) for p in pairs}
        inv = {p: inv[p] + _dot(bf(inv[p]), bf(pw[p])) for p in pairs}
        span *= 2
    inv = {p: bf(inv[p]) for p in pairs}

    s_cur = [state[h] for h in heads]
    for c in chunks:
        from_state = [_dot3s(ar[c, h], _split2(s_cur[h]), trans_b=True) for h in heads]
        z = [_dot(inv[c, h], bf(from_state[h][0:tc] + l_akv[c, h])) for h in heads]
        zv = [jnp.concatenate([z[h], vh[c, h]], axis=0) for h in heads]
        y = [from_state[h][tc:2 * tc] + _dot(m_rbk[c, h], bf(zv[h])) for h in heads]
        upd = [_dot3(zv[h].T, bk[c, h] * g_end[c][:, lanes[h]]) for h in heads]
        s_cur = [s_cur[h] * g_end[c][:, lanes[h]] + upd[h] for h in heads]
        for h in heads:
            y_sc[toks[c], lanes[h]] = y[h]
    for h in heads:
        state[h] = s_cur[h]

    y = y_sc[...]
    mean = _dot_exact_rhs(y, bd_ref[...]) * (1.0 / n)
    d = y - mean
    var = _dot_exact_rhs(d * d, bd_ref[...]) * (1.0 / n)
    yn = d * lax.rsqrt(var + GN_EPS) * gnw_ref[...] + gnb_ref[...]
    yb_ref[0] = ((yn + bonus) * g).astype(yb_ref.dtype)

    @pl.when(t == pl.num_programs(1) - 1)
    def _():
        s_out_ref[0] = state[...]


def _rwkv(pb3, shift_prev, s0, p, tc):
    b, s, _ = pb3.shape
    tb = min(s, 4 * tc)
    tri = jnp.asarray(np.kron(np.eye(tb // tc), np.tril(np.ones((tc, tc)))), BF16)
    vec = lambda w: pl.BlockSpec((1, w), lambda bi, t: (0, 0))
    mat = lambda r, c: pl.BlockSpec((r, c), lambda bi, t: (0, 0))
    st = pl.BlockSpec((1, B_HEADS, B_HEAD_DIM, B_HEAD_DIM), lambda bi, t: (bi, 0, 0, 0))
    return pl.pallas_call(
        functools.partial(_rwkv_kernel, tc=tc),
        grid=(b, s // tb),
        in_specs=[
            pl.BlockSpec((1, tb, SHIFT_W), lambda bi, t: (bi, t, CB_U // SHIFT_W)),
            pl.BlockSpec((1, 1, SHIFT_W), lambda bi, t: (bi, 0, 0)),
            st,
            vec(SHIFT_W), vec(B_WIDTH), vec(B_WIDTH), vec(B_WIDTH), vec(B_WIDTH), vec(B_WIDTH),
            vec(B_WIDTH), vec(B_WIDTH),
            mat(LANES, B_WIDTH), mat(LANES, B_WIDTH), mat(GATE_LORA, B_WIDTH), mat(B_WIDTH, B_WIDTH), mat(tb, tb),
        ],
        out_specs=[pl.BlockSpec((1, tb, B_WIDTH), lambda bi, t: (bi, t, 0)), st],
        out_shape=[jax.ShapeDtypeStruct((b, s, B_WIDTH), BF16),
                   jax.ShapeDtypeStruct((b, B_HEADS, B_HEAD_DIM, B_HEAD_DIM), F32)],
        scratch_shapes=[pltpu.VMEM((1, SHIFT_W), F32),
                        pltpu.VMEM((B_HEADS, B_HEAD_DIM, B_HEAD_DIM), F32),
                        pltpu.VMEM((tb, B_WIDTH), F32)],
        compiler_params=pltpu.CompilerParams(
            dimension_semantics=("arbitrary", "arbitrary"), vmem_limit_bytes=VMEM_LIMIT),
        name="rwkv",
    )(pb3, shift_prev, s0, p['shift_mu'], p['w0'], p['a0'], p['k_k'], p['k_a'], p['r_k'], p['gn_w'], p['gn_b'],
      p['w2p'], p['a2p'], p['g2'], p['bd'], tri)


def _merge_kernel(oa_ref, yb_ref, ga_ref, gb_ref, x_ref, woa_ref, wob_ref, wout_ref, h_ref):
    merged = (_sigmoid(ga_ref[...].astype(F32)) * _dot(oa_ref[...], woa_ref[...])
              + _sigmoid(gb_ref[...].astype(F32)) * _dot(yb_ref[...], wob_ref[...]))
    h_ref[...] = x_ref[...] + _dot(merged.astype(BF16), wout_ref[...])


def _merge(o_a, y_b, pa, x2, p):
    n = x2.shape[0]
    tm = min(n, 512)
    row = lambda w, j=0: pl.BlockSpec((tm, w), lambda i: (i, j))
    full = lambda r, c: pl.BlockSpec((r, c), lambda i: (0, 0))
    return pl.pallas_call(
        _merge_kernel,
        grid=(n // tm,),
        in_specs=[
            row(A_Q), row(B_WIDTH),
            row(D_MODEL, CA_GA // D_MODEL), row(D_MODEL, CA_GB // D_MODEL), row(D_MODEL),
            full(A_Q, D_MODEL), full(B_WIDTH, D_MODEL), full(D_MODEL, D_MODEL),
        ],
        out_specs=row(D_MODEL),
        out_shape=jax.ShapeDtypeStruct((n, D_MODEL), F32),
        compiler_params=pltpu.CompilerParams(
            dimension_semantics=("arbitrary",), vmem_limit_bytes=VMEM_LIMIT),
        name="merge_out_proj",
    )(o_a, y_b, pa, pa, x2, p['w_oa'], p['w_ob'], p['w_out'])


def _ffn_kernel(h_ref, n2_ref, nf_ref, wg_ref, wu_ref, wd_ref, o_ref):
    h = h_ref[...]
    hn = _rms(h, n2_ref[...]).astype(BF16)
    out = h
    for c, n in _col_chunks(D_FF, 4 * MXU_COLS):
        gate = _dot(hn, wg_ref[:, c:c + n])
        up = _dot(hn, wu_ref[:, c:c + n])
        act = (gate * _sigmoid(gate) * up).astype(BF16)
        out = out + _dot(act, wd_ref[c:c + n, :])
    o_ref[...] = _rms(out, nf_ref[...])


def _ffn(h, p, norm_f):
    n = h.shape[0]
    tm = min(n, 512)
    resident = lambda r, c: pl.BlockSpec((r, c), lambda i: (0, 0), pipeline_mode=pl.Buffered(1))
    return pl.pallas_call(
        _ffn_kernel,
        grid=(n // tm,),
        in_specs=[
            pl.BlockSpec((tm, D_MODEL), lambda i: (i, 0)),
            pl.BlockSpec((1, D_MODEL), lambda i: (0, 0)),
            pl.BlockSpec((1, D_MODEL), lambda i: (0, 0)),
            resident(D_MODEL, D_FF), resident(D_MODEL, D_FF), resident(D_FF, D_MODEL),
        ],
        out_specs=pl.BlockSpec((tm, D_MODEL), lambda i: (i, 0)),
        out_shape=jax.ShapeDtypeStruct((n, D_MODEL), F32),
        compiler_params=pltpu.CompilerParams(
            dimension_semantics=("arbitrary",), vmem_limit_bytes=VMEM_LIMIT),
        name="ffn_final_norm",
    )(h, p['norm2'], norm_f, p['w_gate'], p['w_up'], p['w_down'])


def _prepare_params(l, norm1, w_in, idx_k_g, idx_k_b, shift_mu, w0, w2, a0, a2, g2, k_k, k_a, r_k,
                    gn_w, gn_b, w_oa, w_ob, w_out, norm2, w_gate, w_up, w_down):
    w = w_in[l]
    o = np.cumsum([0, A_Q, A_KV, A_KV, IDX_Q, IDX_DIM, IDX_HEADS, SHIFT_W, D_MODEL, D_MODEL])
    seg = lambda i: w[:, o[i]:o[i + 1]]
    pad = jnp.zeros((D_MODEL, LANES - IDX_DIM - IDX_HEADS), w.dtype)
    w_in_p = jnp.concatenate([seg(3), seg(7), seg(8), seg(0), seg(6), seg(1), seg(2), seg(4), seg(5), pad],
                             axis=1).astype(BF16)
    zeros = jnp.zeros((LANES - DECAY_LORA, B_WIDTH), F32)
    head = np.arange(B_WIDTH) // B_HEAD_DIM
    row = lambda x: x[l].reshape(1, -1)
    return {
        'norm1': row(norm1), 'w_in_p': w_in_p,
        'idx_k_g': row(idx_k_g), 'idx_k_b': row(idx_k_b),
        'shift_mu': row(shift_mu), 'w0': row(w0), 'a0': row(a0),
        'k_k': row(k_k), 'k_a': row(k_a), 'r_k': row(r_k), 'gn_w': row(gn_w), 'gn_b': row(gn_b),
        'w2p': jnp.concatenate([w2[l], zeros], axis=0).astype(BF16),
        'a2p': jnp.concatenate([zeros, a2[l]], axis=0).astype(BF16),
        'g2': g2[l].astype(BF16),
        'bd': jnp.asarray(head[:, None] == head[None, :], BF16),
        'w_oa': w_oa[l].astype(BF16), 'w_ob': w_ob[l].astype(BF16), 'w_out': w_out[l].astype(BF16),
        'norm2': row(norm2),
        'w_gate': w_gate[l].astype(BF16), 'w_up': w_up[l].astype(BF16), 'w_down': w_down[l].astype(BF16),
    }


def _layer(x, attend, shift_prev, wkv0, p, norm_f, chunk):
    b, t, _ = x.shape
    x2 = x.reshape(b * t, D_MODEL)
    pa, pb = _in_proj(x2, p['norm1'], p['w_in_p'])
    pa3 = pa.reshape(b, t, PA_WIDTH)
    pb3 = pb.reshape(b, t, PB_WIDTH)
    o_a, ik_ln = attend(pa3, pb3)
    y_b, wkv_new = _rwkv(pb3, shift_prev, wkv0, p, chunk)
    flat = lambda z: z.reshape(b * t, z.shape[-1])
    h = _merge(flat(o_a), flat(y_b), pa, x2, p)
    out = _ffn(h, p, norm_f).reshape(b, t, D_MODEL)
    k_new = pb3[:, :, CB_K:CB_K + A_KV].reshape(b, t, A_KV_HEADS, A_HEAD_DIM)
    v_new = pb3[:, :, CB_V:CB_V + A_KV].reshape(b, t, A_KV_HEADS, A_HEAD_DIM)
    shift_new = pb3[:, t - 1:t, CB_U:CB_U + SHIFT_W]
    return out, k_new, v_new, ik_ln, wkv_new, shift_new


def kernel(x_prompt, x_sample, cache_k, cache_v, cache_idx_k, state_wkv, state_shift,
           norm1, w_in, idx_k_g, idx_k_b, shift_mu, w0, w2, a0, a2, g2, k_k, k_a, r_k,
           gn_w, gn_b, w_oa, w_ob, w_out, norm2, w_gate, w_up, w_down, norm_f):
    assert w_in.shape[0] == 1, "single-layer kernel"
    l = 0
    p = _prepare_params(l, norm1, w_in, idx_k_g, idx_k_b, shift_mu, w0, w2, a0, a2, g2, k_k, k_a, r_k,
                        gn_w, gn_b, w_oa, w_ob, w_out, norm2, w_gate, w_up, w_down)
    nf = norm_f.reshape(1, -1)
    tril = jnp.asarray(np.tril(np.ones((KEY_TILE, KEY_TILE), np.float32)), BF16)
    eye = jnp.asarray(np.eye(KEY_TILE, dtype=np.float32), BF16)

    n_p = x_prompt.shape[0]
    shift0 = jnp.zeros((n_p, 1, SHIFT_W), F32)
    wkv_zero = jnp.zeros((n_p, B_HEADS, B_HEAD_DIM, B_HEAD_DIM), F32)
    attend_p = lambda pa3, pb3: _attn_prompt(pa3, pb3, p['idx_k_g'], p['idx_k_b'], tril, eye)
    y_p, k_p, v_p, ik_p, wkv_p, shift_p = _layer(x_prompt, attend_p, shift0, wkv_zero, p, nf, 64)

    n_s, t_s = x_sample.shape[0], x_sample.shape[1]
    past = cache_k.shape[2]
    ck = cache_k[l].reshape(n_s, past, A_KV)
    cv = cache_v[l].reshape(n_s, past, A_KV)
    attend_s = lambda pa3, pb3: _attn_sample(pa3, pb3, ck, cv, cache_idx_k[l], p['idx_k_g'], p['idx_k_b'], tril, eye)
    y_s, k_s, v_s, ik_s, wkv_s, shift_s = _layer(x_sample, attend_s, state_shift[l], state_wkv[l], p, nf, t_s)

    lead = lambda z: z[None]
    return (y_p, y_s, lead(k_p), lead(v_p), lead(ik_p), lead(wkv_p), lead(shift_p),
            lead(k_s), lead(v_s), lead(ik_s), lead(wkv_s), lead(shift_s))
```

```python
import functools

import jax
import jax.numpy as jnp
import numpy as np
from jax import lax
from jax.experimental import pallas as pl
from jax.experimental.pallas import tpu as pltpu

F32 = jnp.float32
BF16 = jnp.bfloat16

D_MODEL = 1024
CHUNK = 64
A_HEADS = 8
A_KV_HEADS = 2
A_GROUP = A_HEADS // A_KV_HEADS
A_HEAD_DIM = 64
A_Q = A_HEADS * A_HEAD_DIM
A_KV = A_KV_HEADS * A_HEAD_DIM
IDX_HEADS = 16
IDX_DIM = 64
IDX_Q = IDX_HEADS * IDX_DIM
TOPK_MAX = 256
B_HEADS = 8
B_HEAD_DIM = 64
B_WIDTH = B_HEADS * B_HEAD_DIM
DECAY_LORA = 64
AAA_LORA = 64
GATE_LORA = 128
SHIFT_W = 3 * B_WIDTH + DECAY_LORA + AAA_LORA + GATE_LORA
D_FF = 2816
RMS_EPS = 1e-6
LN_EPS = 1e-6
GN_EPS = 64e-5

LANES = 128
SUB = 8
VMEM_LIMIT = 56 * 1024 * 1024
NEG = -1e30
KEY_TILE = 256

CA_IQ = 0
CA_GA = CA_IQ + IDX_Q
CA_GB = CA_GA + D_MODEL
CA_Q = CA_GB + D_MODEL
PA_WIDTH = CA_Q + A_Q
CB_U = 0
CB_K = CB_U + SHIFT_W
CB_V = CB_K + A_KV
CB_IKW = CB_V + A_KV
PB_WIDTH = CB_IKW + LANES
MXU_COLS = 256


def _dot(a, b, trans_b=False):
    dn = (((1,), (1 if trans_b else 0,)), ((), ()))
    return lax.dot_general(a, b, dn, preferred_element_type=F32)


def _split2(x):
    hi = x.astype(BF16)
    lo = (x - hi.astype(F32)).astype(BF16)
    return hi, lo


def _dot3s(a_split, b_split, trans_b=False):
    ah, al = a_split
    bh, bl = b_split
    return _dot(ah, bh, trans_b) + (_dot(ah, bl, trans_b) + _dot(al, bh, trans_b))


def _dot3(a, b, trans_b=False):
    return _dot3s(_split2(a), _split2(b), trans_b)


def _dot_exact_rhs(a, b_bf16, terms=2):
    out = None
    rem = a
    for _ in range(terms):
        part = rem.astype(BF16)
        rem = rem - part.astype(F32)
        d = _dot(part, b_bf16)
        out = d if out is None else out + d
    return out


def _dot_exact_lhs(a_bf16, b, terms=3, trans_b=False):
    out = None
    rem = b
    for _ in range(terms):
        part = rem.astype(BF16)
        rem = rem - part.astype(F32)
        d = _dot(a_bf16, part, trans_b)
        out = d if out is None else out + d
    return out


def _sigmoid(x):
    return 1.0 / (1.0 + jnp.exp(-x))


def _rms(x, g):
    ms = jnp.mean(x * x, axis=-1, keepdims=True)
    return x * lax.rsqrt(ms + RMS_EPS) * g


def _col_chunks(width, chunk):
    return [(c, min(chunk, width - c)) for c in range(0, width, chunk)]


def _in_proj_kernel(x_ref, g_ref, w_ref, pa_ref, pb_ref, ko_ref, vo_ref):
    xn = _rms(x_ref[...], g_ref[...]).astype(BF16)
    for c, n in _col_chunks(PA_WIDTH, 2 * MXU_COLS):
        pa_ref[:, c:c + n] = _dot(xn, w_ref[:, c:c + n]).astype(BF16)
    for c, n in _col_chunks(PB_WIDTH, 2 * MXU_COLS):
        pb_ref[:, c:c + n] = _dot(xn, w_ref[:, PA_WIDTH + c:PA_WIDTH + c + n])
    tm = x_ref.shape[0]
    for g in range(A_KV_HEADS):
        ko_ref[pl.ds(g, tm, stride=A_KV_HEADS), :] = pb_ref[:, CB_K + g * A_HEAD_DIM:CB_K + (g + 1) * A_HEAD_DIM]
        vo_ref[pl.ds(g, tm, stride=A_KV_HEADS), :] = pb_ref[:, CB_V + g * A_HEAD_DIM:CB_V + (g + 1) * A_HEAD_DIM]


def _in_proj(x2, norm1, w_in_p):
    n = x2.shape[0]
    tm = min(n, 512)
    kv_spec = pl.BlockSpec((tm * A_KV_HEADS, A_HEAD_DIM), lambda i: (i, 0))
    kv_shape = jax.ShapeDtypeStruct((n * A_KV_HEADS, A_HEAD_DIM), F32)
    return pl.pallas_call(
        _in_proj_kernel,
        grid=(n // tm,),
        in_specs=[
            pl.BlockSpec((tm, D_MODEL), lambda i: (i, 0)),
            pl.BlockSpec((1, D_MODEL), lambda i: (0, 0)),
            pl.BlockSpec((D_MODEL, PA_WIDTH + PB_WIDTH), lambda i: (0, 0), pipeline_mode=pl.Buffered(1)),
        ],
        out_specs=[pl.BlockSpec((tm, PA_WIDTH), lambda i: (i, 0)), pl.BlockSpec((tm, PB_WIDTH), lambda i: (i, 0)),
                   kv_spec, kv_spec],
        out_shape=[jax.ShapeDtypeStruct((n, PA_WIDTH), BF16), jax.ShapeDtypeStruct((n, PB_WIDTH), F32),
                   kv_shape, kv_shape],
        compiler_params=pltpu.CompilerParams(
            dimension_semantics=("arbitrary",), vmem_limit_bytes=VMEM_LIMIT),
        name="in_proj",
    )(x2, norm1, w_in_p)


def _layer_norm(x, g, b):
    mu = jnp.mean(x, axis=-1, keepdims=True)
    d = x - mu
    var = jnp.mean(d * d, axis=-1, keepdims=True)
    return d * lax.rsqrt(var + LN_EPS) * g + b


def _transpose_bf16(x, eye):
    return _dot(eye, x, trans_b=True)


def _key_loop(nt, body, init):
    if isinstance(nt, int):
        carry = init
        for j in range(nt):
            carry = body(j, carry)
        return carry
    return lax.fori_loop(0, nt, body, init)


def _key_off(j):
    return j * KEY_TILE if isinstance(j, int) else pl.multiple_of(j * KEY_TILE, KEY_TILE)


def _fold(x, op):
    x = x.reshape(x.shape[0] // SUB, SUB, x.shape[1])
    while x.shape[0] > 1:
        half = x.shape[0] // 2
        x = op(x[:half], x[half:])
    return x[0]


def _stage_queries(iq, q, iw, eye_ref, iq_t, w_full, q_t, rows):
    eye = eye_ref[0:IDX_DIM, 0:IDX_DIM]
    for h in range(IDX_HEADS):
        x = (iq[:, h * IDX_DIM:(h + 1) * IDX_DIM] * (IDX_DIM ** -0.5)).astype(BF16)
        iq_t[:, h * rows:(h + 1) * rows] = _transpose_bf16(x, eye).astype(BF16)
    w_t = _dot_exact_lhs(eye_ref[0:IDX_HEADS, 0:IDX_HEADS], iw, trans_b=True)
    for h in range(IDX_HEADS):
        w_full[:, h * rows:(h + 1) * rows] = w_t[h:h + 1, :] * (IDX_HEADS ** -0.5)
    for g in range(A_KV_HEADS):
        for hh in range(A_GROUP):
            h = g * A_GROUP + hh
            x = (q[:, h * A_HEAD_DIM:(h + 1) * A_HEAD_DIM] * (A_HEAD_DIM ** -0.5)).astype(BF16)
            q_t[g, :, hh * rows:(hh + 1) * rows] = _transpose_bf16(x, eye).astype(BF16)


PACK_ORDER = (0, 2, 1, 3)


def _pack_quarters(acc):
    n = acc.shape[0] // 4
    a, b, c, d = (acc[i * n:(i + 1) * n] for i in range(4))
    lane = lax.broadcasted_iota(jnp.int32, (n, LANES), 1)
    low64 = lane < LANES // 2
    low32 = (lane & (LANES // 4)) == 0
    ab = jnp.where(low64, a, b) + pltpu.roll(jnp.where(low64, b, a), LANES // 2, 1)
    cd = jnp.where(low64, c, d) + pltpu.roll(jnp.where(low64, d, c), LANES // 2, 1)
    t1 = ab + pltpu.roll(ab, 3 * LANES // 4, 1)
    t2 = cd + pltpu.roll(cd, LANES // 4, 1)
    return jnp.where(low32, t1, t2)


def _index_scores(iq_t, w_full, ikb, sc, sc_pk, nt, adm_fn, adm_pk_fn, rows):
    width = sc.shape[1]

    def body(j, carry):
        off = _key_off(j)
        lg = _dot(ikb[pl.ds(off, KEY_TILE), :], iq_t[...])
        acc = None
        for c in range(IDX_HEADS * rows // width):
            x = jnp.maximum(lg[:, c * width:(c + 1) * width], 0.0) * w_full[:, c * width:(c + 1) * width]
            acc = x if acc is None else acc + x
        if sc_pk is not None:
            assert LANES // rows == 4
            pk = KEY_TILE // 4
            sc_pk[pl.ds(j * pk, pk), :] = jnp.where(adm_pk_fn(j), _pack_quarters(acc), -jnp.inf)
        shift = LANES // 2
        while shift >= rows:
            acc = acc + pltpu.roll(acc, shift, 1)
            shift //= 2
        sc[pl.ds(off, KEY_TILE), :] = jnp.where(adm_fn(j), acc, -jnp.inf)
        return carry

    _key_loop(nt, body, 0)


def _select_topk(sc, sc_pk, tril_ref, nt):
    shape = (SUB, sc.shape[1])
    inf = jnp.float32(jnp.inf)
    src = sc if sc_pk is None else sc_pk
    rows_t = KEY_TILE if sc_pk is None else KEY_TILE // 4

    def tile(j):
        start = _key_off(j) if sc_pk is None else j * rows_t
        return src[pl.ds(start, rows_t), :]

    def allsub(x, red):
        op = {jnp.sum: jnp.add, jnp.min: jnp.minimum, jnp.max: jnp.maximum}[red]
        if sc_pk is not None:
            x = op(op(x, pltpu.roll(x, LANES // 4, 1)),
                   op(pltpu.roll(x, LANES // 2, 1), pltpu.roll(x, 3 * LANES // 4, 1)))
        return jnp.broadcast_to(red(x, axis=0, keepdims=True), shape)

    def count_ge(t):
        t1 = t[0:1, :]
        acc = _key_loop(
            nt, lambda j, a: a + _fold(jnp.where(tile(j) >= t1, 1.0, 0.0), jnp.add), jnp.zeros(shape, F32))
        return allsub(acc, jnp.sum)

    def stats(j, c):
        mn, mx, na = c
        s = tile(j)
        fin = s > -inf
        return (jnp.minimum(mn, _fold(jnp.where(fin, s, inf), jnp.minimum)), jnp.maximum(mx, _fold(s, jnp.maximum)),
                na + _fold(jnp.where(fin, 1.0, 0.0), jnp.add))

    mn, mx, na = _key_loop(
        nt, stats, (jnp.full(shape, inf, F32), jnp.full(shape, -inf, F32), jnp.zeros(shape, F32)))
    lo0 = allsub(mn, jnp.min)
    mx = allsub(mx, jnp.max)
    n_adm = allsub(na, jnp.sum)
    kq = jnp.minimum(n_adm, float(TOPK_MAX))
    hi0 = mx + (jnp.abs(mx) * 1e-6 + 1e-30)

    def cond(c):
        return jnp.logical_and(c[0] < 400, c[5] > 0.5)

    def probe(x, lo, hi, c_lo, c_hi):
        c_x = count_ge(x)
        ge = c_x >= kq
        return jnp.where(ge, x, lo), jnp.where(ge, hi, x), jnp.where(ge, c_x, c_lo), jnp.where(ge, c_hi, c_x)

    def body(c):
        it, lo, hi, c_lo, c_hi, _ = c
        for _ in range(2):
            lo, hi, c_lo, c_hi = probe(0.5 * (lo + hi), lo, hi, c_lo, c_hi)
        nxt = 0.5 * (lo + hi)
        active = jnp.where(c_lo - c_hi > 1.5, jnp.where(nxt > lo, jnp.where(nxt < hi, 1.0, 0.0), 0.0), 0.0)
        active = jnp.where(n_adm > kq, active, 0.0)
        return it + 1, lo, hi, c_lo, c_hi, jnp.max(active)

    _, lo, _, c_lo, c_hi, _ = lax.while_loop(
        cond, body, (jnp.int32(0), lo0, hi0, n_adm, jnp.zeros(shape, F32), jnp.float32(1.0)))

    lo1 = lo[0:1, :]
    thr = allsub(
        _key_loop(nt, lambda j, a: jnp.minimum(a, _fold(jnp.where(tile(j) >= lo1, tile(j), inf), jnp.minimum)),
                  jnp.full(shape, inf, F32)),
        jnp.min)[0:1, :]
    take = (kq - c_hi)[0:1, :]
    has_tie = jnp.max(c_lo - kq) > 0.5

    @pl.when(jnp.logical_not(has_tie))
    def _():
        def wr(j, carry):
            off = _key_off(j)
            sc[pl.ds(off, KEY_TILE), :] = jnp.where(sc[pl.ds(off, KEY_TILE), :] >= thr, 0.0, NEG)
            return carry
        _key_loop(nt, wr, 0)

    @pl.when(has_tie)
    def _():
        def wr(j, seen):
            off = _key_off(j)
            s = sc[pl.ds(off, KEY_TILE), :]
            tie = jnp.where(s == thr, 1.0, 0.0)
            rank = _dot(tril_ref[...], tie.astype(BF16)) + seen
            keep_tie = jnp.where(s == thr, jnp.where(rank <= take, 0.0, NEG), NEG)
            sc[pl.ds(off, KEY_TILE), :] = jnp.where(s > thr, 0.0, keep_tie)
            return seen + jnp.sum(tie, axis=0, keepdims=True)
        lax.fori_loop(0, nt, wr, jnp.zeros((1, sc.shape[1]), F32))


def _attend(sc, q_t, kb, v_t, s_sc, acc_sc, nt):
    lanes_g = q_t.shape[2]
    rep = lanes_g // sc.shape[1]
    groups = range(A_KV_HEADS)

    def scores(j, macc):
        off = _key_off(j)
        bias = sc[pl.ds(off, KEY_TILE), :]
        bias = bias if rep == 1 else jnp.concatenate([bias] * rep, axis=1)
        out = []
        for g in groups:
            s = _dot(kb[g, pl.ds(off, KEY_TILE), :], q_t[g]) + bias
            s_sc[g, pl.ds(off, KEY_TILE), :] = s
            out.append(jnp.maximum(macc[g], _fold(s, jnp.maximum)))
        return tuple(out)

    macc = _key_loop(nt, scores, tuple(jnp.full((SUB, lanes_g), NEG, F32) for _ in groups))
    m = [jnp.max(macc[g], axis=0, keepdims=True) for g in groups]
    acc_sc[...] = jnp.zeros(acc_sc.shape, F32)

    def weighted(j, lacc):
        off = _key_off(j)
        out = []
        for g in groups:
            p = jnp.exp(s_sc[g, pl.ds(off, KEY_TILE), :] - m[g])
            acc_sc[g] += _dot(v_t[g, :, pl.ds(off, KEY_TILE)], p.astype(BF16))
            out.append(lacc[g] + _fold(p, jnp.add))
        return tuple(out)

    lacc = _key_loop(nt, weighted, tuple(jnp.zeros((SUB, lanes_g), F32) for _ in groups))
    return [jnp.sum(lacc[g], axis=0, keepdims=True) for g in groups]


def _write_heads(o_ref, l, acc_sc, eye_ref, rows):
    eye = eye_ref[0:rows, 0:rows]
    for g in range(A_KV_HEADS):
        o_t = (acc_sc[g] * (1.0 / l[g])).astype(BF16)
        for hh in range(A_GROUP):
            h = g * A_GROUP + hh
            o_ref[0, :, h * A_HEAD_DIM:(h + 1) * A_HEAD_DIM] = _transpose_bf16(
                o_t[:, hh * rows:(hh + 1) * rows], eye).astype(o_ref.dtype)


_ATTN_SCRATCH = lambda rows, keys: [
    pltpu.VMEM((A_KV_HEADS, keys, A_HEAD_DIM), BF16),
    pltpu.VMEM((A_KV_HEADS, A_HEAD_DIM, keys), BF16),
    pltpu.VMEM((keys, IDX_DIM), BF16),
    pltpu.VMEM((IDX_DIM, IDX_HEADS * rows), BF16),
    pltpu.VMEM((1, IDX_HEADS * rows), F32),
    pltpu.VMEM((A_KV_HEADS, A_HEAD_DIM, A_GROUP * rows), BF16),
    pltpu.VMEM((keys, max(rows, LANES)), F32),
    pltpu.VMEM((A_KV_HEADS, keys, A_GROUP * rows), F32),
    pltpu.VMEM((A_KV_HEADS, A_HEAD_DIM, A_GROUP * rows), F32),
] + ([pltpu.VMEM((keys // (LANES // rows), LANES), F32)] if rows < LANES else [])


def _attn_prompt_kernel(iq_ref, q_ref, k_ref, v_ref, ikw_ref, ikwq_ref, lng_ref, lnb_ref, tril_ref, eye_ref,
                        o_ref, ikln_ref, kb, v_t, ikb, iq_t, w_full, q_t, sc, s_sc, acc_sc, *, rows):
    i = pl.program_id(1)

    @pl.when(i == 0)
    def _():
        kf = k_ref[0]
        vf = v_ref[0]
        eye = eye_ref[0:A_HEAD_DIM, 0:A_HEAD_DIM]
        for g in range(A_KV_HEADS):
            sl = slice(g * A_HEAD_DIM, (g + 1) * A_HEAD_DIM)
            kb[g] = kf[:, sl].astype(BF16)
            v_t[g] = _transpose_bf16(vf[:, sl].astype(BF16), eye).astype(BF16)
        ln = _layer_norm(ikw_ref[0][:, :IDX_DIM], lng_ref[...], lnb_ref[...])
        ikln_ref[0] = ln
        ikb[...] = ln.astype(BF16)

    _stage_queries(iq_ref[0], q_ref[0], ikwq_ref[0][:, IDX_DIM:IDX_DIM + IDX_HEADS], eye_ref, iq_t, w_full, q_t, rows)

    nt = ((i + 1) * rows + KEY_TILE - 1) // KEY_TILE
    q_pos = i * rows + lax.broadcasted_iota(jnp.int32, (KEY_TILE, rows), 1)
    q_end = (q_pos // CHUNK + 1) * CHUNK

    def adm(j):
        return j * KEY_TILE + lax.broadcasted_iota(jnp.int32, (KEY_TILE, rows), 0) < q_end

    _index_scores(iq_t, w_full, ikb, sc, None, nt, adm, None, rows)
    _select_topk(sc, None, tril_ref, nt)
    l = _attend(sc, q_t, kb, v_t, s_sc, acc_sc, nt)
    _write_heads(o_ref, l, acc_sc, eye_ref, rows)


def _attn_prompt(pa3, pb3, idx_k_g, idx_k_b, tril, eye):
    b, s, _ = pa3.shape
    rows = 2 * LANES
    kernel = functools.partial(_attn_prompt_kernel, rows=rows)
    return pl.pallas_call(
        kernel,
        grid=(b, s // rows),
        in_specs=[
            pl.BlockSpec((1, rows, IDX_Q), lambda bi, i: (bi, i, CA_IQ // IDX_Q)),
            pl.BlockSpec((1, rows, A_Q), lambda bi, i: (bi, i, CA_Q // A_Q)),
            pl.BlockSpec((1, s, A_KV), lambda bi, i: (bi, 0, CB_K // A_KV)),
            pl.BlockSpec((1, s, A_KV), lambda bi, i: (bi, 0, CB_V // A_KV)),
            pl.BlockSpec((1, s, LANES), lambda bi, i: (bi, 0, CB_IKW // LANES)),
            pl.BlockSpec((1, rows, LANES), lambda bi, i: (bi, i, CB_IKW // LANES)),
            pl.BlockSpec((1, IDX_DIM), lambda bi, i: (0, 0)),
            pl.BlockSpec((1, IDX_DIM), lambda bi, i: (0, 0)),
            pl.BlockSpec((KEY_TILE, KEY_TILE), lambda bi, i: (0, 0)),
            pl.BlockSpec((KEY_TILE, KEY_TILE), lambda bi, i: (0, 0)),
        ],
        out_specs=[
            pl.BlockSpec((1, rows, A_Q), lambda bi, i: (bi, i, 0)),
            pl.BlockSpec((1, s, IDX_DIM), lambda bi, i: (bi, 0, 0)),
        ],
        out_shape=[
            jax.ShapeDtypeStruct((b, s, A_Q), BF16),
            jax.ShapeDtypeStruct((b, s, IDX_DIM), F32),
        ],
        scratch_shapes=_ATTN_SCRATCH(rows, s),
        compiler_params=pltpu.CompilerParams(
            dimension_semantics=("arbitrary", "arbitrary"), vmem_limit_bytes=VMEM_LIMIT),
        name="attn_prompt",
    )(pa3, pa3, pb3, pb3, pb3, pb3, idx_k_g, idx_k_b, tril, eye)


def _attn_sample_kernel(iq_ref, q_ref, k_ref, v_ref, ikw_ref, ck_ref, cv_ref, cik_ref, lng_ref, lnb_ref, tril_ref,
                        eye_ref, o_ref, ikln_ref, kb, v_t, ikb, iq_t, w_full, q_t, sc, s_sc, acc_sc, sc_pk,
                        *, rows, past, keys, pieces):
    new = rows
    pad = keys - past - new
    piece = pl.program_id(1)
    per_piece = past // pieces
    eye = eye_ref[0:A_HEAD_DIM, 0:A_HEAD_DIM]
    step = 4 * KEY_TILE
    for c in range(0, per_piece, step):
        dst = pl.ds(pl.multiple_of(piece * per_piece + c, step), step)
        for g in range(A_KV_HEADS):
            rows_g = pl.ds(c * A_KV_HEADS + g, step, stride=A_KV_HEADS)
            kb[g, dst, :] = ck_ref[0, rows_g, :].astype(BF16)
            v_t[g, :, dst] = _transpose_bf16(cv_ref[0, rows_g, :].astype(BF16), eye).astype(BF16)
        ikb[dst, :] = cik_ref[0, c:c + step, :].astype(BF16)

    @pl.when(piece == pieces - 1)
    def _():
        kf = k_ref[0]
        vf = v_ref[0]
        for g in range(A_KV_HEADS):
            sl = slice(g * A_HEAD_DIM, (g + 1) * A_HEAD_DIM)
            kb[g, past:past + new, :] = kf[:, sl].astype(BF16)
            kb[g, past + new:keys, :] = jnp.zeros((pad, A_HEAD_DIM), BF16)
            v_t[g, :, past:past + new] = _transpose_bf16(vf[:, sl].astype(BF16), eye).astype(BF16)
            v_t[g, :, past + new:keys] = jnp.zeros((A_HEAD_DIM, pad), BF16)
        ikw = ikw_ref[0]
        ln = _layer_norm(ikw[:, :IDX_DIM], lng_ref[...], lnb_ref[...])
        ikln_ref[0] = ln
        ikb[past:past + new, :] = ln.astype(BF16)
        ikb[past + new:keys, :] = jnp.zeros((pad, IDX_DIM), BF16)

        _stage_queries(iq_ref[0], q_ref[0], ikw[:, IDX_DIM:IDX_DIM + IDX_HEADS], eye_ref, iq_t, w_full, q_t, rows)

        nt = keys // KEY_TILE

        def adm(j):
            return j * KEY_TILE + lax.broadcasted_iota(jnp.int32, (KEY_TILE, LANES), 0) < past + new

        pk = KEY_TILE // 4
        group = lax.broadcasted_iota(jnp.int32, (pk, LANES), 1) // rows
        sub_block = jnp.zeros((pk, LANES), jnp.int32)
        for c, blk in enumerate(PACK_ORDER):
            sub_block = jnp.where(group == c, blk, sub_block)
        key_in_tile = sub_block * pk + lax.broadcasted_iota(jnp.int32, (pk, LANES), 0)

        def adm_pk(j):
            return j * KEY_TILE + key_in_tile < past + new

        _index_scores(iq_t, w_full, ikb, sc, sc_pk, nt, adm, adm_pk, rows)
        _select_topk(sc, sc_pk, tril_ref, nt)
        l = _attend(sc, q_t, kb, v_t, s_sc, acc_sc, nt)
        _write_heads(o_ref, l, acc_sc, eye_ref, rows)


def _attn_sample(pa3, pb3, cache_k, cache_v, cache_ik, idx_k_g, idx_k_b, tril, eye):
    b, t, _ = pa3.shape
    past = cache_ik.shape[1]
    pieces = 2
    keys = -(-(past + t) // KEY_TILE) * KEY_TILE
    kernel = functools.partial(_attn_sample_kernel, rows=t, past=past, keys=keys, pieces=pieces)
    return pl.pallas_call(
        kernel,
        grid=(b, pieces),
        in_specs=[
            pl.BlockSpec((1, t, IDX_Q), lambda bi, pc: (bi, 0, CA_IQ // IDX_Q)),
            pl.BlockSpec((1, t, A_Q), lambda bi, pc: (bi, 0, CA_Q // A_Q)),
            pl.BlockSpec((1, t, A_KV), lambda bi, pc: (bi, 0, CB_K // A_KV)),
            pl.BlockSpec((1, t, A_KV), lambda bi, pc: (bi, 0, CB_V // A_KV)),
            pl.BlockSpec((1, t, LANES), lambda bi, pc: (bi, 0, CB_IKW // LANES)),
            pl.BlockSpec((1, past * A_KV_HEADS // pieces, A_HEAD_DIM), lambda bi, pc: (bi, pc, 0)),
            pl.BlockSpec((1, past * A_KV_HEADS // pieces, A_HEAD_DIM), lambda bi, pc: (bi, pc, 0)),
            pl.BlockSpec((1, past // pieces, IDX_DIM), lambda bi, pc: (bi, pc, 0)),
            pl.BlockSpec((1, IDX_DIM), lambda bi, pc: (0, 0)),
            pl.BlockSpec((1, IDX_DIM), lambda bi, pc: (0, 0)),
            pl.BlockSpec((KEY_TILE, KEY_TILE), lambda bi, pc: (0, 0)),
            pl.BlockSpec((KEY_TILE, KEY_TILE), lambda bi, pc: (0, 0)),
        ],
        out_specs=[
            pl.BlockSpec((1, t, A_Q), lambda bi, pc: (bi, 0, 0)),
            pl.BlockSpec((1, t, IDX_DIM), lambda bi, pc: (bi, 0, 0)),
        ],
        out_shape=[
            jax.ShapeDtypeStruct((b, t, A_Q), BF16),
            jax.ShapeDtypeStruct((b, t, IDX_DIM), F32),
        ],
        scratch_shapes=_ATTN_SCRATCH(t, keys),
        compiler_params=pltpu.CompilerParams(
            dimension_semantics=("arbitrary", "arbitrary"), vmem_limit_bytes=VMEM_LIMIT),
        name="attn_sample",
    )(pa3, pa3, pb3, pb3, pb3, cache_k, cache_v, cache_ik, idx_k_g, idx_k_b, tril, eye)


def _rwkv_kernel(u_ref, shift_ref, s0_ref, mu_ref, w0_ref, a0_ref, kk_ref, ka_ref, rk_ref, gnw_ref, gnb_ref,
                 w2_ref, a2_ref, g2_ref, bd_ref, tri_ref, yb_ref, s_out_ref, carry, state, y_sc, *, tc):
    t = pl.program_id(1)
    u = u_ref[0]
    tb = u.shape[0]
    n = B_HEAD_DIM

    @pl.when(t == 0)
    def _():
        carry[...] = shift_ref[0]
        state[...] = s0_ref[0]

    row = lax.broadcasted_iota(jnp.int32, u.shape, 0)
    u_prev = jnp.where(row == 0, carry[...], pltpu.roll(u, 1, 0))
    carry[...] = u[tb - 1:tb, :]
    m = u + (u_prev - u) * mu_ref[...]

    r = m[:, 0:B_WIDTH]
    k = m[:, B_WIDTH:2 * B_WIDTH]
    v = m[:, 2 * B_WIDTH:3 * B_WIDTH]
    lora = m[:, 3 * B_WIDTH:3 * B_WIDTH + LANES]
    gl = m[:, 3 * B_WIDTH + LANES:]
    lane = lax.broadcasted_iota(jnp.int32, lora.shape, 1)
    lora = jnp.where(lane < DECAY_LORA, jnp.tanh(lora), lora).astype(BF16)
    z = w0_ref[...] + _dot(lora, w2_ref[...])
    softplus = jnp.maximum(-z, 0.0) + jnp.log(1.0 + jnp.exp(-jnp.abs(z)))
    lw = -jnp.exp(-softplus - 0.5)
    a = _sigmoid(a0_ref[...] + _dot(lora, a2_ref[...]))
    g = _dot(_sigmoid(gl).astype(BF16), g2_ref[...])

    kk = k * kk_ref[...]
    ss = _dot_exact_rhs(kk * kk, bd_ref[...])
    kk = kk / jnp.maximum(jnp.sqrt(ss), 1e-12)
    k2 = k * (1.0 + (a - 1.0) * ka_ref[...])
    bonus = _dot_exact_rhs(r * k2 * rk_ref[...], bd_ref[...]) * v

    cum = _dot_exact_lhs(tri_ref[...], lw)
    g_in = jnp.exp(cum)
    g_ex = jnp.exp(cum - lw)
    g_inv = jnp.exp(-cum)
    a_t = -kk * g_ex
    b_t = kk * a * g_inv
    k_t = k2 * g_inv
    r_t = r * g_in

    ri = lax.broadcasted_iota(jnp.int32, (tc, tc), 0)
    ci = lax.broadcasted_iota(jnp.int32, (tc, tc), 1)
    strict = ci < ri
    incl = ci <= ri
    eye = jnp.where(ci == ri, 1.0, 0.0)

    heads = range(B_HEADS)
    chunks = range(tb // tc)
    pairs = [(c, h) for c in chunks for h in heads]
    lanes = [slice(h * n, (h + 1) * n) for h in heads]
    toks = [slice(c * tc, (c + 1) * tc) for c in chunks]
    bf = lambda x: x.astype(BF16)
    g_end = [g_in[(c + 1) * tc - 1:(c + 1) * tc, :] for c in chunks]
    vh = {(c, h): v[toks[c], lanes[h]] for c, h in pairs}
    ar = {(c, h): _split2(jnp.concatenate([a_t[toks[c], lanes[h]], r_t[toks[c], lanes[h]]], axis=0))
          for c, h in pairs}
    bk = {(c, h): jnp.concatenate([b_t[toks[c], lanes[h]], k_t[toks[c], lanes[h]]], axis=0) for c, h in pairs}
    cross = {p: _dot(ar[p][0], bf(bk[p]), trans_b=True) for p in pairs}
    l_ab = {p: jnp.where(strict, cross[p][0:tc, 0:tc], 0.0) for p in pairs}
    l_akv = {p: _dot(bf(jnp.where(strict, cross[p][0:tc, tc:2 * tc], 0.0)), bf(vh[p])) for p in pairs}
    m_rbk = {p: bf(jnp.concatenate([jnp.where(incl, cross[p][tc:2 * tc, 0:tc], 0.0),
                                    jnp.where(incl, cross[p][tc:2 * tc, tc:2 * tc], 0.0)], axis=1)) for p in pairs}

    inv = {p: eye + l_ab[p] for p in pairs}
    pw = l_ab
    span = 2
    while span < tc:
        pwb = {p: bf(pw[p]) for p in pairs}
        pw = {p: _dot(pwb[p], pwb[p]) for p in pairs}
        inv = {p: inv[p] + _dot(bf(inv[p]), bf(pw[p])) for p in pairs}
        span *= 2
    inv = {p: bf(inv[p]) for p in pairs}

    s_cur = [state[h] for h in heads]
    for c in chunks:
        from_state = [_dot3s(ar[c, h], _split2(s_cur[h]), trans_b=True) for h in heads]
        z = [_dot(inv[c, h], bf(from_state[h][0:tc] + l_akv[c, h])) for h in heads]
        zv = [jnp.concatenate([z[h], vh[c, h]], axis=0) for h in heads]
        y = [from_state[h][tc:2 * tc] + _dot(m_rbk[c, h], bf(zv[h])) for h in heads]
        upd = [_dot3(zv[h].T, bk[c, h] * g_end[c][:, lanes[h]]) for h in heads]
        s_cur = [s_cur[h] * g_end[c][:, lanes[h]] + upd[h] for h in heads]
        for h in heads:
            y_sc[toks[c], lanes[h]] = y[h]
    for h in heads:
        state[h] = s_cur[h]

    y = y_sc[...]
    mean = _dot_exact_rhs(y, bd_ref[...]) * (1.0 / n)
    d = y - mean
    var = _dot_exact_rhs(d * d, bd_ref[...]) * (1.0 / n)
    yn = d * lax.rsqrt(var + GN_EPS) * gnw_ref[...] + gnb_ref[...]
    yb_ref[0] = ((yn + bonus) * g).astype(yb_ref.dtype)

    @pl.when(t == pl.num_programs(1) - 1)
    def _():
        s_out_ref[0] = state[...]


def _rwkv(pb3, shift_prev, s0, p, tc):
    b, s, _ = pb3.shape
    tb = min(s, 4 * tc)
    tri = jnp.asarray(np.kron(np.eye(tb // tc), np.tril(np.ones((tc, tc)))), BF16)
    vec = lambda w: pl.BlockSpec((1, w), lambda bi, t: (0, 0))
    mat = lambda r, c: pl.BlockSpec((r, c), lambda bi, t: (0, 0))
    st = pl.BlockSpec((1, B_HEADS, B_HEAD_DIM, B_HEAD_DIM), lambda bi, t: (bi, 0, 0, 0))
    return pl.pallas_call(
        functools.partial(_rwkv_kernel, tc=tc),
        grid=(b, s // tb),
        in_specs=[
            pl.BlockSpec((1, tb, SHIFT_W), lambda bi, t: (bi, t, CB_U // SHIFT_W)),
            pl.BlockSpec((1, 1, SHIFT_W), lambda bi, t: (bi, 0, 0)),
            st,
            vec(SHIFT_W), vec(B_WIDTH), vec(B_WIDTH), vec(B_WIDTH), vec(B_WIDTH), vec(B_WIDTH),
            vec(B_WIDTH), vec(B_WIDTH),
            mat(LANES, B_WIDTH), mat(LANES, B_WIDTH), mat(GATE_LORA, B_WIDTH), mat(B_WIDTH, B_WIDTH), mat(tb, tb),
        ],
        out_specs=[pl.BlockSpec((1, tb, B_WIDTH), lambda bi, t: (bi, t, 0)), st],
        out_shape=[jax.ShapeDtypeStruct((b, s, B_WIDTH), BF16),
                   jax.ShapeDtypeStruct((b, B_HEADS, B_HEAD_DIM, B_HEAD_DIM), F32)],
        scratch_shapes=[pltpu.VMEM((1, SHIFT_W), F32),
                        pltpu.VMEM((B_HEADS, B_HEAD_DIM, B_HEAD_DIM), F32),
                        pltpu.VMEM((tb, B_WIDTH), F32)],
        compiler_params=pltpu.CompilerParams(
            dimension_semantics=("arbitrary", "arbitrary"), vmem_limit_bytes=VMEM_LIMIT),
        name="rwkv",
    )(pb3, shift_prev, s0, p['shift_mu'], p['w0'], p['a0'], p['k_k'], p['k_a'], p['r_k'], p['gn_w'], p['gn_b'],
      p['w2p'], p['a2p'], p['g2'], p['bd'], tri)


def _merge_kernel(oa_ref, yb_ref, ga_ref, gb_ref, x_ref, woa_ref, wob_ref, wout_ref, h_ref):
    merged = (_sigmoid(ga_ref[...].astype(F32)) * _dot(oa_ref[...], woa_ref[...])
              + _sigmoid(gb_ref[...].astype(F32)) * _dot(yb_ref[...], wob_ref[...]))
    h_ref[...] = x_ref[...] + _dot(merged.astype(BF16), wout_ref[...])


def _merge(o_a, y_b, pa, x2, p):
    n = x2.shape[0]
    tm = min(n, 512)
    row = lambda w, j=0: pl.BlockSpec((tm, w), lambda i: (i, j))
    full = lambda r, c: pl.BlockSpec((r, c), lambda i: (0, 0))
    return pl.pallas_call(
        _merge_kernel,
        grid=(n // tm,),
        in_specs=[
            row(A_Q), row(B_WIDTH),
            row(D_MODEL, CA_GA // D_MODEL), row(D_MODEL, CA_GB // D_MODEL), row(D_MODEL),
            full(A_Q, D_MODEL), full(B_WIDTH, D_MODEL), full(D_MODEL, D_MODEL),
        ],
        out_specs=row(D_MODEL),
        out_shape=jax.ShapeDtypeStruct((n, D_MODEL), F32),
        compiler_params=pltpu.CompilerParams(
            dimension_semantics=("arbitrary",), vmem_limit_bytes=VMEM_LIMIT),
        name="merge_out_proj",
    )(o_a, y_b, pa, pa, x2, p['w_oa'], p['w_ob'], p['w_out'])


def _ffn_kernel(h_ref, n2_ref, nf_ref, wg_ref, wu_ref, wd_ref, o_ref):
    h = h_ref[...]
    hn = _rms(h, n2_ref[...]).astype(BF16)
    out = h
    for c, n in _col_chunks(D_FF, 4 * MXU_COLS):
        gate = _dot(hn, wg_ref[:, c:c + n])
        up = _dot(hn, wu_ref[:, c:c + n])
        act = (gate * _sigmoid(gate) * up).astype(BF16)
        out = out + _dot(act, wd_ref[c:c + n, :])
    o_ref[...] = _rms(out, nf_ref[...])


def _ffn(h, p, norm_f):
    n = h.shape[0]
    tm = min(n, 512)
    resident = lambda r, c: pl.BlockSpec((r, c), lambda i: (0, 0), pipeline_mode=pl.Buffered(1))
    return pl.pallas_call(
        _ffn_kernel,
        grid=(n // tm,),
        in_specs=[
            pl.BlockSpec((tm, D_MODEL), lambda i: (i, 0)),
            pl.BlockSpec((1, D_MODEL), lambda i: (0, 0)),
            pl.BlockSpec((1, D_MODEL), lambda i: (0, 0)),
            resident(D_MODEL, D_FF), resident(D_MODEL, D_FF), resident(D_FF, D_MODEL),
        ],
        out_specs=pl.BlockSpec((tm, D_MODEL), lambda i: (i, 0)),
        out_shape=jax.ShapeDtypeStruct((n, D_MODEL), F32),
        compiler_params=pltpu.CompilerParams(
            dimension_semantics=("arbitrary",), vmem_limit_bytes=VMEM_LIMIT),
        name="ffn_final_norm",
    )(h, p['norm2'], norm_f, p['w_gate'], p['w_up'], p['w_down'])


def _prepare_params(l, norm1, w_in, idx_k_g, idx_k_b, shift_mu, w0, w2, a0, a2, g2, k_k, k_a, r_k,
                    gn_w, gn_b, w_oa, w_ob, w_out, norm2, w_gate, w_up, w_down):
    w = w_in[l]
    o = np.cumsum([0, A_Q, A_KV, A_KV, IDX_Q, IDX_DIM, IDX_HEADS, SHIFT_W, D_MODEL, D_MODEL])
    seg = lambda i: w[:, o[i]:o[i + 1]]
    pad = jnp.zeros((D_MODEL, LANES - IDX_DIM - IDX_HEADS), w.dtype)
    w_in_p = jnp.concatenate([seg(3), seg(7), seg(8), seg(0), seg(6), seg(1), seg(2), seg(4), seg(5), pad],
                             axis=1).astype(BF16)
    zeros = jnp.zeros((LANES - DECAY_LORA, B_WIDTH), F32)
    head = np.arange(B_WIDTH) // B_HEAD_DIM
    row = lambda x: x[l].reshape(1, -1)
    return {
        'norm1': row(norm1), 'w_in_p': w_in_p,
        'idx_k_g': row(idx_k_g), 'idx_k_b': row(idx_k_b),
        'shift_mu': row(shift_mu), 'w0': row(w0), 'a0': row(a0),
        'k_k': row(k_k), 'k_a': row(k_a), 'r_k': row(r_k), 'gn_w': row(gn_w), 'gn_b': row(gn_b),
        'w2p': jnp.concatenate([w2[l], zeros], axis=0).astype(BF16),
        'a2p': jnp.concatenate([zeros, a2[l]], axis=0).astype(BF16),
        'g2': g2[l].astype(BF16),
        'bd': jnp.asarray(head[:, None] == head[None, :], BF16),
        'w_oa': w_oa[l].astype(BF16), 'w_ob': w_ob[l].astype(BF16), 'w_out': w_out[l].astype(BF16),
        'norm2': row(norm2),
        'w_gate': w_gate[l].astype(BF16), 'w_up': w_up[l].astype(BF16), 'w_down': w_down[l].astype(BF16),
    }


def _layer(x, attend, shift_prev, wkv0, p, norm_f, chunk):
    b, t, _ = x.shape
    x2 = x.reshape(b * t, D_MODEL)
    pa, pb, k_rows, v_rows = _in_proj(x2, p['norm1'], p['w_in_p'])
    pa3 = pa.reshape(b, t, PA_WIDTH)
    pb3 = pb.reshape(b, t, PB_WIDTH)
    o_a, ik_ln = attend(pa3, pb3)
    y_b, wkv_new = _rwkv(pb3, shift_prev, wkv0, p, chunk)
    flat = lambda z: z.reshape(b * t, z.shape[-1])
    h = _merge(flat(o_a), flat(y_b), pa, x2, p)
    out = _ffn(h, p, norm_f).reshape(b, t, D_MODEL)
    k_new = k_rows.reshape(b, t, A_KV_HEADS, A_HEAD_DIM)
    v_new = v_rows.reshape(b, t, A_KV_HEADS, A_HEAD_DIM)
    shift_new = pb3[:, t - 1:t, CB_U:CB_U + SHIFT_W]
    return out, k_new, v_new, ik_ln, wkv_new, shift_new


def kernel(x_prompt, x_sample, cache_k, cache_v, cache_idx_k, state_wkv, state_shift,
           norm1, w_in, idx_k_g, idx_k_b, shift_mu, w0, w2, a0, a2, g2, k_k, k_a, r_k,
           gn_w, gn_b, w_oa, w_ob, w_out, norm2, w_gate, w_up, w_down, norm_f):
    assert w_in.shape[0] == 1, "single-layer kernel"
    l = 0
    p = _prepare_params(l, norm1, w_in, idx_k_g, idx_k_b, shift_mu, w0, w2, a0, a2, g2, k_k, k_a, r_k,
                        gn_w, gn_b, w_oa, w_ob, w_out, norm2, w_gate, w_up, w_down)
    nf = norm_f.reshape(1, -1)
    tril = jnp.asarray(np.tril(np.ones((KEY_TILE, KEY_TILE), np.float32)), BF16)
    eye = jnp.asarray(np.eye(KEY_TILE, dtype=np.float32), BF16)

    n_p = x_prompt.shape[0]
    shift0 = jnp.zeros((n_p, 1, SHIFT_W), F32)
    wkv_zero = jnp.zeros((n_p, B_HEADS, B_HEAD_DIM, B_HEAD_DIM), F32)
    attend_p = lambda pa3, pb3: _attn_prompt(pa3, pb3, p['idx_k_g'], p['idx_k_b'], tril, eye)
    y_p, k_p, v_p, ik_p, wkv_p, shift_p = _layer(x_prompt, attend_p, shift0, wkv_zero, p, nf, 64)

    n_s, t_s = x_sample.shape[0], x_sample.shape[1]
    past = cache_k.shape[2]
    ck = cache_k[l].reshape(n_s, past * A_KV_HEADS, A_HEAD_DIM)
    cv = cache_v[l].reshape(n_s, past * A_KV_HEADS, A_HEAD_DIM)
    attend_s = lambda pa3, pb3: _attn_sample(pa3, pb3, ck, cv, cache_idx_k[l], p['idx_k_g'], p['idx_k_b'], tril, eye)
    y_s, k_s, v_s, ik_s, wkv_s, shift_s = _layer(x_sample, attend_s, state_shift[l], state_wkv[l], p, nf, t_s)

    lead = lambda z: z[None]
    return (y_p, y_s, lead(k_p), lead(v_p), lead(ik_p), lead(wkv_p), lead(shift_p),
            lead(k_s), lead(v_s), lead(ik_s), lead(wkv_s), lead(shift_s))
```

```python
import functools

import jax
import jax.numpy as jnp
import numpy as np
from jax import lax
from jax.experimental import pallas as pl
from jax.experimental.pallas import tpu as pltpu

F32 = jnp.float32
BF16 = jnp.bfloat16

D_MODEL = 1024
CHUNK = 64
A_HEADS = 8
A_KV_HEADS = 2
A_GROUP = A_HEADS // A_KV_HEADS
A_HEAD_DIM = 64
A_Q = A_HEADS * A_HEAD_DIM
A_KV = A_KV_HEADS * A_HEAD_DIM
IDX_HEADS = 16
IDX_DIM = 64
IDX_Q = IDX_HEADS * IDX_DIM
TOPK_MAX = 256
B_HEADS = 8
B_HEAD_DIM = 64
B_WIDTH = B_HEADS * B_HEAD_DIM
DECAY_LORA = 64
AAA_LORA = 64
GATE_LORA = 128
SHIFT_W = 3 * B_WIDTH + DECAY_LORA + AAA_LORA + GATE_LORA
D_FF = 2816
RMS_EPS = 1e-6
LN_EPS = 1e-6
GN_EPS = 64e-5

LANES = 128
SUB = 8
VMEM_LIMIT = 56 * 1024 * 1024
NEG = -1e30
KEY_TILE = 256

CA_IQ = 0
CA_GA = CA_IQ + IDX_Q
CA_GB = CA_GA + D_MODEL
CA_Q = CA_GB + D_MODEL
PA_WIDTH = CA_Q + A_Q
CB_U = 0
CB_K = CB_U + SHIFT_W
CB_V = CB_K + A_KV
CB_IKW = CB_V + A_KV
PB_WIDTH = CB_IKW + LANES
MXU_COLS = 256


def _dot(a, b, trans_b=False):
    dn = (((1,), (1 if trans_b else 0,)), ((), ()))
    return lax.dot_general(a, b, dn, preferred_element_type=F32)


def _split2(x):
    hi = x.astype(BF16)
    lo = (x - hi.astype(F32)).astype(BF16)
    return hi, lo


def _dot3s(a_split, b_split, trans_b=False):
    ah, al = a_split
    bh, bl = b_split
    return _dot(ah, bh, trans_b) + (_dot(ah, bl, trans_b) + _dot(al, bh, trans_b))


def _dot3(a, b, trans_b=False):
    return _dot3s(_split2(a), _split2(b), trans_b)


def _dot_exact_rhs(a, b_bf16, terms=2):
    out = None
    rem = a
    for _ in range(terms):
        part = rem.astype(BF16)
        rem = rem - part.astype(F32)
        d = _dot(part, b_bf16)
        out = d if out is None else out + d
    return out


def _dot_exact_lhs(a_bf16, b, terms=3, trans_b=False):
    out = None
    rem = b
    for _ in range(terms):
        part = rem.astype(BF16)
        rem = rem - part.astype(F32)
        d = _dot(a_bf16, part, trans_b)
        out = d if out is None else out + d
    return out


def _sigmoid(x):
    return 1.0 / (1.0 + jnp.exp(-x))


def _rms(x, g):
    ms = jnp.mean(x * x, axis=-1, keepdims=True)
    return x * lax.rsqrt(ms + RMS_EPS) * g


def _col_chunks(width, chunk):
    return [(c, min(chunk, width - c)) for c in range(0, width, chunk)]


def _in_proj_kernel(x_ref, g_ref, w_ref, pa_ref, pb_ref, *kv_t_refs):
    xn = _rms(x_ref[...], g_ref[...]).astype(BF16)
    for c, n in _col_chunks(PA_WIDTH, 2 * MXU_COLS):
        pa_ref[:, c:c + n] = _dot(xn, w_ref[:, c:c + n]).astype(BF16)
    for c, n in _col_chunks(PB_WIDTH, 2 * MXU_COLS):
        pb_ref[:, c:c + n] = _dot(xn, w_ref[:, PA_WIDTH + c:PA_WIDTH + c + n])
    for ref, col in zip(kv_t_refs, (CB_K, CB_V)):
        ref[0] = pb_ref[:, col:col + A_KV].T


def _in_proj(x2, norm1, w_in_p, seq):
    n = x2.shape[0]
    tm = min(n, 512)
    out_specs = [pl.BlockSpec((tm, PA_WIDTH), lambda i: (i, 0)), pl.BlockSpec((tm, PB_WIDTH), lambda i: (i, 0))]
    out_shape = [jax.ShapeDtypeStruct((n, PA_WIDTH), BF16), jax.ShapeDtypeStruct((n, PB_WIDTH), F32)]
    if seq % tm == 0:
        per_seq = seq // tm
        out_specs += [pl.BlockSpec((1, A_KV, tm), lambda i: (i // per_seq, 0, i % per_seq))] * 2
        out_shape += [jax.ShapeDtypeStruct((n // seq, A_KV, seq), F32)] * 2
    return pl.pallas_call(
        _in_proj_kernel,
        grid=(n // tm,),
        in_specs=[
            pl.BlockSpec((tm, D_MODEL), lambda i: (i, 0)),
            pl.BlockSpec((1, D_MODEL), lambda i: (0, 0)),
            pl.BlockSpec((D_MODEL, PA_WIDTH + PB_WIDTH), lambda i: (0, 0), pipeline_mode=pl.Buffered(1)),
        ],
        out_specs=out_specs,
        out_shape=out_shape,
        compiler_params=pltpu.CompilerParams(
            dimension_semantics=("arbitrary",), vmem_limit_bytes=VMEM_LIMIT),
        name="in_proj",
    )(x2, norm1, w_in_p)


def _layer_norm(x, g, b):
    mu = jnp.mean(x, axis=-1, keepdims=True)
    d = x - mu
    var = jnp.mean(d * d, axis=-1, keepdims=True)
    return d * lax.rsqrt(var + LN_EPS) * g + b


def _transpose_bf16(x, eye):
    return _dot(eye, x, trans_b=True)


def _key_loop(nt, body, init):
    if isinstance(nt, int):
        carry = init
        for j in range(nt):
            carry = body(j, carry)
        return carry
    return lax.fori_loop(0, nt, body, init)


def _key_off(j):
    return j * KEY_TILE if isinstance(j, int) else pl.multiple_of(j * KEY_TILE, KEY_TILE)


def _fold(x, op):
    x = x.reshape(x.shape[0] // SUB, SUB, x.shape[1])
    while x.shape[0] > 1:
        half = x.shape[0] // 2
        x = op(x[:half], x[half:])
    return x[0]


def _stage_queries(iq, q, iw, eye_ref, iq_t, w_full, q_t, rows):
    eye = eye_ref[0:IDX_DIM, 0:IDX_DIM]
    for h in range(IDX_HEADS):
        x = (iq[:, h * IDX_DIM:(h + 1) * IDX_DIM] * (IDX_DIM ** -0.5)).astype(BF16)
        iq_t[:, h * rows:(h + 1) * rows] = _transpose_bf16(x, eye).astype(BF16)
    w_t = _dot_exact_lhs(eye_ref[0:IDX_HEADS, 0:IDX_HEADS], iw, trans_b=True)
    for h in range(IDX_HEADS):
        w_full[:, h * rows:(h + 1) * rows] = w_t[h:h + 1, :] * (IDX_HEADS ** -0.5)
    for g in range(A_KV_HEADS):
        for hh in range(A_GROUP):
            h = g * A_GROUP + hh
            x = (q[:, h * A_HEAD_DIM:(h + 1) * A_HEAD_DIM] * (A_HEAD_DIM ** -0.5)).astype(BF16)
            q_t[g, :, hh * rows:(hh + 1) * rows] = _transpose_bf16(x, eye).astype(BF16)


PACK_ORDER = (0, 2, 1, 3)


def _pack_quarters(acc):
    n = acc.shape[0] // 4
    a, b, c, d = (acc[i * n:(i + 1) * n] for i in range(4))
    lane = lax.broadcasted_iota(jnp.int32, (n, LANES), 1)
    low64 = lane < LANES // 2
    low32 = (lane & (LANES // 4)) == 0
    ab = jnp.where(low64, a, b) + pltpu.roll(jnp.where(low64, b, a), LANES // 2, 1)
    cd = jnp.where(low64, c, d) + pltpu.roll(jnp.where(low64, d, c), LANES // 2, 1)
    t1 = ab + pltpu.roll(ab, 3 * LANES // 4, 1)
    t2 = cd + pltpu.roll(cd, LANES // 4, 1)
    return jnp.where(low32, t1, t2)


def _index_scores(iq_t, w_full, ikb, sc, sc_pk, nt, adm_fn, adm_pk_fn, rows):
    width = sc.shape[1]

    def body(j, carry):
        off = _key_off(j)
        lg = _dot(ikb[pl.ds(off, KEY_TILE), :], iq_t[...])
        acc = None
        for c in range(IDX_HEADS * rows // width):
            x = jnp.maximum(lg[:, c * width:(c + 1) * width], 0.0) * w_full[:, c * width:(c + 1) * width]
            acc = x if acc is None else acc + x
        if sc_pk is not None:
            assert LANES // rows == 4
            pk = KEY_TILE // 4
            sc_pk[pl.ds(j * pk, pk), :] = jnp.where(adm_pk_fn(j), _pack_quarters(acc), -jnp.inf)
        shift = LANES // 2
        while shift >= rows:
            acc = acc + pltpu.roll(acc, shift, 1)
            shift //= 2
        sc[pl.ds(off, KEY_TILE), :] = jnp.where(adm_fn(j), acc, -jnp.inf)
        return carry

    _key_loop(nt, body, 0)


def _select_topk(sc, sc_pk, tril_ref, nt):
    shape = (SUB, sc.shape[1])
    inf = jnp.float32(jnp.inf)
    src = sc if sc_pk is None else sc_pk
    rows_t = KEY_TILE if sc_pk is None else KEY_TILE // 4

    def tile(j):
        start = _key_off(j) if sc_pk is None else j * rows_t
        return src[pl.ds(start, rows_t), :]

    def allsub(x, red):
        op = {jnp.sum: jnp.add, jnp.min: jnp.minimum, jnp.max: jnp.maximum}[red]
        if sc_pk is not None:
            x = op(op(x, pltpu.roll(x, LANES // 4, 1)),
                   op(pltpu.roll(x, LANES // 2, 1), pltpu.roll(x, 3 * LANES // 4, 1)))
        return jnp.broadcast_to(red(x, axis=0, keepdims=True), shape)

    def count_ge(t):
        t1 = t[0:1, :]
        acc = _key_loop(
            nt, lambda j, a: a + _fold(jnp.where(tile(j) >= t1, 1.0, 0.0), jnp.add), jnp.zeros(shape, F32))
        return allsub(acc, jnp.sum)

    def stats(j, c):
        mn, mx, na = c
        s = tile(j)
        fin = s > -inf
        return (jnp.minimum(mn, _fold(jnp.where(fin, s, inf), jnp.minimum)), jnp.maximum(mx, _fold(s, jnp.maximum)),
                na + _fold(jnp.where(fin, 1.0, 0.0), jnp.add))

    mn, mx, na = _key_loop(
        nt, stats, (jnp.full(shape, inf, F32), jnp.full(shape, -inf, F32), jnp.zeros(shape, F32)))
    lo0 = allsub(mn, jnp.min)
    mx = allsub(mx, jnp.max)
    n_adm = allsub(na, jnp.sum)
    kq = jnp.minimum(n_adm, float(TOPK_MAX))
    hi0 = mx + (jnp.abs(mx) * 1e-6 + 1e-30)

    def cond(c):
        return jnp.logical_and(c[0] < 400, c[5] > 0.5)

    def probe(x, lo, hi, c_lo, c_hi):
        c_x = count_ge(x)
        ge = c_x >= kq
        return jnp.where(ge, x, lo), jnp.where(ge, hi, x), jnp.where(ge, c_x, c_lo), jnp.where(ge, c_hi, c_x)

    def body(c):
        it, lo, hi, c_lo, c_hi, _ = c
        for _ in range(2):
            lo, hi, c_lo, c_hi = probe(0.5 * (lo + hi), lo, hi, c_lo, c_hi)
        nxt = 0.5 * (lo + hi)
        active = jnp.where(c_lo - c_hi > 1.5, jnp.where(nxt > lo, jnp.where(nxt < hi, 1.0, 0.0), 0.0), 0.0)
        active = jnp.where(n_adm > kq, active, 0.0)
        return it + 1, lo, hi, c_lo, c_hi, jnp.max(active)

    _, lo, _, c_lo, c_hi, _ = lax.while_loop(
        cond, body, (jnp.int32(0), lo0, hi0, n_adm, jnp.zeros(shape, F32), jnp.float32(1.0)))

    lo1 = lo[0:1, :]
    thr = allsub(
        _key_loop(nt, lambda j, a: jnp.minimum(a, _fold(jnp.where(tile(j) >= lo1, tile(j), inf), jnp.minimum)),
                  jnp.full(shape, inf, F32)),
        jnp.min)[0:1, :]
    take = (kq - c_hi)[0:1, :]
    has_tie = jnp.max(c_lo - kq) > 0.5

    @pl.when(jnp.logical_not(has_tie))
    def _():
        def wr(j, carry):
            off = _key_off(j)
            sc[pl.ds(off, KEY_TILE), :] = jnp.where(sc[pl.ds(off, KEY_TILE), :] >= thr, 0.0, NEG)
            return carry
        _key_loop(nt, wr, 0)

    @pl.when(has_tie)
    def _():
        def wr(j, seen):
            off = _key_off(j)
            s = sc[pl.ds(off, KEY_TILE), :]
            tie = jnp.where(s == thr, 1.0, 0.0)
            rank = _dot(tril_ref[...], tie.astype(BF16)) + seen
            keep_tie = jnp.where(s == thr, jnp.where(rank <= take, 0.0, NEG), NEG)
            sc[pl.ds(off, KEY_TILE), :] = jnp.where(s > thr, 0.0, keep_tie)
            return seen + jnp.sum(tie, axis=0, keepdims=True)
        lax.fori_loop(0, nt, wr, jnp.zeros((1, sc.shape[1]), F32))


def _attend(sc, q_t, kb, v_t, s_sc, acc_sc, nt):
    lanes_g = q_t.shape[2]
    rep = lanes_g // sc.shape[1]
    groups = range(A_KV_HEADS)

    def scores(j, macc):
        off = _key_off(j)
        bias = sc[pl.ds(off, KEY_TILE), :]
        bias = bias if rep == 1 else jnp.concatenate([bias] * rep, axis=1)
        out = []
        for g in groups:
            s = _dot(kb[g, pl.ds(off, KEY_TILE), :], q_t[g]) + bias
            s_sc[g, pl.ds(off, KEY_TILE), :] = s
            out.append(jnp.maximum(macc[g], _fold(s, jnp.maximum)))
        return tuple(out)

    macc = _key_loop(nt, scores, tuple(jnp.full((SUB, lanes_g), NEG, F32) for _ in groups))
    m = [jnp.max(macc[g], axis=0, keepdims=True) for g in groups]
    acc_sc[...] = jnp.zeros(acc_sc.shape, F32)

    def weighted(j, lacc):
        off = _key_off(j)
        out = []
        for g in groups:
            p = jnp.exp(s_sc[g, pl.ds(off, KEY_TILE), :] - m[g])
            acc_sc[g] += _dot(v_t[g, :, pl.ds(off, KEY_TILE)], p.astype(BF16))
            out.append(lacc[g] + _fold(p, jnp.add))
        return tuple(out)

    lacc = _key_loop(nt, weighted, tuple(jnp.zeros((SUB, lanes_g), F32) for _ in groups))
    return [jnp.sum(lacc[g], axis=0, keepdims=True) for g in groups]


def _write_heads(o_ref, l, acc_sc, eye_ref, rows):
    eye = eye_ref[0:rows, 0:rows]
    for g in range(A_KV_HEADS):
        o_t = (acc_sc[g] * (1.0 / l[g])).astype(BF16)
        for hh in range(A_GROUP):
            h = g * A_GROUP + hh
            o_ref[0, :, h * A_HEAD_DIM:(h + 1) * A_HEAD_DIM] = _transpose_bf16(
                o_t[:, hh * rows:(hh + 1) * rows], eye).astype(o_ref.dtype)


_ATTN_SCRATCH = lambda rows, keys: [
    pltpu.VMEM((A_KV_HEADS, keys, A_HEAD_DIM), BF16),
    pltpu.VMEM((A_KV_HEADS, A_HEAD_DIM, keys), BF16),
    pltpu.VMEM((keys, IDX_DIM), BF16),
    pltpu.VMEM((IDX_DIM, IDX_HEADS * rows), BF16),
    pltpu.VMEM((1, IDX_HEADS * rows), F32),
    pltpu.VMEM((A_KV_HEADS, A_HEAD_DIM, A_GROUP * rows), BF16),
    pltpu.VMEM((keys, max(rows, LANES)), F32),
    pltpu.VMEM((A_KV_HEADS, keys, A_GROUP * rows), F32),
    pltpu.VMEM((A_KV_HEADS, A_HEAD_DIM, A_GROUP * rows), F32),
] + ([pltpu.VMEM((keys // (LANES // rows), LANES), F32)] if rows < LANES else [])


def _attn_prompt_kernel(iq_ref, q_ref, k_ref, v_ref, ikw_ref, ikwq_ref, lng_ref, lnb_ref, tril_ref, eye_ref,
                        o_ref, ikln_ref, kb, v_t, ikb, iq_t, w_full, q_t, sc, s_sc, acc_sc, *, rows):
    i = pl.program_id(1)

    @pl.when(i == 0)
    def _():
        kf = k_ref[0]
        for g in range(A_KV_HEADS):
            sl = slice(g * A_HEAD_DIM, (g + 1) * A_HEAD_DIM)
            kb[g] = kf[:, sl].astype(BF16)
            v_t[g] = v_ref[0, sl, :].astype(BF16)
        ln = _layer_norm(ikw_ref[0][:, :IDX_DIM], lng_ref[...], lnb_ref[...])
        ikln_ref[0] = ln.T
        ikb[...] = ln.astype(BF16)

    _stage_queries(iq_ref[0], q_ref[0], ikwq_ref[0][:, IDX_DIM:IDX_DIM + IDX_HEADS], eye_ref, iq_t, w_full, q_t, rows)

    nt = ((i + 1) * rows + KEY_TILE - 1) // KEY_TILE
    q_pos = i * rows + lax.broadcasted_iota(jnp.int32, (KEY_TILE, rows), 1)
    q_end = (q_pos // CHUNK + 1) * CHUNK

    def adm(j):
        return j * KEY_TILE + lax.broadcasted_iota(jnp.int32, (KEY_TILE, rows), 0) < q_end

    _index_scores(iq_t, w_full, ikb, sc, None, nt, adm, None, rows)
    _select_topk(sc, None, tril_ref, nt)
    l = _attend(sc, q_t, kb, v_t, s_sc, acc_sc, nt)
    _write_heads(o_ref, l, acc_sc, eye_ref, rows)


def _attn_prompt(pa3, pb3, v_t, idx_k_g, idx_k_b, tril, eye):
    b, s, _ = pa3.shape
    rows = 2 * LANES
    kernel = functools.partial(_attn_prompt_kernel, rows=rows)
    o_a, ik_t = pl.pallas_call(
        kernel,
        grid=(b, s // rows),
        in_specs=[
            pl.BlockSpec((1, rows, IDX_Q), lambda bi, i: (bi, i, CA_IQ // IDX_Q)),
            pl.BlockSpec((1, rows, A_Q), lambda bi, i: (bi, i, CA_Q // A_Q)),
            pl.BlockSpec((1, s, A_KV), lambda bi, i: (bi, 0, CB_K // A_KV)),
            pl.BlockSpec((1, A_KV, s), lambda bi, i: (bi, 0, 0)),
            pl.BlockSpec((1, s, LANES), lambda bi, i: (bi, 0, CB_IKW // LANES)),
            pl.BlockSpec((1, rows, LANES), lambda bi, i: (bi, i, CB_IKW // LANES)),
            pl.BlockSpec((1, IDX_DIM), lambda bi, i: (0, 0)),
            pl.BlockSpec((1, IDX_DIM), lambda bi, i: (0, 0)),
            pl.BlockSpec((KEY_TILE, KEY_TILE), lambda bi, i: (0, 0)),
            pl.BlockSpec((KEY_TILE, KEY_TILE), lambda bi, i: (0, 0)),
        ],
        out_specs=[
            pl.BlockSpec((1, rows, A_Q), lambda bi, i: (bi, i, 0)),
            pl.BlockSpec((1, IDX_DIM, s), lambda bi, i: (bi, 0, 0)),
        ],
        out_shape=[
            jax.ShapeDtypeStruct((b, s, A_Q), BF16),
            jax.ShapeDtypeStruct((b, IDX_DIM, s), F32),
        ],
        scratch_shapes=_ATTN_SCRATCH(rows, s),
        compiler_params=pltpu.CompilerParams(
            dimension_semantics=("arbitrary", "arbitrary"), vmem_limit_bytes=VMEM_LIMIT),
        name="attn_prompt",
    )(pa3, pa3, pb3, v_t, pb3, pb3, idx_k_g, idx_k_b, tril, eye)
    return o_a, jnp.transpose(ik_t, (0, 2, 1))


def _attn_sample_kernel(iq_ref, q_ref, k_ref, v_ref, ikw_ref, ck_ref, cv_ref, cik_ref, lng_ref, lnb_ref, tril_ref,
                        eye_ref, o_ref, ikln_ref, kb, v_t, ikb, iq_t, w_full, q_t, sc, s_sc, acc_sc, sc_pk,
                        *, rows, past, keys, pieces):
    new = rows
    pad = keys - past - new
    piece = pl.program_id(1)
    per_piece = past // pieces
    eye = eye_ref[0:A_HEAD_DIM, 0:A_HEAD_DIM]
    for c in range(0, per_piece, KEY_TILE):
        dst = pl.ds(pl.multiple_of(piece * per_piece + c, KEY_TILE), KEY_TILE)
        for g in range(A_KV_HEADS):
            kb[g, dst, :] = _transpose_bf16(ck_ref[0, g, :, c:c + KEY_TILE].astype(BF16), eye_ref[...]).astype(BF16)
            v_t[g, :, dst] = cv_ref[0, g, :, c:c + KEY_TILE].astype(BF16)
        ikb[dst, :] = _transpose_bf16(cik_ref[0, :, c:c + KEY_TILE].astype(BF16), eye_ref[...]).astype(BF16)

    @pl.when(piece == pieces - 1)
    def _():
        kf = k_ref[0]
        vf = v_ref[0]
        for g in range(A_KV_HEADS):
            sl = slice(g * A_HEAD_DIM, (g + 1) * A_HEAD_DIM)
            kb[g, past:past + new, :] = kf[:, sl].astype(BF16)
            kb[g, past + new:keys, :] = jnp.zeros((pad, A_HEAD_DIM), BF16)
            v_t[g, :, past:past + new] = _transpose_bf16(vf[:, sl].astype(BF16), eye).astype(BF16)
            v_t[g, :, past + new:keys] = jnp.zeros((A_HEAD_DIM, pad), BF16)
        ikw = ikw_ref[0]
        ln = _layer_norm(ikw[:, :IDX_DIM], lng_ref[...], lnb_ref[...])
        ikln_ref[0] = ln
        ikb[past:past + new, :] = ln.astype(BF16)
        ikb[past + new:keys, :] = jnp.zeros((pad, IDX_DIM), BF16)

        _stage_queries(iq_ref[0], q_ref[0], ikw[:, IDX_DIM:IDX_DIM + IDX_HEADS], eye_ref, iq_t, w_full, q_t, rows)

        nt = keys // KEY_TILE

        def adm(j):
            return j * KEY_TILE + lax.broadcasted_iota(jnp.int32, (KEY_TILE, LANES), 0) < past + new

        pk = KEY_TILE // 4
        group = lax.broadcasted_iota(jnp.int32, (pk, LANES), 1) // rows
        sub_block = jnp.zeros((pk, LANES), jnp.int32)
        for c, blk in enumerate(PACK_ORDER):
            sub_block = jnp.where(group == c, blk, sub_block)
        key_in_tile = sub_block * pk + lax.broadcasted_iota(jnp.int32, (pk, LANES), 0)

        def adm_pk(j):
            return j * KEY_TILE + key_in_tile < past + new

        _index_scores(iq_t, w_full, ikb, sc, sc_pk, nt, adm, adm_pk, rows)
        _select_topk(sc, sc_pk, tril_ref, nt)
        l = _attend(sc, q_t, kb, v_t, s_sc, acc_sc, nt)
        _write_heads(o_ref, l, acc_sc, eye_ref, rows)


def _attn_sample(pa3, pb3, cache_k, cache_v, cache_ik, idx_k_g, idx_k_b, tril, eye):
    b, t, _ = pa3.shape
    past = cache_ik.shape[2]
    pieces = 1
    keys = -(-(past + t) // KEY_TILE) * KEY_TILE
    kernel = functools.partial(_attn_sample_kernel, rows=t, past=past, keys=keys, pieces=pieces)
    return pl.pallas_call(
        kernel,
        grid=(b, pieces),
        in_specs=[
            pl.BlockSpec((1, t, IDX_Q), lambda bi, pc: (bi, 0, CA_IQ // IDX_Q)),
            pl.BlockSpec((1, t, A_Q), lambda bi, pc: (bi, 0, CA_Q // A_Q)),
            pl.BlockSpec((1, t, A_KV), lambda bi, pc: (bi, 0, CB_K // A_KV)),
            pl.BlockSpec((1, t, A_KV), lambda bi, pc: (bi, 0, CB_V // A_KV)),
            pl.BlockSpec((1, t, LANES), lambda bi, pc: (bi, 0, CB_IKW // LANES)),
            pl.BlockSpec((1, A_KV_HEADS, A_HEAD_DIM, past // pieces), lambda bi, pc: (bi, 0, 0, pc)),
            pl.BlockSpec((1, A_KV_HEADS, A_HEAD_DIM, past // pieces), lambda bi, pc: (bi, 0, 0, pc)),
            pl.BlockSpec((1, IDX_DIM, past // pieces), lambda bi, pc: (bi, 0, pc)),
            pl.BlockSpec((1, IDX_DIM), lambda bi, pc: (0, 0)),
            pl.BlockSpec((1, IDX_DIM), lambda bi, pc: (0, 0)),
            pl.BlockSpec((KEY_TILE, KEY_TILE), lambda bi, pc: (0, 0)),
            pl.BlockSpec((KEY_TILE, KEY_TILE), lambda bi, pc: (0, 0)),
        ],
        out_specs=[
            pl.BlockSpec((1, t, A_Q), lambda bi, pc: (bi, 0, 0)),
            pl.BlockSpec((1, t, IDX_DIM), lambda bi, pc: (bi, 0, 0)),
        ],
        out_shape=[
            jax.ShapeDtypeStruct((b, t, A_Q), BF16),
            jax.ShapeDtypeStruct((b, t, IDX_DIM), F32),
        ],
        scratch_shapes=_ATTN_SCRATCH(t, keys),
        compiler_params=pltpu.CompilerParams(
            dimension_semantics=("arbitrary", "arbitrary"), vmem_limit_bytes=VMEM_LIMIT),
        name="attn_sample",
    )(pa3, pa3, pb3, pb3, pb3, cache_k, cache_v, cache_ik, idx_k_g, idx_k_b, tril, eye)


def _rwkv_kernel(u_ref, shift_ref, s0_ref, mu_ref, w0_ref, a0_ref, kk_ref, ka_ref, rk_ref, gnw_ref, gnb_ref,
                 w2_ref, a2_ref, g2_ref, bd_ref, tri_ref, yb_ref, s_out_ref, carry, state, y_sc, *, tc):
    t = pl.program_id(1)
    u = u_ref[0]
    tb = u.shape[0]
    n = B_HEAD_DIM

    @pl.when(t == 0)
    def _():
        carry[...] = shift_ref[0]
        state[...] = s0_ref[0]

    row = lax.broadcasted_iota(jnp.int32, u.shape, 0)
    u_prev = jnp.where(row == 0, carry[...], pltpu.roll(u, 1, 0))
    carry[...] = u[tb - 1:tb, :]
    m = u + (u_prev - u) * mu_ref[...]

    r = m[:, 0:B_WIDTH]
    k = m[:, B_WIDTH:2 * B_WIDTH]
    v = m[:, 2 * B_WIDTH:3 * B_WIDTH]
    lora = m[:, 3 * B_WIDTH:3 * B_WIDTH + LANES]
    gl = m[:, 3 * B_WIDTH + LANES:]
    lane = lax.broadcasted_iota(jnp.int32, lora.shape, 1)
    lora = jnp.where(lane < DECAY_LORA, jnp.tanh(lora), lora).astype(BF16)
    z = w0_ref[...] + _dot(lora, w2_ref[...])
    softplus = jnp.maximum(-z, 0.0) + jnp.log(1.0 + jnp.exp(-jnp.abs(z)))
    lw = -jnp.exp(-softplus - 0.5)
    a = _sigmoid(a0_ref[...] + _dot(lora, a2_ref[...]))
    g = _dot(_sigmoid(gl).astype(BF16), g2_ref[...])

    kk = k * kk_ref[...]
    ss = _dot_exact_rhs(kk * kk, bd_ref[...])
    kk = kk / jnp.maximum(jnp.sqrt(ss), 1e-12)
    k2 = k * (1.0 + (a - 1.0) * ka_ref[...])
    bonus = _dot_exact_rhs(r * k2 * rk_ref[...], bd_ref[...]) * v

    cum = _dot_exact_lhs(tri_ref[...], lw)
    g_in = jnp.exp(cum)
    g_ex = jnp.exp(cum - lw)
    g_inv = jnp.exp(-cum)
    a_t = -kk * g_ex
    b_t = kk * a * g_inv
    k_t = k2 * g_inv
    r_t = r * g_in

    ri = lax.broadcasted_iota(jnp.int32, (tc, tc), 0)
    ci = lax.broadcasted_iota(jnp.int32, (tc, tc), 1)
    strict = ci < ri
    incl = ci <= ri
    eye = jnp.where(ci == ri, 1.0, 0.0)

    heads = range(B_HEADS)
    chunks = range(tb // tc)
    pairs = [(c, h) for c in chunks for h in heads]
    lanes = [slice(h * n, (h + 1) * n) for h in heads]
    toks = [slice(c * tc, (c + 1) * tc) for c in chunks]
    bf = lambda x: x.astype(BF16)
    g_end = [g_in[(c + 1) * tc - 1:(c + 1) * tc, :] for c in chunks]
    vh = {(c, h): v[toks[c], lanes[h]] for c, h in pairs}
    ar = {(c, h): _split2(jnp.concatenate([a_t[toks[c], lanes[h]], r_t[toks[c], lanes[h]]], axis=0))
          for c, h in pairs}
    bk = {(c, h): jnp.concatenate([b_t[toks[c], lanes[h]], k_t[toks[c], lanes[h]]], axis=0) for c, h in pairs}
    cross = {p: _dot(ar[p][0], bf(bk[p]), trans_b=True) for p in pairs}
    l_ab = {p: jnp.where(strict, cross[p][0:tc, 0:tc], 0.0) for p in pairs}
    l_akv = {p: _dot(bf(jnp.where(strict, cross[p][0:tc, tc:2 * tc], 0.0)), bf(vh[p])) for p in pairs}
    m_rbk = {p: bf(jnp.concatenate([jnp.where(incl, cross[p][tc:2 * tc, 0:tc], 0.0),
                                    jnp.where(incl, cross[p][tc:2 * tc, tc:2 * tc], 0.0)], axis=1)) for p in pairs}

    inv = {p: eye + l_ab[p] for p in pairs}
    pw = l_ab
    span = 2
    while span < tc:
        pwb = {p: bf(pw[p]) for p in pairs}
        pw = {p: _dot(pwb[p], pwb[p]) for p in pairs}
        inv = {p: inv[p] + _dot(bf(inv[p]), bf(pw[p])) for p in pairs}
        span *= 2
    inv = {p: bf(inv[p]) for p in pairs}

    s_cur = [state[h] for h in heads]
    for c in chunks:
        from_state = [_dot3s(ar[c, h], _split2(s_cur[h]), trans_b=True) for h in heads]
        z = [_dot(inv[c, h], bf(from_state[h][0:tc] + l_akv[c, h])) for h in heads]
        zv = [jnp.concatenate([z[h], vh[c, h]], axis=0) for h in heads]
        y = [from_state[h][tc:2 * tc] + _dot(m_rbk[c, h], bf(zv[h])) for h in heads]
        upd = [_dot3(zv[h].T, bk[c, h] * g_end[c][:, lanes[h]]) for h in heads]
        s_cur = [s_cur[h] * g_end[c][:, lanes[h]] + upd[h] for h in heads]
        for h in heads:
            y_sc[toks[c], lanes[h]] = y[h]
    for h in heads:
        state[h] = s_cur[h]

    y = y_sc[...]
    mean = _dot_exact_rhs(y, bd_ref[...]) * (1.0 / n)
    d = y - mean
    var = _dot_exact_rhs(d * d, bd_ref[...]) * (1.0 / n)
    yn = d * lax.rsqrt(var + GN_EPS) * gnw_ref[...] + gnb_ref[...]
    yb_ref[0] = ((yn + bonus) * g).astype(yb_ref.dtype)

    @pl.when(t == pl.num_programs(1) - 1)
    def _():
        s_out_ref[0] = state[...]


def _rwkv(pb3, shift_prev, s0, p, tc):
    b, s, _ = pb3.shape
    tb = min(s, 4 * tc)
    tri = jnp.asarray(np.kron(np.eye(tb // tc), np.tril(np.ones((tc, tc)))), BF16)
    vec = lambda w: pl.BlockSpec((1, w), lambda bi, t: (0, 0))
    mat = lambda r, c: pl.BlockSpec((r, c), lambda bi, t: (0, 0))
    st = pl.BlockSpec((1, B_HEADS, B_HEAD_DIM, B_HEAD_DIM), lambda bi, t: (bi, 0, 0, 0))
    return pl.pallas_call(
        functools.partial(_rwkv_kernel, tc=tc),
        grid=(b, s // tb),
        in_specs=[
            pl.BlockSpec((1, tb, SHIFT_W), lambda bi, t: (bi, t, CB_U // SHIFT_W)),
            pl.BlockSpec((1, 1, SHIFT_W), lambda bi, t: (bi, 0, 0)),
            st,
            vec(SHIFT_W), vec(B_WIDTH), vec(B_WIDTH), vec(B_WIDTH), vec(B_WIDTH), vec(B_WIDTH),
            vec(B_WIDTH), vec(B_WIDTH),
            mat(LANES, B_WIDTH), mat(LANES, B_WIDTH), mat(GATE_LORA, B_WIDTH), mat(B_WIDTH, B_WIDTH), mat(tb, tb),
        ],
        out_specs=[pl.BlockSpec((1, tb, B_WIDTH), lambda bi, t: (bi, t, 0)), st],
        out_shape=[jax.ShapeDtypeStruct((b, s, B_WIDTH), BF16),
                   jax.ShapeDtypeStruct((b, B_HEADS, B_HEAD_DIM, B_HEAD_DIM), F32)],
        scratch_shapes=[pltpu.VMEM((1, SHIFT_W), F32),
                        pltpu.VMEM((B_HEADS, B_HEAD_DIM, B_HEAD_DIM), F32),
                        pltpu.VMEM((tb, B_WIDTH), F32)],
        compiler_params=pltpu.CompilerParams(
            dimension_semantics=("arbitrary", "arbitrary"), vmem_limit_bytes=VMEM_LIMIT),
        name="rwkv",
    )(pb3, shift_prev, s0, p['shift_mu'], p['w0'], p['a0'], p['k_k'], p['k_a'], p['r_k'], p['gn_w'], p['gn_b'],
      p['w2p'], p['a2p'], p['g2'], p['bd'], tri)


def _merge_kernel(oa_ref, yb_ref, ga_ref, gb_ref, x_ref, woa_ref, wob_ref, wout_ref, h_ref):
    merged = (_sigmoid(ga_ref[...].astype(F32)) * _dot(oa_ref[...], woa_ref[...])
              + _sigmoid(gb_ref[...].astype(F32)) * _dot(yb_ref[...], wob_ref[...]))
    h_ref[...] = x_ref[...] + _dot(merged.astype(BF16), wout_ref[...])


def _merge(o_a, y_b, pa, x2, p):
    n = x2.shape[0]
    tm = min(n, 512)
    row = lambda w, j=0: pl.BlockSpec((tm, w), lambda i: (i, j))
    full = lambda r, c: pl.BlockSpec((r, c), lambda i: (0, 0))
    return pl.pallas_call(
        _merge_kernel,
        grid=(n // tm,),
        in_specs=[
            row(A_Q), row(B_WIDTH),
            row(D_MODEL, CA_GA // D_MODEL), row(D_MODEL, CA_GB // D_MODEL), row(D_MODEL),
            full(A_Q, D_MODEL), full(B_WIDTH, D_MODEL), full(D_MODEL, D_MODEL),
        ],
        out_specs=row(D_MODEL),
        out_shape=jax.ShapeDtypeStruct((n, D_MODEL), F32),
        compiler_params=pltpu.CompilerParams(
            dimension_semantics=("arbitrary",), vmem_limit_bytes=VMEM_LIMIT),
        name="merge_out_proj",
    )(o_a, y_b, pa, pa, x2, p['w_oa'], p['w_ob'], p['w_out'])


def _ffn_kernel(h_ref, n2_ref, nf_ref, wg_ref, wu_ref, wd_ref, o_ref):
    h = h_ref[...]
    hn = _rms(h, n2_ref[...]).astype(BF16)
    out = h
    for c, n in _col_chunks(D_FF, 4 * MXU_COLS):
        gate = _dot(hn, wg_ref[:, c:c + n])
        up = _dot(hn, wu_ref[:, c:c + n])
        act = (gate * _sigmoid(gate) * up).astype(BF16)
        out = out + _dot(act, wd_ref[c:c + n, :])
    o_ref[...] = _rms(out, nf_ref[...])


def _ffn(h, p, norm_f):
    n = h.shape[0]
    tm = min(n, 512)
    resident = lambda r, c: pl.BlockSpec((r, c), lambda i: (0, 0), pipeline_mode=pl.Buffered(1))
    return pl.pallas_call(
        _ffn_kernel,
        grid=(n // tm,),
        in_specs=[
            pl.BlockSpec((tm, D_MODEL), lambda i: (i, 0)),
            pl.BlockSpec((1, D_MODEL), lambda i: (0, 0)),
            pl.BlockSpec((1, D_MODEL), lambda i: (0, 0)),
            resident(D_MODEL, D_FF), resident(D_MODEL, D_FF), resident(D_FF, D_MODEL),
        ],
        out_specs=pl.BlockSpec((tm, D_MODEL), lambda i: (i, 0)),
        out_shape=jax.ShapeDtypeStruct((n, D_MODEL), F32),
        compiler_params=pltpu.CompilerParams(
            dimension_semantics=("arbitrary",), vmem_limit_bytes=VMEM_LIMIT),
        name="ffn_final_norm",
    )(h, p['norm2'], norm_f, p['w_gate'], p['w_up'], p['w_down'])


def _prepare_params(l, norm1, w_in, idx_k_g, idx_k_b, shift_mu, w0, w2, a0, a2, g2, k_k, k_a, r_k,
                    gn_w, gn_b, w_oa, w_ob, w_out, norm2, w_gate, w_up, w_down):
    w = w_in[l]
    o = np.cumsum([0, A_Q, A_KV, A_KV, IDX_Q, IDX_DIM, IDX_HEADS, SHIFT_W, D_MODEL, D_MODEL])
    seg = lambda i: w[:, o[i]:o[i + 1]]
    pad = jnp.zeros((D_MODEL, LANES - IDX_DIM - IDX_HEADS), w.dtype)
    w_in_p = jnp.concatenate([seg(3), seg(7), seg(8), seg(0), seg(6), seg(1), seg(2), seg(4), seg(5), pad],
                             axis=1).astype(BF16)
    zeros = jnp.zeros((LANES - DECAY_LORA, B_WIDTH), F32)
    head = np.arange(B_WIDTH) // B_HEAD_DIM
    row = lambda x: x[l].reshape(1, -1)
    return {
        'norm1': row(norm1), 'w_in_p': w_in_p,
        'idx_k_g': row(idx_k_g), 'idx_k_b': row(idx_k_b),
        'shift_mu': row(shift_mu), 'w0': row(w0), 'a0': row(a0),
        'k_k': row(k_k), 'k_a': row(k_a), 'r_k': row(r_k), 'gn_w': row(gn_w), 'gn_b': row(gn_b),
        'w2p': jnp.concatenate([w2[l], zeros], axis=0).astype(BF16),
        'a2p': jnp.concatenate([zeros, a2[l]], axis=0).astype(BF16),
        'g2': g2[l].astype(BF16),
        'bd': jnp.asarray(head[:, None] == head[None, :], BF16),
        'w_oa': w_oa[l].astype(BF16), 'w_ob': w_ob[l].astype(BF16), 'w_out': w_out[l].astype(BF16),
        'norm2': row(norm2),
        'w_gate': w_gate[l].astype(BF16), 'w_up': w_up[l].astype(BF16), 'w_down': w_down[l].astype(BF16),
    }


def _layer(x, attend, shift_prev, wkv0, p, norm_f, chunk):
    b, t, _ = x.shape
    x2 = x.reshape(b * t, D_MODEL)
    pa, pb, *kv_t = _in_proj(x2, p['norm1'], p['w_in_p'], t)
    pa3 = pa.reshape(b, t, PA_WIDTH)
    pb3 = pb.reshape(b, t, PB_WIDTH)
    o_a, ik_ln = attend(pa3, pb3, kv_t[1] if kv_t else None)
    y_b, wkv_new = _rwkv(pb3, shift_prev, wkv0, p, chunk)
    flat = lambda z: z.reshape(b * t, z.shape[-1])
    h = _merge(flat(o_a), flat(y_b), pa, x2, p)
    out = _ffn(h, p, norm_f).reshape(b, t, D_MODEL)
    if kv_t:
        k_new, v_new = (jnp.transpose(z.reshape(b, A_KV_HEADS, A_HEAD_DIM, t), (0, 3, 1, 2)) for z in kv_t)
    else:
        k_new = pb3[:, :, CB_K:CB_K + A_KV].reshape(b, t, A_KV_HEADS, A_HEAD_DIM)
        v_new = pb3[:, :, CB_V:CB_V + A_KV].reshape(b, t, A_KV_HEADS, A_HEAD_DIM)
    shift_new = pb3[:, t - 1:t, CB_U:CB_U + SHIFT_W]
    return out, k_new, v_new, ik_ln, wkv_new, shift_new


def kernel(x_prompt, x_sample, cache_k, cache_v, cache_idx_k, state_wkv, state_shift,
           norm1, w_in, idx_k_g, idx_k_b, shift_mu, w0, w2, a0, a2, g2, k_k, k_a, r_k,
           gn_w, gn_b, w_oa, w_ob, w_out, norm2, w_gate, w_up, w_down, norm_f):
    assert w_in.shape[0] == 1, "single-layer kernel"
    l = 0
    p = _prepare_params(l, norm1, w_in, idx_k_g, idx_k_b, shift_mu, w0, w2, a0, a2, g2, k_k, k_a, r_k,
                        gn_w, gn_b, w_oa, w_ob, w_out, norm2, w_gate, w_up, w_down)
    nf = norm_f.reshape(1, -1)
    tril = jnp.asarray(np.tril(np.ones((KEY_TILE, KEY_TILE), np.float32)), BF16)
    eye = jnp.asarray(np.eye(KEY_TILE, dtype=np.float32), BF16)

    n_p = x_prompt.shape[0]
    shift0 = jnp.zeros((n_p, 1, SHIFT_W), F32)
    wkv_zero = jnp.zeros((n_p, B_HEADS, B_HEAD_DIM, B_HEAD_DIM), F32)
    attend_p = lambda pa3, pb3, v_t: _attn_prompt(pa3, pb3, v_t, p['idx_k_g'], p['idx_k_b'], tril, eye)
    y_p, k_p, v_p, ik_p, wkv_p, shift_p = _layer(x_prompt, attend_p, shift0, wkv_zero, p, nf, 64)

    n_s, t_s = x_sample.shape[0], x_sample.shape[1]
    past = cache_k.shape[2]
    ck = jnp.transpose(cache_k[l], (0, 2, 3, 1))
    cv = jnp.transpose(cache_v[l], (0, 2, 3, 1))
    cik = jnp.transpose(cache_idx_k[l], (0, 2, 1))
    attend_s = lambda pa3, pb3, v_t: _attn_sample(pa3, pb3, ck, cv, cik, p['idx_k_g'], p['idx_k_b'], tril, eye)
    y_s, k_s, v_s, ik_s, wkv_s, shift_s = _layer(x_sample, attend_s, state_shift[l], state_wkv[l], p, nf, t_s)

    lead = lambda z: z[None]
    return (y_p, y_s, lead(k_p), lead(v_p), lead(ik_p), lead(wkv_p), lead(shift_p),
            lead(k_s), lead(v_s), lead(ik_s), lead(wkv_s), lead(shift_s))
```

```python
import functools

import jax
import jax.numpy as jnp
import numpy as np
from jax import lax
from jax.experimental import pallas as pl
from jax.experimental.pallas import tpu as pltpu

F32 = jnp.float32
BF16 = jnp.bfloat16

D_MODEL = 1024
CHUNK = 64
A_HEADS = 8
A_KV_HEADS = 2
A_GROUP = A_HEADS // A_KV_HEADS
A_HEAD_DIM = 64
A_Q = A_HEADS * A_HEAD_DIM
A_KV = A_KV_HEADS * A_HEAD_DIM
IDX_HEADS = 16
IDX_DIM = 64
IDX_Q = IDX_HEADS * IDX_DIM
TOPK_MAX = 256
B_HEADS = 8
B_HEAD_DIM = 64
B_WIDTH = B_HEADS * B_HEAD_DIM
DECAY_LORA = 64
AAA_LORA = 64
GATE_LORA = 128
SHIFT_W = 3 * B_WIDTH + DECAY_LORA + AAA_LORA + GATE_LORA
D_FF = 2816
RMS_EPS = 1e-6
LN_EPS = 1e-6
GN_EPS = 64e-5

LANES = 128
SUB = 8
VMEM_LIMIT = 56 * 1024 * 1024
NEG = -1e30
KEY_TILE = 256

CA_IQ = 0
CA_GA = CA_IQ + IDX_Q
CA_GB = CA_GA + D_MODEL
CA_Q = CA_GB + D_MODEL
PA_WIDTH = CA_Q + A_Q
CB_U = 0
CB_K = CB_U + SHIFT_W
CB_V = CB_K + A_KV
CB_IKW = CB_V + A_KV
PB_WIDTH = CB_IKW + LANES
MXU_COLS = 256


def _dot(a, b, trans_b=False):
    dn = (((1,), (1 if trans_b else 0,)), ((), ()))
    return lax.dot_general(a, b, dn, preferred_element_type=F32)


def _split2(x):
    hi = x.astype(BF16)
    lo = (x - hi.astype(F32)).astype(BF16)
    return hi, lo


def _dot3s(a_split, b_split, trans_b=False):
    ah, al = a_split
    bh, bl = b_split
    return _dot(ah, bh, trans_b) + (_dot(ah, bl, trans_b) + _dot(al, bh, trans_b))


def _dot3(a, b, trans_b=False):
    return _dot3s(_split2(a), _split2(b), trans_b)


def _dot_exact_rhs(a, b_bf16, terms=2):
    out = None
    rem = a
    for _ in range(terms):
        part = rem.astype(BF16)
        rem = rem - part.astype(F32)
        d = _dot(part, b_bf16)
        out = d if out is None else out + d
    return out


def _dot_exact_lhs(a_bf16, b, terms=3, trans_b=False):
    out = None
    rem = b
    for _ in range(terms):
        part = rem.astype(BF16)
        rem = rem - part.astype(F32)
        d = _dot(a_bf16, part, trans_b)
        out = d if out is None else out + d
    return out


def _sigmoid(x):
    return 1.0 / (1.0 + jnp.exp(-x))


def _rms(x, g):
    ms = jnp.mean(x * x, axis=-1, keepdims=True)
    return x * lax.rsqrt(ms + RMS_EPS) * g


def _col_chunks(width, chunk):
    return [(c, min(chunk, width - c)) for c in range(0, width, chunk)]


QT_WIDTH = IDX_Q + A_Q


def _in_proj_kernel(x_ref, g_ref, w_ref, eye_ref, pa_ref, pb_ref, *t_refs):
    xn = _rms(x_ref[...], g_ref[...]).astype(BF16)
    for c, n in _col_chunks(PA_WIDTH, 2 * MXU_COLS):
        pa_ref[:, c:c + n] = _dot(xn, w_ref[:, c:c + n]).astype(BF16)
    for c, n in _col_chunks(PB_WIDTH, 2 * MXU_COLS):
        pb_ref[:, c:c + n] = _dot(xn, w_ref[:, PA_WIDTH + c:PA_WIDTH + c + n])
    if t_refs:
        kt_ref, vt_ref, qt_ref = t_refs
        kt_ref[0] = pb_ref[:, CB_K:CB_K + A_KV].T
        vt_ref[0] = pb_ref[:, CB_V:CB_V + A_KV].T
        for dst, src in ((0, CA_IQ), (IDX_Q, CA_Q)):
            for c, n in _col_chunks(IDX_Q if src == CA_IQ else A_Q, MXU_COLS):
                qt_ref[0, dst + c:dst + c + n, :] = _transpose_bf16(
                    pa_ref[:, src + c:src + c + n], eye_ref[0:n, 0:n]).astype(BF16)


def _in_proj(x2, norm1, w_in_p, eye, seq):
    n = x2.shape[0]
    tm = min(n, 512)
    out_specs = [pl.BlockSpec((tm, PA_WIDTH), lambda i: (i, 0)), pl.BlockSpec((tm, PB_WIDTH), lambda i: (i, 0))]
    out_shape = [jax.ShapeDtypeStruct((n, PA_WIDTH), BF16), jax.ShapeDtypeStruct((n, PB_WIDTH), F32)]
    if seq % tm == 0:
        per_seq = seq // tm
        t_spec = lambda w: pl.BlockSpec((1, w, tm), lambda i: (i // per_seq, 0, i % per_seq))
        out_specs += [t_spec(A_KV), t_spec(A_KV), t_spec(QT_WIDTH)]
        out_shape += [jax.ShapeDtypeStruct((n // seq, A_KV, seq), F32)] * 2
        out_shape += [jax.ShapeDtypeStruct((n // seq, QT_WIDTH, seq), BF16)]
    return pl.pallas_call(
        _in_proj_kernel,
        grid=(n // tm,),
        in_specs=[
            pl.BlockSpec((tm, D_MODEL), lambda i: (i, 0)),
            pl.BlockSpec((1, D_MODEL), lambda i: (0, 0)),
            pl.BlockSpec((D_MODEL, PA_WIDTH + PB_WIDTH), lambda i: (0, 0), pipeline_mode=pl.Buffered(1)),
            pl.BlockSpec((KEY_TILE, KEY_TILE), lambda i: (0, 0)),
        ],
        out_specs=out_specs,
        out_shape=out_shape,
        compiler_params=pltpu.CompilerParams(
            dimension_semantics=("arbitrary",), vmem_limit_bytes=VMEM_LIMIT),
        name="in_proj",
    )(x2, norm1, w_in_p, eye)


def _layer_norm(x, g, b):
    mu = jnp.mean(x, axis=-1, keepdims=True)
    d = x - mu
    var = jnp.mean(d * d, axis=-1, keepdims=True)
    return d * lax.rsqrt(var + LN_EPS) * g + b


def _transpose_bf16(x, eye):
    return _dot(eye, x, trans_b=True)


def _key_loop(nt, body, init):
    if isinstance(nt, int):
        carry = init
        for j in range(nt):
            carry = body(j, carry)
        return carry
    return lax.fori_loop(0, nt, body, init)


def _key_off(j):
    return j * KEY_TILE if isinstance(j, int) else pl.multiple_of(j * KEY_TILE, KEY_TILE)


def _fold(x, op):
    x = x.reshape(x.shape[0] // SUB, SUB, x.shape[1])
    while x.shape[0] > 1:
        half = x.shape[0] // 2
        x = op(x[:half], x[half:])
    return x[0]


def _stage_queries(iq, q, iw, eye_ref, iq_t, w_full, q_t, rows, transposed):
    eye = eye_ref[0:IDX_DIM, 0:IDX_DIM]

    def head_t(x, h, dim):
        if transposed:
            return x[h * dim:(h + 1) * dim, :] * (dim ** -0.5)
        return _transpose_bf16((x[:, h * dim:(h + 1) * dim] * (dim ** -0.5)).astype(BF16), eye)

    for h in range(IDX_HEADS):
        iq_t[:, h * rows:(h + 1) * rows] = head_t(iq, h, IDX_DIM).astype(BF16)
    w_t = _dot_exact_lhs(eye_ref[0:IDX_HEADS, 0:IDX_HEADS], iw, trans_b=True)
    for h in range(IDX_HEADS):
        w_full[:, h * rows:(h + 1) * rows] = w_t[h:h + 1, :] * (IDX_HEADS ** -0.5)
    for g in range(A_KV_HEADS):
        for hh in range(A_GROUP):
            q_t[g, :, hh * rows:(hh + 1) * rows] = head_t(q, g * A_GROUP + hh, A_HEAD_DIM).astype(BF16)


PACK_ORDER = (0, 2, 1, 3)


def _pack_quarters(acc):
    n = acc.shape[0] // 4
    a, b, c, d = (acc[i * n:(i + 1) * n] for i in range(4))
    lane = lax.broadcasted_iota(jnp.int32, (n, LANES), 1)
    low64 = lane < LANES // 2
    low32 = (lane & (LANES // 4)) == 0
    ab = jnp.where(low64, a, b) + pltpu.roll(jnp.where(low64, b, a), LANES // 2, 1)
    cd = jnp.where(low64, c, d) + pltpu.roll(jnp.where(low64, d, c), LANES // 2, 1)
    t1 = ab + pltpu.roll(ab, 3 * LANES // 4, 1)
    t2 = cd + pltpu.roll(cd, LANES // 4, 1)
    return jnp.where(low32, t1, t2)


def _index_scores(iq_t, w_full, ikb, sc, sc_pk, nt, adm_fn, adm_pk_fn, rows):
    width = sc.shape[1]

    def body(j, carry):
        off = _key_off(j)
        lg = _dot(ikb[pl.ds(off, KEY_TILE), :], iq_t[...])
        acc = None
        for c in range(IDX_HEADS * rows // width):
            x = jnp.maximum(lg[:, c * width:(c + 1) * width], 0.0) * w_full[:, c * width:(c + 1) * width]
            acc = x if acc is None else acc + x
        if sc_pk is not None:
            assert LANES // rows == 4
            pk = KEY_TILE // 4
            sc_pk[pl.ds(j * pk, pk), :] = jnp.where(adm_pk_fn(j), _pack_quarters(acc), -jnp.inf)
        shift = LANES // 2
        while shift >= rows:
            acc = acc + pltpu.roll(acc, shift, 1)
            shift //= 2
        sc[pl.ds(off, KEY_TILE), :] = jnp.where(adm_fn(j), acc, -jnp.inf)
        return carry

    _key_loop(nt, body, 0)


def _select_topk(sc, sc_pk, tril_ref, nt):
    shape = (SUB, sc.shape[1])
    inf = jnp.float32(jnp.inf)
    src = sc if sc_pk is None else sc_pk
    rows_t = KEY_TILE if sc_pk is None else KEY_TILE // 4

    def tile(j):
        start = _key_off(j) if sc_pk is None else j * rows_t
        return src[pl.ds(start, rows_t), :]

    def allsub(x, red):
        op = {jnp.sum: jnp.add, jnp.min: jnp.minimum, jnp.max: jnp.maximum}[red]
        if sc_pk is not None:
            x = op(op(x, pltpu.roll(x, LANES // 4, 1)),
                   op(pltpu.roll(x, LANES // 2, 1), pltpu.roll(x, 3 * LANES // 4, 1)))
        return jnp.broadcast_to(red(x, axis=0, keepdims=True), shape)

    def count_ge(t):
        t1 = t[0:1, :]
        acc = _key_loop(
            nt, lambda j, a: a + _fold(jnp.where(tile(j) >= t1, 1.0, 0.0), jnp.add), jnp.zeros(shape, F32))
        return allsub(acc, jnp.sum)

    def stats(j, c):
        mn, mx, na = c
        s = tile(j)
        fin = s > -inf
        return (jnp.minimum(mn, _fold(jnp.where(fin, s, inf), jnp.minimum)), jnp.maximum(mx, _fold(s, jnp.maximum)),
                na + _fold(jnp.where(fin, 1.0, 0.0), jnp.add))

    mn, mx, na = _key_loop(
        nt, stats, (jnp.full(shape, inf, F32), jnp.full(shape, -inf, F32), jnp.zeros(shape, F32)))
    lo0 = allsub(mn, jnp.min)
    mx = allsub(mx, jnp.max)
    n_adm = allsub(na, jnp.sum)
    kq = jnp.minimum(n_adm, float(TOPK_MAX))
    hi0 = mx + (jnp.abs(mx) * 1e-6 + 1e-30)

    def cond(c):
        return jnp.logical_and(c[0] < 400, c[5] > 0.5)

    def probe(x, lo, hi, c_lo, c_hi):
        c_x = count_ge(x)
        ge = c_x >= kq
        return jnp.where(ge, x, lo), jnp.where(ge, hi, x), jnp.where(ge, c_x, c_lo), jnp.where(ge, c_hi, c_x)

    def body(c):
        it, lo, hi, c_lo, c_hi, _ = c
        for _ in range(2):
            lo, hi, c_lo, c_hi = probe(0.5 * (lo + hi), lo, hi, c_lo, c_hi)
        nxt = 0.5 * (lo + hi)
        active = jnp.where(c_lo - c_hi > 1.5, jnp.where(nxt > lo, jnp.where(nxt < hi, 1.0, 0.0), 0.0), 0.0)
        active = jnp.where(n_adm > kq, active, 0.0)
        return it + 1, lo, hi, c_lo, c_hi, jnp.max(active)

    _, lo, _, c_lo, c_hi, _ = lax.while_loop(
        cond, body, (jnp.int32(0), lo0, hi0, n_adm, jnp.zeros(shape, F32), jnp.float32(1.0)))

    lo1 = lo[0:1, :]
    thr = allsub(
        _key_loop(nt, lambda j, a: jnp.minimum(a, _fold(jnp.where(tile(j) >= lo1, tile(j), inf), jnp.minimum)),
                  jnp.full(shape, inf, F32)),
        jnp.min)[0:1, :]
    take = (kq - c_hi)[0:1, :]
    has_tie = jnp.max(c_lo - kq) > 0.5

    @pl.when(jnp.logical_not(has_tie))
    def _():
        def wr(j, carry):
            off = _key_off(j)
            sc[pl.ds(off, KEY_TILE), :] = jnp.where(sc[pl.ds(off, KEY_TILE), :] >= thr, 0.0, NEG)
            return carry
        _key_loop(nt, wr, 0)

    @pl.when(has_tie)
    def _():
        def wr(j, seen):
            off = _key_off(j)
            s = sc[pl.ds(off, KEY_TILE), :]
            tie = jnp.where(s == thr, 1.0, 0.0)
            rank = _dot(tril_ref[...], tie.astype(BF16)) + seen
            keep_tie = jnp.where(s == thr, jnp.where(rank <= take, 0.0, NEG), NEG)
            sc[pl.ds(off, KEY_TILE), :] = jnp.where(s > thr, 0.0, keep_tie)
            return seen + jnp.sum(tie, axis=0, keepdims=True)
        lax.fori_loop(0, nt, wr, jnp.zeros((1, sc.shape[1]), F32))


def _attend(sc, q_t, kb, v_t, s_sc, acc_sc, nt):
    lanes_g = q_t.shape[2]
    rep = lanes_g // sc.shape[1]
    groups = range(A_KV_HEADS)

    def scores(j, macc):
        off = _key_off(j)
        bias = sc[pl.ds(off, KEY_TILE), :]
        bias = bias if rep == 1 else jnp.concatenate([bias] * rep, axis=1)
        out = []
        for g in groups:
            s = _dot(kb[g, pl.ds(off, KEY_TILE), :], q_t[g]) + bias
            s_sc[g, pl.ds(off, KEY_TILE), :] = s
            out.append(jnp.maximum(macc[g], _fold(s, jnp.maximum)))
        return tuple(out)

    macc = _key_loop(nt, scores, tuple(jnp.full((SUB, lanes_g), NEG, F32) for _ in groups))
    m = [jnp.max(macc[g], axis=0, keepdims=True) for g in groups]
    acc_sc[...] = jnp.zeros(acc_sc.shape, F32)

    def weighted(j, lacc):
        off = _key_off(j)
        out = []
        for g in groups:
            p = jnp.exp(s_sc[g, pl.ds(off, KEY_TILE), :] - m[g])
            acc_sc[g] += _dot(v_t[g, :, pl.ds(off, KEY_TILE)], p.astype(BF16))
            out.append(lacc[g] + _fold(p, jnp.add))
        return tuple(out)

    lacc = _key_loop(nt, weighted, tuple(jnp.zeros((SUB, lanes_g), F32) for _ in groups))
    return [jnp.sum(lacc[g], axis=0, keepdims=True) for g in groups]


def _write_heads(o_ref, l, acc_sc, eye_ref, rows):
    eye = eye_ref[0:rows, 0:rows]
    for g in range(A_KV_HEADS):
        o_t = (acc_sc[g] * (1.0 / l[g])).astype(BF16)
        for hh in range(A_GROUP):
            h = g * A_GROUP + hh
            o_ref[0, :, h * A_HEAD_DIM:(h + 1) * A_HEAD_DIM] = _transpose_bf16(
                o_t[:, hh * rows:(hh + 1) * rows], eye).astype(o_ref.dtype)


_ATTN_SCRATCH = lambda rows, keys: [
    pltpu.VMEM((A_KV_HEADS, keys, A_HEAD_DIM), BF16),
    pltpu.VMEM((A_KV_HEADS, A_HEAD_DIM, keys), BF16),
    pltpu.VMEM((keys, IDX_DIM), BF16),
    pltpu.VMEM((IDX_DIM, IDX_HEADS * rows), BF16),
    pltpu.VMEM((1, IDX_HEADS * rows), F32),
    pltpu.VMEM((A_KV_HEADS, A_HEAD_DIM, A_GROUP * rows), BF16),
    pltpu.VMEM((keys, max(rows, LANES)), F32),
    pltpu.VMEM((A_KV_HEADS, keys, A_GROUP * rows), F32),
    pltpu.VMEM((A_KV_HEADS, A_HEAD_DIM, A_GROUP * rows), F32),
] + ([pltpu.VMEM((keys // (LANES // rows), LANES), F32)] if rows < LANES else [])


def _attn_prompt_kernel(iq_ref, q_ref, k_ref, v_ref, ikw_ref, ikwq_ref, lng_ref, lnb_ref, tril_ref, eye_ref,
                        o_ref, ikln_ref, kb, v_t, ikb, iq_t, w_full, q_t, sc, s_sc, acc_sc, *, rows):
    i = pl.program_id(1)

    @pl.when(i == 0)
    def _():
        kf = k_ref[0]
        for g in range(A_KV_HEADS):
            sl = slice(g * A_HEAD_DIM, (g + 1) * A_HEAD_DIM)
            kb[g] = kf[:, sl].astype(BF16)
            v_t[g] = v_ref[0, sl, :].astype(BF16)
        ln = _layer_norm(ikw_ref[0][:, :IDX_DIM], lng_ref[...], lnb_ref[...])
        ikln_ref[0] = ln.T
        ikb[...] = ln.astype(BF16)

    _stage_queries(iq_ref[0], q_ref[0], ikwq_ref[0][:, IDX_DIM:IDX_DIM + IDX_HEADS], eye_ref, iq_t, w_full, q_t, rows,
                   transposed=True)

    nt = ((i + 1) * rows + KEY_TILE - 1) // KEY_TILE
    q_pos = i * rows + lax.broadcasted_iota(jnp.int32, (KEY_TILE, rows), 1)
    q_end = (q_pos // CHUNK + 1) * CHUNK

    def adm(j):
        return j * KEY_TILE + lax.broadcasted_iota(jnp.int32, (KEY_TILE, rows), 0) < q_end

    _index_scores(iq_t, w_full, ikb, sc, None, nt, adm, None, rows)
    _select_topk(sc, None, tril_ref, nt)
    l = _attend(sc, q_t, kb, v_t, s_sc, acc_sc, nt)
    _write_heads(o_ref, l, acc_sc, eye_ref, rows)


def _attn_prompt(q_t, pb3, v_t, idx_k_g, idx_k_b, tril, eye):
    b, s, _ = pb3.shape
    rows = 2 * LANES
    kernel = functools.partial(_attn_prompt_kernel, rows=rows)
    o_a, ik_t = pl.pallas_call(
        kernel,
        grid=(b, s // rows),
        in_specs=[
            pl.BlockSpec((1, IDX_Q, rows), lambda bi, i: (bi, 0, i)),
            pl.BlockSpec((1, A_Q, rows), lambda bi, i: (bi, IDX_Q // A_Q, i)),
            pl.BlockSpec((1, s, A_KV), lambda bi, i: (bi, 0, CB_K // A_KV)),
            pl.BlockSpec((1, A_KV, s), lambda bi, i: (bi, 0, 0)),
            pl.BlockSpec((1, s, LANES), lambda bi, i: (bi, 0, CB_IKW // LANES)),
            pl.BlockSpec((1, rows, LANES), lambda bi, i: (bi, i, CB_IKW // LANES)),
            pl.BlockSpec((1, IDX_DIM), lambda bi, i: (0, 0)),
            pl.BlockSpec((1, IDX_DIM), lambda bi, i: (0, 0)),
            pl.BlockSpec((KEY_TILE, KEY_TILE), lambda bi, i: (0, 0)),
            pl.BlockSpec((KEY_TILE, KEY_TILE), lambda bi, i: (0, 0)),
        ],
        out_specs=[
            pl.BlockSpec((1, rows, A_Q), lambda bi, i: (bi, i, 0)),
            pl.BlockSpec((1, IDX_DIM, s), lambda bi, i: (bi, 0, 0)),
        ],
        out_shape=[
            jax.ShapeDtypeStruct((b, s, A_Q), BF16),
            jax.ShapeDtypeStruct((b, IDX_DIM, s), F32),
        ],
        scratch_shapes=_ATTN_SCRATCH(rows, s),
        compiler_params=pltpu.CompilerParams(
            dimension_semantics=("arbitrary", "arbitrary"), vmem_limit_bytes=VMEM_LIMIT),
        name="attn_prompt",
    )(q_t, q_t, pb3, v_t, pb3, pb3, idx_k_g, idx_k_b, tril, eye)
    return o_a, jnp.transpose(ik_t, (0, 2, 1))


def _attn_sample_kernel(iq_ref, q_ref, k_ref, v_ref, ikw_ref, ck_ref, cv_ref, cik_ref, lng_ref, lnb_ref, tril_ref,
                        eye_ref, o_ref, ikln_ref, kb, v_t, ikb, iq_t, w_full, q_t, sc, s_sc, acc_sc, sc_pk,
                        *, rows, past, keys, pieces):
    new = rows
    pad = keys - past - new
    piece = pl.program_id(1)
    per_piece = past // pieces
    eye = eye_ref[0:A_HEAD_DIM, 0:A_HEAD_DIM]
    for c in range(0, per_piece, KEY_TILE):
        dst = pl.ds(pl.multiple_of(piece * per_piece + c, KEY_TILE), KEY_TILE)
        for g in range(A_KV_HEADS):
            kb[g, dst, :] = _transpose_bf16(ck_ref[0, g, :, c:c + KEY_TILE].astype(BF16), eye_ref[...]).astype(BF16)
            v_t[g, :, dst] = cv_ref[0, g, :, c:c + KEY_TILE].astype(BF16)
        ikb[dst, :] = _transpose_bf16(cik_ref[0, :, c:c + KEY_TILE].astype(BF16), eye_ref[...]).astype(BF16)

    @pl.when(piece == pieces - 1)
    def _():
        kf = k_ref[0]
        vf = v_ref[0]
        for g in range(A_KV_HEADS):
            sl = slice(g * A_HEAD_DIM, (g + 1) * A_HEAD_DIM)
            kb[g, past:past + new, :] = kf[:, sl].astype(BF16)
            kb[g, past + new:keys, :] = jnp.zeros((pad, A_HEAD_DIM), BF16)
            v_t[g, :, past:past + new] = _transpose_bf16(vf[:, sl].astype(BF16), eye).astype(BF16)
            v_t[g, :, past + new:keys] = jnp.zeros((A_HEAD_DIM, pad), BF16)
        ikw = ikw_ref[0]
        ln = _layer_norm(ikw[:, :IDX_DIM], lng_ref[...], lnb_ref[...])
        ikln_ref[0] = ln
        ikb[past:past + new, :] = ln.astype(BF16)
        ikb[past + new:keys, :] = jnp.zeros((pad, IDX_DIM), BF16)

        _stage_queries(iq_ref[0], q_ref[0], ikw[:, IDX_DIM:IDX_DIM + IDX_HEADS], eye_ref, iq_t, w_full, q_t, rows,
                       transposed=False)

        nt = keys // KEY_TILE

        def adm(j):
            return j * KEY_TILE + lax.broadcasted_iota(jnp.int32, (KEY_TILE, LANES), 0) < past + new

        pk = KEY_TILE // 4
        group = lax.broadcasted_iota(jnp.int32, (pk, LANES), 1) // rows
        sub_block = jnp.zeros((pk, LANES), jnp.int32)
        for c, blk in enumerate(PACK_ORDER):
            sub_block = jnp.where(group == c, blk, sub_block)
        key_in_tile = sub_block * pk + lax.broadcasted_iota(jnp.int32, (pk, LANES), 0)

        def adm_pk(j):
            return j * KEY_TILE + key_in_tile < past + new

        _index_scores(iq_t, w_full, ikb, sc, sc_pk, nt, adm, adm_pk, rows)
        _select_topk(sc, sc_pk, tril_ref, nt)
        l = _attend(sc, q_t, kb, v_t, s_sc, acc_sc, nt)
        _write_heads(o_ref, l, acc_sc, eye_ref, rows)


def _attn_sample(pa3, pb3, cache_k, cache_v, cache_ik, idx_k_g, idx_k_b, tril, eye):
    b, t, _ = pa3.shape
    past = cache_ik.shape[2]
    pieces = 1
    keys = -(-(past + t) // KEY_TILE) * KEY_TILE
    kernel = functools.partial(_attn_sample_kernel, rows=t, past=past, keys=keys, pieces=pieces)
    return pl.pallas_call(
        kernel,
        grid=(b, pieces),
        in_specs=[
            pl.BlockSpec((1, t, IDX_Q), lambda bi, pc: (bi, 0, CA_IQ // IDX_Q)),
            pl.BlockSpec((1, t, A_Q), lambda bi, pc: (bi, 0, CA_Q // A_Q)),
            pl.BlockSpec((1, t, A_KV), lambda bi, pc: (bi, 0, CB_K // A_KV)),
            pl.BlockSpec((1, t, A_KV), lambda bi, pc: (bi, 0, CB_V // A_KV)),
            pl.BlockSpec((1, t, LANES), lambda bi, pc: (bi, 0, CB_IKW // LANES)),
            pl.BlockSpec((1, A_KV_HEADS, A_HEAD_DIM, past // pieces), lambda bi, pc: (bi, 0, 0, pc)),
            pl.BlockSpec((1, A_KV_HEADS, A_HEAD_DIM, past // pieces), lambda bi, pc: (bi, 0, 0, pc)),
            pl.BlockSpec((1, IDX_DIM, past // pieces), lambda bi, pc: (bi, 0, pc)),
            pl.BlockSpec((1, IDX_DIM), lambda bi, pc: (0, 0)),
            pl.BlockSpec((1, IDX_DIM), lambda bi, pc: (0, 0)),
            pl.BlockSpec((KEY_TILE, KEY_TILE), lambda bi, pc: (0, 0)),
            pl.BlockSpec((KEY_TILE, KEY_TILE), lambda bi, pc: (0, 0)),
        ],
        out_specs=[
            pl.BlockSpec((1, t, A_Q), lambda bi, pc: (bi, 0, 0)),
            pl.BlockSpec((1, t, IDX_DIM), lambda bi, pc: (bi, 0, 0)),
        ],
        out_shape=[
            jax.ShapeDtypeStruct((b, t, A_Q), BF16),
            jax.ShapeDtypeStruct((b, t, IDX_DIM), F32),
        ],
        scratch_shapes=_ATTN_SCRATCH(t, keys),
        compiler_params=pltpu.CompilerParams(
            dimension_semantics=("arbitrary", "arbitrary"), vmem_limit_bytes=VMEM_LIMIT),
        name="attn_sample",
    )(pa3, pa3, pb3, pb3, pb3, cache_k, cache_v, cache_ik, idx_k_g, idx_k_b, tril, eye)


def _rwkv_kernel(u_ref, shift_ref, s0_ref, mu_ref, w0_ref, a0_ref, kk_ref, ka_ref, rk_ref, gnw_ref, gnb_ref,
                 w2_ref, a2_ref, g2_ref, bd_ref, tri_ref, yb_ref, s_out_ref, carry, state, y_sc, *, tc):
    t = pl.program_id(1)
    u = u_ref[0]
    tb = u.shape[0]
    n = B_HEAD_DIM

    @pl.when(t == 0)
    def _():
        carry[...] = shift_ref[0]
        state[...] = s0_ref[0]

    row = lax.broadcasted_iota(jnp.int32, u.shape, 0)
    u_prev = jnp.where(row == 0, carry[...], pltpu.roll(u, 1, 0))
    carry[...] = u[tb - 1:tb, :]
    m = u + (u_prev - u) * mu_ref[...]

    r = m[:, 0:B_WIDTH]
    k = m[:, B_WIDTH:2 * B_WIDTH]
    v = m[:, 2 * B_WIDTH:3 * B_WIDTH]
    lora = m[:, 3 * B_WIDTH:3 * B_WIDTH + LANES]
    gl = m[:, 3 * B_WIDTH + LANES:]
    lane = lax.broadcasted_iota(jnp.int32, lora.shape, 1)
    lora = jnp.where(lane < DECAY_LORA, jnp.tanh(lora), lora).astype(BF16)
    z = w0_ref[...] + _dot(lora, w2_ref[...])
    softplus = jnp.maximum(-z, 0.0) + jnp.log(1.0 + jnp.exp(-jnp.abs(z)))
    lw = -jnp.exp(-softplus - 0.5)
    a = _sigmoid(a0_ref[...] + _dot(lora, a2_ref[...]))
    g = _dot(_sigmoid(gl).astype(BF16), g2_ref[...])

    kk = k * kk_ref[...]
    ss = _dot_exact_rhs(kk * kk, bd_ref[...])
    kk = kk / jnp.maximum(jnp.sqrt(ss), 1e-12)
    k2 = k * (1.0 + (a - 1.0) * ka_ref[...])
    bonus = _dot_exact_rhs(r * k2 * rk_ref[...], bd_ref[...]) * v

    cum = _dot_exact_lhs(tri_ref[...], lw)
    g_in = jnp.exp(cum)
    g_ex = jnp.exp(cum - lw)
    g_inv = jnp.exp(-cum)
    a_t = -kk * g_ex
    b_t = kk * a * g_inv
    k_t = k2 * g_inv
    r_t = r * g_in

    ri = lax.broadcasted_iota(jnp.int32, (tc, tc), 0)
    ci = lax.broadcasted_iota(jnp.int32, (tc, tc), 1)
    strict = ci < ri
    incl = ci <= ri
    eye = jnp.where(ci == ri, 1.0, 0.0)

    heads = range(B_HEADS)
    chunks = range(tb // tc)
    pairs = [(c, h) for c in chunks for h in heads]
    lanes = [slice(h * n, (h + 1) * n) for h in heads]
    toks = [slice(c * tc, (c + 1) * tc) for c in chunks]
    bf = lambda x: x.astype(BF16)
    g_end = [g_in[(c + 1) * tc - 1:(c + 1) * tc, :] for c in chunks]
    vh = {(c, h): v[toks[c], lanes[h]] for c, h in pairs}
    ar = {(c, h): _split2(jnp.concatenate([a_t[toks[c], lanes[h]], r_t[toks[c], lanes[h]]], axis=0))
          for c, h in pairs}
    bk = {(c, h): jnp.concatenate([b_t[toks[c], lanes[h]], k_t[toks[c], lanes[h]]], axis=0) for c, h in pairs}
    cross = {p: _dot(ar[p][0], bf(bk[p]), trans_b=True) for p in pairs}
    l_ab = {p: jnp.where(strict, cross[p][0:tc, 0:tc], 0.0) for p in pairs}
    l_akv = {p: _dot(bf(jnp.where(strict, cross[p][0:tc, tc:2 * tc], 0.0)), bf(vh[p])) for p in pairs}
    m_rbk = {p: bf(jnp.concatenate([jnp.where(incl, cross[p][tc:2 * tc, 0:tc], 0.0),
                                    jnp.where(incl, cross[p][tc:2 * tc, tc:2 * tc], 0.0)], axis=1)) for p in pairs}

    inv = {p: eye + l_ab[p] for p in pairs}
    pw = l_ab
    span = 2
    while span < tc:
        pwb = {p: bf(pw[p]) for p in pairs}
        pw = {p: _dot(pwb[p], pwb[p]) for p in pairs}
        inv = {p: inv[p] + _dot(bf(inv[p]), bf(pw[p])) for p in pairs}
        span *= 2
    inv = {p: bf(inv[p]) for p in pairs}

    s_cur = [state[h] for h in heads]
    for c in chunks:
        from_state = [_dot3s(ar[c, h], _split2(s_cur[h]), trans_b=True) for h in heads]
        z = [_dot(inv[c, h], bf(from_state[h][0:tc] + l_akv[c, h])) for h in heads]
        zv = [jnp.concatenate([z[h], vh[c, h]], axis=0) for h in heads]
        y = [from_state[h][tc:2 * tc] + _dot(m_rbk[c, h], bf(zv[h])) for h in heads]
        upd = [_dot3(zv[h].T, bk[c, h] * g_end[c][:, lanes[h]]) for h in heads]
        s_cur = [s_cur[h] * g_end[c][:, lanes[h]] + upd[h] for h in heads]
        for h in heads:
            y_sc[toks[c], lanes[h]] = y[h]
    for h in heads:
        state[h] = s_cur[h]

    y = y_sc[...]
    mean = _dot_exact_rhs(y, bd_ref[...]) * (1.0 / n)
    d = y - mean
    var = _dot_exact_rhs(d * d, bd_ref[...]) * (1.0 / n)
    yn = d * lax.rsqrt(var + GN_EPS) * gnw_ref[...] + gnb_ref[...]
    yb_ref[0] = ((yn + bonus) * g).astype(yb_ref.dtype)

    @pl.when(t == pl.num_programs(1) - 1)
    def _():
        s_out_ref[0] = state[...]


def _rwkv(pb3, shift_prev, s0, p, tc):
    b, s, _ = pb3.shape
    tb = min(s, 4 * tc)
    tri = jnp.asarray(np.kron(np.eye(tb // tc), np.tril(np.ones((tc, tc)))), BF16)
    vec = lambda w: pl.BlockSpec((1, w), lambda bi, t: (0, 0))
    mat = lambda r, c: pl.BlockSpec((r, c), lambda bi, t: (0, 0))
    st = pl.BlockSpec((1, B_HEADS, B_HEAD_DIM, B_HEAD_DIM), lambda bi, t: (bi, 0, 0, 0))
    return pl.pallas_call(
        functools.partial(_rwkv_kernel, tc=tc),
        grid=(b, s // tb),
        in_specs=[
            pl.BlockSpec((1, tb, SHIFT_W), lambda bi, t: (bi, t, CB_U // SHIFT_W)),
            pl.BlockSpec((1, 1, SHIFT_W), lambda bi, t: (bi, 0, 0)),
            st,
            vec(SHIFT_W), vec(B_WIDTH), vec(B_WIDTH), vec(B_WIDTH), vec(B_WIDTH), vec(B_WIDTH),
            vec(B_WIDTH), vec(B_WIDTH),
            mat(LANES, B_WIDTH), mat(LANES, B_WIDTH), mat(GATE_LORA, B_WIDTH), mat(B_WIDTH, B_WIDTH), mat(tb, tb),
        ],
        out_specs=[pl.BlockSpec((1, tb, B_WIDTH), lambda bi, t: (bi, t, 0)), st],
        out_shape=[jax.ShapeDtypeStruct((b, s, B_WIDTH), BF16),
                   jax.ShapeDtypeStruct((b, B_HEADS, B_HEAD_DIM, B_HEAD_DIM), F32)],
        scratch_shapes=[pltpu.VMEM((1, SHIFT_W), F32),
                        pltpu.VMEM((B_HEADS, B_HEAD_DIM, B_HEAD_DIM), F32),
                        pltpu.VMEM((tb, B_WIDTH), F32)],
        compiler_params=pltpu.CompilerParams(
            dimension_semantics=("arbitrary", "arbitrary"), vmem_limit_bytes=VMEM_LIMIT),
        name="rwkv",
    )(pb3, shift_prev, s0, p['shift_mu'], p['w0'], p['a0'], p['k_k'], p['k_a'], p['r_k'], p['gn_w'], p['gn_b'],
      p['w2p'], p['a2p'], p['g2'], p['bd'], tri)


def _out_ffn_kernel(oa_ref, yb_ref, ga_ref, gb_ref, x_ref, n2_ref, nf_ref,
                    woa_ref, wob_ref, wout_ref, wg_ref, wu_ref, wd_ref, o_ref):
    merged = (_sigmoid(ga_ref[...].astype(F32)) * _dot(oa_ref[...], woa_ref[...])
              + _sigmoid(gb_ref[...].astype(F32)) * _dot(yb_ref[...], wob_ref[...]))
    h = x_ref[...] + _dot(merged.astype(BF16), wout_ref[...])
    hn = _rms(h, n2_ref[...]).astype(BF16)
    out = h
    for c, n in _col_chunks(D_FF, 4 * MXU_COLS):
        gate = _dot(hn, wg_ref[:, c:c + n])
        up = _dot(hn, wu_ref[:, c:c + n])
        act = (gate * _sigmoid(gate) * up).astype(BF16)
        out = out + _dot(act, wd_ref[c:c + n, :])
    o_ref[...] = _rms(out, nf_ref[...])


def _out_ffn(o_a, y_b, pa, x2, p, norm_f):
    n = x2.shape[0]
    tm = min(n, 512)
    row = lambda w, j=0: pl.BlockSpec((tm, w), lambda i: (i, j))
    vec = pl.BlockSpec((1, D_MODEL), lambda i: (0, 0))
    resident = lambda r, c: pl.BlockSpec((r, c), lambda i: (0, 0), pipeline_mode=pl.Buffered(1))
    return pl.pallas_call(
        _out_ffn_kernel,
        grid=(n // tm,),
        in_specs=[
            row(A_Q), row(B_WIDTH),
            row(D_MODEL, CA_GA // D_MODEL), row(D_MODEL, CA_GB // D_MODEL), row(D_MODEL),
            vec, vec,
            resident(A_Q, D_MODEL), resident(B_WIDTH, D_MODEL), resident(D_MODEL, D_MODEL),
            resident(D_MODEL, D_FF), resident(D_MODEL, D_FF), resident(D_FF, D_MODEL),
        ],
        out_specs=row(D_MODEL),
        out_shape=jax.ShapeDtypeStruct((n, D_MODEL), F32),
        compiler_params=pltpu.CompilerParams(
            dimension_semantics=("arbitrary",), vmem_limit_bytes=VMEM_LIMIT),
        name="out_proj_ffn",
    )(o_a, y_b, pa, pa, x2, p['norm2'], norm_f, p['w_oa'], p['w_ob'], p['w_out'],
      p['w_gate'], p['w_up'], p['w_down'])


def _prepare_params(l, norm1, w_in, idx_k_g, idx_k_b, shift_mu, w0, w2, a0, a2, g2, k_k, k_a, r_k,
                    gn_w, gn_b, w_oa, w_ob, w_out, norm2, w_gate, w_up, w_down):
    w = w_in[l]
    o = np.cumsum([0, A_Q, A_KV, A_KV, IDX_Q, IDX_DIM, IDX_HEADS, SHIFT_W, D_MODEL, D_MODEL])
    seg = lambda i: w[:, o[i]:o[i + 1]]
    pad = jnp.zeros((D_MODEL, LANES - IDX_DIM - IDX_HEADS), w.dtype)
    w_in_p = jnp.concatenate([seg(3), seg(7), seg(8), seg(0), seg(6), seg(1), seg(2), seg(4), seg(5), pad],
                             axis=1).astype(BF16)
    zeros = jnp.zeros((LANES - DECAY_LORA, B_WIDTH), F32)
    head = np.arange(B_WIDTH) // B_HEAD_DIM
    row = lambda x: x[l].reshape(1, -1)
    return {
        'norm1': row(norm1), 'w_in_p': w_in_p,
        'idx_k_g': row(idx_k_g), 'idx_k_b': row(idx_k_b),
        'shift_mu': row(shift_mu), 'w0': row(w0), 'a0': row(a0),
        'k_k': row(k_k), 'k_a': row(k_a), 'r_k': row(r_k), 'gn_w': row(gn_w), 'gn_b': row(gn_b),
        'w2p': jnp.concatenate([w2[l], zeros], axis=0).astype(BF16),
        'a2p': jnp.concatenate([zeros, a2[l]], axis=0).astype(BF16),
        'g2': g2[l].astype(BF16),
        'bd': jnp.asarray(head[:, None] == head[None, :], BF16),
        'w_oa': w_oa[l].astype(BF16), 'w_ob': w_ob[l].astype(BF16), 'w_out': w_out[l].astype(BF16),
        'norm2': row(norm2),
        'w_gate': w_gate[l].astype(BF16), 'w_up': w_up[l].astype(BF16), 'w_down': w_down[l].astype(BF16),
    }


def _layer(x, attend, shift_prev, wkv0, p, norm_f, eye, chunk):
    b, t, _ = x.shape
    x2 = x.reshape(b * t, D_MODEL)
    pa, pb, *transposed = _in_proj(x2, p['norm1'], p['w_in_p'], eye, t)
    pa3 = pa.reshape(b, t, PA_WIDTH)
    pb3 = pb.reshape(b, t, PB_WIDTH)
    o_a, ik_ln = attend(pa3, pb3, *transposed)
    y_b, wkv_new = _rwkv(pb3, shift_prev, wkv0, p, chunk)
    flat = lambda z: z.reshape(b * t, z.shape[-1])
    out = _out_ffn(flat(o_a), flat(y_b), pa, x2, p, norm_f).reshape(b, t, D_MODEL)
    if transposed:
        k_new, v_new = (jnp.transpose(z.reshape(b, A_KV_HEADS, A_HEAD_DIM, t), (0, 3, 1, 2)) for z in transposed[:2])
    else:
        k_new = pb3[:, :, CB_K:CB_K + A_KV].reshape(b, t, A_KV_HEADS, A_HEAD_DIM)
        v_new = pb3[:, :, CB_V:CB_V + A_KV].reshape(b, t, A_KV_HEADS, A_HEAD_DIM)
    shift_new = pb3[:, t - 1:t, CB_U:CB_U + SHIFT_W]
    return out, k_new, v_new, ik_ln, wkv_new, shift_new


def kernel(x_prompt, x_sample, cache_k, cache_v, cache_idx_k, state_wkv, state_shift,
           norm1, w_in, idx_k_g, idx_k_b, shift_mu, w0, w2, a0, a2, g2, k_k, k_a, r_k,
           gn_w, gn_b, w_oa, w_ob, w_out, norm2, w_gate, w_up, w_down, norm_f):
    assert w_in.shape[0] == 1, "single-layer kernel"
    l = 0
    p = _prepare_params(l, norm1, w_in, idx_k_g, idx_k_b, shift_mu, w0, w2, a0, a2, g2, k_k, k_a, r_k,
                        gn_w, gn_b, w_oa, w_ob, w_out, norm2, w_gate, w_up, w_down)
    nf = norm_f.reshape(1, -1)
    tril = jnp.asarray(np.tril(np.ones((KEY_TILE, KEY_TILE), np.float32)), BF16)
    eye = jnp.asarray(np.eye(KEY_TILE, dtype=np.float32), BF16)

    n_p = x_prompt.shape[0]
    shift0 = jnp.zeros((n_p, 1, SHIFT_W), F32)
    wkv_zero = jnp.zeros((n_p, B_HEADS, B_HEAD_DIM, B_HEAD_DIM), F32)
    attend_p = lambda pa3, pb3, k_t, v_t, q_t: _attn_prompt(q_t, pb3, v_t, p['idx_k_g'], p['idx_k_b'], tril, eye)
    y_p, k_p, v_p, ik_p, wkv_p, shift_p = _layer(x_prompt, attend_p, shift0, wkv_zero, p, nf, eye, 64)

    n_s, t_s = x_sample.shape[0], x_sample.shape[1]
    past = cache_k.shape[2]
    ck = jnp.transpose(cache_k[l], (0, 2, 3, 1))
    cv = jnp.transpose(cache_v[l], (0, 2, 3, 1))
    cik = jnp.transpose(cache_idx_k[l], (0, 2, 1))
    attend_s = lambda pa3, pb3: _attn_sample(pa3, pb3, ck, cv, cik, p['idx_k_g'], p['idx_k_b'], tril, eye)
    y_s, k_s, v_s, ik_s, wkv_s, shift_s = _layer(x_sample, attend_s, state_shift[l], state_wkv[l], p, nf, eye, t_s)

    lead = lambda z: z[None]
    return (y_p, y_s, lead(k_p), lead(v_p), lead(ik_p), lead(wkv_p), lead(shift_p),
            lead(k_s), lead(v_s), lead(ik_s), lead(wkv_s), lead(shift_s))
```

```python
import functools

import jax
import jax.numpy as jnp
import numpy as np
from jax import lax
from jax.experimental import pallas as pl
from jax.experimental.pallas import tpu as pltpu

F32 = jnp.float32
BF16 = jnp.bfloat16

D_MODEL = 1024
CHUNK = 64
A_HEADS = 8
A_KV_HEADS = 2
A_GROUP = A_HEADS // A_KV_HEADS
A_HEAD_DIM = 64
A_Q = A_HEADS * A_HEAD_DIM
A_KV = A_KV_HEADS * A_HEAD_DIM
IDX_HEADS = 16
IDX_DIM = 64
IDX_Q = IDX_HEADS * IDX_DIM
TOPK_MAX = 256
B_HEADS = 8
B_HEAD_DIM = 64
B_WIDTH = B_HEADS * B_HEAD_DIM
DECAY_LORA = 64
AAA_LORA = 64
GATE_LORA = 128
SHIFT_W = 3 * B_WIDTH + DECAY_LORA + AAA_LORA + GATE_LORA
D_FF = 2816
RMS_EPS = 1e-6
LN_EPS = 1e-6
GN_EPS = 64e-5

LANES = 128
SUB = 8
VMEM_LIMIT = 56 * 1024 * 1024
NEG = -1e30
KEY_TILE = 256
V_ROWS = A_HEAD_DIM + 16

CA_IQ = 0
CA_GA = CA_IQ + IDX_Q
CA_GB = CA_GA + D_MODEL
CA_Q = CA_GB + D_MODEL
PA_WIDTH = CA_Q + A_Q
CB_U = 0
CB_K = CB_U + SHIFT_W
CB_V = CB_K + A_KV
CB_IKW = CB_V + A_KV
PB_WIDTH = CB_IKW + LANES
MXU_COLS = 256


def _dot(a, b, trans_b=False):
    dn = (((1,), (1 if trans_b else 0,)), ((), ()))
    return lax.dot_general(a, b, dn, preferred_element_type=F32)


def _split2(x):
    hi = x.astype(BF16)
    lo = (x - hi.astype(F32)).astype(BF16)
    return hi, lo


def _dot3s(a_split, b_split, trans_b=False):
    ah, al = a_split
    bh, bl = b_split
    return _dot(ah, bh, trans_b) + (_dot(ah, bl, trans_b) + _dot(al, bh, trans_b))


def _dot3(a, b, trans_b=False):
    return _dot3s(_split2(a), _split2(b), trans_b)


def _dot_exact_rhs(a, b_bf16, terms=2):
    out = None
    rem = a
    for _ in range(terms):
        part = rem.astype(BF16)
        rem = rem - part.astype(F32)
        d = _dot(part, b_bf16)
        out = d if out is None else out + d
    return out


def _dot_exact_lhs(a_bf16, b, terms=3, trans_b=False):
    out = None
    rem = b
    for _ in range(terms):
        part = rem.astype(BF16)
        rem = rem - part.astype(F32)
        d = _dot(a_bf16, part, trans_b)
        out = d if out is None else out + d
    return out


def _sigmoid(x):
    return 1.0 / (1.0 + jnp.exp(-x))


def _rms(x, g):
    ms = jnp.mean(x * x, axis=-1, keepdims=True)
    return x * lax.rsqrt(ms + RMS_EPS) * g


def _col_chunks(width, chunk):
    return [(c, min(chunk, width - c)) for c in range(0, width, chunk)]


QT_WIDTH = IDX_Q + A_Q


def _in_proj_kernel(x_ref, g_ref, w_ref, eye_ref, pa_ref, pb_ref, *t_refs):
    xn = _rms(x_ref[...], g_ref[...]).astype(BF16)
    for c, n in _col_chunks(PA_WIDTH, 2 * MXU_COLS):
        pa_ref[:, c:c + n] = _dot(xn, w_ref[c:c + n, :], trans_b=True).astype(BF16)
    for c, n in _col_chunks(PB_WIDTH, 2 * MXU_COLS):
        pb_ref[:, c:c + n] = _dot(xn, w_ref[PA_WIDTH + c:PA_WIDTH + c + n, :], trans_b=True)
    if t_refs:
        kt_ref, vt_ref, qt_ref = t_refs
        kt_ref[0] = pb_ref[:, CB_K:CB_K + A_KV].T
        vt_ref[0] = pb_ref[:, CB_V:CB_V + A_KV].T
        for dst, src in ((0, CA_IQ), (IDX_Q, CA_Q)):
            for c, n in _col_chunks(IDX_Q if src == CA_IQ else A_Q, MXU_COLS):
                qt_ref[0, dst + c:dst + c + n, :] = _transpose_bf16(
                    pa_ref[:, src + c:src + c + n], eye_ref[0:n, 0:n]).astype(BF16)


def _in_proj(x2, norm1, w_in_p, eye, seq):
    n = x2.shape[0]
    tm = min(n, 512)
    out_specs = [pl.BlockSpec((tm, PA_WIDTH), lambda i: (i, 0)), pl.BlockSpec((tm, PB_WIDTH), lambda i: (i, 0))]
    out_shape = [jax.ShapeDtypeStruct((n, PA_WIDTH), BF16), jax.ShapeDtypeStruct((n, PB_WIDTH), F32)]
    if seq % tm == 0:
        per_seq = seq // tm
        t_spec = lambda w: pl.BlockSpec((1, w, tm), lambda i: (i // per_seq, 0, i % per_seq))
        out_specs += [t_spec(A_KV), t_spec(A_KV), t_spec(QT_WIDTH)]
        out_shape += [jax.ShapeDtypeStruct((n // seq, A_KV, seq), F32)] * 2
        out_shape += [jax.ShapeDtypeStruct((n // seq, QT_WIDTH, seq), BF16)]
    return pl.pallas_call(
        _in_proj_kernel,
        grid=(n // tm,),
        in_specs=[
            pl.BlockSpec((tm, D_MODEL), lambda i: (i, 0)),
            pl.BlockSpec((1, D_MODEL), lambda i: (0, 0)),
            pl.BlockSpec((PA_WIDTH + PB_WIDTH, D_MODEL), lambda i: (0, 0), pipeline_mode=pl.Buffered(1)),
            pl.BlockSpec((KEY_TILE, KEY_TILE), lambda i: (0, 0)),
        ],
        out_specs=out_specs,
        out_shape=out_shape,
        compiler_params=pltpu.CompilerParams(
            dimension_semantics=("arbitrary",), vmem_limit_bytes=VMEM_LIMIT),
        name="in_proj",
    )(x2, norm1, w_in_p, eye)


def _layer_norm(x, g, b):
    mu = jnp.mean(x, axis=-1, keepdims=True)
    d = x - mu
    var = jnp.mean(d * d, axis=-1, keepdims=True)
    return d * lax.rsqrt(var + LN_EPS) * g + b


def _transpose_bf16(x, eye):
    return _dot(eye, x, trans_b=True)


def _key_loop(nt, body, init):
    if isinstance(nt, int):
        carry = init
        for j in range(nt):
            carry = body(j, carry)
        return carry
    carry = lax.fori_loop(0, nt // 2, lambda k, c: body(2 * k + 1, body(2 * k, c)), init)
    return lax.cond(nt % 2 == 1, lambda c: body(nt - 1, c), lambda c: c, carry)


def _key_off(j):
    return j * KEY_TILE if isinstance(j, int) else pl.multiple_of(j * KEY_TILE, KEY_TILE)


def _fold(x, op):
    x = x.reshape(x.shape[0] // SUB, SUB, x.shape[1])
    while x.shape[0] > 1:
        half = x.shape[0] // 2
        x = op(x[:half], x[half:])
    return x[0]


def _stage_queries(iq, q, iw, eye_ref, iq_t, w_full, q_t, rows, transposed):
    eye = eye_ref[0:IDX_DIM, 0:IDX_DIM]

    def head_t(x, h, dim):
        if transposed:
            return x[h * dim:(h + 1) * dim, :] * (dim ** -0.5)
        return _transpose_bf16((x[:, h * dim:(h + 1) * dim] * (dim ** -0.5)).astype(BF16), eye)

    for h in range(IDX_HEADS):
        iq_t[:, h * rows:(h + 1) * rows] = head_t(iq, h, IDX_DIM).astype(BF16)
    w_t = _dot_exact_lhs(eye_ref[0:IDX_HEADS, 0:IDX_HEADS], iw, trans_b=True)
    for h in range(IDX_HEADS):
        w_full[:, h * rows:(h + 1) * rows] = w_t[h:h + 1, :] * (IDX_HEADS ** -0.5)
    for g in range(A_KV_HEADS):
        for hh in range(A_GROUP):
            q_t[g, :, hh * rows:(hh + 1) * rows] = head_t(q, g * A_GROUP + hh, A_HEAD_DIM).astype(BF16)


PACK_ORDER = (0, 2, 1, 3)


def _pack_quarters(acc):
    n = acc.shape[0] // 4
    a, b, c, d = (acc[i * n:(i + 1) * n] for i in range(4))
    lane = lax.broadcasted_iota(jnp.int32, (n, LANES), 1)
    low64 = lane < LANES // 2
    low32 = (lane & (LANES // 4)) == 0
    ab = jnp.where(low64, a, b) + pltpu.roll(jnp.where(low64, b, a), LANES // 2, 1)
    cd = jnp.where(low64, c, d) + pltpu.roll(jnp.where(low64, d, c), LANES // 2, 1)
    t1 = ab + pltpu.roll(ab, 3 * LANES // 4, 1)
    t2 = cd + pltpu.roll(cd, LANES // 4, 1)
    return jnp.where(low32, t1, t2)


def _index_scores(iq_t, w_full, ikb, sc, sc_pk, nt, adm_fn, adm_pk_fn, rows):
    width = sc.shape[1]

    def body(j, carry):
        off = _key_off(j)
        lg = _dot(ikb[pl.ds(off, KEY_TILE), :], iq_t[...])
        acc = None
        for c in range(IDX_HEADS * rows // width):
            x = jnp.maximum(lg[:, c * width:(c + 1) * width], 0.0) * w_full[:, c * width:(c + 1) * width]
            acc = x if acc is None else acc + x
        if sc_pk is not None:
            assert LANES // rows == 4
            pk = KEY_TILE // 4
            sc_pk[pl.ds(j * pk, pk), :] = jnp.where(adm_pk_fn(j), _pack_quarters(acc), -jnp.inf)
        shift = LANES // 2
        while shift >= rows:
            acc = acc + pltpu.roll(acc, shift, 1)
            shift //= 2
        sc[pl.ds(off, KEY_TILE), :] = jnp.where(adm_fn(j), acc, -jnp.inf)
        return carry

    _key_loop(nt, body, 0)


def _select_topk(sc, sc_pk, tril_ref, nt):
    shape = (SUB, sc.shape[1])
    inf = jnp.float32(jnp.inf)
    src = sc if sc_pk is None else sc_pk
    rows_t = KEY_TILE if sc_pk is None else KEY_TILE // 4

    def tile(j):
        start = _key_off(j) if sc_pk is None else j * rows_t
        return src[pl.ds(start, rows_t), :]

    def allsub(x, red):
        op = {jnp.sum: jnp.add, jnp.min: jnp.minimum, jnp.max: jnp.maximum}[red]
        if sc_pk is not None:
            x = op(op(x, pltpu.roll(x, LANES // 4, 1)),
                   op(pltpu.roll(x, LANES // 2, 1), pltpu.roll(x, 3 * LANES // 4, 1)))
        return jnp.broadcast_to(red(x, axis=0, keepdims=True), shape)

    def count_ge(t):
        t1 = t[0:1, :]
        acc = _key_loop(
            nt, lambda j, a: a + _fold(jnp.where(tile(j) >= t1, 1.0, 0.0), jnp.add), jnp.zeros(shape, F32))
        return allsub(acc, jnp.sum)

    def stats(j, c):
        mn, mx, na = c
        s = tile(j)
        fin = s > -inf
        return (jnp.minimum(mn, _fold(jnp.where(fin, s, inf), jnp.minimum)), jnp.maximum(mx, _fold(s, jnp.maximum)),
                na + _fold(jnp.where(fin, 1.0, 0.0), jnp.add))

    mn, mx, na = _key_loop(
        nt, stats, (jnp.full(shape, inf, F32), jnp.full(shape, -inf, F32), jnp.zeros(shape, F32)))
    lo0 = allsub(mn, jnp.min)
    mx = allsub(mx, jnp.max)
    n_adm = allsub(na, jnp.sum)
    kq = jnp.minimum(n_adm, float(TOPK_MAX))
    hi0 = mx + (jnp.abs(mx) * 1e-6 + 1e-30)

    def cond(c):
        return jnp.logical_and(c[0] < 400, c[5] > 0.5)

    def probe(x, lo, hi, c_lo, c_hi):
        c_x = count_ge(x)
        ge = c_x >= kq
        return jnp.where(ge, x, lo), jnp.where(ge, hi, x), jnp.where(ge, c_x, c_lo), jnp.where(ge, c_hi, c_x)

    def body(c):
        it, lo, hi, c_lo, c_hi, _ = c
        for _ in range(2):
            lo, hi, c_lo, c_hi = probe(0.5 * (lo + hi), lo, hi, c_lo, c_hi)
        nxt = 0.5 * (lo + hi)
        active = jnp.where(c_lo - c_hi > 1.5, jnp.where(nxt > lo, jnp.where(nxt < hi, 1.0, 0.0), 0.0), 0.0)
        active = jnp.where(n_adm > kq, active, 0.0)
        return it + 1, lo, hi, c_lo, c_hi, jnp.max(active)

    _, lo, _, c_lo, c_hi, _ = lax.while_loop(
        cond, body, (jnp.int32(0), lo0, hi0, n_adm, jnp.zeros(shape, F32), jnp.float32(1.0)))

    lo1 = lo[0:1, :]
    thr = allsub(
        _key_loop(nt, lambda j, a: jnp.minimum(a, _fold(jnp.where(tile(j) >= lo1, tile(j), inf), jnp.minimum)),
                  jnp.full(shape, inf, F32)),
        jnp.min)[0:1, :]
    take = (kq - c_hi)[0:1, :]
    has_tie = jnp.max(c_lo - kq) > 0.5

    @pl.when(jnp.logical_not(has_tie))
    def _():
        def wr(j, carry):
            off = _key_off(j)
            sc[pl.ds(off, KEY_TILE), :] = jnp.where(sc[pl.ds(off, KEY_TILE), :] >= thr, 0.0, NEG)
            return carry
        _key_loop(nt, wr, 0)

    @pl.when(has_tie)
    def _():
        def wr(j, seen):
            off = _key_off(j)
            s = sc[pl.ds(off, KEY_TILE), :]
            tie = jnp.where(s == thr, 1.0, 0.0)
            rank = _dot(tril_ref[...], tie.astype(BF16)) + seen
            keep_tie = jnp.where(s == thr, jnp.where(rank <= take, 0.0, NEG), NEG)
            sc[pl.ds(off, KEY_TILE), :] = jnp.where(s > thr, 0.0, keep_tie)
            return seen + jnp.sum(tie, axis=0, keepdims=True)
        lax.fori_loop(0, nt, wr, jnp.zeros((1, sc.shape[1]), F32))


def _attend(sc, q_t, kb, v_t, s_sc, acc_sc, nt):
    lanes_g = q_t.shape[2]
    rep = lanes_g // sc.shape[1]
    groups = range(A_KV_HEADS)

    def scores(j, macc):
        off = _key_off(j)
        bias = sc[pl.ds(off, KEY_TILE), :]
        bias = bias if rep == 1 else jnp.concatenate([bias] * rep, axis=1)
        out = []
        for g in groups:
            s = _dot(kb[g, pl.ds(off, KEY_TILE), :], q_t[g]) + bias
            s_sc[g, pl.ds(off, KEY_TILE), :] = s
            out.append(jnp.maximum(macc[g], _fold(s, jnp.maximum)))
        return tuple(out)

    macc = _key_loop(nt, scores, tuple(jnp.full((SUB, lanes_g), NEG, F32) for _ in groups))
    m = [jnp.max(macc[g], axis=0, keepdims=True) for g in groups]
    acc_sc[...] = jnp.zeros(acc_sc.shape, F32)

    def weighted(j, carry):
        off = _key_off(j)
        for g in groups:
            p = jnp.exp((s_sc[g, pl.ds(off, KEY_TILE), :] - m[g]).astype(BF16))
            acc_sc[g] += _dot(v_t[g, :, pl.ds(off, KEY_TILE)], p)
        return carry

    _key_loop(nt, weighted, 0)


def _ones_row(cols):
    first = lax.broadcasted_iota(jnp.int32, (V_ROWS - A_HEAD_DIM, cols), 0) == 0
    return jnp.where(first, 1.0, 0.0).astype(BF16)


def _write_heads(o_ref, acc_sc, eye_ref, rows):
    eye = eye_ref[0:rows, 0:rows]
    for g in range(A_KV_HEADS):
        acc = acc_sc[g]
        o_t = (acc[0:A_HEAD_DIM] * (1.0 / acc[A_HEAD_DIM:A_HEAD_DIM + 1])).astype(BF16)
        for hh in range(A_GROUP):
            h = g * A_GROUP + hh
            o_ref[0, :, h * A_HEAD_DIM:(h + 1) * A_HEAD_DIM] = _transpose_bf16(
                o_t[:, hh * rows:(hh + 1) * rows], eye).astype(o_ref.dtype)


_ATTN_SCRATCH = lambda rows, keys: [
    pltpu.VMEM((A_KV_HEADS, keys, A_HEAD_DIM), BF16),
    pltpu.VMEM((A_KV_HEADS, V_ROWS, keys), BF16),
    pltpu.VMEM((keys, IDX_DIM), BF16),
    pltpu.VMEM((IDX_DIM, IDX_HEADS * rows), BF16),
    pltpu.VMEM((1, IDX_HEADS * rows), F32),
    pltpu.VMEM((A_KV_HEADS, A_HEAD_DIM, A_GROUP * rows), BF16),
    pltpu.VMEM((keys, max(rows, LANES)), F32),
    pltpu.VMEM((A_KV_HEADS, keys, A_GROUP * rows), F32),
    pltpu.VMEM((A_KV_HEADS, V_ROWS, A_GROUP * rows), F32),
] + ([pltpu.VMEM((keys // (LANES // rows), LANES), F32)] if rows < LANES else [])


def _attn_prompt_kernel(iq_ref, q_ref, k_ref, v_ref, ikw_ref, ikwq_ref, lng_ref, lnb_ref, tril_ref, eye_ref,
                        o_ref, ikln_ref, kb, v_t, ikb, iq_t, w_full, q_t, sc, s_sc, acc_sc, *, rows):
    i = pl.program_id(1)

    @pl.when(i == 0)
    def _():
        kf = k_ref[0]
        for g in range(A_KV_HEADS):
            sl = slice(g * A_HEAD_DIM, (g + 1) * A_HEAD_DIM)
            kb[g] = kf[:, sl].astype(BF16)
            v_t[g, 0:A_HEAD_DIM, :] = v_ref[0, sl, :].astype(BF16)
            v_t[g, A_HEAD_DIM:V_ROWS, :] = _ones_row(v_t.shape[2])
        ln = _layer_norm(ikw_ref[0][:, :IDX_DIM], lng_ref[...], lnb_ref[...])
        ikln_ref[0] = ln.T
        ikb[...] = ln.astype(BF16)

    _stage_queries(iq_ref[0], q_ref[0], ikwq_ref[0][:, IDX_DIM:IDX_DIM + IDX_HEADS], eye_ref, iq_t, w_full, q_t, rows,
                   transposed=True)

    nt = ((i + 1) * rows + KEY_TILE - 1) // KEY_TILE
    q_pos = i * rows + lax.broadcasted_iota(jnp.int32, (KEY_TILE, rows), 1)
    q_end = (q_pos // CHUNK + 1) * CHUNK

    def adm(j):
        return j * KEY_TILE + lax.broadcasted_iota(jnp.int32, (KEY_TILE, rows), 0) < q_end

    _index_scores(iq_t, w_full, ikb, sc, None, nt, adm, None, rows)
    _select_topk(sc, None, tril_ref, nt)
    _attend(sc, q_t, kb, v_t, s_sc, acc_sc, nt)
    _write_heads(o_ref, acc_sc, eye_ref, rows)


def _attn_prompt(q_t, pb3, v_t, idx_k_g, idx_k_b, tril, eye):
    b, s, _ = pb3.shape
    rows = 2 * LANES
    kernel = functools.partial(_attn_prompt_kernel, rows=rows)
    o_a, ik_t = pl.pallas_call(
        kernel,
        grid=(b, s // rows),
        in_specs=[
            pl.BlockSpec((1, IDX_Q, rows), lambda bi, i: (bi, 0, i)),
            pl.BlockSpec((1, A_Q, rows), lambda bi, i: (bi, IDX_Q // A_Q, i)),
            pl.BlockSpec((1, s, A_KV), lambda bi, i: (bi, 0, CB_K // A_KV)),
            pl.BlockSpec((1, A_KV, s), lambda bi, i: (bi, 0, 0)),
            pl.BlockSpec((1, s, LANES), lambda bi, i: (bi, 0, CB_IKW // LANES)),
            pl.BlockSpec((1, rows, LANES), lambda bi, i: (bi, i, CB_IKW // LANES)),
            pl.BlockSpec((1, IDX_DIM), lambda bi, i: (0, 0)),
            pl.BlockSpec((1, IDX_DIM), lambda bi, i: (0, 0)),
            pl.BlockSpec((KEY_TILE, KEY_TILE), lambda bi, i: (0, 0)),
            pl.BlockSpec((KEY_TILE, KEY_TILE), lambda bi, i: (0, 0)),
        ],
        out_specs=[
            pl.BlockSpec((1, rows, A_Q), lambda bi, i: (bi, i, 0)),
            pl.BlockSpec((1, IDX_DIM, s), lambda bi, i: (bi, 0, 0)),
        ],
        out_shape=[
            jax.ShapeDtypeStruct((b, s, A_Q), BF16),
            jax.ShapeDtypeStruct((b, IDX_DIM, s), F32),
        ],
        scratch_shapes=_ATTN_SCRATCH(rows, s),
        compiler_params=pltpu.CompilerParams(
            dimension_semantics=("arbitrary", "arbitrary"), vmem_limit_bytes=VMEM_LIMIT),
        name="attn_prompt",
    )(q_t, q_t, pb3, v_t, pb3, pb3, idx_k_g, idx_k_b, tril, eye)
    return o_a, jnp.transpose(ik_t, (0, 2, 1))


def _attn_sample_kernel(iq_ref, q_ref, k_ref, v_ref, ikw_ref, ck_ref, cv_ref, cik_ref, lng_ref, lnb_ref, tril_ref,
                        eye_ref, o_ref, ikln_ref, kb, v_t, ikb, iq_t, w_full, q_t, sc, s_sc, acc_sc, sc_pk,
                        *, rows, past, keys, pieces):
    new = rows
    pad = keys - past - new
    piece = pl.program_id(1)
    per_piece = past // pieces
    eye = eye_ref[0:A_HEAD_DIM, 0:A_HEAD_DIM]
    for c in range(0, per_piece, KEY_TILE):
        dst = pl.ds(pl.multiple_of(piece * per_piece + c, KEY_TILE), KEY_TILE)
        for g in range(A_KV_HEADS):
            kb[g, dst, :] = _transpose_bf16(ck_ref[0, g, :, c:c + KEY_TILE].astype(BF16), eye_ref[...]).astype(BF16)
            v_t[g, 0:A_HEAD_DIM, dst] = cv_ref[0, g, :, c:c + KEY_TILE].astype(BF16)
        ikb[dst, :] = _transpose_bf16(cik_ref[0, :, c:c + KEY_TILE].astype(BF16), eye_ref[...]).astype(BF16)

    @pl.when(piece == pieces - 1)
    def _():
        kf = k_ref[0]
        vf = v_ref[0]
        for g in range(A_KV_HEADS):
            sl = slice(g * A_HEAD_DIM, (g + 1) * A_HEAD_DIM)
            kb[g, past:past + new, :] = kf[:, sl].astype(BF16)
            kb[g, past + new:keys, :] = jnp.zeros((pad, A_HEAD_DIM), BF16)
            v_t[g, 0:A_HEAD_DIM, past:past + new] = _transpose_bf16(vf[:, sl].astype(BF16), eye).astype(BF16)
            v_t[g, 0:A_HEAD_DIM, past + new:keys] = jnp.zeros((A_HEAD_DIM, pad), BF16)
            v_t[g, A_HEAD_DIM:V_ROWS, :] = _ones_row(keys)
        ikw = ikw_ref[0]
        ln = _layer_norm(ikw[:, :IDX_DIM], lng_ref[...], lnb_ref[...])
        ikln_ref[0] = ln
        ikb[past:past + new, :] = ln.astype(BF16)
        ikb[past + new:keys, :] = jnp.zeros((pad, IDX_DIM), BF16)

        _stage_queries(iq_ref[0], q_ref[0], ikw[:, IDX_DIM:IDX_DIM + IDX_HEADS], eye_ref, iq_t, w_full, q_t, rows,
                       transposed=False)

        nt = keys // KEY_TILE

        def adm(j):
            return j * KEY_TILE + lax.broadcasted_iota(jnp.int32, (KEY_TILE, LANES), 0) < past + new

        pk = KEY_TILE // 4
        group = lax.broadcasted_iota(jnp.int32, (pk, LANES), 1) // rows
        sub_block = jnp.zeros((pk, LANES), jnp.int32)
        for c, blk in enumerate(PACK_ORDER):
            sub_block = jnp.where(group == c, blk, sub_block)
        key_in_tile = sub_block * pk + lax.broadcasted_iota(jnp.int32, (pk, LANES), 0)

        def adm_pk(j):
            return j * KEY_TILE + key_in_tile < past + new

        _index_scores(iq_t, w_full, ikb, sc, sc_pk, nt, adm, adm_pk, rows)
        _select_topk(sc, sc_pk, tril_ref, nt)
        _attend(sc, q_t, kb, v_t, s_sc, acc_sc, nt)
        _write_heads(o_ref, acc_sc, eye_ref, rows)


def _attn_sample(pa3, pb3, cache_k, cache_v, cache_ik, idx_k_g, idx_k_b, tril, eye):
    b, t, _ = pa3.shape
    past = cache_ik.shape[2]
    pieces = 1
    keys = -(-(past + t) // KEY_TILE) * KEY_TILE
    kernel = functools.partial(_attn_sample_kernel, rows=t, past=past, keys=keys, pieces=pieces)
    return pl.pallas_call(
        kernel,
        grid=(b, pieces),
        in_specs=[
            pl.BlockSpec((1, t, IDX_Q), lambda bi, pc: (bi, 0, CA_IQ // IDX_Q)),
            pl.BlockSpec((1, t, A_Q), lambda bi, pc: (bi, 0, CA_Q // A_Q)),
            pl.BlockSpec((1, t, A_KV), lambda bi, pc: (bi, 0, CB_K // A_KV)),
            pl.BlockSpec((1, t, A_KV), lambda bi, pc: (bi, 0, CB_V // A_KV)),
            pl.BlockSpec((1, t, LANES), lambda bi, pc: (bi, 0, CB_IKW // LANES)),
            pl.BlockSpec((1, A_KV_HEADS, A_HEAD_DIM, past // pieces), lambda bi, pc: (bi, 0, 0, pc)),
            pl.BlockSpec((1, A_KV_HEADS, A_HEAD_DIM, past // pieces), lambda bi, pc: (bi, 0, 0, pc)),
            pl.BlockSpec((1, IDX_DIM, past // pieces), lambda bi, pc: (bi, 0, pc)),
            pl.BlockSpec((1, IDX_DIM), lambda bi, pc: (0, 0)),
            pl.BlockSpec((1, IDX_DIM), lambda bi, pc: (0, 0)),
            pl.BlockSpec((KEY_TILE, KEY_TILE), lambda bi, pc: (0, 0)),
            pl.BlockSpec((KEY_TILE, KEY_TILE), lambda bi, pc: (0, 0)),
        ],
        out_specs=[
            pl.BlockSpec((1, t, A_Q), lambda bi, pc: (bi, 0, 0)),
            pl.BlockSpec((1, t, IDX_DIM), lambda bi, pc: (bi, 0, 0)),
        ],
        out_shape=[
            jax.ShapeDtypeStruct((b, t, A_Q), BF16),
            jax.ShapeDtypeStruct((b, t, IDX_DIM), F32),
        ],
        scratch_shapes=_ATTN_SCRATCH(t, keys),
        compiler_params=pltpu.CompilerParams(
            dimension_semantics=("arbitrary", "arbitrary"), vmem_limit_bytes=VMEM_LIMIT),
        name="attn_sample",
    )(pa3, pa3, pb3, pb3, pb3, cache_k, cache_v, cache_ik, idx_k_g, idx_k_b, tril, eye)


def _rwkv_kernel(u_ref, shift_ref, s0_ref, mu_ref, w0_ref, a0_ref, kk_ref, ka_ref, rk_ref, gnw_ref, gnb_ref,
                 w2_ref, a2_ref, g2_ref, bd_ref, tri_ref, yb_ref, s_out_ref, carry, state, y_sc, *, tc):
    t = pl.program_id(1)
    seqs, tb = u_ref.shape[0], u_ref.shape[1]
    u = u_ref[...].reshape(seqs * tb, SHIFT_W)
    n = B_HEAD_DIM

    @pl.when(t == 0)
    def _():
        carry[...] = shift_ref[...]
        state[...] = s0_ref[...]

    row = lax.broadcasted_iota(jnp.int32, u.shape, 0)
    u_prev = pltpu.roll(u, 1, 0)
    for s in range(seqs):
        u_prev = jnp.where(row == s * tb, carry[s], u_prev)
        carry[s] = u[(s + 1) * tb - 1:(s + 1) * tb, :]
    m = u + (u_prev - u) * mu_ref[...]

    r = m[:, 0:B_WIDTH]
    k = m[:, B_WIDTH:2 * B_WIDTH]
    v = m[:, 2 * B_WIDTH:3 * B_WIDTH]
    lora = m[:, 3 * B_WIDTH:3 * B_WIDTH + LANES]
    gl = m[:, 3 * B_WIDTH + LANES:]
    lane = lax.broadcasted_iota(jnp.int32, lora.shape, 1)
    lora = jnp.where(lane < DECAY_LORA, jnp.tanh(lora), lora).astype(BF16)
    z = w0_ref[...] + _dot(lora, w2_ref[...])
    softplus = jnp.maximum(-z, 0.0) + jnp.log(1.0 + jnp.exp(-jnp.abs(z)))
    lw = -jnp.exp(-softplus - 0.5)
    a = _sigmoid(a0_ref[...] + _dot(lora, a2_ref[...]))
    g = _dot(_sigmoid(gl).astype(BF16), g2_ref[...])

    kk = k * kk_ref[...]
    ss = _dot_exact_rhs(kk * kk, bd_ref[...])
    kk = kk / jnp.maximum(jnp.sqrt(ss), 1e-12)
    k2 = k * (1.0 + (a - 1.0) * ka_ref[...])
    bonus = _dot_exact_rhs(r * k2 * rk_ref[...], bd_ref[...]) * v

    cum = _dot_exact_lhs(tri_ref[...], lw)
    g_in = jnp.exp(cum)
    g_ex = jnp.exp(cum - lw)
    g_inv = jnp.exp(-cum)
    a_t = -kk * g_ex
    b_t = kk * a * g_inv
    k_t = k2 * g_inv
    r_t = r * g_in

    ri = lax.broadcasted_iota(jnp.int32, (tc, tc), 0)
    ci = lax.broadcasted_iota(jnp.int32, (tc, tc), 1)
    strict = ci < ri
    incl = ci <= ri
    eye = jnp.where(ci == ri, 1.0, 0.0)

    heads = range(B_HEADS)
    per_seq = tb // tc
    chunks = range(seqs * per_seq)
    pairs = [(c, h) for c in chunks for h in heads]
    lanes = [slice(h * n, (h + 1) * n) for h in heads]
    toks = [slice(c * tc, (c + 1) * tc) for c in chunks]
    bf = lambda x: x.astype(BF16)
    g_end = [g_in[(c + 1) * tc - 1:(c + 1) * tc, :] for c in chunks]
    vh = {(c, h): v[toks[c], lanes[h]] for c, h in pairs}
    ar = {(c, h): _split2(jnp.concatenate([a_t[toks[c], lanes[h]], r_t[toks[c], lanes[h]]], axis=0))
          for c, h in pairs}
    bk = {(c, h): jnp.concatenate([b_t[toks[c], lanes[h]], k_t[toks[c], lanes[h]]], axis=0) for c, h in pairs}
    cross = {p: _dot(ar[p][0], bf(bk[p]), trans_b=True) for p in pairs}
    l_ab = {p: jnp.where(strict, cross[p][0:tc, 0:tc], 0.0) for p in pairs}
    l_akv = {p: _dot(bf(jnp.where(strict, cross[p][0:tc, tc:2 * tc], 0.0)), bf(vh[p])) for p in pairs}
    m_rbk = {p: bf(jnp.concatenate([jnp.where(incl, cross[p][tc:2 * tc, 0:tc], 0.0),
                                    jnp.where(incl, cross[p][tc:2 * tc, tc:2 * tc], 0.0)], axis=1)) for p in pairs}

    inv = {p: eye + l_ab[p] for p in pairs}
    pw = l_ab
    span = 2
    while span < tc:
        pwb = {p: bf(pw[p]) for p in pairs}
        pw = {p: _dot(pwb[p], pwb[p]) for p in pairs}
        inv = {p: inv[p] + _dot(bf(inv[p]), bf(pw[p])) for p in pairs}
        span *= 2
    inv = {p: bf(inv[p]) for p in pairs}

    s_cur = {(s, h): state[s, h] for s in range(seqs) for h in heads}
    for j in range(per_seq):
        here = [(s * per_seq + j, s, h) for s in range(seqs) for h in heads]
        from_state = {(c, h): _dot3s(ar[c, h], _split2(s_cur[s, h]), trans_b=True) for c, s, h in here}
        z = {(c, h): _dot(inv[c, h], bf(from_state[c, h][0:tc] + l_akv[c, h])) for c, s, h in here}
        zv = {(c, h): jnp.concatenate([z[c, h], vh[c, h]], axis=0) for c, s, h in here}
        y = {(c, h): from_state[c, h][tc:2 * tc] + _dot(m_rbk[c, h], bf(zv[c, h])) for c, s, h in here}
        upd = {(c, h): _dot3(zv[c, h].T, bk[c, h] * g_end[c][:, lanes[h]]) for c, s, h in here}
        s_cur = {(s, h): s_cur[s, h] * g_end[c][:, lanes[h]] + upd[c, h] for c, s, h in here}
        for c, s, h in here:
            y_sc[toks[c], lanes[h]] = y[c, h]
    for (s, h), val in s_cur.items():
        state[s, h] = val

    y = y_sc[...]
    mean = _dot_exact_rhs(y, bd_ref[...]) * (1.0 / n)
    d = y - mean
    var = _dot_exact_rhs(d * d, bd_ref[...]) * (1.0 / n)
    yn = d * lax.rsqrt(var + GN_EPS) * gnw_ref[...] + gnb_ref[...]
    yb_ref[...] = ((yn + bonus) * g).astype(yb_ref.dtype).reshape(seqs, tb, B_WIDTH)

    @pl.when(t == pl.num_programs(1) - 1)
    def _():
        s_out_ref[...] = state[...]


def _rwkv(pb3, shift_prev, s0, p, tc):
    b, s, _ = pb3.shape
    tb = min(s, 4 * tc)
    seqs = max(1, 4 * tc // tb)
    tri = jnp.asarray(np.kron(np.eye(seqs * tb // tc), np.tril(np.ones((tc, tc)))), BF16)
    vec = lambda w: pl.BlockSpec((1, w), lambda bi, t: (0, 0))
    mat = lambda r, c: pl.BlockSpec((r, c), lambda bi, t: (0, 0))
    st = pl.BlockSpec((seqs, B_HEADS, B_HEAD_DIM, B_HEAD_DIM), lambda bi, t: (bi, 0, 0, 0))
    return pl.pallas_call(
        functools.partial(_rwkv_kernel, tc=tc),
        grid=(b // seqs, s // tb),
        in_specs=[
            pl.BlockSpec((seqs, tb, SHIFT_W), lambda bi, t: (bi, t, CB_U // SHIFT_W)),
            pl.BlockSpec((seqs, 1, SHIFT_W), lambda bi, t: (bi, 0, 0)),
            st,
            vec(SHIFT_W), vec(B_WIDTH), vec(B_WIDTH), vec(B_WIDTH), vec(B_WIDTH), vec(B_WIDTH),
            vec(B_WIDTH), vec(B_WIDTH),
            mat(LANES, B_WIDTH), mat(LANES, B_WIDTH), mat(GATE_LORA, B_WIDTH), mat(B_WIDTH, B_WIDTH),
            mat(seqs * tb, seqs * tb),
        ],
        out_specs=[pl.BlockSpec((seqs, tb, B_WIDTH), lambda bi, t: (bi, t, 0)), st],
        out_shape=[jax.ShapeDtypeStruct((b, s, B_WIDTH), BF16),
                   jax.ShapeDtypeStruct((b, B_HEADS, B_HEAD_DIM, B_HEAD_DIM), F32)],
        scratch_shapes=[pltpu.VMEM((seqs, 1, SHIFT_W), F32),
                        pltpu.VMEM((seqs, B_HEADS, B_HEAD_DIM, B_HEAD_DIM), F32),
                        pltpu.VMEM((seqs * tb, B_WIDTH), F32)],
        compiler_params=pltpu.CompilerParams(
            dimension_semantics=("arbitrary", "arbitrary"), vmem_limit_bytes=VMEM_LIMIT),
        name="rwkv",
    )(pb3, shift_prev, s0, p['shift_mu'], p['w0'], p['a0'], p['k_k'], p['k_a'], p['r_k'], p['gn_w'], p['gn_b'],
      p['w2p'], p['a2p'], p['g2'], p['bd'], tri)


def _out_ffn_kernel(oa_ref, yb_ref, ga_ref, gb_ref, x_ref, n2_ref, nf_ref,
                    woa_ref, wob_ref, wout_ref, wg_ref, wu_ref, wd_ref, o_ref):
    merged = (_sigmoid(ga_ref[...].astype(F32)) * _dot(oa_ref[...], woa_ref[...])
              + _sigmoid(gb_ref[...].astype(F32)) * _dot(yb_ref[...], wob_ref[...]))
    h = x_ref[...] + _dot(merged.astype(BF16), wout_ref[...])
    hn = _rms(h, n2_ref[...]).astype(BF16)
    out = h
    for c, n in _col_chunks(D_FF, 4 * MXU_COLS):
        gate = _dot(hn, wg_ref[:, c:c + n])
        up = _dot(hn, wu_ref[:, c:c + n])
        act = (gate * _sigmoid(gate) * up).astype(BF16)
        out = out + _dot(act, wd_ref[c:c + n, :])
    o_ref[...] = _rms(out, nf_ref[...])


def _out_ffn(o_a, y_b, pa, x2, p, norm_f):
    n = x2.shape[0]
    tm = min(n, 512)
    row = lambda w, j=0: pl.BlockSpec((tm, w), lambda i: (i, j))
    vec = pl.BlockSpec((1, D_MODEL), lambda i: (0, 0))
    resident = lambda r, c: pl.BlockSpec((r, c), lambda i: (0, 0), pipeline_mode=pl.Buffered(1))
    return pl.pallas_call(
        _out_ffn_kernel,
        grid=(n // tm,),
        in_specs=[
            row(A_Q), row(B_WIDTH),
            row(D_MODEL, CA_GA // D_MODEL), row(D_MODEL, CA_GB // D_MODEL), row(D_MODEL),
            vec, vec,
            resident(A_Q, D_MODEL), resident(B_WIDTH, D_MODEL), resident(D_MODEL, D_MODEL),
            resident(D_MODEL, D_FF), resident(D_MODEL, D_FF), resident(D_FF, D_MODEL),
        ],
        out_specs=row(D_MODEL),
        out_shape=jax.ShapeDtypeStruct((n, D_MODEL), F32),
        compiler_params=pltpu.CompilerParams(
            dimension_semantics=("arbitrary",), vmem_limit_bytes=VMEM_LIMIT),
        name="out_proj_ffn",
    )(o_a, y_b, pa, pa, x2, p['norm2'], norm_f, p['w_oa'], p['w_ob'], p['w_out'],
      p['w_gate'], p['w_up'], p['w_down'])


def _prepare_params(l, norm1, w_in, idx_k_g, idx_k_b, shift_mu, w0, w2, a0, a2, g2, k_k, k_a, r_k,
                    gn_w, gn_b, w_oa, w_ob, w_out, norm2, w_gate, w_up, w_down):
    w = jnp.transpose(w_in[l])
    o = np.cumsum([0, A_Q, A_KV, A_KV, IDX_Q, IDX_DIM, IDX_HEADS, SHIFT_W, D_MODEL, D_MODEL])
    seg = lambda i: w[o[i]:o[i + 1]]
    pad = jnp.zeros((LANES - IDX_DIM - IDX_HEADS, D_MODEL), w.dtype)
    w_in_p = jnp.concatenate([seg(3), seg(7), seg(8), seg(0), seg(6), seg(1), seg(2), seg(4), seg(5), pad],
                             axis=0).astype(BF16)
    zeros = jnp.zeros((LANES - DECAY_LORA, B_WIDTH), F32)
    head = np.arange(B_WIDTH) // B_HEAD_DIM
    row = lambda x: x[l].reshape(1, -1)
    return {
        'norm1': row(norm1), 'w_in_p': w_in_p,
        'idx_k_g': row(idx_k_g), 'idx_k_b': row(idx_k_b),
        'shift_mu': row(shift_mu), 'w0': row(w0), 'a0': row(a0),
        'k_k': row(k_k), 'k_a': row(k_a), 'r_k': row(r_k), 'gn_w': row(gn_w), 'gn_b': row(gn_b),
        'w2p': jnp.concatenate([w2[l], zeros], axis=0).astype(BF16),
        'a2p': jnp.concatenate([zeros, a2[l]], axis=0).astype(BF16),
        'g2': g2[l].astype(BF16),
        'bd': jnp.asarray(head[:, None] == head[None, :], BF16),
        'w_oa': w_oa[l].astype(BF16), 'w_ob': w_ob[l].astype(BF16), 'w_out': w_out[l].astype(BF16),
        'norm2': row(norm2),
        'w_gate': w_gate[l].astype(BF16), 'w_up': w_up[l].astype(BF16), 'w_down': w_down[l].astype(BF16),
    }


def _layer(x, attend, shift_prev, wkv0, p, norm_f, eye, chunk):
    b, t, _ = x.shape
    x2 = x.reshape(b * t, D_MODEL)
    pa, pb, *transposed = _in_proj(x2, p['norm1'], p['w_in_p'], eye, t)
    pa3 = pa.reshape(b, t, PA_WIDTH)
    pb3 = pb.reshape(b, t, PB_WIDTH)
    o_a, ik_ln = attend(pa3, pb3, *transposed)
    y_b, wkv_new = _rwkv(pb3, shift_prev, wkv0, p, chunk)
    flat = lambda z: z.reshape(b * t, z.shape[-1])
    out = _out_ffn(flat(o_a), flat(y_b), pa, x2, p, norm_f).reshape(b, t, D_MODEL)
    if transposed:
        k_new, v_new = (jnp.transpose(z.reshape(b, A_KV_HEADS, A_HEAD_DIM, t), (0, 3, 1, 2)) for z in transposed[:2])
    else:
        k_new = pb3[:, :, CB_K:CB_K + A_KV].reshape(b, t, A_KV_HEADS, A_HEAD_DIM)
        v_new = pb3[:, :, CB_V:CB_V + A_KV].reshape(b, t, A_KV_HEADS, A_HEAD_DIM)
    shift_new = pb3[:, t - 1:t, CB_U:CB_U + SHIFT_W]
    return out, k_new, v_new, ik_ln, wkv_new, shift_new


def kernel(x_prompt, x_sample, cache_k, cache_v, cache_idx_k, state_wkv, state_shift,
           norm1, w_in, idx_k_g, idx_k_b, shift_mu, w0, w2, a0, a2, g2, k_k, k_a, r_k,
           gn_w, gn_b, w_oa, w_ob, w_out, norm2, w_gate, w_up, w_down, norm_f):
    assert w_in.shape[0] == 1, "single-layer kernel"
    l = 0
    p = _prepare_params(l, norm1, w_in, idx_k_g, idx_k_b, shift_mu, w0, w2, a0, a2, g2, k_k, k_a, r_k,
                        gn_w, gn_b, w_oa, w_ob, w_out, norm2, w_gate, w_up, w_down)
    nf = norm_f.reshape(1, -1)
    tril = jnp.asarray(np.tril(np.ones((KEY_TILE, KEY_TILE), np.float32)), BF16)
    eye = jnp.asarray(np.eye(KEY_TILE, dtype=np.float32), BF16)

    n_p = x_prompt.shape[0]
    shift0 = jnp.zeros((n_p, 1, SHIFT_W), F32)
    wkv_zero = jnp.zeros((n_p, B_HEADS, B_HEAD_DIM, B_HEAD_DIM), F32)
    attend_p = lambda pa3, pb3, k_t, v_t, q_t: _attn_prompt(q_t, pb3, v_t, p['idx_k_g'], p['idx_k_b'], tril, eye)
    y_p, k_p, v_p, ik_p, wkv_p, shift_p = _layer(x_prompt, attend_p, shift0, wkv_zero, p, nf, eye, 64)

    n_s, t_s = x_sample.shape[0], x_sample.shape[1]
    past = cache_k.shape[2]
    ck = jnp.transpose(cache_k[l], (0, 2, 3, 1))
    cv = jnp.transpose(cache_v[l], (0, 2, 3, 1))
    cik = jnp.transpose(cache_idx_k[l], (0, 2, 1))
    attend_s = lambda pa3, pb3: _attn_sample(pa3, pb3, ck, cv, cik, p['idx_k_g'], p['idx_k_b'], tril, eye)
    y_s, k_s, v_s, ik_s, wkv_s, shift_s = _layer(x_sample, attend_s, state_shift[l], state_wkv[l], p, nf, eye, t_s)

    lead = lambda z: z[None]
    return (y_p, y_s, lead(k_p), lead(v_p), lead(ik_p), lead(wkv_p), lead(shift_p),
            lead(k_s), lead(v_s), lead(ik_s), lead(wkv_s), lead(shift_s))
```

```python
import functools

import jax
import jax.numpy as jnp
import numpy as np
from jax import lax
from jax.experimental import pallas as pl
from jax.experimental.pallas import tpu as pltpu

F32 = jnp.float32
BF16 = jnp.bfloat16

D_MODEL = 1024
CHUNK = 64
A_HEADS = 8
A_KV_HEADS = 2
A_GROUP = A_HEADS // A_KV_HEADS
A_HEAD_DIM = 64
A_Q = A_HEADS * A_HEAD_DIM
A_KV = A_KV_HEADS * A_HEAD_DIM
IDX_HEADS = 16
IDX_DIM = 64
IDX_Q = IDX_HEADS * IDX_DIM
TOPK_MAX = 256
B_HEADS = 8
B_HEAD_DIM = 64
B_WIDTH = B_HEADS * B_HEAD_DIM
DECAY_LORA = 64
AAA_LORA = 64
GATE_LORA = 128
SHIFT_W = 3 * B_WIDTH + DECAY_LORA + AAA_LORA + GATE_LORA
D_FF = 2816
RMS_EPS = 1e-6
LN_EPS = 1e-6
GN_EPS = 64e-5

LANES = 128
SUB = 8
VMEM_LIMIT = 56 * 1024 * 1024
NEG = -1e30
KEY_TILE = 256
V_ROWS = A_HEAD_DIM + 16

CA_IQ = 0
CA_GA = CA_IQ + IDX_Q
CA_GB = CA_GA + D_MODEL
CA_Q = CA_GB + D_MODEL
PA_WIDTH = CA_Q + A_Q
CB_U = 0
CB_K = CB_U + SHIFT_W
CB_V = CB_K + A_KV
CB_IKW = CB_V + A_KV
PB_WIDTH = CB_IKW + LANES
MXU_COLS = 256


def _dot(a, b, trans_b=False):
    dn = (((1,), (1 if trans_b else 0,)), ((), ()))
    return lax.dot_general(a, b, dn, preferred_element_type=F32)


def _split2(x):
    hi = x.astype(BF16)
    lo = (x - hi.astype(F32)).astype(BF16)
    return hi, lo


def _dot3s(a_split, b_split, trans_b=False):
    ah, al = a_split
    bh, bl = b_split
    return _dot(ah, bh, trans_b) + (_dot(ah, bl, trans_b) + _dot(al, bh, trans_b))


def _dot3(a, b, trans_b=False):
    return _dot3s(_split2(a), _split2(b), trans_b)


def _dot_exact_rhs(a, b_bf16, terms=2):
    out = None
    rem = a
    for _ in range(terms):
        part = rem.astype(BF16)
        rem = rem - part.astype(F32)
        d = _dot(part, b_bf16)
        out = d if out is None else out + d
    return out


def _dot_exact_lhs(a_bf16, b, terms=3, trans_b=False):
    out = None
    rem = b
    for _ in range(terms):
        part = rem.astype(BF16)
        rem = rem - part.astype(F32)
        d = _dot(a_bf16, part, trans_b)
        out = d if out is None else out + d
    return out


def _sigmoid(x):
    return 1.0 / (1.0 + jnp.exp(-x))


def _rms(x, g):
    ms = jnp.mean(x * x, axis=-1, keepdims=True)
    return x * lax.rsqrt(ms + RMS_EPS) * g


def _col_chunks(width, chunk):
    return [(c, min(chunk, width - c)) for c in range(0, width, chunk)]


QT_WIDTH = IDX_Q + A_Q


def _in_proj_kernel(x_ref, g_ref, w_ref, eye_ref, pa_ref, pb_ref, *t_refs):
    xn = _rms(x_ref[...], g_ref[...]).astype(BF16)
    for c, n in _col_chunks(PA_WIDTH, 2 * MXU_COLS):
        pa_ref[:, c:c + n] = _dot(xn, w_ref[c:c + n, :], trans_b=True).astype(BF16)
    for c, n in _col_chunks(PB_WIDTH, 2 * MXU_COLS):
        pb_ref[:, c:c + n] = _dot(xn, w_ref[PA_WIDTH + c:PA_WIDTH + c + n, :], trans_b=True)
    if t_refs:
        kt_ref, vt_ref, qt_ref = t_refs
        kt_ref[0] = pb_ref[:, CB_K:CB_K + A_KV].T
        vt_ref[0] = pb_ref[:, CB_V:CB_V + A_KV].T
        for dst, src in ((0, CA_IQ), (IDX_Q, CA_Q)):
            for c, n in _col_chunks(IDX_Q if src == CA_IQ else A_Q, MXU_COLS):
                qt_ref[0, dst + c:dst + c + n, :] = _transpose_bf16(
                    pa_ref[:, src + c:src + c + n], eye_ref[0:n, 0:n]).astype(BF16)


def _in_proj(x2, norm1, w_in_p, eye, seq):
    n = x2.shape[0]
    tm = min(n, 512)
    out_specs = [pl.BlockSpec((tm, PA_WIDTH), lambda i: (i, 0)), pl.BlockSpec((tm, PB_WIDTH), lambda i: (i, 0))]
    out_shape = [jax.ShapeDtypeStruct((n, PA_WIDTH), BF16), jax.ShapeDtypeStruct((n, PB_WIDTH), F32)]
    if seq % tm == 0:
        per_seq = seq // tm
        t_spec = lambda w: pl.BlockSpec((1, w, tm), lambda i: (i // per_seq, 0, i % per_seq))
        out_specs += [t_spec(A_KV), t_spec(A_KV), t_spec(QT_WIDTH)]
        out_shape += [jax.ShapeDtypeStruct((n // seq, A_KV, seq), F32)] * 2
        out_shape += [jax.ShapeDtypeStruct((n // seq, QT_WIDTH, seq), BF16)]
    return pl.pallas_call(
        _in_proj_kernel,
        grid=(n // tm,),
        in_specs=[
            pl.BlockSpec((tm, D_MODEL), lambda i: (i, 0)),
            pl.BlockSpec((1, D_MODEL), lambda i: (0, 0)),
            pl.BlockSpec((PA_WIDTH + PB_WIDTH, D_MODEL), lambda i: (0, 0), pipeline_mode=pl.Buffered(1)),
            pl.BlockSpec((KEY_TILE, KEY_TILE), lambda i: (0, 0)),
        ],
        out_specs=out_specs,
        out_shape=out_shape,
        compiler_params=pltpu.CompilerParams(
            dimension_semantics=("arbitrary",), vmem_limit_bytes=VMEM_LIMIT),
        name="in_proj",
    )(x2, norm1, w_in_p, eye)


def _layer_norm(x, g, b):
    mu = jnp.mean(x, axis=-1, keepdims=True)
    d = x - mu
    var = jnp.mean(d * d, axis=-1, keepdims=True)
    return d * lax.rsqrt(var + LN_EPS) * g + b


def _transpose_bf16(x, eye):
    return _dot(eye, x, trans_b=True)


def _key_loop(nt, body, init):
    if isinstance(nt, int):
        carry = init
        for j in range(nt):
            carry = body(j, carry)
        return carry
    carry = lax.fori_loop(0, nt // 2, lambda k, c: body(2 * k + 1, body(2 * k, c)), init)
    return lax.cond(nt % 2 == 1, lambda c: body(nt - 1, c), lambda c: c, carry)


def _key_off(j):
    return j * KEY_TILE if isinstance(j, int) else pl.multiple_of(j * KEY_TILE, KEY_TILE)


def _fold(x, op):
    x = x.reshape(x.shape[0] // SUB, SUB, x.shape[1])
    while x.shape[0] > 1:
        half = x.shape[0] // 2
        x = op(x[:half], x[half:])
    return x[0]


def _stage_queries(iq, q, iw, eye_ref, iq_t, w_full, q_t, rows, transposed):
    eye = eye_ref[0:IDX_DIM, 0:IDX_DIM]

    def head_t(x, h, dim):
        if transposed:
            return x[h * dim:(h + 1) * dim, :] * (dim ** -0.5)
        return _transpose_bf16((x[:, h * dim:(h + 1) * dim] * (dim ** -0.5)).astype(BF16), eye)

    for h in range(IDX_HEADS):
        iq_t[:, h * rows:(h + 1) * rows] = head_t(iq, h, IDX_DIM).astype(BF16)
    w_t = _dot_exact_lhs(eye_ref[0:IDX_HEADS, 0:IDX_HEADS], iw, trans_b=True)
    for h in range(IDX_HEADS):
        w_full[:, h * rows:(h + 1) * rows] = w_t[h:h + 1, :] * (IDX_HEADS ** -0.5)
    for g in range(A_KV_HEADS):
        for hh in range(A_GROUP):
            q_t[g, :, hh * rows:(hh + 1) * rows] = head_t(q, g * A_GROUP + hh, A_HEAD_DIM).astype(BF16)


PACK_ORDER = (0, 2, 1, 3)


def _pack_quarters(acc):
    n = acc.shape[0] // 4
    a, b, c, d = (acc[i * n:(i + 1) * n] for i in range(4))
    lane = lax.broadcasted_iota(jnp.int32, (n, LANES), 1)
    low64 = lane < LANES // 2
    low32 = (lane & (LANES // 4)) == 0
    ab = jnp.where(low64, a, b) + pltpu.roll(jnp.where(low64, b, a), LANES // 2, 1)
    cd = jnp.where(low64, c, d) + pltpu.roll(jnp.where(low64, d, c), LANES // 2, 1)
    t1 = ab + pltpu.roll(ab, 3 * LANES // 4, 1)
    t2 = cd + pltpu.roll(cd, LANES // 4, 1)
    return jnp.where(low32, t1, t2)


def _index_scores(iq_t, w_full, ikb, sc, sc_pk, nt, adm_fn, adm_pk_fn, rows):
    width = sc.shape[1]

    def body(j, carry):
        off = _key_off(j)
        lg = _dot(ikb[pl.ds(off, KEY_TILE), :], iq_t[...])
        acc = None
        for c in range(IDX_HEADS * rows // width):
            x = jnp.maximum(lg[:, c * width:(c + 1) * width], 0.0) * w_full[:, c * width:(c + 1) * width]
            acc = x if acc is None else acc + x
        if sc_pk is not None:
            assert LANES // rows == 4
            pk = KEY_TILE // 4
            sc_pk[pl.ds(j * pk, pk), :] = jnp.where(adm_pk_fn(j), _pack_quarters(acc), -jnp.inf)
        shift = LANES // 2
        while shift >= rows:
            acc = acc + pltpu.roll(acc, shift, 1)
            shift //= 2
        sc[pl.ds(off, KEY_TILE), :] = jnp.where(adm_fn(j), acc, -jnp.inf)
        return carry

    _key_loop(nt, body, 0)


def _select_topk(sc, sc_pk, tril_ref, nt):
    shape = (SUB, sc.shape[1])
    inf = jnp.float32(jnp.inf)
    src = sc if sc_pk is None else sc_pk
    rows_t = KEY_TILE if sc_pk is None else KEY_TILE // 4

    def tile(j):
        start = _key_off(j) if sc_pk is None else j * rows_t
        return src[pl.ds(start, rows_t), :]

    def allsub(x, red):
        op = {jnp.sum: jnp.add, jnp.min: jnp.minimum, jnp.max: jnp.maximum}[red]
        if sc_pk is not None:
            x = op(op(x, pltpu.roll(x, LANES // 4, 1)),
                   op(pltpu.roll(x, LANES // 2, 1), pltpu.roll(x, 3 * LANES // 4, 1)))
        return jnp.broadcast_to(red(x, axis=0, keepdims=True), shape)

    def count_ge(t):
        t1 = t[0:1, :]
        acc = _key_loop(
            nt, lambda j, a: a + _fold(jnp.where(tile(j) >= t1, 1.0, 0.0), jnp.add), jnp.zeros(shape, F32))
        return allsub(acc, jnp.sum)

    def stats(j, c):
        mn, mx, na = c
        s = tile(j)
        fin = s > -inf
        return (jnp.minimum(mn, _fold(jnp.where(fin, s, inf), jnp.minimum)), jnp.maximum(mx, _fold(s, jnp.maximum)),
                na + _fold(jnp.where(fin, 1.0, 0.0), jnp.add))

    mn, mx, na = _key_loop(
        nt, stats, (jnp.full(shape, inf, F32), jnp.full(shape, -inf, F32), jnp.zeros(shape, F32)))
    lo0 = allsub(mn, jnp.min)
    mx = allsub(mx, jnp.max)
    n_adm = allsub(na, jnp.sum)
    kq = jnp.minimum(n_adm, float(TOPK_MAX))
    hi0 = mx + (jnp.abs(mx) * 1e-6 + 1e-30)

    def cond(c):
        return jnp.logical_and(c[0] < 400, c[5] > 0.5)

    def probe(x, lo, hi, c_lo, c_hi):
        c_x = count_ge(x)
        ge = c_x >= kq
        return jnp.where(ge, x, lo), jnp.where(ge, hi, x), jnp.where(ge, c_x, c_lo), jnp.where(ge, c_hi, c_x)

    def body(c):
        it, lo, hi, c_lo, c_hi, _ = c
        for _ in range(2):
            lo, hi, c_lo, c_hi = probe(0.5 * (lo + hi), lo, hi, c_lo, c_hi)
        nxt = 0.5 * (lo + hi)
        active = jnp.where(c_lo - c_hi > 1.5, jnp.where(nxt > lo, jnp.where(nxt < hi, 1.0, 0.0), 0.0), 0.0)
        active = jnp.where(n_adm > kq, active, 0.0)
        return it + 1, lo, hi, c_lo, c_hi, jnp.max(active)

    _, lo, _, c_lo, c_hi, _ = lax.while_loop(
        cond, body, (jnp.int32(0), lo0, hi0, n_adm, jnp.zeros(shape, F32), jnp.float32(1.0)))

    lo1 = lo[0:1, :]
    thr = allsub(
        _key_loop(nt, lambda j, a: jnp.minimum(a, _fold(jnp.where(tile(j) >= lo1, tile(j), inf), jnp.minimum)),
                  jnp.full(shape, inf, F32)),
        jnp.min)[0:1, :]
    take = (kq - c_hi)[0:1, :]
    has_tie = jnp.max(c_lo - kq) > 0.5

    @pl.when(jnp.logical_not(has_tie))
    def _():
        def wr(j, carry):
            off = _key_off(j)
            sc[pl.ds(off, KEY_TILE), :] = jnp.where(sc[pl.ds(off, KEY_TILE), :] >= thr, 0.0, NEG)
            return carry
        _key_loop(nt, wr, 0)

    @pl.when(has_tie)
    def _():
        def wr(j, seen):
            off = _key_off(j)
            s = sc[pl.ds(off, KEY_TILE), :]
            tie = jnp.where(s == thr, 1.0, 0.0)
            rank = _dot(tril_ref[...], tie.astype(BF16)) + seen
            keep_tie = jnp.where(s == thr, jnp.where(rank <= take, 0.0, NEG), NEG)
            sc[pl.ds(off, KEY_TILE), :] = jnp.where(s > thr, 0.0, keep_tie)
            return seen + jnp.sum(tie, axis=0, keepdims=True)
        lax.fori_loop(0, nt, wr, jnp.zeros((1, sc.shape[1]), F32))


def _attend(sc, q_t, kb, v_t, s_sc, acc_sc, nt):
    lanes_g = q_t.shape[2]
    rep = lanes_g // sc.shape[1]
    groups = range(A_KV_HEADS)

    def scores(j, macc):
        off = _key_off(j)
        bias = sc[pl.ds(off, KEY_TILE), :]
        bias = bias if rep == 1 else jnp.concatenate([bias] * rep, axis=1)
        out = []
        for g in groups:
            s = _dot(kb[g, pl.ds(off, KEY_TILE), :], q_t[g]) + bias
            s_sc[g, pl.ds(off, KEY_TILE), :] = s
            out.append(jnp.maximum(macc[g], _fold(s, jnp.maximum)))
        return tuple(out)

    macc = _key_loop(nt, scores, tuple(jnp.full((SUB, lanes_g), NEG, F32) for _ in groups))
    m = [jnp.max(macc[g], axis=0, keepdims=True) for g in groups]
    acc_sc[...] = jnp.zeros(acc_sc.shape, F32)

    def weighted(j, carry):
        off = _key_off(j)
        for g in groups:
            p = jnp.exp((s_sc[g, pl.ds(off, KEY_TILE), :] - m[g]).astype(BF16))
            acc_sc[g] += _dot(v_t[g, :, pl.ds(off, KEY_TILE)], p)
        return carry

    _key_loop(nt, weighted, 0)


def _ones_row(cols):
    first = lax.broadcasted_iota(jnp.int32, (V_ROWS - A_HEAD_DIM, cols), 0) == 0
    return jnp.where(first, 1.0, 0.0).astype(BF16)


def _write_heads(o_ref, acc_sc, eye_ref, rows):
    eye = eye_ref[0:rows, 0:rows]
    for g in range(A_KV_HEADS):
        acc = acc_sc[g]
        o_t = (acc[0:A_HEAD_DIM] * (1.0 / acc[A_HEAD_DIM:A_HEAD_DIM + 1])).astype(BF16)
        for hh in range(A_GROUP):
            h = g * A_GROUP + hh
            o_ref[0, :, h * A_HEAD_DIM:(h + 1) * A_HEAD_DIM] = _transpose_bf16(
                o_t[:, hh * rows:(hh + 1) * rows], eye).astype(o_ref.dtype)


_ATTN_SCRATCH = lambda rows, keys: [
    pltpu.VMEM((A_KV_HEADS, keys, A_HEAD_DIM), BF16),
    pltpu.VMEM((A_KV_HEADS, V_ROWS, keys), BF16),
    pltpu.VMEM((keys, IDX_DIM), BF16),
    pltpu.VMEM((IDX_DIM, IDX_HEADS * rows), BF16),
    pltpu.VMEM((1, IDX_HEADS * rows), F32),
    pltpu.VMEM((A_KV_HEADS, A_HEAD_DIM, A_GROUP * rows), BF16),
    pltpu.VMEM((keys, max(rows, LANES)), F32),
    pltpu.VMEM((A_KV_HEADS, keys, A_GROUP * rows), F32),
    pltpu.VMEM((A_KV_HEADS, V_ROWS, A_GROUP * rows), F32),
] + ([pltpu.VMEM((keys // (LANES // rows), LANES), F32)] if rows < LANES else [])


def _attn_prompt_kernel(iq_ref, q_ref, k_ref, v_ref, ikw_ref, ikwq_ref, lng_ref, lnb_ref, tril_ref, eye_ref,
                        o_ref, ikln_ref, kb, v_t, ikb, iq_t, w_full, q_t, sc, s_sc, acc_sc, *, rows):
    i = pl.program_id(1)

    @pl.when(i == 0)
    def _():
        kf = k_ref[0]
        for g in range(A_KV_HEADS):
            sl = slice(g * A_HEAD_DIM, (g + 1) * A_HEAD_DIM)
            kb[g] = kf[:, sl].astype(BF16)
            v_t[g, 0:A_HEAD_DIM, :] = v_ref[0, sl, :].astype(BF16)
            v_t[g, A_HEAD_DIM:V_ROWS, :] = _ones_row(v_t.shape[2])
        ln = _layer_norm(ikw_ref[0][:, :IDX_DIM], lng_ref[...], lnb_ref[...])
        ikln_ref[0] = ln.T
        ikb[...] = ln.astype(BF16)

    _stage_queries(iq_ref[0], q_ref[0], ikwq_ref[0][:, IDX_DIM:IDX_DIM + IDX_HEADS], eye_ref, iq_t, w_full, q_t, rows,
                   transposed=True)

    nt = ((i + 1) * rows + KEY_TILE - 1) // KEY_TILE
    q_pos = i * rows + lax.broadcasted_iota(jnp.int32, (KEY_TILE, rows), 1)
    q_end = (q_pos // CHUNK + 1) * CHUNK

    def adm(j):
        return j * KEY_TILE + lax.broadcasted_iota(jnp.int32, (KEY_TILE, rows), 0) < q_end

    _index_scores(iq_t, w_full, ikb, sc, None, nt, adm, None, rows)
    _select_topk(sc, None, tril_ref, nt)
    _attend(sc, q_t, kb, v_t, s_sc, acc_sc, nt)
    _write_heads(o_ref, acc_sc, eye_ref, rows)


def _attn_prompt(q_t, pb3, v_t, idx_k_g, idx_k_b, tril, eye):
    b, s, _ = pb3.shape
    rows = 2 * LANES
    kernel = functools.partial(_attn_prompt_kernel, rows=rows)
    o_a, ik_t = pl.pallas_call(
        kernel,
        grid=(b, s // rows),
        in_specs=[
            pl.BlockSpec((1, IDX_Q, rows), lambda bi, i: (bi, 0, i)),
            pl.BlockSpec((1, A_Q, rows), lambda bi, i: (bi, IDX_Q // A_Q, i)),
            pl.BlockSpec((1, s, A_KV), lambda bi, i: (bi, 0, CB_K // A_KV)),
            pl.BlockSpec((1, A_KV, s), lambda bi, i: (bi, 0, 0)),
            pl.BlockSpec((1, s, LANES), lambda bi, i: (bi, 0, CB_IKW // LANES)),
            pl.BlockSpec((1, rows, LANES), lambda bi, i: (bi, i, CB_IKW // LANES)),
            pl.BlockSpec((1, IDX_DIM), lambda bi, i: (0, 0)),
            pl.BlockSpec((1, IDX_DIM), lambda bi, i: (0, 0)),
            pl.BlockSpec((KEY_TILE, KEY_TILE), lambda bi, i: (0, 0)),
            pl.BlockSpec((KEY_TILE, KEY_TILE), lambda bi, i: (0, 0)),
        ],
        out_specs=[
            pl.BlockSpec((1, rows, A_Q), lambda bi, i: (bi, i, 0)),
            pl.BlockSpec((1, IDX_DIM, s), lambda bi, i: (bi, 0, 0)),
        ],
        out_shape=[
            jax.ShapeDtypeStruct((b, s, A_Q), BF16),
            jax.ShapeDtypeStruct((b, IDX_DIM, s), F32),
        ],
        scratch_shapes=_ATTN_SCRATCH(rows, s),
        compiler_params=pltpu.CompilerParams(
            dimension_semantics=("arbitrary", "arbitrary"), vmem_limit_bytes=VMEM_LIMIT),
        name="attn_prompt",
    )(q_t, q_t, pb3, v_t, pb3, pb3, idx_k_g, idx_k_b, tril, eye)
    return o_a, jnp.transpose(ik_t, (0, 2, 1))


def _attn_sample_kernel(iq_ref, q_ref, k_ref, v_ref, ikw_ref, ck_ref, cv_ref, cik_ref, lng_ref, lnb_ref, tril_ref,
                        eye_ref, o_ref, ikln_ref, kb, v_t, ikb, iq_t, w_full, q_t, sc, s_sc, acc_sc, sc_pk,
                        *, rows, past, keys, pieces):
    new = rows
    pad = keys - past - new
    piece = pl.program_id(1)
    per_piece = past // pieces
    eye = eye_ref[0:A_HEAD_DIM, 0:A_HEAD_DIM]
    for c in range(0, per_piece, KEY_TILE):
        dst = pl.ds(pl.multiple_of(piece * per_piece + c, KEY_TILE), KEY_TILE)
        for g in range(A_KV_HEADS):
            kb[g, dst, :] = _transpose_bf16(ck_ref[0, g, :, c:c + KEY_TILE].astype(BF16), eye_ref[...]).astype(BF16)
            v_t[g, 0:A_HEAD_DIM, dst] = cv_ref[0, g, :, c:c + KEY_TILE].astype(BF16)
        ikb[dst, :] = _transpose_bf16(cik_ref[0, :, c:c + KEY_TILE].astype(BF16), eye_ref[...]).astype(BF16)

    @pl.when(piece == pieces - 1)
    def _():
        kf = k_ref[0]
        vf = v_ref[0]
        for g in range(A_KV_HEADS):
            sl = slice(g * A_HEAD_DIM, (g + 1) * A_HEAD_DIM)
            kb[g, past:past + new, :] = kf[:, sl].astype(BF16)
            kb[g, past + new:keys, :] = jnp.zeros((pad, A_HEAD_DIM), BF16)
            v_t[g, 0:A_HEAD_DIM, past:past + new] = _transpose_bf16(vf[:, sl].astype(BF16), eye).astype(BF16)
            v_t[g, 0:A_HEAD_DIM, past + new:keys] = jnp.zeros((A_HEAD_DIM, pad), BF16)
            v_t[g, A_HEAD_DIM:V_ROWS, :] = _ones_row(keys)
        ikw = ikw_ref[0]
        ln = _layer_norm(ikw[:, :IDX_DIM], lng_ref[...], lnb_ref[...])
        ikln_ref[0] = ln
        ikb[past:past + new, :] = ln.astype(BF16)
        ikb[past + new:keys, :] = jnp.zeros((pad, IDX_DIM), BF16)

        _stage_queries(iq_ref[0], q_ref[0], ikw[:, IDX_DIM:IDX_DIM + IDX_HEADS], eye_ref, iq_t, w_full, q_t, rows,
                       transposed=False)

        nt = keys // KEY_TILE

        def adm(j):
            return j * KEY_TILE + lax.broadcasted_iota(jnp.int32, (KEY_TILE, LANES), 0) < past + new

        pk = KEY_TILE // 4
        group = lax.broadcasted_iota(jnp.int32, (pk, LANES), 1) // rows
        sub_block = jnp.zeros((pk, LANES), jnp.int32)
        for c, blk in enumerate(PACK_ORDER):
            sub_block = jnp.where(group == c, blk, sub_block)
        key_in_tile = sub_block * pk + lax.broadcasted_iota(jnp.int32, (pk, LANES), 0)

        def adm_pk(j):
            return j * KEY_TILE + key_in_tile < past + new

        _index_scores(iq_t, w_full, ikb, sc, sc_pk, nt, adm, adm_pk, rows)
        _select_topk(sc, sc_pk, tril_ref, nt)
        _attend(sc, q_t, kb, v_t, s_sc, acc_sc, nt)
        _write_heads(o_ref, acc_sc, eye_ref, rows)


def _attn_sample(pa3, pb3, cache_k, cache_v, cache_ik, idx_k_g, idx_k_b, tril, eye):
    b, t, _ = pa3.shape
    past = cache_ik.shape[2]
    pieces = 1
    keys = -(-(past + t) // KEY_TILE) * KEY_TILE
    kernel = functools.partial(_attn_sample_kernel, rows=t, past=past, keys=keys, pieces=pieces)
    return pl.pallas_call(
        kernel,
        grid=(b, pieces),
        in_specs=[
            pl.BlockSpec((1, t, IDX_Q), lambda bi, pc: (bi, 0, CA_IQ // IDX_Q)),
            pl.BlockSpec((1, t, A_Q), lambda bi, pc: (bi, 0, CA_Q // A_Q)),
            pl.BlockSpec((1, t, A_KV), lambda bi, pc: (bi, 0, CB_K // A_KV)),
            pl.BlockSpec((1, t, A_KV), lambda bi, pc: (bi, 0, CB_V // A_KV)),
            pl.BlockSpec((1, t, LANES), lambda bi, pc: (bi, 0, CB_IKW // LANES)),
            pl.BlockSpec((1, A_KV_HEADS, A_HEAD_DIM, past // pieces), lambda bi, pc: (bi, 0, 0, pc)),
            pl.BlockSpec((1, A_KV_HEADS, A_HEAD_DIM, past // pieces), lambda bi, pc: (bi, 0, 0, pc)),
            pl.BlockSpec((1, IDX_DIM, past // pieces), lambda bi, pc: (bi, 0, pc)),
            pl.BlockSpec((1, IDX_DIM), lambda bi, pc: (0, 0)),
            pl.BlockSpec((1, IDX_DIM), lambda bi, pc: (0, 0)),
            pl.BlockSpec((KEY_TILE, KEY_TILE), lambda bi, pc: (0, 0)),
            pl.BlockSpec((KEY_TILE, KEY_TILE), lambda bi, pc: (0, 0)),
        ],
        out_specs=[
            pl.BlockSpec((1, t, A_Q), lambda bi, pc: (bi, 0, 0)),
            pl.BlockSpec((1, t, IDX_DIM), lambda bi, pc: (bi, 0, 0)),
        ],
        out_shape=[
            jax.ShapeDtypeStruct((b, t, A_Q), BF16),
            jax.ShapeDtypeStruct((b, t, IDX_DIM), F32),
        ],
        scratch_shapes=_ATTN_SCRATCH(t, keys),
        compiler_params=pltpu.CompilerParams(
            dimension_semantics=("arbitrary", "arbitrary"), vmem_limit_bytes=VMEM_LIMIT),
        name="attn_sample",
    )(pa3, pa3, pb3, pb3, pb3, cache_k, cache_v, cache_ik, idx_k_g, idx_k_b, tril, eye)


def _head_pair_diag(x):
    w = x.shape[1] // 2
    first = lax.broadcasted_iota(jnp.int32, x.shape, 1) < w
    zero = jnp.zeros_like(x)
    return jnp.concatenate([jnp.where(first, x, zero), jnp.where(first, zero, x)], axis=0)


def _rwkv_kernel(u_ref, shift_ref, s0_ref, mu_ref, w0_ref, a0_ref, kk_ref, ka_ref, rk_ref, gnw_ref, gnb_ref,
                 w2_ref, a2_ref, g2_ref, bd_ref, tri_ref, yb_ref, s_out_ref, carry, state, y_sc, *, tc):
    t = pl.program_id(1)
    seqs, tb = u_ref.shape[0], u_ref.shape[1]
    u = u_ref[...].reshape(seqs * tb, SHIFT_W)
    n = B_HEAD_DIM

    @pl.when(t == 0)
    def _():
        carry[...] = shift_ref[...]
        for s in range(seqs):
            for p in range(B_HEADS // 2):
                state[s, p] = jnp.concatenate([s0_ref[s, 2 * p], s0_ref[s, 2 * p + 1]], axis=1)

    row = lax.broadcasted_iota(jnp.int32, u.shape, 0)
    u_prev = pltpu.roll(u, 1, 0)
    for s in range(seqs):
        u_prev = jnp.where(row == s * tb, carry[s], u_prev)
        carry[s] = u[(s + 1) * tb - 1:(s + 1) * tb, :]
    m = u + (u_prev - u) * mu_ref[...]

    r = m[:, 0:B_WIDTH]
    k = m[:, B_WIDTH:2 * B_WIDTH]
    v = m[:, 2 * B_WIDTH:3 * B_WIDTH]
    lora = m[:, 3 * B_WIDTH:3 * B_WIDTH + LANES]
    gl = m[:, 3 * B_WIDTH + LANES:]
    lane = lax.broadcasted_iota(jnp.int32, lora.shape, 1)
    lora = jnp.where(lane < DECAY_LORA, jnp.tanh(lora), lora).astype(BF16)
    z = w0_ref[...] + _dot(lora, w2_ref[...])
    lw = -float(np.exp(-0.5)) * _sigmoid(z)
    a = _sigmoid(a0_ref[...] + _dot(lora, a2_ref[...]))
    g = _dot(_sigmoid(gl).astype(BF16), g2_ref[...])

    kk = k * kk_ref[...]
    ss = _dot_exact_rhs(kk * kk, bd_ref[...])
    kk = kk / jnp.maximum(jnp.sqrt(ss), 1e-12)
    k2 = k * (1.0 + (a - 1.0) * ka_ref[...])
    bonus = _dot_exact_rhs(r * k2 * rk_ref[...], bd_ref[...]) * v

    cum = _dot_exact_lhs(tri_ref[...], lw)
    g_in = jnp.exp(cum)
    g_ex = jnp.exp(cum - lw)
    g_inv = jnp.exp(-cum)
    a_t = -kk * g_ex
    b_t = kk * a * g_inv
    k_t = k2 * g_inv
    r_t = r * g_in

    ri = lax.broadcasted_iota(jnp.int32, (tc, 2 * tc), 0)
    ci = lax.broadcasted_iota(jnp.int32, (tc, 2 * tc), 1) % tc
    strict = ci < ri
    incl = ci <= ri
    eye = jnp.where(ci == ri, 1.0, 0.0)

    bd = _head_pair_diag
    bf = lambda x: x.astype(BF16)
    per_seq = tb // tc
    chunks = range(seqs * per_seq)
    items = [(c, p) for c in chunks for p in range(B_HEADS // 2)]
    lanes = [slice(p * 2 * n, (p + 1) * 2 * n) for p in range(B_HEADS // 2)]
    toks = [slice(c * tc, (c + 1) * tc) for c in chunks]
    g_end = [g_in[(c + 1) * tc - 1:(c + 1) * tc, :] for c in chunks]
    v2 = {(c, p): v[toks[c], lanes[p]] for c, p in items}
    v_bd = {i: bd(bf(v2[i])) for i in items}
    a2 = {(c, p): a_t[toks[c], lanes[p]] for c, p in items}
    r2 = {(c, p): r_t[toks[c], lanes[p]] for c, p in items}
    bk = {(c, p): jnp.concatenate([b_t[toks[c], lanes[p]], k_t[toks[c], lanes[p]]], axis=0) for c, p in items}
    bk_bd = {i: jnp.concatenate([bd(bf(bk[i][0:tc])), bd(bf(bk[i][tc:2 * tc]))], axis=0) for i in items}
    cross = {i: _dot(bf(jnp.concatenate([a2[i], r2[i]], axis=0)), bk_bd[i], trans_b=True)
             for i in items}
    l_ab = {i: jnp.where(strict, cross[i][0:tc, 0:2 * tc], 0.0) for i in items}
    l_akv = {i: _dot(bf(jnp.where(strict, cross[i][0:tc, 2 * tc:4 * tc], 0.0)), v_bd[i]) for i in items}
    m_rbk = {i: bf(jnp.concatenate([jnp.where(incl, cross[i][tc:2 * tc, 0:2 * tc], 0.0),
                                    jnp.where(incl, cross[i][tc:2 * tc, 2 * tc:4 * tc], 0.0)], axis=1))
             for i in items}

    inv = {i: eye + l_ab[i] for i in items}
    pwb = {i: bf(l_ab[i]) for i in items}
    span = 2
    while span < tc:
        pwb = {i: bf(_dot(pwb[i], bd(pwb[i]))) for i in items}
        inv = {i: inv[i] + _dot(bf(inv[i]), bd(pwb[i])) for i in items}
        span *= 2
    inv = {i: bf(inv[i]) for i in items}
    ar = {i: _split2(jnp.concatenate([_dot(inv[i], bd(bf(a2[i]))), r2[i]], axis=0)) for i in items}
    z_free = {i: _dot(inv[i], bd(bf(l_akv[i]))) for i in items}

    first_head = lax.broadcasted_iota(jnp.int32, (n, 2 * n), 1) < n
    s_cur = {(s, p): state[s, p] for s in range(seqs) for p in range(B_HEADS // 2)}
    for j in range(per_seq):
        here = [(s * per_seq + j, s, p) for s in range(seqs) for p in range(B_HEADS // 2)]
        from_state = {(c, p): _dot3s(ar[c, p], _split2(bd(s_cur[s, p])), trans_b=True) for c, s, p in here}
        z = {(c, p): from_state[c, p][0:tc] + z_free[c, p] for c, s, p in here}
        y = {(c, p): from_state[c, p][tc:2 * tc]
             + _dot(m_rbk[c, p], jnp.concatenate([bd(bf(z[c, p])), v_bd[c, p]], axis=0)) for c, s, p in here}
        zv = {(c, p): jnp.concatenate([z[c, p], v2[c, p]], axis=0) for c, s, p in here}
        full = {(c, p): _dot3(zv[c, p].T, bk[c, p] * g_end[c][:, lanes[p]]) for c, s, p in here}
        s_cur = {(s, p): s_cur[s, p] * g_end[c][:, lanes[p]]
                 + jnp.where(first_head, full[c, p][0:n], full[c, p][n:2 * n]) for c, s, p in here}
        for c, s, p in here:
            y_sc[toks[c], lanes[p]] = y[c, p]
    for (s, p), val in s_cur.items():
        state[s, p] = val

    y = y_sc[...]
    mean = _dot_exact_rhs(y, bd_ref[...]) * (1.0 / n)
    d = y - mean
    var = _dot_exact_rhs(d * d, bd_ref[...]) * (1.0 / n)
    yn = d * lax.rsqrt(var + GN_EPS) * gnw_ref[...] + gnb_ref[...]
    yb_ref[...] = ((yn + bonus) * g).astype(yb_ref.dtype).reshape(seqs, tb, B_WIDTH)

    @pl.when(t == pl.num_programs(1) - 1)
    def _():
        for s in range(seqs):
            for p in range(B_HEADS // 2):
                s_out_ref[s, 2 * p] = state[s, p][:, 0:n]
                s_out_ref[s, 2 * p + 1] = state[s, p][:, n:2 * n]


def _rwkv(pb3, shift_prev, s0, p, tc):
    b, s, _ = pb3.shape
    tb = min(s, 4 * tc)
    seqs = max(1, 4 * tc // tb)
    tri = jnp.asarray(np.kron(np.eye(seqs * tb // tc), np.tril(np.ones((tc, tc)))), BF16)
    vec = lambda w: pl.BlockSpec((1, w), lambda bi, t: (0, 0))
    mat = lambda r, c: pl.BlockSpec((r, c), lambda bi, t: (0, 0))
    st = pl.BlockSpec((seqs, B_HEADS, B_HEAD_DIM, B_HEAD_DIM), lambda bi, t: (bi, 0, 0, 0))
    return pl.pallas_call(
        functools.partial(_rwkv_kernel, tc=tc),
        grid=(b // seqs, s // tb),
        in_specs=[
            pl.BlockSpec((seqs, tb, SHIFT_W), lambda bi, t: (bi, t, CB_U // SHIFT_W)),
            pl.BlockSpec((seqs, 1, SHIFT_W), lambda bi, t: (bi, 0, 0)),
            st,
            vec(SHIFT_W), vec(B_WIDTH), vec(B_WIDTH), vec(B_WIDTH), vec(B_WIDTH), vec(B_WIDTH),
            vec(B_WIDTH), vec(B_WIDTH),
            mat(LANES, B_WIDTH), mat(LANES, B_WIDTH), mat(GATE_LORA, B_WIDTH), mat(B_WIDTH, B_WIDTH),
            mat(seqs * tb, seqs * tb),
        ],
        out_specs=[pl.BlockSpec((seqs, tb, B_WIDTH), lambda bi, t: (bi, t, 0)), st],
        out_shape=[jax.ShapeDtypeStruct((b, s, B_WIDTH), BF16),
                   jax.ShapeDtypeStruct((b, B_HEADS, B_HEAD_DIM, B_HEAD_DIM), F32)],
        scratch_shapes=[pltpu.VMEM((seqs, 1, SHIFT_W), F32),
                        pltpu.VMEM((seqs, B_HEADS // 2, B_HEAD_DIM, 2 * B_HEAD_DIM), F32),
                        pltpu.VMEM((seqs * tb, B_WIDTH), F32)],
        compiler_params=pltpu.CompilerParams(
            dimension_semantics=("arbitrary", "arbitrary"), vmem_limit_bytes=VMEM_LIMIT),
        name="rwkv",
    )(pb3, shift_prev, s0, p['shift_mu'], p['w0'], p['a0'], p['k_k'], p['k_a'], p['r_k'], p['gn_w'], p['gn_b'],
      p['w2p'], p['a2p'], p['g2'], p['bd'], tri)


def _out_ffn_kernel(oa_ref, yb_ref, ga_ref, gb_ref, x_ref, n2_ref, nf_ref,
                    woa_ref, wob_ref, wout_ref, wg_ref, wu_ref, wd_ref, o_ref):
    merged = (_sigmoid(ga_ref[...].astype(F32)) * _dot(oa_ref[...], woa_ref[...])
              + _sigmoid(gb_ref[...].astype(F32)) * _dot(yb_ref[...], wob_ref[...]))
    h = x_ref[...] + _dot(merged.astype(BF16), wout_ref[...])
    hn = _rms(h, n2_ref[...]).astype(BF16)
    out = h
    for c, n in _col_chunks(D_FF, 4 * MXU_COLS):
        gate = _dot(hn, wg_ref[:, c:c + n])
        up = _dot(hn, wu_ref[:, c:c + n])
        act = (gate * _sigmoid(gate) * up).astype(BF16)
        out = out + _dot(act, wd_ref[c:c + n, :])
    o_ref[...] = _rms(out, nf_ref[...])


def _out_ffn(o_a, y_b, pa, x2, p, norm_f):
    n = x2.shape[0]
    tm = min(n, 512)
    row = lambda w, j=0: pl.BlockSpec((tm, w), lambda i: (i, j))
    vec = pl.BlockSpec((1, D_MODEL), lambda i: (0, 0))
    resident = lambda r, c: pl.BlockSpec((r, c), lambda i: (0, 0), pipeline_mode=pl.Buffered(1))
    return pl.pallas_call(
        _out_ffn_kernel,
        grid=(n // tm,),
        in_specs=[
            row(A_Q), row(B_WIDTH),
            row(D_MODEL, CA_GA // D_MODEL), row(D_MODEL, CA_GB // D_MODEL), row(D_MODEL),
            vec, vec,
            resident(A_Q, D_MODEL), resident(B_WIDTH, D_MODEL), resident(D_MODEL, D_MODEL),
            resident(D_MODEL, D_FF), resident(D_MODEL, D_FF), resident(D_FF, D_MODEL),
        ],
        out_specs=row(D_MODEL),
        out_shape=jax.ShapeDtypeStruct((n, D_MODEL), F32),
        compiler_params=pltpu.CompilerParams(
            dimension_semantics=("arbitrary",), vmem_limit_bytes=VMEM_LIMIT),
        name="out_proj_ffn",
    )(o_a, y_b, pa, pa, x2, p['norm2'], norm_f, p['w_oa'], p['w_ob'], p['w_out'],
      p['w_gate'], p['w_up'], p['w_down'])


def _prepare_params(l, norm1, w_in, idx_k_g, idx_k_b, shift_mu, w0, w2, a0, a2, g2, k_k, k_a, r_k,
                    gn_w, gn_b, w_oa, w_ob, w_out, norm2, w_gate, w_up, w_down):
    w = jnp.transpose(w_in[l])
    o = np.cumsum([0, A_Q, A_KV, A_KV, IDX_Q, IDX_DIM, IDX_HEADS, SHIFT_W, D_MODEL, D_MODEL])
    seg = lambda i: w[o[i]:o[i + 1]]
    pad = jnp.zeros((LANES - IDX_DIM - IDX_HEADS, D_MODEL), w.dtype)
    w_in_p = jnp.concatenate([seg(3), seg(7), seg(8), seg(0), seg(6), seg(1), seg(2), seg(4), seg(5), pad],
                             axis=0).astype(BF16)
    zeros = jnp.zeros((LANES - DECAY_LORA, B_WIDTH), F32)
    head = np.arange(B_WIDTH) // B_HEAD_DIM
    row = lambda x: x[l].reshape(1, -1)
    return {
        'norm1': row(norm1), 'w_in_p': w_in_p,
        'idx_k_g': row(idx_k_g), 'idx_k_b': row(idx_k_b),
        'shift_mu': row(shift_mu), 'w0': row(w0), 'a0': row(a0),
        'k_k': row(k_k), 'k_a': row(k_a), 'r_k': row(r_k), 'gn_w': row(gn_w), 'gn_b': row(gn_b),
        'w2p': jnp.concatenate([w2[l], zeros], axis=0).astype(BF16),
        'a2p': jnp.concatenate([zeros, a2[l]], axis=0).astype(BF16),
        'g2': g2[l].astype(BF16),
        'bd': jnp.asarray(head[:, None] == head[None, :], BF16),
        'w_oa': w_oa[l].astype(BF16), 'w_ob': w_ob[l].astype(BF16), 'w_out': w_out[l].astype(BF16),
        'norm2': row(norm2),
        'w_gate': w_gate[l].astype(BF16), 'w_up': w_up[l].astype(BF16), 'w_down': w_down[l].astype(BF16),
    }


def _layer(x, attend, shift_prev, wkv0, p, norm_f, eye, chunk):
    b, t, _ = x.shape
    x2 = x.reshape(b * t, D_MODEL)
    pa, pb, *transposed = _in_proj(x2, p['norm1'], p['w_in_p'], eye, t)
    pa3 = pa.reshape(b, t, PA_WIDTH)
    pb3 = pb.reshape(b, t, PB_WIDTH)
    o_a, ik_ln = attend(pa3, pb3, *transposed)
    y_b, wkv_new = _rwkv(pb3, shift_prev, wkv0, p, chunk)
    flat = lambda z: z.reshape(b * t, z.shape[-1])
    out = _out_ffn(flat(o_a), flat(y_b), pa, x2, p, norm_f).reshape(b, t, D_MODEL)
    if transposed:
        k_new, v_new = (jnp.transpose(z.reshape(b, A_KV_HEADS, A_HEAD_DIM, t), (0, 3, 1, 2)) for z in transposed[:2])
    else:
        k_new = pb3[:, :, CB_K:CB_K + A_KV].reshape(b, t, A_KV_HEADS, A_HEAD_DIM)
        v_new = pb3[:, :, CB_V:CB_V + A_KV].reshape(b, t, A_KV_HEADS, A_HEAD_DIM)
    shift_new = pb3[:, t - 1:t, CB_U:CB_U + SHIFT_W]
    return out, k_new, v_new, ik_ln, wkv_new, shift_new


def kernel(x_prompt, x_sample, cache_k, cache_v, cache_idx_k, state_wkv, state_shift,
           norm1, w_in, idx_k_g, idx_k_b, shift_mu, w0, w2, a0, a2, g2, k_k, k_a, r_k,
           gn_w, gn_b, w_oa, w_ob, w_out, norm2, w_gate, w_up, w_down, norm_f):
    assert w_in.shape[0] == 1, "single-layer kernel"
    l = 0
    p = _prepare_params(l, norm1, w_in, idx_k_g, idx_k_b, shift_mu, w0, w2, a0, a2, g2, k_k, k_a, r_k,
                        gn_w, gn_b, w_oa, w_ob, w_out, norm2, w_gate, w_up, w_down)
    nf = norm_f.reshape(1, -1)
    tril = jnp.asarray(np.tril(np.ones((KEY_TILE, KEY_TILE), np.float32)), BF16)
    eye = jnp.asarray(np.eye(KEY_TILE, dtype=np.float32), BF16)

    n_p = x_prompt.shape[0]
    shift0 = jnp.zeros((n_p, 1, SHIFT_W), F32)
    wkv_zero = jnp.zeros((n_p, B_HEADS, B_HEAD_DIM, B_HEAD_DIM), F32)
    attend_p = lambda pa3, pb3, k_t, v_t, q_t: _attn_prompt(q_t, pb3, v_t, p['idx_k_g'], p['idx_k_b'], tril, eye)
    y_p, k_p, v_p, ik_p, wkv_p, shift_p = _layer(x_prompt, attend_p, shift0, wkv_zero, p, nf, eye, 64)

    n_s, t_s = x_sample.shape[0], x_sample.shape[1]
    past = cache_k.shape[2]
    ck = jnp.transpose(cache_k[l], (0, 2, 3, 1))
    cv = jnp.transpose(cache_v[l], (0, 2, 3, 1))
    cik = jnp.transpose(cache_idx_k[l], (0, 2, 1))
    attend_s = lambda pa3, pb3: _attn_sample(pa3, pb3, ck, cv, cik, p['idx_k_g'], p['idx_k_b'], tril, eye)
    y_s, k_s, v_s, ik_s, wkv_s, shift_s = _layer(x_sample, attend_s, state_shift[l], state_wkv[l], p, nf, eye, t_s)

    lead = lambda z: z[None]
    return (y_p, y_s, lead(k_p), lead(v_p), lead(ik_p), lead(wkv_p), lead(shift_p),
            lead(k_s), lead(v_s), lead(ik_s), lead(wkv_s), lead(shift_s))
```

```python
import functools

import jax
import jax.numpy as jnp
import numpy as np
from jax import lax
from jax.experimental import pallas as pl
from jax.experimental.pallas import tpu as pltpu

F32 = jnp.float32
BF16 = jnp.bfloat16

D_MODEL = 1024
CHUNK = 64
A_HEADS = 8
A_KV_HEADS = 2
A_GROUP = A_HEADS // A_KV_HEADS
A_HEAD_DIM = 64
A_Q = A_HEADS * A_HEAD_DIM
A_KV = A_KV_HEADS * A_HEAD_DIM
IDX_HEADS = 16
IDX_DIM = 64
IDX_Q = IDX_HEADS * IDX_DIM
TOPK_MAX = 256
B_HEADS = 8
B_HEAD_DIM = 64
B_WIDTH = B_HEADS * B_HEAD_DIM
DECAY_LORA = 64
AAA_LORA = 64
GATE_LORA = 128
SHIFT_W = 3 * B_WIDTH + DECAY_LORA + AAA_LORA + GATE_LORA
D_FF = 2816
RMS_EPS = 1e-6
LN_EPS = 1e-6
GN_EPS = 64e-5

LANES = 128
SUB = 8
VMEM_LIMIT = 56 * 1024 * 1024
NEG = -1e30
KEY_TILE = 256
V_ROWS = A_HEAD_DIM + 16
KX_WIDTH = 256

CA_IQ = 0
CA_GA = CA_IQ + IDX_Q
CA_GB = CA_GA + D_MODEL
CA_Q = CA_GB + D_MODEL
PA_WIDTH = CA_Q + A_Q
CB_U = 0
CB_K = CB_U + SHIFT_W
CB_V = CB_K + A_KV
CB_IKW = CB_V + A_KV
PB_WIDTH = CB_IKW + LANES
MXU_COLS = 256


def _dot(a, b, trans_b=False):
    dn = (((1,), (1 if trans_b else 0,)), ((), ()))
    return lax.dot_general(a, b, dn, preferred_element_type=F32)


def _split2(x):
    hi = x.astype(BF16)
    lo = (x - hi.astype(F32)).astype(BF16)
    return hi, lo


def _dot3s(a_split, b_split, trans_b=False):
    ah, al = a_split
    bh, bl = b_split
    return _dot(ah, bh, trans_b) + (_dot(ah, bl, trans_b) + _dot(al, bh, trans_b))


def _dot3(a, b, trans_b=False):
    return _dot3s(_split2(a), _split2(b), trans_b)


def _dot_exact_rhs(a, b_bf16, terms=2):
    out = None
    rem = a
    for _ in range(terms):
        part = rem.astype(BF16)
        rem = rem - part.astype(F32)
        d = _dot(part, b_bf16)
        out = d if out is None else out + d
    return out


def _dot_exact_lhs(a_bf16, b, terms=3, trans_b=False):
    out = None
    rem = b
    for _ in range(terms):
        part = rem.astype(BF16)
        rem = rem - part.astype(F32)
        d = _dot(a_bf16, part, trans_b)
        out = d if out is None else out + d
    return out


def _sigmoid(x):
    return 1.0 / (1.0 + jnp.exp(-x))


def _rms(x, g):
    ms = jnp.mean(x * x, axis=-1, keepdims=True)
    return x * lax.rsqrt(ms + RMS_EPS) * g


def _col_chunks(width, chunk):
    return [(c, min(chunk, width - c)) for c in range(0, width, chunk)]


QT_WIDTH = IDX_Q + A_Q


def _in_proj_kernel(x_ref, g_ref, w_ref, eye_ref, pa_ref, pb_ref, *t_refs):
    xn = _rms(x_ref[...], g_ref[...]).astype(BF16)
    for c, n in _col_chunks(PA_WIDTH, 2 * MXU_COLS):
        pa_ref[:, c:c + n] = _dot(xn, w_ref[c:c + n, :], trans_b=True).astype(BF16)
    for c, n in _col_chunks(PB_WIDTH, 2 * MXU_COLS):
        pb_ref[:, c:c + n] = _dot(xn, w_ref[PA_WIDTH + c:PA_WIDTH + c + n, :], trans_b=True)
    if t_refs:
        kt_ref, vt_ref, qt_ref = t_refs
        kt_ref[0] = pb_ref[:, CB_K:CB_K + A_KV].T
        vt_ref[0] = pb_ref[:, CB_V:CB_V + A_KV].T
        for dst, src in ((0, CA_IQ), (IDX_Q, CA_Q)):
            for c, n in _col_chunks(IDX_Q if src == CA_IQ else A_Q, MXU_COLS):
                qt_ref[0, dst + c:dst + c + n, :] = _transpose_bf16(
                    pa_ref[:, src + c:src + c + n], eye_ref[0:n, 0:n]).astype(BF16)


def _in_proj(x2, norm1, w_in_p, eye, seq):
    n = x2.shape[0]
    tm = min(n, 512)
    out_specs = [pl.BlockSpec((tm, PA_WIDTH), lambda i: (i, 0)), pl.BlockSpec((tm, PB_WIDTH), lambda i: (i, 0))]
    out_shape = [jax.ShapeDtypeStruct((n, PA_WIDTH), BF16), jax.ShapeDtypeStruct((n, PB_WIDTH), F32)]
    if seq % tm == 0:
        per_seq = seq // tm
        t_spec = lambda w: pl.BlockSpec((1, w, tm), lambda i: (i // per_seq, 0, i % per_seq))
        out_specs += [t_spec(A_KV), t_spec(A_KV), t_spec(QT_WIDTH)]
        out_shape += [jax.ShapeDtypeStruct((n // seq, A_KV, seq), F32)] * 2
        out_shape += [jax.ShapeDtypeStruct((n // seq, QT_WIDTH, seq), BF16)]
    return pl.pallas_call(
        _in_proj_kernel,
        grid=(n // tm,),
        in_specs=[
            pl.BlockSpec((tm, D_MODEL), lambda i: (i, 0)),
            pl.BlockSpec((1, D_MODEL), lambda i: (0, 0)),
            pl.BlockSpec((PA_WIDTH + PB_WIDTH, D_MODEL), lambda i: (0, 0), pipeline_mode=pl.Buffered(1)),
            pl.BlockSpec((KEY_TILE, KEY_TILE), lambda i: (0, 0)),
        ],
        out_specs=out_specs,
        out_shape=out_shape,
        compiler_params=pltpu.CompilerParams(
            dimension_semantics=("arbitrary",), vmem_limit_bytes=VMEM_LIMIT),
        name="in_proj",
    )(x2, norm1, w_in_p, eye)


def _layer_norm(x, g, b):
    mu = jnp.mean(x, axis=-1, keepdims=True)
    d = x - mu
    var = jnp.mean(d * d, axis=-1, keepdims=True)
    return d * lax.rsqrt(var + LN_EPS) * g + b


def _transpose_bf16(x, eye):
    return _dot(eye, x, trans_b=True)


def _key_loop(nt, body, init):
    if isinstance(nt, int):
        carry = init
        for j in range(nt):
            carry = body(j, carry)
        return carry
    carry = lax.fori_loop(0, nt // 2, lambda k, c: body(2 * k + 1, body(2 * k, c)), init)
    return lax.cond(nt % 2 == 1, lambda c: body(nt - 1, c), lambda c: c, carry)


def _key_off(j):
    return j * KEY_TILE if isinstance(j, int) else pl.multiple_of(j * KEY_TILE, KEY_TILE)


def _fold(x, op):
    x = x.reshape(x.shape[0] // SUB, SUB, x.shape[1])
    while x.shape[0] > 1:
        half = x.shape[0] // 2
        x = op(x[:half], x[half:])
    return x[0]


def _stage_queries(iq, q, iw, eye_ref, iq_t, w_full, q_t, rows, transposed):
    eye = eye_ref[0:IDX_DIM, 0:IDX_DIM]

    def head_t(x, h, dim):
        if transposed:
            return x[h * dim:(h + 1) * dim, :] * (dim ** -0.5)
        return _transpose_bf16((x[:, h * dim:(h + 1) * dim] * (dim ** -0.5)).astype(BF16), eye)

    for h in range(IDX_HEADS):
        iq_t[:, h * rows:(h + 1) * rows] = head_t(iq, h, IDX_DIM).astype(BF16)
    w_t = _dot_exact_lhs(eye_ref[0:IDX_HEADS, 0:IDX_HEADS], iw, trans_b=True)
    for h in range(IDX_HEADS):
        w_full[:, h * rows:(h + 1) * rows] = w_t[h:h + 1, :] * (IDX_HEADS ** -0.5)
    lanes_g = A_GROUP * rows
    q_t[...] = jnp.zeros(q_t.shape, BF16)
    for g in range(A_KV_HEADS):
        for hh in range(A_GROUP):
            col = g * lanes_g + hh * rows
            q_t[g * A_HEAD_DIM:(g + 1) * A_HEAD_DIM, col:col + rows] = head_t(
                q, g * A_GROUP + hh, A_HEAD_DIM).astype(BF16)


PACK_ORDER = (0, 2, 1, 3)


def _pack_quarters(acc):
    n = acc.shape[0] // 4
    a, b, c, d = (acc[i * n:(i + 1) * n] for i in range(4))
    lane = lax.broadcasted_iota(jnp.int32, (n, LANES), 1)
    low64 = lane < LANES // 2
    low32 = (lane & (LANES // 4)) == 0
    ab = jnp.where(low64, a, b) + pltpu.roll(jnp.where(low64, b, a), LANES // 2, 1)
    cd = jnp.where(low64, c, d) + pltpu.roll(jnp.where(low64, d, c), LANES // 2, 1)
    t1 = ab + pltpu.roll(ab, 3 * LANES // 4, 1)
    t2 = cd + pltpu.roll(cd, LANES // 4, 1)
    return jnp.where(low32, t1, t2)


def _index_scores(iq_t, w_full, kx, sc, sc_pk, nt, adm_fn, adm_pk_fn, rows):
    width = sc.shape[1]

    def body(j, carry):
        off = _key_off(j)
        lg = _dot(kx[pl.ds(off, KEY_TILE), A_KV:A_KV + IDX_DIM], iq_t[...])
        acc = None
        for c in range(IDX_HEADS * rows // width):
            x = jnp.maximum(lg[:, c * width:(c + 1) * width], 0.0) * w_full[:, c * width:(c + 1) * width]
            acc = x if acc is None else acc + x
        if sc_pk is not None:
            assert LANES // rows == 4
            pk = KEY_TILE // 4
            sc_pk[pl.ds(j * pk, pk), :] = jnp.where(adm_pk_fn(j), _pack_quarters(acc), -jnp.inf)
        shift = LANES // 2
        while shift >= rows:
            acc = acc + pltpu.roll(acc, shift, 1)
            shift //= 2
        sc[pl.ds(off, KEY_TILE), :] = jnp.where(adm_fn(j), acc, -jnp.inf)
        return carry

    _key_loop(nt, body, 0)


def _select_topk(sc, sc_pk, tril_ref, nt):
    shape = (SUB, sc.shape[1])
    inf = jnp.float32(jnp.inf)
    src = sc if sc_pk is None else sc_pk
    rows_t = KEY_TILE if sc_pk is None else KEY_TILE // 4

    def tile(j):
        start = _key_off(j) if sc_pk is None else j * rows_t
        return src[pl.ds(start, rows_t), :]

    def allsub(x, red):
        op = {jnp.sum: jnp.add, jnp.min: jnp.minimum, jnp.max: jnp.maximum}[red]
        if sc_pk is not None:
            x = op(op(x, pltpu.roll(x, LANES // 4, 1)),
                   op(pltpu.roll(x, LANES // 2, 1), pltpu.roll(x, 3 * LANES // 4, 1)))
        return jnp.broadcast_to(red(x, axis=0, keepdims=True), shape)

    def count_ge(t):
        t1 = t[0:1, :]
        acc = _key_loop(
            nt, lambda j, a: a + _fold(jnp.where(tile(j) >= t1, 1.0, 0.0), jnp.add), jnp.zeros(shape, F32))
        return allsub(acc, jnp.sum)

    def stats(j, c):
        mn, mx, na = c
        s = tile(j)
        fin = s > -inf
        return (jnp.minimum(mn, _fold(jnp.where(fin, s, inf), jnp.minimum)), jnp.maximum(mx, _fold(s, jnp.maximum)),
                na + _fold(jnp.where(fin, 1.0, 0.0), jnp.add))

    mn, mx, na = _key_loop(
        nt, stats, (jnp.full(shape, inf, F32), jnp.full(shape, -inf, F32), jnp.zeros(shape, F32)))
    lo0 = allsub(mn, jnp.min)
    mx = allsub(mx, jnp.max)
    n_adm = allsub(na, jnp.sum)
    kq = jnp.minimum(n_adm, float(TOPK_MAX))
    hi0 = mx + (jnp.abs(mx) * 1e-6 + 1e-30)

    def cond(c):
        return jnp.logical_and(c[0] < 400, c[5] > 0.5)

    def probe(x, lo, hi, c_lo, c_hi):
        c_x = count_ge(x)
        ge = c_x >= kq
        return jnp.where(ge, x, lo), jnp.where(ge, hi, x), jnp.where(ge, c_x, c_lo), jnp.where(ge, c_hi, c_x)

    def body(c):
        it, lo, hi, c_lo, c_hi, _ = c
        for _ in range(2):
            lo, hi, c_lo, c_hi = probe(0.5 * (lo + hi), lo, hi, c_lo, c_hi)
        nxt = 0.5 * (lo + hi)
        active = jnp.where(c_lo - c_hi > 1.5, jnp.where(nxt > lo, jnp.where(nxt < hi, 1.0, 0.0), 0.0), 0.0)
        active = jnp.where(n_adm > kq, active, 0.0)
        return it + 1, lo, hi, c_lo, c_hi, jnp.max(active)

    _, lo, _, c_lo, c_hi, _ = lax.while_loop(
        cond, body, (jnp.int32(0), lo0, hi0, n_adm, jnp.zeros(shape, F32), jnp.float32(1.0)))

    lo1 = lo[0:1, :]
    thr = allsub(
        _key_loop(nt, lambda j, a: jnp.minimum(a, _fold(jnp.where(tile(j) >= lo1, tile(j), inf), jnp.minimum)),
                  jnp.full(shape, inf, F32)),
        jnp.min)[0:1, :]
    take = (kq - c_hi)[0:1, :]
    has_tie = jnp.max(c_lo - kq) > 0.5

    @pl.when(jnp.logical_not(has_tie))
    def _():
        def wr(j, carry):
            off = _key_off(j)
            sc[pl.ds(off, KEY_TILE), :] = jnp.where(sc[pl.ds(off, KEY_TILE), :] >= thr, 0.0, NEG)
            return carry
        _key_loop(nt, wr, 0)

    @pl.when(has_tie)
    def _():
        def wr(j, seen):
            off = _key_off(j)
            s = sc[pl.ds(off, KEY_TILE), :]
            tie = jnp.where(s == thr, 1.0, 0.0)
            rank = _dot(tril_ref[...], tie.astype(BF16)) + seen
            keep_tie = jnp.where(s == thr, jnp.where(rank <= take, 0.0, NEG), NEG)
            sc[pl.ds(off, KEY_TILE), :] = jnp.where(s > thr, 0.0, keep_tie)
            return seen + jnp.sum(tie, axis=0, keepdims=True)
        lax.fori_loop(0, nt, wr, jnp.zeros((1, sc.shape[1]), F32))


def _attend(sc, q_t, kx, v_t, s_sc, acc_sc, nt):
    lanes = q_t.shape[1]
    lanes_g = lanes // A_KV_HEADS
    rep = lanes // sc.shape[1]

    def scores(j, macc):
        off = _key_off(j)
        bias = jnp.concatenate([sc[pl.ds(off, KEY_TILE), :]] * rep, axis=1)
        s = _dot(kx[pl.ds(off, KEY_TILE), 0:A_KV], q_t[...]) + bias
        s_sc[pl.ds(off, KEY_TILE), :] = s
        return jnp.maximum(macc, _fold(s, jnp.maximum))

    macc = _key_loop(nt, scores, jnp.full((SUB, lanes), NEG, F32))
    m = jnp.max(macc, axis=0, keepdims=True)
    acc_sc[...] = jnp.zeros(acc_sc.shape, F32)

    def weighted(j, carry):
        off = _key_off(j)
        for g in range(A_KV_HEADS):
            cols = slice(g * lanes_g, (g + 1) * lanes_g)
            p = jnp.exp((s_sc[pl.ds(off, KEY_TILE), cols] - m[:, cols]).astype(BF16))
            acc_sc[g] += _dot(v_t[g, :, pl.ds(off, KEY_TILE)], p)
        return carry

    _key_loop(nt, weighted, 0)


def _ones_row(cols):
    first = lax.broadcasted_iota(jnp.int32, (V_ROWS - A_HEAD_DIM, cols), 0) == 0
    return jnp.where(first, 1.0, 0.0).astype(BF16)


def _write_heads(o_ref, acc_sc, eye_ref, rows):
    eye = eye_ref[0:rows, 0:rows]
    for g in range(A_KV_HEADS):
        acc = acc_sc[g]
        o_t = (acc[0:A_HEAD_DIM] * (1.0 / acc[A_HEAD_DIM:A_HEAD_DIM + 1])).astype(BF16)
        for hh in range(A_GROUP):
            h = g * A_GROUP + hh
            o_ref[0, :, h * A_HEAD_DIM:(h + 1) * A_HEAD_DIM] = _transpose_bf16(
                o_t[:, hh * rows:(hh + 1) * rows], eye).astype(o_ref.dtype)


_ATTN_SCRATCH = lambda rows, keys: [
    pltpu.VMEM((keys, KX_WIDTH), BF16),
    pltpu.VMEM((A_KV_HEADS, V_ROWS, keys), BF16),
    pltpu.VMEM((IDX_DIM, IDX_HEADS * rows), BF16),
    pltpu.VMEM((1, IDX_HEADS * rows), F32),
    pltpu.VMEM((A_KV, A_HEADS * rows), BF16),
    pltpu.VMEM((keys, max(rows, LANES)), F32),
    pltpu.VMEM((keys, A_HEADS * rows), F32),
    pltpu.VMEM((A_KV_HEADS, V_ROWS, A_GROUP * rows), F32),
] + ([pltpu.VMEM((keys // (LANES // rows), LANES), F32)] if rows < LANES else [])


def _attn_prompt_kernel(iq_ref, q_ref, k_ref, v_ref, ikw_ref, ikwq_ref, lng_ref, lnb_ref, tril_ref, eye_ref,
                        o_ref, ikln_ref, kx, v_t, iq_t, w_full, q_t, sc, s_sc, acc_sc, *, rows):
    i = pl.program_id(1)

    @pl.when(i == 0)
    def _():
        keys = kx.shape[0]
        for g in range(A_KV_HEADS):
            sl = slice(g * A_HEAD_DIM, (g + 1) * A_HEAD_DIM)
            v_t[g, 0:A_HEAD_DIM, :] = v_ref[0, sl, :].astype(BF16)
            v_t[g, A_HEAD_DIM:V_ROWS, :] = _ones_row(keys)
        ln = _layer_norm(ikw_ref[0][:, :IDX_DIM], lng_ref[...], lnb_ref[...])
        ikln_ref[0] = ln.T
        kx[:, 0:A_KV] = k_ref[0].astype(BF16)
        kx[:, A_KV:A_KV + IDX_DIM] = ln.astype(BF16)
        kx[:, A_KV + IDX_DIM:KX_WIDTH] = jnp.zeros((keys, KX_WIDTH - A_KV - IDX_DIM), BF16)

    _stage_queries(iq_ref[0], q_ref[0], ikwq_ref[0][:, IDX_DIM:IDX_DIM + IDX_HEADS], eye_ref, iq_t, w_full, q_t, rows,
                   transposed=True)

    nt = ((i + 1) * rows + KEY_TILE - 1) // KEY_TILE
    q_pos = i * rows + lax.broadcasted_iota(jnp.int32, (KEY_TILE, rows), 1)
    q_end = (q_pos // CHUNK + 1) * CHUNK

    def adm(j):
        return j * KEY_TILE + lax.broadcasted_iota(jnp.int32, (KEY_TILE, rows), 0) < q_end

    _index_scores(iq_t, w_full, kx, sc, None, nt, adm, None, rows)
    _select_topk(sc, None, tril_ref, nt)
    _attend(sc, q_t, kx, v_t, s_sc, acc_sc, nt)
    _write_heads(o_ref, acc_sc, eye_ref, rows)


def _attn_prompt(q_t, pb3, v_t, idx_k_g, idx_k_b, tril, eye):
    b, s, _ = pb3.shape
    rows = 2 * LANES
    kernel = functools.partial(_attn_prompt_kernel, rows=rows)
    o_a, ik_t = pl.pallas_call(
        kernel,
        grid=(b, s // rows),
        in_specs=[
            pl.BlockSpec((1, IDX_Q, rows), lambda bi, i: (bi, 0, i)),
            pl.BlockSpec((1, A_Q, rows), lambda bi, i: (bi, IDX_Q // A_Q, i)),
            pl.BlockSpec((1, s, A_KV), lambda bi, i: (bi, 0, CB_K // A_KV)),
            pl.BlockSpec((1, A_KV, s), lambda bi, i: (bi, 0, 0)),
            pl.BlockSpec((1, s, LANES), lambda bi, i: (bi, 0, CB_IKW // LANES)),
            pl.BlockSpec((1, rows, LANES), lambda bi, i: (bi, i, CB_IKW // LANES)),
            pl.BlockSpec((1, IDX_DIM), lambda bi, i: (0, 0)),
            pl.BlockSpec((1, IDX_DIM), lambda bi, i: (0, 0)),
            pl.BlockSpec((KEY_TILE, KEY_TILE), lambda bi, i: (0, 0)),
            pl.BlockSpec((KEY_TILE, KEY_TILE), lambda bi, i: (0, 0)),
        ],
        out_specs=[
            pl.BlockSpec((1, rows, A_Q), lambda bi, i: (bi, i, 0)),
            pl.BlockSpec((1, IDX_DIM, s), lambda bi, i: (bi, 0, 0)),
        ],
        out_shape=[
            jax.ShapeDtypeStruct((b, s, A_Q), BF16),
            jax.ShapeDtypeStruct((b, IDX_DIM, s), F32),
        ],
        scratch_shapes=_ATTN_SCRATCH(rows, s),
        compiler_params=pltpu.CompilerParams(
            dimension_semantics=("arbitrary", "arbitrary"), vmem_limit_bytes=VMEM_LIMIT),
        name="attn_prompt",
    )(q_t, q_t, pb3, v_t, pb3, pb3, idx_k_g, idx_k_b, tril, eye)
    return o_a, jnp.transpose(ik_t, (0, 2, 1))


def _attn_sample_kernel(iq_ref, q_ref, k_ref, v_ref, ikw_ref, ck_ref, cv_ref, cik_ref, lng_ref, lnb_ref, tril_ref,
                        eye_ref, o_ref, ikln_ref, kx, v_t, iq_t, w_full, q_t, sc, s_sc, acc_sc, sc_pk,
                        *, rows, past, keys, pieces):
    new = rows
    pad = keys - past - new
    piece = pl.program_id(1)
    per_piece = past // pieces
    eye = eye_ref[0:A_HEAD_DIM, 0:A_HEAD_DIM]
    spare = jnp.zeros((KX_WIDTH - A_KV - IDX_DIM, KEY_TILE), BF16)
    for c in range(0, per_piece, KEY_TILE):
        dst = pl.ds(pl.multiple_of(piece * per_piece + c, KEY_TILE), KEY_TILE)
        cols = slice(c, c + KEY_TILE)
        stack = jnp.concatenate([ck_ref[0, g, :, cols].astype(BF16) for g in range(A_KV_HEADS)]
                                + [cik_ref[0, :, cols].astype(BF16), spare], axis=0)
        kx[dst, :] = _transpose_bf16(stack, eye_ref[...]).astype(BF16)
        for g in range(A_KV_HEADS):
            v_t[g, 0:A_HEAD_DIM, dst] = cv_ref[0, g, :, cols].astype(BF16)

    @pl.when(piece == pieces - 1)
    def _():
        vf = v_ref[0]
        for g in range(A_KV_HEADS):
            sl = slice(g * A_HEAD_DIM, (g + 1) * A_HEAD_DIM)
            v_t[g, 0:A_HEAD_DIM, past:past + new] = _transpose_bf16(vf[:, sl].astype(BF16), eye).astype(BF16)
            v_t[g, 0:A_HEAD_DIM, past + new:keys] = jnp.zeros((A_HEAD_DIM, pad), BF16)
            v_t[g, A_HEAD_DIM:V_ROWS, :] = _ones_row(keys)
        ikw = ikw_ref[0]
        ln = _layer_norm(ikw[:, :IDX_DIM], lng_ref[...], lnb_ref[...])
        ikln_ref[0] = ln
        kx[past:past + new, 0:A_KV] = k_ref[0].astype(BF16)
        kx[past:past + new, A_KV:A_KV + IDX_DIM] = ln.astype(BF16)
        kx[past:past + new, A_KV + IDX_DIM:KX_WIDTH] = jnp.zeros((new, KX_WIDTH - A_KV - IDX_DIM), BF16)
        kx[past + new:keys, :] = jnp.zeros((pad, KX_WIDTH), BF16)

        _stage_queries(iq_ref[0], q_ref[0], ikw[:, IDX_DIM:IDX_DIM + IDX_HEADS], eye_ref, iq_t, w_full, q_t, rows,
                       transposed=False)

        nt = keys // KEY_TILE

        def adm(j):
            return j * KEY_TILE + lax.broadcasted_iota(jnp.int32, (KEY_TILE, LANES), 0) < past + new

        pk = KEY_TILE // 4
        group = lax.broadcasted_iota(jnp.int32, (pk, LANES), 1) // rows
        sub_block = jnp.zeros((pk, LANES), jnp.int32)
        for c, blk in enumerate(PACK_ORDER):
            sub_block = jnp.where(group == c, blk, sub_block)
        key_in_tile = sub_block * pk + lax.broadcasted_iota(jnp.int32, (pk, LANES), 0)

        def adm_pk(j):
            return j * KEY_TILE + key_in_tile < past + new

        _index_scores(iq_t, w_full, kx, sc, sc_pk, nt, adm, adm_pk, rows)
        _select_topk(sc, sc_pk, tril_ref, nt)
        _attend(sc, q_t, kx, v_t, s_sc, acc_sc, nt)
        _write_heads(o_ref, acc_sc, eye_ref, rows)


def _attn_sample(pa3, pb3, cache_k, cache_v, cache_ik, idx_k_g, idx_k_b, tril, eye):
    b, t, _ = pa3.shape
    past = cache_ik.shape[2]
    pieces = 1
    keys = -(-(past + t) // KEY_TILE) * KEY_TILE
    kernel = functools.partial(_attn_sample_kernel, rows=t, past=past, keys=keys, pieces=pieces)
    return pl.pallas_call(
        kernel,
        grid=(b, pieces),
        in_specs=[
            pl.BlockSpec((1, t, IDX_Q), lambda bi, pc: (bi, 0, CA_IQ // IDX_Q)),
            pl.BlockSpec((1, t, A_Q), lambda bi, pc: (bi, 0, CA_Q // A_Q)),
            pl.BlockSpec((1, t, A_KV), lambda bi, pc: (bi, 0, CB_K // A_KV)),
            pl.BlockSpec((1, t, A_KV), lambda bi, pc: (bi, 0, CB_V // A_KV)),
            pl.BlockSpec((1, t, LANES), lambda bi, pc: (bi, 0, CB_IKW // LANES)),
            pl.BlockSpec((1, A_KV_HEADS, A_HEAD_DIM, past // pieces), lambda bi, pc: (bi, 0, 0, pc)),
            pl.BlockSpec((1, A_KV_HEADS, A_HEAD_DIM, past // pieces), lambda bi, pc: (bi, 0, 0, pc)),
            pl.BlockSpec((1, IDX_DIM, past // pieces), lambda bi, pc: (bi, 0, pc)),
            pl.BlockSpec((1, IDX_DIM), lambda bi, pc: (0, 0)),
            pl.BlockSpec((1, IDX_DIM), lambda bi, pc: (0, 0)),
            pl.BlockSpec((KEY_TILE, KEY_TILE), lambda bi, pc: (0, 0)),
            pl.BlockSpec((KEY_TILE, KEY_TILE), lambda bi, pc: (0, 0)),
        ],
        out_specs=[
            pl.BlockSpec((1, t, A_Q), lambda bi, pc: (bi, 0, 0)),
            pl.BlockSpec((1, t, IDX_DIM), lambda bi, pc: (bi, 0, 0)),
        ],
        out_shape=[
            jax.ShapeDtypeStruct((b, t, A_Q), BF16),
            jax.ShapeDtypeStruct((b, t, IDX_DIM), F32),
        ],
        scratch_shapes=_ATTN_SCRATCH(t, keys),
        compiler_params=pltpu.CompilerParams(
            dimension_semantics=("arbitrary", "arbitrary"), vmem_limit_bytes=VMEM_LIMIT),
        name="attn_sample",
    )(pa3, pa3, pb3, pb3, pb3, cache_k, cache_v, cache_ik, idx_k_g, idx_k_b, tril, eye)


def _head_pair_diag(x):
    w = x.shape[1] // 2
    first = lax.broadcasted_iota(jnp.int32, x.shape, 1) < w
    zero = jnp.zeros_like(x)
    return jnp.concatenate([jnp.where(first, x, zero), jnp.where(first, zero, x)], axis=0)


def _rwkv_kernel(u_ref, shift_ref, s0_ref, mu_ref, w0_ref, a0_ref, kk_ref, ka_ref, rk_ref, gnw_ref, gnb_ref,
                 w2_ref, a2_ref, g2_ref, bd_ref, tri_ref, yb_ref, s_out_ref, carry, state, y_sc, *, tc):
    t = pl.program_id(1)
    seqs, tb = u_ref.shape[0], u_ref.shape[1]
    u = u_ref[...].reshape(seqs * tb, SHIFT_W)
    n = B_HEAD_DIM

    @pl.when(t == 0)
    def _():
        carry[...] = shift_ref[...]
        for s in range(seqs):
            for p in range(B_HEADS // 2):
                state[s, p] = jnp.concatenate([s0_ref[s, 2 * p], s0_ref[s, 2 * p + 1]], axis=1)

    row = lax.broadcasted_iota(jnp.int32, u.shape, 0)
    u_prev = pltpu.roll(u, 1, 0)
    for s in range(seqs):
        u_prev = jnp.where(row == s * tb, carry[s], u_prev)
        carry[s] = u[(s + 1) * tb - 1:(s + 1) * tb, :]
    m = u + (u_prev - u) * mu_ref[...]

    r = m[:, 0:B_WIDTH]
    k = m[:, B_WIDTH:2 * B_WIDTH]
    v = m[:, 2 * B_WIDTH:3 * B_WIDTH]
    lora = m[:, 3 * B_WIDTH:3 * B_WIDTH + LANES]
    gl = m[:, 3 * B_WIDTH + LANES:]
    lane = lax.broadcasted_iota(jnp.int32, lora.shape, 1)
    lora = jnp.where(lane < DECAY_LORA, jnp.tanh(lora), lora).astype(BF16)
    z = w0_ref[...] + _dot(lora, w2_ref[...])
    lw = -float(np.exp(-0.5)) * _sigmoid(z)
    a = _sigmoid(a0_ref[...] + _dot(lora, a2_ref[...]))
    g = _dot(_sigmoid(gl).astype(BF16), g2_ref[...])

    kk = k * kk_ref[...]
    ss = _dot_exact_rhs(kk * kk, bd_ref[...])
    kk = kk / jnp.maximum(jnp.sqrt(ss), 1e-12)
    k2 = k * (1.0 + (a - 1.0) * ka_ref[...])
    bonus = _dot_exact_rhs(r * k2 * rk_ref[...], bd_ref[...]) * v

    cum = _dot_exact_lhs(tri_ref[...], lw)
    g_in = jnp.exp(cum)
    g_ex = jnp.exp(cum - lw)
    g_inv = jnp.exp(-cum)
    a_t = -kk * g_ex
    b_t = kk * a * g_inv
    k_t = k2 * g_inv
    r_t = r * g_in

    ri = lax.broadcasted_iota(jnp.int32, (tc, 2 * tc), 0)
    ci = lax.broadcasted_iota(jnp.int32, (tc, 2 * tc), 1) % tc
    strict = ci < ri
    incl = ci <= ri
    eye = jnp.where(ci == ri, 1.0, 0.0)

    bd = _head_pair_diag
    bf = lambda x: x.astype(BF16)
    per_seq = tb // tc
    chunks = range(seqs * per_seq)
    items = [(c, p) for c in chunks for p in range(B_HEADS // 2)]
    lanes = [slice(p * 2 * n, (p + 1) * 2 * n) for p in range(B_HEADS // 2)]
    toks = [slice(c * tc, (c + 1) * tc) for c in chunks]
    g_end = [g_in[(c + 1) * tc - 1:(c + 1) * tc, :] for c in chunks]
    v2 = {(c, p): v[toks[c], lanes[p]] for c, p in items}
    v_bd = {i: bd(bf(v2[i])) for i in items}
    a2 = {(c, p): a_t[toks[c], lanes[p]] for c, p in items}
    r2 = {(c, p): r_t[toks[c], lanes[p]] for c, p in items}
    bk = {(c, p): jnp.concatenate([b_t[toks[c], lanes[p]], k_t[toks[c], lanes[p]]], axis=0) for c, p in items}
    bk_bd = {i: jnp.concatenate([bd(bf(bk[i][0:tc])), bd(bf(bk[i][tc:2 * tc]))], axis=0) for i in items}
    cross = {i: _dot(bf(jnp.concatenate([a2[i], r2[i]], axis=0)), bk_bd[i], trans_b=True)
             for i in items}
    l_ab = {i: jnp.where(strict, cross[i][0:tc, 0:2 * tc], 0.0) for i in items}
    l_akv = {i: _dot(bf(jnp.where(strict, cross[i][0:tc, 2 * tc:4 * tc], 0.0)), v_bd[i]) for i in items}
    m_rbk = {i: bf(jnp.concatenate([jnp.where(incl, cross[i][tc:2 * tc, 0:2 * tc], 0.0),
                                    jnp.where(incl, cross[i][tc:2 * tc, 2 * tc:4 * tc], 0.0)], axis=1))
             for i in items}

    inv = {i: eye + l_ab[i] for i in items}
    pwb = {i: bf(l_ab[i]) for i in items}
    span = 2
    while span < tc:
        pwb = {i: bf(_dot(pwb[i], bd(pwb[i]))) for i in items}
        inv = {i: inv[i] + _dot(bf(inv[i]), bd(pwb[i])) for i in items}
        span *= 2
    inv = {i: bf(inv[i]) for i in items}
    ar = {i: _split2(jnp.concatenate([_dot(inv[i], bd(bf(a2[i]))), r2[i]], axis=0)) for i in items}
    z_free = {i: _dot(inv[i], bd(bf(l_akv[i]))) for i in items}

    first_head = lax.broadcasted_iota(jnp.int32, (n, 2 * n), 1) < n
    s_cur = {(s, p): state[s, p] for s in range(seqs) for p in range(B_HEADS // 2)}
    for j in range(per_seq):
        here = [(s * per_seq + j, s, p) for s in range(seqs) for p in range(B_HEADS // 2)]
        from_state = {(c, p): _dot3s(ar[c, p], _split2(bd(s_cur[s, p])), trans_b=True) for c, s, p in here}
        z = {(c, p): from_state[c, p][0:tc] + z_free[c, p] for c, s, p in here}
        y = {(c, p): from_state[c, p][tc:2 * tc]
             + _dot(m_rbk[c, p], jnp.concatenate([bd(bf(z[c, p])), v_bd[c, p]], axis=0)) for c, s, p in here}
        zv = {(c, p): jnp.concatenate([z[c, p], v2[c, p]], axis=0) for c, s, p in here}
        full = {(c, p): _dot3(zv[c, p].T, bk[c, p] * g_end[c][:, lanes[p]]) for c, s, p in here}
        s_cur = {(s, p): s_cur[s, p] * g_end[c][:, lanes[p]]
                 + jnp.where(first_head, full[c, p][0:n], full[c, p][n:2 * n]) for c, s, p in here}
        for c, s, p in here:
            y_sc[toks[c], lanes[p]] = y[c, p]
    for (s, p), val in s_cur.items():
        state[s, p] = val

    y = y_sc[...]
    mean = _dot_exact_rhs(y, bd_ref[...]) * (1.0 / n)
    d = y - mean
    var = _dot_exact_rhs(d * d, bd_ref[...]) * (1.0 / n)
    yn = d * lax.rsqrt(var + GN_EPS) * gnw_ref[...] + gnb_ref[...]
    yb_ref[...] = ((yn + bonus) * g).astype(yb_ref.dtype).reshape(seqs, tb, B_WIDTH)

    @pl.when(t == pl.num_programs(1) - 1)
    def _():
        for s in range(seqs):
            for p in range(B_HEADS // 2):
                s_out_ref[s, 2 * p] = state[s, p][:, 0:n]
                s_out_ref[s, 2 * p + 1] = state[s, p][:, n:2 * n]


def _rwkv(pb3, shift_prev, s0, p, tc):
    b, s, _ = pb3.shape
    tb = min(s, 4 * tc)
    seqs = max(1, 4 * tc // tb)
    tri = jnp.asarray(np.kron(np.eye(seqs * tb // tc), np.tril(np.ones((tc, tc)))), BF16)
    vec = lambda w: pl.BlockSpec((1, w), lambda bi, t: (0, 0))
    mat = lambda r, c: pl.BlockSpec((r, c), lambda bi, t: (0, 0))
    st = pl.BlockSpec((seqs, B_HEADS, B_HEAD_DIM, B_HEAD_DIM), lambda bi, t: (bi, 0, 0, 0))
    return pl.pallas_call(
        functools.partial(_rwkv_kernel, tc=tc),
        grid=(b // seqs, s // tb),
        in_specs=[
            pl.BlockSpec((seqs, tb, SHIFT_W), lambda bi, t: (bi, t, CB_U // SHIFT_W)),
            pl.BlockSpec((seqs, 1, SHIFT_W), lambda bi, t: (bi, 0, 0)),
            st,
            vec(SHIFT_W), vec(B_WIDTH), vec(B_WIDTH), vec(B_WIDTH), vec(B_WIDTH), vec(B_WIDTH),
            vec(B_WIDTH), vec(B_WIDTH),
            mat(LANES, B_WIDTH), mat(LANES, B_WIDTH), mat(GATE_LORA, B_WIDTH), mat(B_WIDTH, B_WIDTH),
            mat(seqs * tb, seqs * tb),
        ],
        out_specs=[pl.BlockSpec((seqs, tb, B_WIDTH), lambda bi, t: (bi, t, 0)), st],
        out_shape=[jax.ShapeDtypeStruct((b, s, B_WIDTH), BF16),
                   jax.ShapeDtypeStruct((b, B_HEADS, B_HEAD_DIM, B_HEAD_DIM), F32)],
        scratch_shapes=[pltpu.VMEM((seqs, 1, SHIFT_W), F32),
                        pltpu.VMEM((seqs, B_HEADS // 2, B_HEAD_DIM, 2 * B_HEAD_DIM), F32),
                        pltpu.VMEM((seqs * tb, B_WIDTH), F32)],
        compiler_params=pltpu.CompilerParams(
            dimension_semantics=("arbitrary", "arbitrary"), vmem_limit_bytes=VMEM_LIMIT),
        name="rwkv",
    )(pb3, shift_prev, s0, p['shift_mu'], p['w0'], p['a0'], p['k_k'], p['k_a'], p['r_k'], p['gn_w'], p['gn_b'],
      p['w2p'], p['a2p'], p['g2'], p['bd'], tri)


def _out_ffn_kernel(oa_ref, yb_ref, ga_ref, gb_ref, x_ref, n2_ref, nf_ref,
                    woa_ref, wob_ref, wout_ref, wg_ref, wu_ref, wd_ref, o_ref):
    merged = (_sigmoid(ga_ref[...].astype(F32)) * _dot(oa_ref[...], woa_ref[...])
              + _sigmoid(gb_ref[...].astype(F32)) * _dot(yb_ref[...], wob_ref[...]))
    h = x_ref[...] + _dot(merged.astype(BF16), wout_ref[...])
    hn = _rms(h, n2_ref[...]).astype(BF16)
    out = h
    for c, n in _col_chunks(D_FF, 4 * MXU_COLS):
        gate = _dot(hn, wg_ref[:, c:c + n])
        up = _dot(hn, wu_ref[:, c:c + n])
        act = (gate * _sigmoid(gate) * up).astype(BF16)
        out = out + _dot(act, wd_ref[c:c + n, :])
    o_ref[...] = _rms(out, nf_ref[...])


def _out_ffn(o_a, y_b, pa, x2, p, norm_f):
    n = x2.shape[0]
    tm = min(n, 512)
    row = lambda w, j=0: pl.BlockSpec((tm, w), lambda i: (i, j))
    vec = pl.BlockSpec((1, D_MODEL), lambda i: (0, 0))
    resident = lambda r, c: pl.BlockSpec((r, c), lambda i: (0, 0), pipeline_mode=pl.Buffered(1))
    return pl.pallas_call(
        _out_ffn_kernel,
        grid=(n // tm,),
        in_specs=[
            row(A_Q), row(B_WIDTH),
            row(D_MODEL, CA_GA // D_MODEL), row(D_MODEL, CA_GB // D_MODEL), row(D_MODEL),
            vec, vec,
            resident(A_Q, D_MODEL), resident(B_WIDTH, D_MODEL), resident(D_MODEL, D_MODEL),
            resident(D_MODEL, D_FF), resident(D_MODEL, D_FF), resident(D_FF, D_MODEL),
        ],
        out_specs=row(D_MODEL),
        out_shape=jax.ShapeDtypeStruct((n, D_MODEL), F32),
        compiler_params=pltpu.CompilerParams(
            dimension_semantics=("arbitrary",), vmem_limit_bytes=VMEM_LIMIT),
        name="out_proj_ffn",
    )(o_a, y_b, pa, pa, x2, p['norm2'], norm_f, p['w_oa'], p['w_ob'], p['w_out'],
      p['w_gate'], p['w_up'], p['w_down'])


def _prepare_params(l, norm1, w_in, idx_k_g, idx_k_b, shift_mu, w0, w2, a0, a2, g2, k_k, k_a, r_k,
                    gn_w, gn_b, w_oa, w_ob, w_out, norm2, w_gate, w_up, w_down):
    w = jnp.transpose(w_in[l])
    o = np.cumsum([0, A_Q, A_KV, A_KV, IDX_Q, IDX_DIM, IDX_HEADS, SHIFT_W, D_MODEL, D_MODEL])
    seg = lambda i: w[o[i]:o[i + 1]]
    pad = jnp.zeros((LANES - IDX_DIM - IDX_HEADS, D_MODEL), w.dtype)
    w_in_p = jnp.concatenate([seg(3), seg(7), seg(8), seg(0), seg(6), seg(1), seg(2), seg(4), seg(5), pad],
                             axis=0).astype(BF16)
    zeros = jnp.zeros((LANES - DECAY_LORA, B_WIDTH), F32)
    head = np.arange(B_WIDTH) // B_HEAD_DIM
    row = lambda x: x[l].reshape(1, -1)
    return {
        'norm1': row(norm1), 'w_in_p': w_in_p,
        'idx_k_g': row(idx_k_g), 'idx_k_b': row(idx_k_b),
        'shift_mu': row(shift_mu), 'w0': row(w0), 'a0': row(a0),
        'k_k': row(k_k), 'k_a': row(k_a), 'r_k': row(r_k), 'gn_w': row(gn_w), 'gn_b': row(gn_b),
        'w2p': jnp.concatenate([w2[l], zeros], axis=0).astype(BF16),
        'a2p': jnp.concatenate([zeros, a2[l]], axis=0).astype(BF16),
        'g2': g2[l].astype(BF16),
        'bd': jnp.asarray(head[:, None] == head[None, :], BF16),
        'w_oa': w_oa[l].astype(BF16), 'w_ob': w_ob[l].astype(BF16), 'w_out': w_out[l].astype(BF16),
        'norm2': row(norm2),
        'w_gate': w_gate[l].astype(BF16), 'w_up': w_up[l].astype(BF16), 'w_down': w_down[l].astype(BF16),
    }


def _layer(x, attend, shift_prev, wkv0, p, norm_f, eye, chunk):
    b, t, _ = x.shape
    x2 = x.reshape(b * t, D_MODEL)
    pa, pb, *transposed = _in_proj(x2, p['norm1'], p['w_in_p'], eye, t)
    pa3 = pa.reshape(b, t, PA_WIDTH)
    pb3 = pb.reshape(b, t, PB_WIDTH)
    o_a, ik_ln = attend(pa3, pb3, *transposed)
    y_b, wkv_new = _rwkv(pb3, shift_prev, wkv0, p, chunk)
    flat = lambda z: z.reshape(b * t, z.shape[-1])
    out = _out_ffn(flat(o_a), flat(y_b), pa, x2, p, norm_f).reshape(b, t, D_MODEL)
    if transposed:
        k_new, v_new = (jnp.transpose(z.reshape(b, A_KV_HEADS, A_HEAD_DIM, t), (0, 3, 1, 2)) for z in transposed[:2])
    else:
        k_new = pb3[:, :, CB_K:CB_K + A_KV].reshape(b, t, A_KV_HEADS, A_HEAD_DIM)
        v_new = pb3[:, :, CB_V:CB_V + A_KV].reshape(b, t, A_KV_HEADS, A_HEAD_DIM)
    shift_new = pb3[:, t - 1:t, CB_U:CB_U + SHIFT_W]
    return out, k_new, v_new, ik_ln, wkv_new, shift_new


def kernel(x_prompt, x_sample, cache_k, cache_v, cache_idx_k, state_wkv, state_shift,
           norm1, w_in, idx_k_g, idx_k_b, shift_mu, w0, w2, a0, a2, g2, k_k, k_a, r_k,
           gn_w, gn_b, w_oa, w_ob, w_out, norm2, w_gate, w_up, w_down, norm_f):
    assert w_in.shape[0] == 1, "single-layer kernel"
    l = 0
    p = _prepare_params(l, norm1, w_in, idx_k_g, idx_k_b, shift_mu, w0, w2, a0, a2, g2, k_k, k_a, r_k,
                        gn_w, gn_b, w_oa, w_ob, w_out, norm2, w_gate, w_up, w_down)
    nf = norm_f.reshape(1, -1)
    tril = jnp.asarray(np.tril(np.ones((KEY_TILE, KEY_TILE), np.float32)), BF16)
    eye = jnp.asarray(np.eye(KEY_TILE, dtype=np.float32), BF16)

    n_p = x_prompt.shape[0]
    shift0 = jnp.zeros((n_p, 1, SHIFT_W), F32)
    wkv_zero = jnp.zeros((n_p, B_HEADS, B_HEAD_DIM, B_HEAD_DIM), F32)
    attend_p = lambda pa3, pb3, k_t, v_t, q_t: _attn_prompt(q_t, pb3, v_t, p['idx_k_g'], p['idx_k_b'], tril, eye)
    y_p, k_p, v_p, ik_p, wkv_p, shift_p = _layer(x_prompt, attend_p, shift0, wkv_zero, p, nf, eye, 64)

    n_s, t_s = x_sample.shape[0], x_sample.shape[1]
    past = cache_k.shape[2]
    ck = jnp.transpose(cache_k[l], (0, 2, 3, 1))
    cv = jnp.transpose(cache_v[l], (0, 2, 3, 1))
    cik = jnp.transpose(cache_idx_k[l], (0, 2, 1))
    attend_s = lambda pa3, pb3: _attn_sample(pa3, pb3, ck, cv, cik, p['idx_k_g'], p['idx_k_b'], tril, eye)
    y_s, k_s, v_s, ik_s, wkv_s, shift_s = _layer(x_sample, attend_s, state_shift[l], state_wkv[l], p, nf, eye, t_s)

    lead = lambda z: z[None]
    return (y_p, y_s, lead(k_p), lead(v_p), lead(ik_p), lead(wkv_p), lead(shift_p),
            lead(k_s), lead(v_s), lead(ik_s), lead(wkv_s), lead(shift_s))
```

```python
import functools

import jax
import jax.numpy as jnp
import numpy as np
from jax import lax
from jax.experimental import pallas as pl
from jax.experimental.pallas import tpu as pltpu

F32 = jnp.float32
BF16 = jnp.bfloat16

D_MODEL = 1024
CHUNK = 64
A_HEADS = 8
A_KV_HEADS = 2
A_GROUP = A_HEADS // A_KV_HEADS
A_HEAD_DIM = 64
A_Q = A_HEADS * A_HEAD_DIM
A_KV = A_KV_HEADS * A_HEAD_DIM
IDX_HEADS = 16
IDX_DIM = 64
IDX_Q = IDX_HEADS * IDX_DIM
TOPK_MAX = 256
B_HEADS = 8
B_HEAD_DIM = 64
B_WIDTH = B_HEADS * B_HEAD_DIM
DECAY_LORA = 64
AAA_LORA = 64
GATE_LORA = 128
SHIFT_W = 3 * B_WIDTH + DECAY_LORA + AAA_LORA + GATE_LORA
D_FF = 2816
RMS_EPS = 1e-6
LN_EPS = 1e-6
GN_EPS = 64e-5

LANES = 128
SUB = 8
VMEM_LIMIT = 56 * 1024 * 1024
NEG = -1e30
KEY_TILE = 256
V_ROWS = A_HEAD_DIM + 16
KX_WIDTH = 256

CA_IQ = 0
CA_GA = CA_IQ + IDX_Q
CA_GB = CA_GA + D_MODEL
CA_Q = CA_GB + D_MODEL
PA_WIDTH = CA_Q + A_Q
CB_U = 0
CB_K = CB_U + SHIFT_W
CB_V = CB_K + A_KV
CB_IKW = CB_V + A_KV
PB_WIDTH = CB_IKW + LANES
MXU_COLS = 256


def _dot(a, b, trans_b=False):
    dn = (((1,), (1 if trans_b else 0,)), ((), ()))
    return lax.dot_general(a, b, dn, preferred_element_type=F32)


def _split2(x):
    hi = x.astype(BF16)
    lo = (x - hi.astype(F32)).astype(BF16)
    return hi, lo


def _dot3s(a_split, b_split, trans_b=False):
    ah, al = a_split
    bh, bl = b_split
    return _dot(ah, bh, trans_b) + (_dot(ah, bl, trans_b) + _dot(al, bh, trans_b))


def _dot3(a, b, trans_b=False):
    return _dot3s(_split2(a), _split2(b), trans_b)


def _dot_exact_rhs(a, b_bf16, terms=2):
    out = None
    rem = a
    for _ in range(terms):
        part = rem.astype(BF16)
        rem = rem - part.astype(F32)
        d = _dot(part, b_bf16)
        out = d if out is None else out + d
    return out


def _dot_exact_lhs(a_bf16, b, terms=3, trans_b=False):
    out = None
    rem = b
    for _ in range(terms):
        part = rem.astype(BF16)
        rem = rem - part.astype(F32)
        d = _dot(a_bf16, part, trans_b)
        out = d if out is None else out + d
    return out


def _sigmoid(x):
    return 1.0 / (1.0 + jnp.exp(-x))


def _rms(x, g):
    ms = jnp.mean(x * x, axis=-1, keepdims=True)
    return x * lax.rsqrt(ms + RMS_EPS) * g


def _col_chunks(width, chunk):
    return [(c, min(chunk, width - c)) for c in range(0, width, chunk)]


QT_WIDTH = IDX_Q + A_Q


def _in_proj_kernel(x_ref, g_ref, w_ref, eye_ref, pa_ref, pb_ref, *t_refs):
    xn = _rms(x_ref[...], g_ref[...]).astype(BF16)
    for c, n in _col_chunks(PA_WIDTH, 2 * MXU_COLS):
        pa_ref[:, c:c + n] = _dot(xn, w_ref[c:c + n, :], trans_b=True).astype(BF16)
    for c, n in _col_chunks(PB_WIDTH, 2 * MXU_COLS):
        pb_ref[:, c:c + n] = _dot(xn, w_ref[PA_WIDTH + c:PA_WIDTH + c + n, :], trans_b=True)
    if t_refs:
        kt_ref, vt_ref, qt_ref = t_refs
        kt_ref[0] = pb_ref[:, CB_K:CB_K + A_KV].T
        vt_ref[0] = pb_ref[:, CB_V:CB_V + A_KV].T
        for dst, src in ((0, CA_IQ), (IDX_Q, CA_Q)):
            for c, n in _col_chunks(IDX_Q if src == CA_IQ else A_Q, MXU_COLS):
                qt_ref[0, dst + c:dst + c + n, :] = _transpose_bf16(
                    pa_ref[:, src + c:src + c + n], eye_ref[0:n, 0:n]).astype(BF16)


def _in_proj(x2, norm1, w_in_p, eye, seq):
    n = x2.shape[0]
    tm = min(n, 512)
    out_specs = [pl.BlockSpec((tm, PA_WIDTH), lambda i: (i, 0)), pl.BlockSpec((tm, PB_WIDTH), lambda i: (i, 0))]
    out_shape = [jax.ShapeDtypeStruct((n, PA_WIDTH), BF16), jax.ShapeDtypeStruct((n, PB_WIDTH), F32)]
    if seq % tm == 0:
        per_seq = seq // tm
        t_spec = lambda w: pl.BlockSpec((1, w, tm), lambda i: (i // per_seq, 0, i % per_seq))
        out_specs += [t_spec(A_KV), t_spec(A_KV), t_spec(QT_WIDTH)]
        out_shape += [jax.ShapeDtypeStruct((n // seq, A_KV, seq), F32)] * 2
        out_shape += [jax.ShapeDtypeStruct((n // seq, QT_WIDTH, seq), BF16)]
    return pl.pallas_call(
        _in_proj_kernel,
        grid=(n // tm,),
        in_specs=[
            pl.BlockSpec((tm, D_MODEL), lambda i: (i, 0)),
            pl.BlockSpec((1, D_MODEL), lambda i: (0, 0)),
            pl.BlockSpec((PA_WIDTH + PB_WIDTH, D_MODEL), lambda i: (0, 0), pipeline_mode=pl.Buffered(1)),
            pl.BlockSpec((KEY_TILE, KEY_TILE), lambda i: (0, 0)),
        ],
        out_specs=out_specs,
        out_shape=out_shape,
        compiler_params=pltpu.CompilerParams(
            dimension_semantics=("arbitrary",), vmem_limit_bytes=VMEM_LIMIT),
        name="in_proj",
    )(x2, norm1, w_in_p, eye)


def _layer_norm(x, g, b):
    mu = jnp.mean(x, axis=-1, keepdims=True)
    d = x - mu
    var = jnp.mean(d * d, axis=-1, keepdims=True)
    return d * lax.rsqrt(var + LN_EPS) * g + b


def _transpose_bf16(x, eye):
    return _dot(eye, x, trans_b=True)


def _key_loop(nt, body, init):
    if isinstance(nt, int):
        carry = init
        for j in range(nt):
            carry = body(j, carry)
        return carry
    carry = lax.fori_loop(0, nt // 2, lambda k, c: body(2 * k + 1, body(2 * k, c)), init)
    return lax.cond(nt % 2 == 1, lambda c: body(nt - 1, c), lambda c: c, carry)


def _key_off(j):
    return j * KEY_TILE if isinstance(j, int) else pl.multiple_of(j * KEY_TILE, KEY_TILE)


def _fold(x, op):
    x = x.reshape(x.shape[0] // SUB, SUB, x.shape[1])
    while x.shape[0] > 1:
        half = x.shape[0] // 2
        x = op(x[:half], x[half:])
    return x[0]


def _stage_queries(iq, q, iw, eye_ref, iq_t, w_full, q_t, rows, transposed):
    eye = eye_ref[0:IDX_DIM, 0:IDX_DIM]

    def head_t(x, h, dim):
        if transposed:
            return x[h * dim:(h + 1) * dim, :] * (dim ** -0.5)
        return _transpose_bf16((x[:, h * dim:(h + 1) * dim] * (dim ** -0.5)).astype(BF16), eye)

    for h in range(IDX_HEADS):
        iq_t[:, h * rows:(h + 1) * rows] = head_t(iq, h, IDX_DIM).astype(BF16)
    w_t = _dot_exact_lhs(eye_ref[0:IDX_HEADS, 0:IDX_HEADS], iw, trans_b=True)
    for h in range(IDX_HEADS):
        w_full[:, h * rows:(h + 1) * rows] = w_t[h:h + 1, :] * (IDX_HEADS ** -0.5)
    lanes_g = A_GROUP * rows
    q_t[...] = jnp.zeros(q_t.shape, BF16)
    for g in range(A_KV_HEADS):
        for hh in range(A_GROUP):
            col = g * lanes_g + hh * rows
            q_t[g * A_HEAD_DIM:(g + 1) * A_HEAD_DIM, col:col + rows] = head_t(
                q, g * A_GROUP + hh, A_HEAD_DIM).astype(BF16)


PACK_ORDER = (0, 2, 1, 3)


def _pack_quarters(acc):
    n = acc.shape[0] // 4
    a, b, c, d = (acc[i * n:(i + 1) * n] for i in range(4))
    lane = lax.broadcasted_iota(jnp.int32, (n, LANES), 1)
    low64 = lane < LANES // 2
    low32 = (lane & (LANES // 4)) == 0
    ab = jnp.where(low64, a, b) + pltpu.roll(jnp.where(low64, b, a), LANES // 2, 1)
    cd = jnp.where(low64, c, d) + pltpu.roll(jnp.where(low64, d, c), LANES // 2, 1)
    t1 = ab + pltpu.roll(ab, 3 * LANES // 4, 1)
    t2 = cd + pltpu.roll(cd, LANES // 4, 1)
    return jnp.where(low32, t1, t2)


def _index_scores(iq_t, w_full, kx, sc, sc_pk, nt, adm_fn, adm_pk_fn, rows):
    width = sc.shape[1]

    def body(j, carry):
        off = _key_off(j)
        lg = _dot(kx[pl.ds(off, KEY_TILE), A_KV:A_KV + IDX_DIM], iq_t[...])
        acc = None
        for c in range(IDX_HEADS * rows // width):
            x = jnp.maximum(lg[:, c * width:(c + 1) * width], 0.0) * w_full[:, c * width:(c + 1) * width]
            acc = x if acc is None else acc + x
        if sc_pk is not None:
            assert LANES // rows == 4
            pk = KEY_TILE // 4
            sc_pk[pl.ds(j * pk, pk), :] = jnp.where(adm_pk_fn(j), _pack_quarters(acc), -jnp.inf)
        shift = LANES // 2
        while shift >= rows:
            acc = acc + pltpu.roll(acc, shift, 1)
            shift //= 2
        sc[pl.ds(off, KEY_TILE), :] = jnp.where(adm_fn(j), acc, -jnp.inf)
        return carry

    _key_loop(nt, body, 0)


def _select_topk(sc, sc_pk, tril_ref, nt):
    shape = (SUB, sc.shape[1])
    inf = jnp.float32(jnp.inf)
    src = sc if sc_pk is None else sc_pk
    rows_t = KEY_TILE if sc_pk is None else KEY_TILE // 4

    def tile(j):
        start = _key_off(j) if sc_pk is None else j * rows_t
        return src[pl.ds(start, rows_t), :]

    def allsub(x, red):
        op = {jnp.sum: jnp.add, jnp.min: jnp.minimum, jnp.max: jnp.maximum}[red]
        if sc_pk is not None:
            x = op(op(x, pltpu.roll(x, LANES // 4, 1)),
                   op(pltpu.roll(x, LANES // 2, 1), pltpu.roll(x, 3 * LANES // 4, 1)))
        return jnp.broadcast_to(red(x, axis=0, keepdims=True), shape)

    def count_ge(t):
        t1 = t[0:1, :]
        acc = _key_loop(
            nt, lambda j, a: a + _fold(jnp.where(tile(j) >= t1, 1.0, 0.0), jnp.add), jnp.zeros(shape, F32))
        return allsub(acc, jnp.sum)

    def stats(j, c):
        mn, mx, na = c
        s = tile(j)
        fin = s > -inf
        return (jnp.minimum(mn, _fold(jnp.where(fin, s, inf), jnp.minimum)), jnp.maximum(mx, _fold(s, jnp.maximum)),
                na + _fold(jnp.where(fin, 1.0, 0.0), jnp.add))

    mn, mx, na = _key_loop(
        nt, stats, (jnp.full(shape, inf, F32), jnp.full(shape, -inf, F32), jnp.zeros(shape, F32)))
    lo0 = allsub(mn, jnp.min)
    mx = allsub(mx, jnp.max)
    n_adm = allsub(na, jnp.sum)
    kq = jnp.minimum(n_adm, float(TOPK_MAX))
    hi0 = mx + (jnp.abs(mx) * 1e-6 + 1e-30)

    def cond(c):
        return jnp.logical_and(c[0] < 400, c[5] > 0.5)

    def probe(x, lo, hi, c_lo, c_hi):
        c_x = count_ge(x)
        ge = c_x >= kq
        return jnp.where(ge, x, lo), jnp.where(ge, hi, x), jnp.where(ge, c_x, c_lo), jnp.where(ge, c_hi, c_x)

    def body(c):
        it, lo, hi, c_lo, c_hi, _ = c
        for _ in range(2):
            lo, hi, c_lo, c_hi = probe(0.5 * (lo + hi), lo, hi, c_lo, c_hi)
        nxt = 0.5 * (lo + hi)
        active = jnp.where(c_lo - c_hi > 1.5, jnp.where(nxt > lo, jnp.where(nxt < hi, 1.0, 0.0), 0.0), 0.0)
        active = jnp.where(n_adm > kq, active, 0.0)
        return it + 1, lo, hi, c_lo, c_hi, jnp.max(active)

    _, lo, _, c_lo, c_hi, _ = lax.while_loop(
        cond, body, (jnp.int32(0), lo0, hi0, n_adm, jnp.zeros(shape, F32), jnp.float32(1.0)))

    lo1 = lo[0:1, :]
    thr = allsub(
        _key_loop(nt, lambda j, a: jnp.minimum(a, _fold(jnp.where(tile(j) >= lo1, tile(j), inf), jnp.minimum)),
                  jnp.full(shape, inf, F32)),
        jnp.min)[0:1, :]
    take = (kq - c_hi)[0:1, :]
    has_tie = jnp.max(c_lo - kq) > 0.5

    @pl.when(jnp.logical_not(has_tie))
    def _():
        def wr(j, carry):
            off = _key_off(j)
            sc[pl.ds(off, KEY_TILE), :] = jnp.where(sc[pl.ds(off, KEY_TILE), :] >= thr, 0.0, NEG)
            return carry
        _key_loop(nt, wr, 0)

    @pl.when(has_tie)
    def _():
        def wr(j, seen):
            off = _key_off(j)
            s = sc[pl.ds(off, KEY_TILE), :]
            tie = jnp.where(s == thr, 1.0, 0.0)
            rank = _dot(tril_ref[...], tie.astype(BF16)) + seen
            keep_tie = jnp.where(s == thr, jnp.where(rank <= take, 0.0, NEG), NEG)
            sc[pl.ds(off, KEY_TILE), :] = jnp.where(s > thr, 0.0, keep_tie)
            return seen + jnp.sum(tie, axis=0, keepdims=True)
        lax.fori_loop(0, nt, wr, jnp.zeros((1, sc.shape[1]), F32))


def _attend(sc, q_t, kx, v_t, s_sc, acc_sc, nt):
    lanes = q_t.shape[1]
    lanes_g = lanes // A_KV_HEADS
    rep = lanes // sc.shape[1]

    def scores(j, macc):
        off = _key_off(j)
        bias = jnp.concatenate([sc[pl.ds(off, KEY_TILE), :]] * rep, axis=1)
        s = _dot(kx[pl.ds(off, KEY_TILE), 0:A_KV], q_t[...]) + bias
        s_sc[pl.ds(off, KEY_TILE), :] = s
        return jnp.maximum(macc, _fold(s, jnp.maximum))

    macc = _key_loop(nt, scores, jnp.full((SUB, lanes), NEG, F32))
    m = jnp.max(macc, axis=0, keepdims=True)
    acc_sc[...] = jnp.zeros(acc_sc.shape, F32)

    def weighted(j, carry):
        off = _key_off(j)
        for g in range(A_KV_HEADS):
            cols = slice(g * lanes_g, (g + 1) * lanes_g)
            p = jnp.exp((s_sc[pl.ds(off, KEY_TILE), cols] - m[:, cols]).astype(BF16))
            acc_sc[g] += _dot(v_t[g, :, pl.ds(off, KEY_TILE)], p)
        return carry

    _key_loop(nt, weighted, 0)


def _ones_row(cols):
    first = lax.broadcasted_iota(jnp.int32, (V_ROWS - A_HEAD_DIM, cols), 0) == 0
    return jnp.where(first, 1.0, 0.0).astype(BF16)


def _write_heads(o_ref, acc_sc, eye_ref, rows):
    width = A_GROUP * A_HEAD_DIM
    for g in range(A_KV_HEADS):
        acc = acc_sc[g]
        o_t = (acc[0:A_HEAD_DIM] * (1.0 / acc[A_HEAD_DIM:A_HEAD_DIM + 1])).astype(BF16)
        stacked = jnp.concatenate([o_t[:, hh * rows:(hh + 1) * rows] for hh in range(A_GROUP)], axis=0)
        o_ref[0, :, g * width:(g + 1) * width] = _transpose_bf16(stacked, eye_ref[0:rows, 0:rows]).astype(o_ref.dtype)


_ATTN_SCRATCH = lambda rows, keys: [
    pltpu.VMEM((keys, KX_WIDTH), BF16),
    pltpu.VMEM((A_KV_HEADS, V_ROWS, keys), BF16),
    pltpu.VMEM((IDX_DIM, IDX_HEADS * rows), BF16),
    pltpu.VMEM((1, IDX_HEADS * rows), F32),
    pltpu.VMEM((A_KV, A_HEADS * rows), BF16),
    pltpu.VMEM((keys, max(rows, LANES)), F32),
    pltpu.VMEM((keys, A_HEADS * rows), F32),
    pltpu.VMEM((A_KV_HEADS, V_ROWS, A_GROUP * rows), F32),
] + ([pltpu.VMEM((keys // (LANES // rows), LANES), F32)] if rows < LANES else [])


def _attn_prompt_kernel(iq_ref, q_ref, k_ref, v_ref, ikw_ref, ikwq_ref, lng_ref, lnb_ref, tril_ref, eye_ref,
                        o_ref, ikln_ref, kx, v_t, iq_t, w_full, q_t, sc, s_sc, acc_sc, *, rows):
    i = pl.program_id(1)

    @pl.when(i == 0)
    def _():
        keys = kx.shape[0]
        for g in range(A_KV_HEADS):
            sl = slice(g * A_HEAD_DIM, (g + 1) * A_HEAD_DIM)
            v_t[g, 0:A_HEAD_DIM, :] = v_ref[0, sl, :].astype(BF16)
            v_t[g, A_HEAD_DIM:V_ROWS, :] = _ones_row(keys)
        ln = _layer_norm(ikw_ref[0][:, :IDX_DIM], lng_ref[...], lnb_ref[...])
        ikln_ref[0] = ln.T
        kx[:, 0:A_KV] = k_ref[0].astype(BF16)
        kx[:, A_KV:A_KV + IDX_DIM] = ln.astype(BF16)
        kx[:, A_KV + IDX_DIM:KX_WIDTH] = jnp.zeros((keys, KX_WIDTH - A_KV - IDX_DIM), BF16)

    _stage_queries(iq_ref[0], q_ref[0], ikwq_ref[0][:, IDX_DIM:IDX_DIM + IDX_HEADS], eye_ref, iq_t, w_full, q_t, rows,
                   transposed=True)

    nt = ((i + 1) * rows + KEY_TILE - 1) // KEY_TILE
    q_pos = i * rows + lax.broadcasted_iota(jnp.int32, (KEY_TILE, rows), 1)
    q_end = (q_pos // CHUNK + 1) * CHUNK

    def adm(j):
        return j * KEY_TILE + lax.broadcasted_iota(jnp.int32, (KEY_TILE, rows), 0) < q_end

    _index_scores(iq_t, w_full, kx, sc, None, nt, adm, None, rows)
    _select_topk(sc, None, tril_ref, nt)
    _attend(sc, q_t, kx, v_t, s_sc, acc_sc, nt)
    _write_heads(o_ref, acc_sc, eye_ref, rows)


def _attn_prompt(q_t, pb3, v_t, idx_k_g, idx_k_b, tril, eye):
    b, s, _ = pb3.shape
    rows = 2 * LANES
    kernel = functools.partial(_attn_prompt_kernel, rows=rows)
    o_a, ik_t = pl.pallas_call(
        kernel,
        grid=(b, s // rows),
        in_specs=[
            pl.BlockSpec((1, IDX_Q, rows), lambda bi, i: (bi, 0, i)),
            pl.BlockSpec((1, A_Q, rows), lambda bi, i: (bi, IDX_Q // A_Q, i)),
            pl.BlockSpec((1, s, A_KV), lambda bi, i: (bi, 0, CB_K // A_KV)),
            pl.BlockSpec((1, A_KV, s), lambda bi, i: (bi, 0, 0)),
            pl.BlockSpec((1, s, LANES), lambda bi, i: (bi, 0, CB_IKW // LANES)),
            pl.BlockSpec((1, rows, LANES), lambda bi, i: (bi, i, CB_IKW // LANES)),
            pl.BlockSpec((1, IDX_DIM), lambda bi, i: (0, 0)),
            pl.BlockSpec((1, IDX_DIM), lambda bi, i: (0, 0)),
            pl.BlockSpec((KEY_TILE, KEY_TILE), lambda bi, i: (0, 0)),
            pl.BlockSpec((KEY_TILE, KEY_TILE), lambda bi, i: (0, 0)),
        ],
        out_specs=[
            pl.BlockSpec((1, rows, A_Q), lambda bi, i: (bi, i, 0)),
            pl.BlockSpec((1, IDX_DIM, s), lambda bi, i: (bi, 0, 0)),
        ],
        out_shape=[
            jax.ShapeDtypeStruct((b, s, A_Q), BF16),
            jax.ShapeDtypeStruct((b, IDX_DIM, s), F32),
        ],
        scratch_shapes=_ATTN_SCRATCH(rows, s),
        compiler_params=pltpu.CompilerParams(
            dimension_semantics=("arbitrary", "arbitrary"), vmem_limit_bytes=VMEM_LIMIT),
        name="attn_prompt",
    )(q_t, q_t, pb3, v_t, pb3, pb3, idx_k_g, idx_k_b, tril, eye)
    return o_a, jnp.transpose(ik_t, (0, 2, 1))


def _attn_sample_kernel(iq_ref, q_ref, k_ref, v_ref, ikw_ref, ck_ref, cv_ref, cik_ref, lng_ref, lnb_ref, tril_ref,
                        eye_ref, o_ref, ikln_ref, kx, v_t, iq_t, w_full, q_t, sc, s_sc, acc_sc, sc_pk,
                        *, rows, past, keys, pieces):
    new = rows
    pad = keys - past - new
    piece = pl.program_id(1)
    per_piece = past // pieces
    eye = eye_ref[0:A_HEAD_DIM, 0:A_HEAD_DIM]
    spare = jnp.zeros((KX_WIDTH - A_KV - IDX_DIM, KEY_TILE), BF16)
    for c in range(0, per_piece, KEY_TILE):
        dst = pl.ds(pl.multiple_of(piece * per_piece + c, KEY_TILE), KEY_TILE)
        cols = slice(c, c + KEY_TILE)
        stack = jnp.concatenate([ck_ref[0, g, :, cols].astype(BF16) for g in range(A_KV_HEADS)]
                                + [cik_ref[0, :, cols].astype(BF16), spare], axis=0)
        kx[dst, :] = _transpose_bf16(stack, eye_ref[...]).astype(BF16)
        for g in range(A_KV_HEADS):
            v_t[g, 0:A_HEAD_DIM, dst] = cv_ref[0, g, :, cols].astype(BF16)

    @pl.when(piece == pieces - 1)
    def _():
        vf = v_ref[0]
        for g in range(A_KV_HEADS):
            sl = slice(g * A_HEAD_DIM, (g + 1) * A_HEAD_DIM)
            v_t[g, 0:A_HEAD_DIM, past:past + new] = _transpose_bf16(vf[:, sl].astype(BF16), eye).astype(BF16)
            v_t[g, 0:A_HEAD_DIM, past + new:keys] = jnp.zeros((A_HEAD_DIM, pad), BF16)
            v_t[g, A_HEAD_DIM:V_ROWS, :] = _ones_row(keys)
        ikw = ikw_ref[0]
        ln = _layer_norm(ikw[:, :IDX_DIM], lng_ref[...], lnb_ref[...])
        ikln_ref[0] = ln
        kx[past:past + new, 0:A_KV] = k_ref[0].astype(BF16)
        kx[past:past + new, A_KV:A_KV + IDX_DIM] = ln.astype(BF16)
        kx[past:past + new, A_KV + IDX_DIM:KX_WIDTH] = jnp.zeros((new, KX_WIDTH - A_KV - IDX_DIM), BF16)
        kx[past + new:keys, :] = jnp.zeros((pad, KX_WIDTH), BF16)

        _stage_queries(iq_ref[0], q_ref[0], ikw[:, IDX_DIM:IDX_DIM + IDX_HEADS], eye_ref, iq_t, w_full, q_t, rows,
                       transposed=False)

        nt = keys // KEY_TILE

        def adm(j):
            return j * KEY_TILE + lax.broadcasted_iota(jnp.int32, (KEY_TILE, LANES), 0) < past + new

        pk = KEY_TILE // 4
        group = lax.broadcasted_iota(jnp.int32, (pk, LANES), 1) // rows
        sub_block = jnp.zeros((pk, LANES), jnp.int32)
        for c, blk in enumerate(PACK_ORDER):
            sub_block = jnp.where(group == c, blk, sub_block)
        key_in_tile = sub_block * pk + lax.broadcasted_iota(jnp.int32, (pk, LANES), 0)

        def adm_pk(j):
            return j * KEY_TILE + key_in_tile < past + new

        _index_scores(iq_t, w_full, kx, sc, sc_pk, nt, adm, adm_pk, rows)
        _select_topk(sc, sc_pk, tril_ref, nt)
        _attend(sc, q_t, kx, v_t, s_sc, acc_sc, nt)
        _write_heads(o_ref, acc_sc, eye_ref, rows)


def _attn_sample(pa3, pb3, cache_k, cache_v, cache_ik, idx_k_g, idx_k_b, tril, eye):
    b, t, _ = pa3.shape
    past = cache_ik.shape[2]
    pieces = 1
    keys = -(-(past + t) // KEY_TILE) * KEY_TILE
    kernel = functools.partial(_attn_sample_kernel, rows=t, past=past, keys=keys, pieces=pieces)
    return pl.pallas_call(
        kernel,
        grid=(b, pieces),
        in_specs=[
            pl.BlockSpec((1, t, IDX_Q), lambda bi, pc: (bi, 0, CA_IQ // IDX_Q)),
            pl.BlockSpec((1, t, A_Q), lambda bi, pc: (bi, 0, CA_Q // A_Q)),
            pl.BlockSpec((1, t, A_KV), lambda bi, pc: (bi, 0, CB_K // A_KV)),
            pl.BlockSpec((1, t, A_KV), lambda bi, pc: (bi, 0, CB_V // A_KV)),
            pl.BlockSpec((1, t, LANES), lambda bi, pc: (bi, 0, CB_IKW // LANES)),
            pl.BlockSpec((1, A_KV_HEADS, A_HEAD_DIM, past // pieces), lambda bi, pc: (bi, 0, 0, pc)),
            pl.BlockSpec((1, A_KV_HEADS, A_HEAD_DIM, past // pieces), lambda bi, pc: (bi, 0, 0, pc)),
            pl.BlockSpec((1, IDX_DIM, past // pieces), lambda bi, pc: (bi, 0, pc)),
            pl.BlockSpec((1, IDX_DIM), lambda bi, pc: (0, 0)),
            pl.BlockSpec((1, IDX_DIM), lambda bi, pc: (0, 0)),
            pl.BlockSpec((KEY_TILE, KEY_TILE), lambda bi, pc: (0, 0)),
            pl.BlockSpec((KEY_TILE, KEY_TILE), lambda bi, pc: (0, 0)),
        ],
        out_specs=[
            pl.BlockSpec((1, t, A_Q), lambda bi, pc: (bi, 0, 0)),
            pl.BlockSpec((1, t, IDX_DIM), lambda bi, pc: (bi, 0, 0)),
        ],
        out_shape=[
            jax.ShapeDtypeStruct((b, t, A_Q), BF16),
            jax.ShapeDtypeStruct((b, t, IDX_DIM), F32),
        ],
        scratch_shapes=_ATTN_SCRATCH(t, keys),
        compiler_params=pltpu.CompilerParams(
            dimension_semantics=("arbitrary", "arbitrary"), vmem_limit_bytes=VMEM_LIMIT),
        name="attn_sample",
    )(pa3, pa3, pb3, pb3, pb3, cache_k, cache_v, cache_ik, idx_k_g, idx_k_b, tril, eye)


def _head_pair_diag(x):
    w = x.shape[1] // 2
    first = lax.broadcasted_iota(jnp.int32, x.shape, 1) < w
    zero = jnp.zeros_like(x)
    return jnp.concatenate([jnp.where(first, x, zero), jnp.where(first, zero, x)], axis=0)


def _rwkv_kernel(u_ref, shift_ref, s0_ref, mu_ref, w0_ref, a0_ref, kk_ref, ka_ref, rk_ref, gnw_ref, gnb_ref,
                 w2_ref, a2_ref, g2_ref, bd_ref, tri_ref, yb_ref, s_out_ref, carry, state, y_sc, *, tc):
    t = pl.program_id(1)
    seqs, tb = u_ref.shape[0], u_ref.shape[1]
    u = u_ref[...].reshape(seqs * tb, SHIFT_W)
    n = B_HEAD_DIM

    @pl.when(t == 0)
    def _():
        carry[...] = shift_ref[...]
        for s in range(seqs):
            for p in range(B_HEADS // 2):
                state[s, p] = jnp.concatenate([s0_ref[s, 2 * p], s0_ref[s, 2 * p + 1]], axis=1)

    row = lax.broadcasted_iota(jnp.int32, u.shape, 0)
    u_prev = pltpu.roll(u, 1, 0)
    for s in range(seqs):
        u_prev = jnp.where(row == s * tb, carry[s], u_prev)
        carry[s] = u[(s + 1) * tb - 1:(s + 1) * tb, :]
    m = u + (u_prev - u) * mu_ref[...]

    r = m[:, 0:B_WIDTH]
    k = m[:, B_WIDTH:2 * B_WIDTH]
    v = m[:, 2 * B_WIDTH:3 * B_WIDTH]
    lora = m[:, 3 * B_WIDTH:3 * B_WIDTH + LANES]
    gl = m[:, 3 * B_WIDTH + LANES:]
    lane = lax.broadcasted_iota(jnp.int32, lora.shape, 1)
    lora = jnp.where(lane < DECAY_LORA, jnp.tanh(lora), lora).astype(BF16)
    z = w0_ref[...] + _dot(lora, w2_ref[...])
    lw = -float(np.exp(-0.5)) * _sigmoid(z)
    a = _sigmoid(a0_ref[...] + _dot(lora, a2_ref[...]))
    g = _dot(_sigmoid(gl).astype(BF16), g2_ref[...])

    kk = k * kk_ref[...]
    ss = _dot_exact_rhs(kk * kk, bd_ref[...])
    kk = kk / jnp.maximum(jnp.sqrt(ss), 1e-12)
    k2 = k * (1.0 + (a - 1.0) * ka_ref[...])
    bonus = _dot_exact_rhs(r * k2 * rk_ref[...], bd_ref[...]) * v

    cum = _dot_exact_lhs(tri_ref[...], lw)
    g_in = jnp.exp(cum)
    g_ex = jnp.exp(cum - lw)
    g_inv = jnp.exp(-cum)
    a_t = -kk * g_ex
    b_t = kk * a * g_inv
    k_t = k2 * g_inv
    r_t = r * g_in

    ri = lax.broadcasted_iota(jnp.int32, (tc, 2 * tc), 0)
    ci = lax.broadcasted_iota(jnp.int32, (tc, 2 * tc), 1) % tc
    strict = ci < ri
    incl = ci <= ri
    eye = jnp.where(ci == ri, 1.0, 0.0)

    bd = _head_pair_diag
    bf = lambda x: x.astype(BF16)
    per_seq = tb // tc
    chunks = range(seqs * per_seq)
    items = [(c, p) for c in chunks for p in range(B_HEADS // 2)]
    lanes = [slice(p * 2 * n, (p + 1) * 2 * n) for p in range(B_HEADS // 2)]
    toks = [slice(c * tc, (c + 1) * tc) for c in chunks]
    g_end = [g_in[(c + 1) * tc - 1:(c + 1) * tc, :] for c in chunks]
    v2 = {(c, p): v[toks[c], lanes[p]] for c, p in items}
    v_bd = {i: bd(bf(v2[i])) for i in items}
    a2 = {(c, p): a_t[toks[c], lanes[p]] for c, p in items}
    r2 = {(c, p): r_t[toks[c], lanes[p]] for c, p in items}
    bk = {(c, p): jnp.concatenate([b_t[toks[c], lanes[p]], k_t[toks[c], lanes[p]]], axis=0) for c, p in items}
    bk_bd = {i: jnp.concatenate([bd(bf(bk[i][0:tc])), bd(bf(bk[i][tc:2 * tc]))], axis=0) for i in items}
    cross = {i: _dot(bf(jnp.concatenate([a2[i], r2[i]], axis=0)), bk_bd[i], trans_b=True)
             for i in items}
    l_ab = {i: jnp.where(strict, cross[i][0:tc, 0:2 * tc], 0.0) for i in items}
    l_akv = {i: _dot(bf(jnp.where(strict, cross[i][0:tc, 2 * tc:4 * tc], 0.0)), v_bd[i]) for i in items}
    m_rbk = {i: bf(jnp.concatenate([jnp.where(incl, cross[i][tc:2 * tc, 0:2 * tc], 0.0),
                                    jnp.where(incl, cross[i][tc:2 * tc, 2 * tc:4 * tc], 0.0)], axis=1))
             for i in items}

    inv = {i: eye + l_ab[i] for i in items}
    pwb = {i: bf(l_ab[i]) for i in items}
    span = 2
    while span < tc:
        pwb = {i: bf(_dot(pwb[i], bd(pwb[i]))) for i in items}
        inv = {i: inv[i] + _dot(bf(inv[i]), bd(pwb[i])) for i in items}
        span *= 2
    inv = {i: bf(inv[i]) for i in items}
    ar = {i: _split2(jnp.concatenate([_dot(inv[i], bd(bf(a2[i]))), r2[i]], axis=0)) for i in items}
    z_free = {i: _dot(inv[i], bd(bf(l_akv[i]))) for i in items}

    first_head = lax.broadcasted_iota(jnp.int32, (n, 2 * n), 1) < n
    s_cur = {(s, p): state[s, p] for s in range(seqs) for p in range(B_HEADS // 2)}
    for j in range(per_seq):
        here = [(s * per_seq + j, s, p) for s in range(seqs) for p in range(B_HEADS // 2)]
        from_state = {(c, p): _dot3s(ar[c, p], _split2(bd(s_cur[s, p])), trans_b=True) for c, s, p in here}
        z = {(c, p): from_state[c, p][0:tc] + z_free[c, p] for c, s, p in here}
        y = {(c, p): from_state[c, p][tc:2 * tc]
             + _dot(m_rbk[c, p], jnp.concatenate([bd(bf(z[c, p])), v_bd[c, p]], axis=0)) for c, s, p in here}
        zv = {(c, p): jnp.concatenate([z[c, p], v2[c, p]], axis=0) for c, s, p in here}
        full = {(c, p): _dot3(zv[c, p].T, bk[c, p] * g_end[c][:, lanes[p]]) for c, s, p in here}
        s_cur = {(s, p): s_cur[s, p] * g_end[c][:, lanes[p]]
                 + jnp.where(first_head, full[c, p][0:n], full[c, p][n:2 * n]) for c, s, p in here}
        for c, s, p in here:
            y_sc[toks[c], lanes[p]] = y[c, p]
    for (s, p), val in s_cur.items():
        state[s, p] = val

    y = y_sc[...]
    mean = _dot_exact_rhs(y, bd_ref[...]) * (1.0 / n)
    d = y - mean
    var = _dot_exact_rhs(d * d, bd_ref[...]) * (1.0 / n)
    yn = d * lax.rsqrt(var + GN_EPS) * gnw_ref[...] + gnb_ref[...]
    yb_ref[...] = ((yn + bonus) * g).astype(yb_ref.dtype).reshape(seqs, tb, B_WIDTH)

    @pl.when(t == pl.num_programs(1) - 1)
    def _():
        for s in range(seqs):
            for p in range(B_HEADS // 2):
                s_out_ref[s, 2 * p] = state[s, p][:, 0:n]
                s_out_ref[s, 2 * p + 1] = state[s, p][:, n:2 * n]


def _rwkv(pb3, shift_prev, s0, p, tc):
    b, s, _ = pb3.shape
    tb = min(s, 4 * tc)
    seqs = max(1, 4 * tc // tb)
    tri = jnp.asarray(np.kron(np.eye(seqs * tb // tc), np.tril(np.ones((tc, tc)))), BF16)
    vec = lambda w: pl.BlockSpec((1, w), lambda bi, t: (0, 0))
    mat = lambda r, c: pl.BlockSpec((r, c), lambda bi, t: (0, 0))
    st = pl.BlockSpec((seqs, B_HEADS, B_HEAD_DIM, B_HEAD_DIM), lambda bi, t: (bi, 0, 0, 0))
    return pl.pallas_call(
        functools.partial(_rwkv_kernel, tc=tc),
        grid=(b // seqs, s // tb),
        in_specs=[
            pl.BlockSpec((seqs, tb, SHIFT_W), lambda bi, t: (bi, t, CB_U // SHIFT_W)),
            pl.BlockSpec((seqs, 1, SHIFT_W), lambda bi, t: (bi, 0, 0)),
            st,
            vec(SHIFT_W), vec(B_WIDTH), vec(B_WIDTH), vec(B_WIDTH), vec(B_WIDTH), vec(B_WIDTH),
            vec(B_WIDTH), vec(B_WIDTH),
            mat(LANES, B_WIDTH), mat(LANES, B_WIDTH), mat(GATE_LORA, B_WIDTH), mat(B_WIDTH, B_WIDTH),
            mat(seqs * tb, seqs * tb),
        ],
        out_specs=[pl.BlockSpec((seqs, tb, B_WIDTH), lambda bi, t: (bi, t, 0)), st],
        out_shape=[jax.ShapeDtypeStruct((b, s, B_WIDTH), BF16),
                   jax.ShapeDtypeStruct((b, B_HEADS, B_HEAD_DIM, B_HEAD_DIM), F32)],
        scratch_shapes=[pltpu.VMEM((seqs, 1, SHIFT_W), F32),
                        pltpu.VMEM((seqs, B_HEADS // 2, B_HEAD_DIM, 2 * B_HEAD_DIM), F32),
                        pltpu.VMEM((seqs * tb, B_WIDTH), F32)],
        compiler_params=pltpu.CompilerParams(
            dimension_semantics=("arbitrary", "arbitrary"), vmem_limit_bytes=VMEM_LIMIT),
        name="rwkv",
    )(pb3, shift_prev, s0, p['shift_mu'], p['w0'], p['a0'], p['k_k'], p['k_a'], p['r_k'], p['gn_w'], p['gn_b'],
      p['w2p'], p['a2p'], p['g2'], p['bd'], tri)


def _out_ffn_kernel(oa_ref, yb_ref, ga_ref, gb_ref, x_ref, n2_ref, nf_ref,
                    woa_ref, wob_ref, wout_ref, wg_ref, wu_ref, wd_ref, o_ref):
    merged = (_sigmoid(ga_ref[...].astype(F32)) * _dot(oa_ref[...], woa_ref[...])
              + _sigmoid(gb_ref[...].astype(F32)) * _dot(yb_ref[...], wob_ref[...]))
    h = x_ref[...] + _dot(merged.astype(BF16), wout_ref[...])
    hn = _rms(h, n2_ref[...]).astype(BF16)
    out = h
    for c, n in _col_chunks(D_FF, 4 * MXU_COLS):
        gate = _dot(hn, wg_ref[:, c:c + n])
        up = _dot(hn, wu_ref[:, c:c + n])
        act = (gate * _sigmoid(gate) * up).astype(BF16)
        out = out + _dot(act, wd_ref[c:c + n, :])
    o_ref[...] = _rms(out, nf_ref[...])


def _out_ffn(o_a, y_b, pa, x2, p, norm_f):
    n = x2.shape[0]
    tm = min(n, 512)
    row = lambda w, j=0: pl.BlockSpec((tm, w), lambda i: (i, j))
    vec = pl.BlockSpec((1, D_MODEL), lambda i: (0, 0))
    resident = lambda r, c: pl.BlockSpec((r, c), lambda i: (0, 0), pipeline_mode=pl.Buffered(1))
    return pl.pallas_call(
        _out_ffn_kernel,
        grid=(n // tm,),
        in_specs=[
            row(A_Q), row(B_WIDTH),
            row(D_MODEL, CA_GA // D_MODEL), row(D_MODEL, CA_GB // D_MODEL), row(D_MODEL),
            vec, vec,
            resident(A_Q, D_MODEL), resident(B_WIDTH, D_MODEL), resident(D_MODEL, D_MODEL),
            resident(D_MODEL, D_FF), resident(D_MODEL, D_FF), resident(D_FF, D_MODEL),
        ],
        out_specs=row(D_MODEL),
        out_shape=jax.ShapeDtypeStruct((n, D_MODEL), F32),
        compiler_params=pltpu.CompilerParams(
            dimension_semantics=("arbitrary",), vmem_limit_bytes=VMEM_LIMIT),
        name="out_proj_ffn",
    )(o_a, y_b, pa, pa, x2, p['norm2'], norm_f, p['w_oa'], p['w_ob'], p['w_out'],
      p['w_gate'], p['w_up'], p['w_down'])


def _prepare_params(l, norm1, w_in, idx_k_g, idx_k_b, shift_mu, w0, w2, a0, a2, g2, k_k, k_a, r_k,
                    gn_w, gn_b, w_oa, w_ob, w_out, norm2, w_gate, w_up, w_down):
    w = jnp.transpose(w_in[l])
    o = np.cumsum([0, A_Q, A_KV, A_KV, IDX_Q, IDX_DIM, IDX_HEADS, SHIFT_W, D_MODEL, D_MODEL])
    seg = lambda i: w[o[i]:o[i + 1]]
    pad = jnp.zeros((LANES - IDX_DIM - IDX_HEADS, D_MODEL), w.dtype)
    w_in_p = jnp.concatenate([seg(3), seg(7), seg(8), seg(0), seg(6), seg(1), seg(2), seg(4), seg(5), pad],
                             axis=0).astype(BF16)
    zeros = jnp.zeros((LANES - DECAY_LORA, B_WIDTH), F32)
    head = np.arange(B_WIDTH) // B_HEAD_DIM
    row = lambda x: x[l].reshape(1, -1)
    return {
        'norm1': row(norm1), 'w_in_p': w_in_p,
        'idx_k_g': row(idx_k_g), 'idx_k_b': row(idx_k_b),
        'shift_mu': row(shift_mu), 'w0': row(w0), 'a0': row(a0),
        'k_k': row(k_k), 'k_a': row(k_a), 'r_k': row(r_k), 'gn_w': row(gn_w), 'gn_b': row(gn_b),
        'w2p': jnp.concatenate([w2[l], zeros], axis=0).astype(BF16),
        'a2p': jnp.concatenate([zeros, a2[l]], axis=0).astype(BF16),
        'g2': g2[l].astype(BF16),
        'bd': jnp.asarray(head[:, None] == head[None, :], BF16),
        'w_oa': w_oa[l].astype(BF16), 'w_ob': w_ob[l].astype(BF16), 'w_out': w_out[l].astype(BF16),
        'norm2': row(norm2),
        'w_gate': w_gate[l].astype(BF16), 'w_up': w_up[l].astype(BF16), 'w_down': w_down[l].astype(BF16),
    }


def _layer(x, attend, shift_prev, wkv0, p, norm_f, eye, chunk):
    b, t, _ = x.shape
    x2 = x.reshape(b * t, D_MODEL)
    pa, pb, *transposed = _in_proj(x2, p['norm1'], p['w_in_p'], eye, t)
    pa3 = pa.reshape(b, t, PA_WIDTH)
    pb3 = pb.reshape(b, t, PB_WIDTH)
    o_a, ik_ln = attend(pa3, pb3, *transposed)
    y_b, wkv_new = _rwkv(pb3, shift_prev, wkv0, p, chunk)
    flat = lambda z: z.reshape(b * t, z.shape[-1])
    out = _out_ffn(flat(o_a), flat(y_b), pa, x2, p, norm_f).reshape(b, t, D_MODEL)
    if transposed:
        k_new, v_new = (jnp.transpose(z.reshape(b, A_KV_HEADS, A_HEAD_DIM, t), (0, 3, 1, 2)) for z in transposed[:2])
    else:
        k_new = pb3[:, :, CB_K:CB_K + A_KV].reshape(b, t, A_KV_HEADS, A_HEAD_DIM)
        v_new = pb3[:, :, CB_V:CB_V + A_KV].reshape(b, t, A_KV_HEADS, A_HEAD_DIM)
    shift_new = pb3[:, t - 1:t, CB_U:CB_U + SHIFT_W]
    return out, k_new, v_new, ik_ln, wkv_new, shift_new


def kernel(x_prompt, x_sample, cache_k, cache_v, cache_idx_k, state_wkv, state_shift,
           norm1, w_in, idx_k_g, idx_k_b, shift_mu, w0, w2, a0, a2, g2, k_k, k_a, r_k,
           gn_w, gn_b, w_oa, w_ob, w_out, norm2, w_gate, w_up, w_down, norm_f):
    assert w_in.shape[0] == 1, "single-layer kernel"
    l = 0
    p = _prepare_params(l, norm1, w_in, idx_k_g, idx_k_b, shift_mu, w0, w2, a0, a2, g2, k_k, k_a, r_k,
                        gn_w, gn_b, w_oa, w_ob, w_out, norm2, w_gate, w_up, w_down)
    nf = norm_f.reshape(1, -1)
    tril = jnp.asarray(np.tril(np.ones((KEY_TILE, KEY_TILE), np.float32)), BF16)
    eye = jnp.asarray(np.eye(KEY_TILE, dtype=np.float32), BF16)

    n_p = x_prompt.shape[0]
    shift0 = jnp.zeros((n_p, 1, SHIFT_W), F32)
    wkv_zero = jnp.zeros((n_p, B_HEADS, B_HEAD_DIM, B_HEAD_DIM), F32)
    attend_p = lambda pa3, pb3, k_t, v_t, q_t: _attn_prompt(q_t, pb3, v_t, p['idx_k_g'], p['idx_k_b'], tril, eye)
    y_p, k_p, v_p, ik_p, wkv_p, shift_p = _layer(x_prompt, attend_p, shift0, wkv_zero, p, nf, eye, 64)

    n_s, t_s = x_sample.shape[0], x_sample.shape[1]
    past = cache_k.shape[2]
    ck = jnp.transpose(cache_k[l], (0, 2, 3, 1))
    cv = jnp.transpose(cache_v[l], (0, 2, 3, 1))
    cik = jnp.transpose(cache_idx_k[l], (0, 2, 1))
    attend_s = lambda pa3, pb3: _attn_sample(pa3, pb3, ck, cv, cik, p['idx_k_g'], p['idx_k_b'], tril, eye)
    y_s, k_s, v_s, ik_s, wkv_s, shift_s = _layer(x_sample, attend_s, state_shift[l], state_wkv[l], p, nf, eye, t_s)

    lead = lambda z: z[None]
    return (y_p, y_s, lead(k_p), lead(v_p), lead(ik_p), lead(wkv_p), lead(shift_p),
            lead(k_s), lead(v_s), lead(ik_s), lead(wkv_s), lead(shift_s))
```

```python
import functools

import jax
import jax.numpy as jnp
import numpy as np
from jax import lax
from jax.experimental import pallas as pl
from jax.experimental.pallas import tpu as pltpu

F32 = jnp.float32
BF16 = jnp.bfloat16

D_MODEL = 1024
CHUNK = 64
A_HEADS = 8
A_KV_HEADS = 2
A_GROUP = A_HEADS // A_KV_HEADS
A_HEAD_DIM = 64
A_Q = A_HEADS * A_HEAD_DIM
A_KV = A_KV_HEADS * A_HEAD_DIM
IDX_HEADS = 16
IDX_DIM = 64
IDX_Q = IDX_HEADS * IDX_DIM
TOPK_MAX = 256
B_HEADS = 8
B_HEAD_DIM = 64
B_WIDTH = B_HEADS * B_HEAD_DIM
DECAY_LORA = 64
AAA_LORA = 64
GATE_LORA = 128
SHIFT_W = 3 * B_WIDTH + DECAY_LORA + AAA_LORA + GATE_LORA
D_FF = 2816
RMS_EPS = 1e-6
LN_EPS = 1e-6
GN_EPS = 64e-5

LANES = 128
SUB = 8
VMEM_LIMIT = 56 * 1024 * 1024
NEG = -1e30
KEY_TILE = 256
V_ROWS = A_HEAD_DIM + 16
KX_WIDTH = 256

CA_IQ = 0
CA_GA = CA_IQ + IDX_Q
CA_GB = CA_GA + D_MODEL
CA_Q = CA_GB + D_MODEL
PA_WIDTH = CA_Q + A_Q
CB_U = 0
CB_K = CB_U + SHIFT_W
CB_V = CB_K + A_KV
CB_IKW = CB_V + A_KV
PB_WIDTH = CB_IKW + LANES
MXU_COLS = 256


def _dot(a, b, trans_b=False):
    dn = (((1,), (1 if trans_b else 0,)), ((), ()))
    return lax.dot_general(a, b, dn, preferred_element_type=F32)


def _split2(x):
    hi = x.astype(BF16)
    lo = (x - hi.astype(F32)).astype(BF16)
    return hi, lo


def _dot3s(a_split, b_split, trans_b=False):
    ah, al = a_split
    bh, bl = b_split
    return _dot(ah, bh, trans_b) + (_dot(ah, bl, trans_b) + _dot(al, bh, trans_b))


def _dot3(a, b, trans_b=False):
    return _dot3s(_split2(a), _split2(b), trans_b)


def _dot_exact_rhs(a, b_bf16, terms=2):
    out = None
    rem = a
    for _ in range(terms):
        part = rem.astype(BF16)
        rem = rem - part.astype(F32)
        d = _dot(part, b_bf16)
        out = d if out is None else out + d
    return out


def _dot_exact_lhs(a_bf16, b, terms=3, trans_b=False):
    out = None
    rem = b
    for _ in range(terms):
        part = rem.astype(BF16)
        rem = rem - part.astype(F32)
        d = _dot(a_bf16, part, trans_b)
        out = d if out is None else out + d
    return out


def _sigmoid(x):
    return 1.0 / (1.0 + jnp.exp(-x))


def _rms(x, g):
    ms = jnp.mean(x * x, axis=-1, keepdims=True)
    return x * lax.rsqrt(ms + RMS_EPS) * g


def _col_chunks(width, chunk):
    return [(c, min(chunk, width - c)) for c in range(0, width, chunk)]


QT_WIDTH = IDX_Q + A_Q


def _in_proj_kernel(x_ref, g_ref, w_ref, eye_ref, pa_ref, pb_ref, *t_refs):
    xn = _rms(x_ref[...], g_ref[...]).astype(BF16)
    for c, n in _col_chunks(PA_WIDTH, 2 * MXU_COLS):
        pa_ref[:, c:c + n] = _dot(xn, w_ref[c:c + n, :], trans_b=True).astype(BF16)
    for c, n in _col_chunks(PB_WIDTH, 2 * MXU_COLS):
        pb_ref[:, c:c + n] = _dot(xn, w_ref[PA_WIDTH + c:PA_WIDTH + c + n, :], trans_b=True)
    if t_refs:
        kt_ref, vt_ref, qt_ref = t_refs
        kt_ref[0] = pb_ref[:, CB_K:CB_K + A_KV].T
        vt_ref[0] = pb_ref[:, CB_V:CB_V + A_KV].T
        for dst, src in ((0, CA_IQ), (IDX_Q, CA_Q)):
            for c, n in _col_chunks(IDX_Q if src == CA_IQ else A_Q, MXU_COLS):
                qt_ref[0, dst + c:dst + c + n, :] = _transpose_bf16(
                    pa_ref[:, src + c:src + c + n], eye_ref[0:n, 0:n]).astype(BF16)


def _in_proj(x2, norm1, w_in_p, eye, seq):
    n = x2.shape[0]
    tm = min(n, 512)
    out_specs = [pl.BlockSpec((tm, PA_WIDTH), lambda i: (i, 0)), pl.BlockSpec((tm, PB_WIDTH), lambda i: (i, 0))]
    out_shape = [jax.ShapeDtypeStruct((n, PA_WIDTH), BF16), jax.ShapeDtypeStruct((n, PB_WIDTH), F32)]
    if seq % tm == 0:
        per_seq = seq // tm
        t_spec = lambda w: pl.BlockSpec((1, w, tm), lambda i: (i // per_seq, 0, i % per_seq))
        out_specs += [t_spec(A_KV), t_spec(A_KV), t_spec(QT_WIDTH)]
        out_shape += [jax.ShapeDtypeStruct((n // seq, A_KV, seq), F32)] * 2
        out_shape += [jax.ShapeDtypeStruct((n // seq, QT_WIDTH, seq), BF16)]
    return pl.pallas_call(
        _in_proj_kernel,
        grid=(n // tm,),
        in_specs=[
            pl.BlockSpec((tm, D_MODEL), lambda i: (i, 0)),
            pl.BlockSpec((1, D_MODEL), lambda i: (0, 0)),
            pl.BlockSpec((PA_WIDTH + PB_WIDTH, D_MODEL), lambda i: (0, 0), pipeline_mode=pl.Buffered(1)),
            pl.BlockSpec((KEY_TILE, KEY_TILE), lambda i: (0, 0)),
        ],
        out_specs=out_specs,
        out_shape=out_shape,
        compiler_params=pltpu.CompilerParams(
            dimension_semantics=("arbitrary",), vmem_limit_bytes=VMEM_LIMIT),
        name="in_proj",
    )(x2, norm1, w_in_p, eye)


def _layer_norm(x, g, b):
    mu = jnp.mean(x, axis=-1, keepdims=True)
    d = x - mu
    var = jnp.mean(d * d, axis=-1, keepdims=True)
    return d * lax.rsqrt(var + LN_EPS) * g + b


def _transpose_bf16(x, eye):
    return _dot(eye, x, trans_b=True)


def _key_loop(nt, body, init):
    if isinstance(nt, int):
        carry = init
        for j in range(nt):
            carry = body(j, carry)
        return carry
    carry = lax.fori_loop(0, nt // 2, lambda k, c: body(2 * k + 1, body(2 * k, c)), init)
    return lax.cond(nt % 2 == 1, lambda c: body(nt - 1, c), lambda c: c, carry)


def _key_off(j):
    return j * KEY_TILE if isinstance(j, int) else pl.multiple_of(j * KEY_TILE, KEY_TILE)


def _fold(x, op):
    x = x.reshape(x.shape[0] // SUB, SUB, x.shape[1])
    while x.shape[0] > 1:
        half = x.shape[0] // 2
        x = op(x[:half], x[half:])
    return x[0]


def _stage_queries(iq, q, iw, eye_ref, iq_t, w_full, q_t, rows, transposed):
    eye = eye_ref[0:IDX_DIM, 0:IDX_DIM]

    def head_t(x, h, dim):
        if transposed:
            return x[h * dim:(h + 1) * dim, :] * (dim ** -0.5)
        return _transpose_bf16((x[:, h * dim:(h + 1) * dim] * (dim ** -0.5)).astype(BF16), eye)

    for h in range(IDX_HEADS):
        iq_t[:, h * rows:(h + 1) * rows] = head_t(iq, h, IDX_DIM).astype(BF16)
    w_t = _dot_exact_lhs(eye_ref[0:IDX_HEADS, 0:IDX_HEADS], iw, trans_b=True)
    for h in range(IDX_HEADS):
        w_full[:, h * rows:(h + 1) * rows] = w_t[h:h + 1, :] * (IDX_HEADS ** -0.5)
    lanes_g = A_GROUP * rows
    q_t[...] = jnp.zeros(q_t.shape, BF16)
    for g in range(A_KV_HEADS):
        for hh in range(A_GROUP):
            col = g * lanes_g + hh * rows
            q_t[g * A_HEAD_DIM:(g + 1) * A_HEAD_DIM, col:col + rows] = head_t(
                q, g * A_GROUP + hh, A_HEAD_DIM).astype(BF16)


PACK_ORDER = (0, 2, 1, 3)


def _pack_quarters(acc):
    n = acc.shape[0] // 4
    a, b, c, d = (acc[i * n:(i + 1) * n] for i in range(4))
    lane = lax.broadcasted_iota(jnp.int32, (n, LANES), 1)
    low64 = lane < LANES // 2
    low32 = (lane & (LANES // 4)) == 0
    ab = jnp.where(low64, a, b) + pltpu.roll(jnp.where(low64, b, a), LANES // 2, 1)
    cd = jnp.where(low64, c, d) + pltpu.roll(jnp.where(low64, d, c), LANES // 2, 1)
    t1 = ab + pltpu.roll(ab, 3 * LANES // 4, 1)
    t2 = cd + pltpu.roll(cd, LANES // 4, 1)
    return jnp.where(low32, t1, t2)


def _index_scores(iq_t, w_full, kx, sc, sc_pk, nt, adm_fn, adm_pk_fn, rows):
    width = sc.shape[1]

    def body(j, carry):
        off = _key_off(j)
        lg = _dot(kx[pl.ds(off, KEY_TILE), A_KV:A_KV + IDX_DIM], iq_t[...])
        acc = None
        for c in range(IDX_HEADS * rows // width):
            x = jnp.maximum(lg[:, c * width:(c + 1) * width], 0.0) * w_full[:, c * width:(c + 1) * width]
            acc = x if acc is None else acc + x
        if sc_pk is not None:
            assert LANES // rows == 4
            pk = KEY_TILE // 4
            sc_pk[pl.ds(j * pk, pk), :] = jnp.where(adm_pk_fn(j), _pack_quarters(acc), -jnp.inf)
        shift = LANES // 2
        while shift >= rows:
            acc = acc + pltpu.roll(acc, shift, 1)
            shift //= 2
        sc[pl.ds(off, KEY_TILE), :] = jnp.where(adm_fn(j), acc, -jnp.inf)
        return carry

    _key_loop(nt, body, 0)


def _select_topk(sc, sc_pk, tril_ref, nt):
    shape = (SUB, sc.shape[1])
    inf = jnp.float32(jnp.inf)
    src = sc if sc_pk is None else sc_pk
    rows_t = KEY_TILE if sc_pk is None else KEY_TILE // 4

    def tile(j):
        start = _key_off(j) if sc_pk is None else j * rows_t
        return src[pl.ds(start, rows_t), :]

    def allsub(x, red):
        op = {jnp.sum: jnp.add, jnp.min: jnp.minimum, jnp.max: jnp.maximum}[red]
        if sc_pk is not None:
            x = op(op(x, pltpu.roll(x, LANES // 4, 1)),
                   op(pltpu.roll(x, LANES // 2, 1), pltpu.roll(x, 3 * LANES // 4, 1)))
        return jnp.broadcast_to(red(x, axis=0, keepdims=True), shape)

    def count_ge(t):
        t1 = t[0:1, :]
        acc = _key_loop(
            nt, lambda j, a: a + _fold(jnp.where(tile(j) >= t1, 1.0, 0.0), jnp.add), jnp.zeros(shape, F32))
        return allsub(acc, jnp.sum)

    def stats(j, c):
        mn, mx, na = c
        s = tile(j)
        fin = s > -inf
        return (jnp.minimum(mn, _fold(jnp.where(fin, s, inf), jnp.minimum)), jnp.maximum(mx, _fold(s, jnp.maximum)),
                na + _fold(jnp.where(fin, 1.0, 0.0), jnp.add))

    mn, mx, na = _key_loop(
        nt, stats, (jnp.full(shape, inf, F32), jnp.full(shape, -inf, F32), jnp.zeros(shape, F32)))
    lo0 = allsub(mn, jnp.min)
    mx = allsub(mx, jnp.max)
    n_adm = allsub(na, jnp.sum)
    kq = jnp.minimum(n_adm, float(TOPK_MAX))
    hi0 = mx + (jnp.abs(mx) * 1e-6 + 1e-30)

    def cond(c):
        return jnp.logical_and(c[0] < 400, c[5] > 0.5)

    def probe(x, lo, hi, c_lo, c_hi):
        c_x = count_ge(x)
        ge = c_x >= kq
        return jnp.where(ge, x, lo), jnp.where(ge, hi, x), jnp.where(ge, c_x, c_lo), jnp.where(ge, c_hi, c_x)

    def body(c):
        it, lo, hi, c_lo, c_hi, _ = c
        for _ in range(2):
            lo, hi, c_lo, c_hi = probe(0.5 * (lo + hi), lo, hi, c_lo, c_hi)
        nxt = 0.5 * (lo + hi)
        active = jnp.where(c_lo - c_hi > 1.5, jnp.where(nxt > lo, jnp.where(nxt < hi, 1.0, 0.0), 0.0), 0.0)
        active = jnp.where(n_adm > kq, active, 0.0)
        return it + 1, lo, hi, c_lo, c_hi, jnp.max(active)

    _, lo, _, c_lo, c_hi, _ = lax.while_loop(
        cond, body, (jnp.int32(0), lo0, hi0, n_adm, jnp.zeros(shape, F32), jnp.float32(1.0)))

    lo1 = lo[0:1, :]
    thr = allsub(
        _key_loop(nt, lambda j, a: jnp.minimum(a, _fold(jnp.where(tile(j) >= lo1, tile(j), inf), jnp.minimum)),
                  jnp.full(shape, inf, F32)),
        jnp.min)[0:1, :]
    take = (kq - c_hi)[0:1, :]
    has_tie = jnp.max(c_lo - kq) > 0.5

    @pl.when(jnp.logical_not(has_tie))
    def _():
        def wr(j, carry):
            off = _key_off(j)
            sc[pl.ds(off, KEY_TILE), :] = jnp.where(sc[pl.ds(off, KEY_TILE), :] >= thr, 0.0, NEG)
            return carry
        _key_loop(nt, wr, 0)

    @pl.when(has_tie)
    def _():
        def wr(j, seen):
            off = _key_off(j)
            s = sc[pl.ds(off, KEY_TILE), :]
            tie = jnp.where(s == thr, 1.0, 0.0)
            rank = _dot(tril_ref[...], tie.astype(BF16)) + seen
            keep_tie = jnp.where(s == thr, jnp.where(rank <= take, 0.0, NEG), NEG)
            sc[pl.ds(off, KEY_TILE), :] = jnp.where(s > thr, 0.0, keep_tie)
            return seen + jnp.sum(tie, axis=0, keepdims=True)
        lax.fori_loop(0, nt, wr, jnp.zeros((1, sc.shape[1]), F32))


def _attend(sc, q_t, kx, v_t, s_sc, acc_sc, nt):
    lanes = q_t.shape[1]
    lanes_g = lanes // A_KV_HEADS
    rep = lanes // sc.shape[1]

    def scores(j, macc):
        off = _key_off(j)
        bias = jnp.concatenate([sc[pl.ds(off, KEY_TILE), :]] * rep, axis=1)
        s = _dot(kx[pl.ds(off, KEY_TILE), 0:A_KV], q_t[...]) + bias
        s_sc[pl.ds(off, KEY_TILE), :] = s
        return jnp.maximum(macc, _fold(s, jnp.maximum))

    macc = _key_loop(nt, scores, jnp.full((SUB, lanes), NEG, F32))
    m = jnp.max(macc, axis=0, keepdims=True)
    acc_sc[...] = jnp.zeros(acc_sc.shape, F32)

    def weighted(j, carry):
        off = _key_off(j)
        for g in range(A_KV_HEADS):
            cols = slice(g * lanes_g, (g + 1) * lanes_g)
            p = jnp.exp((s_sc[pl.ds(off, KEY_TILE), cols] - m[:, cols]).astype(BF16))
            acc_sc[g] += _dot(v_t[g, :, pl.ds(off, KEY_TILE)], p)
        return carry

    _key_loop(nt, weighted, 0)


def _ones_row(cols):
    first = lax.broadcasted_iota(jnp.int32, (V_ROWS - A_HEAD_DIM, cols), 0) == 0
    return jnp.where(first, 1.0, 0.0).astype(BF16)


def _write_heads(o_ref, acc_sc, eye_ref, rows):
    width = A_GROUP * A_HEAD_DIM
    for g in range(A_KV_HEADS):
        acc = acc_sc[g]
        o_t = (acc[0:A_HEAD_DIM] * (1.0 / acc[A_HEAD_DIM:A_HEAD_DIM + 1])).astype(BF16)
        stacked = jnp.concatenate([o_t[:, hh * rows:(hh + 1) * rows] for hh in range(A_GROUP)], axis=0)
        o_ref[0, :, g * width:(g + 1) * width] = _transpose_bf16(stacked, eye_ref[0:rows, 0:rows]).astype(o_ref.dtype)


_ATTN_SCRATCH = lambda rows, keys: [
    pltpu.VMEM((keys, KX_WIDTH), BF16),
    pltpu.VMEM((A_KV_HEADS, V_ROWS, keys), BF16),
    pltpu.VMEM((IDX_DIM, IDX_HEADS * rows), BF16),
    pltpu.VMEM((1, IDX_HEADS * rows), F32),
    pltpu.VMEM((A_KV, A_HEADS * rows), BF16),
    pltpu.VMEM((keys, max(rows, LANES)), F32),
    pltpu.VMEM((keys, A_HEADS * rows), F32),
    pltpu.VMEM((A_KV_HEADS, V_ROWS, A_GROUP * rows), F32),
] + ([pltpu.VMEM((keys // (LANES // rows), LANES), F32)] if rows < LANES else [])


def _attn_prompt_kernel(iq_ref, q_ref, k_ref, v_ref, ikw_ref, ikwq_ref, lng_ref, lnb_ref, tril_ref, eye_ref,
                        o_ref, ikln_ref, kx, v_t, iq_t, w_full, q_t, sc, s_sc, acc_sc, *, rows):
    i = pl.program_id(1)

    @pl.when(i == 0)
    def _():
        keys = kx.shape[0]
        for g in range(A_KV_HEADS):
            sl = slice(g * A_HEAD_DIM, (g + 1) * A_HEAD_DIM)
            v_t[g, 0:A_HEAD_DIM, :] = v_ref[0, sl, :].astype(BF16)
            v_t[g, A_HEAD_DIM:V_ROWS, :] = _ones_row(keys)
        ln = _layer_norm(ikw_ref[0][:, :IDX_DIM], lng_ref[...], lnb_ref[...])
        ikln_ref[0] = ln.T
        kx[:, 0:A_KV] = k_ref[0].astype(BF16)
        kx[:, A_KV:A_KV + IDX_DIM] = ln.astype(BF16)
        kx[:, A_KV + IDX_DIM:KX_WIDTH] = jnp.zeros((keys, KX_WIDTH - A_KV - IDX_DIM), BF16)

    _stage_queries(iq_ref[0], q_ref[0], ikwq_ref[0][:, IDX_DIM:IDX_DIM + IDX_HEADS], eye_ref, iq_t, w_full, q_t, rows,
                   transposed=True)

    nt = ((i + 1) * rows + KEY_TILE - 1) // KEY_TILE
    q_pos = i * rows + lax.broadcasted_iota(jnp.int32, (KEY_TILE, rows), 1)
    q_end = (q_pos // CHUNK + 1) * CHUNK

    def adm(j):
        return j * KEY_TILE + lax.broadcasted_iota(jnp.int32, (KEY_TILE, rows), 0) < q_end

    _index_scores(iq_t, w_full, kx, sc, None, nt, adm, None, rows)
    _select_topk(sc, None, tril_ref, nt)
    _attend(sc, q_t, kx, v_t, s_sc, acc_sc, nt)
    _write_heads(o_ref, acc_sc, eye_ref, rows)


def _attn_prompt(q_t, pb3, v_t, idx_k_g, idx_k_b, tril, eye):
    b, s, _ = pb3.shape
    rows = 2 * LANES
    kernel = functools.partial(_attn_prompt_kernel, rows=rows)
    o_a, ik_t = pl.pallas_call(
        kernel,
        grid=(b, s // rows),
        in_specs=[
            pl.BlockSpec((1, IDX_Q, rows), lambda bi, i: (bi, 0, i)),
            pl.BlockSpec((1, A_Q, rows), lambda bi, i: (bi, IDX_Q // A_Q, i)),
            pl.BlockSpec((1, s, A_KV), lambda bi, i: (bi, 0, CB_K // A_KV)),
            pl.BlockSpec((1, A_KV, s), lambda bi, i: (bi, 0, 0)),
            pl.BlockSpec((1, s, LANES), lambda bi, i: (bi, 0, CB_IKW // LANES)),
            pl.BlockSpec((1, rows, LANES), lambda bi, i: (bi, i, CB_IKW // LANES)),
            pl.BlockSpec((1, IDX_DIM), lambda bi, i: (0, 0)),
            pl.BlockSpec((1, IDX_DIM), lambda bi, i: (0, 0)),
            pl.BlockSpec((KEY_TILE, KEY_TILE), lambda bi, i: (0, 0)),
            pl.BlockSpec((KEY_TILE, KEY_TILE), lambda bi, i: (0, 0)),
        ],
        out_specs=[
            pl.BlockSpec((1, rows, A_Q), lambda bi, i: (bi, i, 0)),
            pl.BlockSpec((1, IDX_DIM, s), lambda bi, i: (bi, 0, 0)),
        ],
        out_shape=[
            jax.ShapeDtypeStruct((b, s, A_Q), BF16),
            jax.ShapeDtypeStruct((b, IDX_DIM, s), F32),
        ],
        scratch_shapes=_ATTN_SCRATCH(rows, s),
        compiler_params=pltpu.CompilerParams(
            dimension_semantics=("arbitrary", "arbitrary"), vmem_limit_bytes=VMEM_LIMIT),
        name="attn_prompt",
    )(q_t, q_t, pb3, v_t, pb3, pb3, idx_k_g, idx_k_b, tril, eye)
    return o_a, jnp.transpose(ik_t, (0, 2, 1))


def _attn_sample_kernel(iq_ref, q_ref, k_ref, v_ref, ikw_ref, ck_ref, cv_ref, cik_ref, lng_ref, lnb_ref, tril_ref,
                        eye_ref, o_ref, ikln_ref, kx, v_t, iq_t, w_full, q_t, sc, s_sc, acc_sc, sc_pk,
                        *, rows, past, keys, pieces):
    new = rows
    pad = keys - past - new
    piece = pl.program_id(1)
    per_piece = past // pieces
    eye = eye_ref[0:A_HEAD_DIM, 0:A_HEAD_DIM]
    spare = jnp.zeros((KX_WIDTH - A_KV - IDX_DIM, KEY_TILE), BF16)
    for c in range(0, per_piece, KEY_TILE):
        dst = pl.ds(pl.multiple_of(piece * per_piece + c, KEY_TILE), KEY_TILE)
        cols = slice(c, c + KEY_TILE)
        stack = jnp.concatenate([ck_ref[0, g, :, cols].astype(BF16) for g in range(A_KV_HEADS)]
                                + [cik_ref[0, :, cols].astype(BF16), spare], axis=0)
        kx[dst, :] = _transpose_bf16(stack, eye_ref[...]).astype(BF16)
        for g in range(A_KV_HEADS):
            v_t[g, 0:A_HEAD_DIM, dst] = cv_ref[0, g, :, cols].astype(BF16)

    @pl.when(piece == pieces - 1)
    def _():
        vf = v_ref[0]
        for g in range(A_KV_HEADS):
            sl = slice(g * A_HEAD_DIM, (g + 1) * A_HEAD_DIM)
            v_t[g, 0:A_HEAD_DIM, past:past + new] = _transpose_bf16(vf[:, sl].astype(BF16), eye).astype(BF16)
            v_t[g, 0:A_HEAD_DIM, past + new:keys] = jnp.zeros((A_HEAD_DIM, pad), BF16)
            v_t[g, A_HEAD_DIM:V_ROWS, :] = _ones_row(keys)
        ikw = ikw_ref[0]
        ln = _layer_norm(ikw[:, :IDX_DIM], lng_ref[...], lnb_ref[...])
        ikln_ref[0] = ln
        kx[past:past + new, 0:A_KV] = k_ref[0].astype(BF16)
        kx[past:past + new, A_KV:A_KV + IDX_DIM] = ln.astype(BF16)
        kx[past:past + new, A_KV + IDX_DIM:KX_WIDTH] = jnp.zeros((new, KX_WIDTH - A_KV - IDX_DIM), BF16)
        kx[past + new:keys, :] = jnp.zeros((pad, KX_WIDTH), BF16)

        _stage_queries(iq_ref[0], q_ref[0], ikw[:, IDX_DIM:IDX_DIM + IDX_HEADS], eye_ref, iq_t, w_full, q_t, rows,
                       transposed=False)

        nt = keys // KEY_TILE

        def adm(j):
            return j * KEY_TILE + lax.broadcasted_iota(jnp.int32, (KEY_TILE, LANES), 0) < past + new

        pk = KEY_TILE // 4
        group = lax.broadcasted_iota(jnp.int32, (pk, LANES), 1) // rows
        sub_block = jnp.zeros((pk, LANES), jnp.int32)
        for c, blk in enumerate(PACK_ORDER):
            sub_block = jnp.where(group == c, blk, sub_block)
        key_in_tile = sub_block * pk + lax.broadcasted_iota(jnp.int32, (pk, LANES), 0)

        def adm_pk(j):
            return j * KEY_TILE + key_in_tile < past + new

        _index_scores(iq_t, w_full, kx, sc, sc_pk, nt, adm, adm_pk, rows)
        _select_topk(sc, sc_pk, tril_ref, nt)
        _attend(sc, q_t, kx, v_t, s_sc, acc_sc, nt)
        _write_heads(o_ref, acc_sc, eye_ref, rows)


def _attn_sample(pa3, pb3, cache_k, cache_v, cache_ik, idx_k_g, idx_k_b, tril, eye):
    b, t, _ = pa3.shape
    past = cache_ik.shape[2]
    pieces = 1
    keys = -(-(past + t) // KEY_TILE) * KEY_TILE
    kernel = functools.partial(_attn_sample_kernel, rows=t, past=past, keys=keys, pieces=pieces)
    return pl.pallas_call(
        kernel,
        grid=(b, pieces),
        in_specs=[
            pl.BlockSpec((1, t, IDX_Q), lambda bi, pc: (bi, 0, CA_IQ // IDX_Q)),
            pl.BlockSpec((1, t, A_Q), lambda bi, pc: (bi, 0, CA_Q // A_Q)),
            pl.BlockSpec((1, t, A_KV), lambda bi, pc: (bi, 0, CB_K // A_KV)),
            pl.BlockSpec((1, t, A_KV), lambda bi, pc: (bi, 0, CB_V // A_KV)),
            pl.BlockSpec((1, t, LANES), lambda bi, pc: (bi, 0, CB_IKW // LANES)),
            pl.BlockSpec((1, A_KV_HEADS, A_HEAD_DIM, past // pieces), lambda bi, pc: (bi, 0, 0, pc)),
            pl.BlockSpec((1, A_KV_HEADS, A_HEAD_DIM, past // pieces), lambda bi, pc: (bi, 0, 0, pc)),
            pl.BlockSpec((1, IDX_DIM, past // pieces), lambda bi, pc: (bi, 0, pc)),
            pl.BlockSpec((1, IDX_DIM), lambda bi, pc: (0, 0)),
            pl.BlockSpec((1, IDX_DIM), lambda bi, pc: (0, 0)),
            pl.BlockSpec((KEY_TILE, KEY_TILE), lambda bi, pc: (0, 0)),
            pl.BlockSpec((KEY_TILE, KEY_TILE), lambda bi, pc: (0, 0)),
        ],
        out_specs=[
            pl.BlockSpec((1, t, A_Q), lambda bi, pc: (bi, 0, 0)),
            pl.BlockSpec((1, t, IDX_DIM), lambda bi, pc: (bi, 0, 0)),
        ],
        out_shape=[
            jax.ShapeDtypeStruct((b, t, A_Q), BF16),
            jax.ShapeDtypeStruct((b, t, IDX_DIM), F32),
        ],
        scratch_shapes=_ATTN_SCRATCH(t, keys),
        compiler_params=pltpu.CompilerParams(
            dimension_semantics=("arbitrary", "arbitrary"), vmem_limit_bytes=VMEM_LIMIT),
        name="attn_sample",
    )(pa3, pa3, pb3, pb3, pb3, cache_k, cache_v, cache_ik, idx_k_g, idx_k_b, tril, eye)


def _head_pair_diag(x):
    w = x.shape[1] // 2
    first = lax.broadcasted_iota(jnp.int32, x.shape, 1) < w
    zero = jnp.zeros_like(x)
    return jnp.concatenate([jnp.where(first, x, zero), jnp.where(first, zero, x)], axis=0)


def _rwkv_kernel(u_ref, shift_ref, s0_ref, mu_ref, w0_ref, a0_ref, kk_ref, ka_ref, rk_ref, gnw_ref, gnb_ref,
                 w2_ref, a2_ref, g2_ref, bd_ref, tri_ref, yb_ref, s_out_ref, carry, state, y_sc, *, tc):
    t = pl.program_id(1)
    seqs, tb = u_ref.shape[0], u_ref.shape[1]
    u = u_ref[...].reshape(seqs * tb, SHIFT_W)
    n = B_HEAD_DIM

    @pl.when(t == 0)
    def _():
        carry[...] = shift_ref[...]
        for s in range(seqs):
            for p in range(B_HEADS // 2):
                state[s, p] = jnp.concatenate([s0_ref[s, 2 * p], s0_ref[s, 2 * p + 1]], axis=1)

    row = lax.broadcasted_iota(jnp.int32, u.shape, 0)
    u_prev = pltpu.roll(u, 1, 0)
    for s in range(seqs):
        u_prev = jnp.where(row == s * tb, carry[s], u_prev)
        carry[s] = u[(s + 1) * tb - 1:(s + 1) * tb, :]
    m = u + (u_prev - u) * mu_ref[...]

    r = m[:, 0:B_WIDTH]
    k = m[:, B_WIDTH:2 * B_WIDTH]
    v = m[:, 2 * B_WIDTH:3 * B_WIDTH]
    lora = m[:, 3 * B_WIDTH:3 * B_WIDTH + LANES]
    gl = m[:, 3 * B_WIDTH + LANES:]
    lane = lax.broadcasted_iota(jnp.int32, lora.shape, 1)
    lora = jnp.where(lane < DECAY_LORA, jnp.tanh(lora), lora).astype(BF16)
    z = w0_ref[...] + _dot(lora, w2_ref[...])
    lw = -float(np.exp(-0.5)) * _sigmoid(z)
    a = _sigmoid(a0_ref[...] + _dot(lora, a2_ref[...]))
    g = _dot(_sigmoid(gl).astype(BF16), g2_ref[...])

    kk = k * kk_ref[...]
    ss = _dot_exact_rhs(kk * kk, bd_ref[...])
    kk = kk / jnp.maximum(jnp.sqrt(ss), 1e-12)
    k2 = k * (1.0 + (a - 1.0) * ka_ref[...])
    bonus = _dot_exact_rhs(r * k2 * rk_ref[...], bd_ref[...]) * v

    cum = _dot_exact_lhs(tri_ref[...], lw)
    g_in = jnp.exp(cum)
    g_ex = jnp.exp(cum - lw)
    g_inv = jnp.exp(-cum)
    a_t = -kk * g_ex
    b_t = kk * a * g_inv
    k_t = k2 * g_inv
    r_t = r * g_in

    ri = lax.broadcasted_iota(jnp.int32, (tc, 2 * tc), 0)
    ci = lax.broadcasted_iota(jnp.int32, (tc, 2 * tc), 1) % tc
    strict = ci < ri
    incl = ci <= ri
    eye = jnp.where(ci == ri, 1.0, 0.0)

    bd = _head_pair_diag
    bf = lambda x: x.astype(BF16)
    per_seq = tb // tc
    chunks = range(seqs * per_seq)
    items = [(c, p) for c in chunks for p in range(B_HEADS // 2)]
    lanes = [slice(p * 2 * n, (p + 1) * 2 * n) for p in range(B_HEADS // 2)]
    toks = [slice(c * tc, (c + 1) * tc) for c in chunks]
    g_end = [g_in[(c + 1) * tc - 1:(c + 1) * tc, :] for c in chunks]
    v2 = {(c, p): v[toks[c], lanes[p]] for c, p in items}
    v_bd = {i: bd(bf(v2[i])) for i in items}
    a2 = {(c, p): a_t[toks[c], lanes[p]] for c, p in items}
    r2 = {(c, p): r_t[toks[c], lanes[p]] for c, p in items}
    bk = {(c, p): jnp.concatenate([b_t[toks[c], lanes[p]], k_t[toks[c], lanes[p]]], axis=0) for c, p in items}
    bk_bd = {i: jnp.concatenate([bd(bf(bk[i][0:tc])), bd(bf(bk[i][tc:2 * tc]))], axis=0) for i in items}
    cross = {i: _dot(bf(jnp.concatenate([a2[i], r2[i]], axis=0)), bk_bd[i], trans_b=True)
             for i in items}
    l_ab = {i: jnp.where(strict, cross[i][0:tc, 0:2 * tc], 0.0) for i in items}
    l_akv = {i: _dot(bf(jnp.where(strict, cross[i][0:tc, 2 * tc:4 * tc], 0.0)), v_bd[i]) for i in items}
    m_rbk = {i: bf(jnp.concatenate([jnp.where(incl, cross[i][tc:2 * tc, 0:2 * tc], 0.0),
                                    jnp.where(incl, cross[i][tc:2 * tc, 2 * tc:4 * tc], 0.0)], axis=1))
             for i in items}

    inv = {i: eye + l_ab[i] for i in items}
    pwb = {i: bf(l_ab[i]) for i in items}
    span = 2
    while span < tc:
        pwb = {i: bf(_dot(pwb[i], bd(pwb[i]))) for i in items}
        inv = {i: inv[i] + _dot(bf(inv[i]), bd(pwb[i])) for i in items}
        span *= 2
    inv = {i: bf(inv[i]) for i in items}
    ar = {i: _split2(jnp.concatenate([_dot(inv[i], bd(bf(a2[i]))), r2[i]], axis=0)) for i in items}
    z_free = {i: _dot(inv[i], bd(bf(l_akv[i]))) for i in items}

    first_head = lax.broadcasted_iota(jnp.int32, (n, 2 * n), 1) < n
    s_cur = {(s, p): state[s, p] for s in range(seqs) for p in range(B_HEADS // 2)}
    for j in range(per_seq):
        here = [(s * per_seq + j, s, p) for s in range(seqs) for p in range(B_HEADS // 2)]
        from_state = {(c, p): _dot3s(ar[c, p], _split2(bd(s_cur[s, p])), trans_b=True) for c, s, p in here}
        z = {(c, p): from_state[c, p][0:tc] + z_free[c, p] for c, s, p in here}
        y = {(c, p): from_state[c, p][tc:2 * tc]
             + _dot(m_rbk[c, p], jnp.concatenate([bd(bf(z[c, p])), v_bd[c, p]], axis=0)) for c, s, p in here}
        zv = {(c, p): jnp.concatenate([z[c, p], v2[c, p]], axis=0) for c, s, p in here}
        full = {(c, p): _dot3(zv[c, p].T, bk[c, p] * g_end[c][:, lanes[p]]) for c, s, p in here}
        s_cur = {(s, p): s_cur[s, p] * g_end[c][:, lanes[p]]
                 + jnp.where(first_head, full[c, p][0:n], full[c, p][n:2 * n]) for c, s, p in here}
        for c, s, p in here:
            y_sc[toks[c], lanes[p]] = y[c, p]
    for (s, p), val in s_cur.items():
        state[s, p] = val

    y = y_sc[...]
    mean = _dot_exact_rhs(y, bd_ref[...]) * (1.0 / n)
    d = y - mean
    var = _dot_exact_rhs(d * d, bd_ref[...]) * (1.0 / n)
    yn = d * lax.rsqrt(var + GN_EPS) * gnw_ref[...] + gnb_ref[...]
    yb_ref[...] = ((yn + bonus) * g).astype(yb_ref.dtype).reshape(seqs, tb, B_WIDTH)

    @pl.when(t == pl.num_programs(1) - 1)
    def _():
        for s in range(seqs):
            for p in range(B_HEADS // 2):
                s_out_ref[s, 2 * p] = state[s, p][:, 0:n]
                s_out_ref[s, 2 * p + 1] = state[s, p][:, n:2 * n]


def _rwkv(pb3, shift_prev, s0, p, tc):
    b, s, _ = pb3.shape
    tb = min(s, 4 * tc)
    seqs = max(2, 4 * tc // tb)
    tri = jnp.asarray(np.kron(np.eye(seqs * tb // tc), np.tril(np.ones((tc, tc)))), BF16)
    vec = lambda w: pl.BlockSpec((1, w), lambda bi, t: (0, 0))
    mat = lambda r, c: pl.BlockSpec((r, c), lambda bi, t: (0, 0))
    st = pl.BlockSpec((seqs, B_HEADS, B_HEAD_DIM, B_HEAD_DIM), lambda bi, t: (bi, 0, 0, 0))
    return pl.pallas_call(
        functools.partial(_rwkv_kernel, tc=tc),
        grid=(b // seqs, s // tb),
        in_specs=[
            pl.BlockSpec((seqs, tb, SHIFT_W), lambda bi, t: (bi, t, CB_U // SHIFT_W)),
            pl.BlockSpec((seqs, 1, SHIFT_W), lambda bi, t: (bi, 0, 0)),
            st,
            vec(SHIFT_W), vec(B_WIDTH), vec(B_WIDTH), vec(B_WIDTH), vec(B_WIDTH), vec(B_WIDTH),
            vec(B_WIDTH), vec(B_WIDTH),
            mat(LANES, B_WIDTH), mat(LANES, B_WIDTH), mat(GATE_LORA, B_WIDTH), mat(B_WIDTH, B_WIDTH),
            mat(seqs * tb, seqs * tb),
        ],
        out_specs=[pl.BlockSpec((seqs, tb, B_WIDTH), lambda bi, t: (bi, t, 0)), st],
        out_shape=[jax.ShapeDtypeStruct((b, s, B_WIDTH), BF16),
                   jax.ShapeDtypeStruct((b, B_HEADS, B_HEAD_DIM, B_HEAD_DIM), F32)],
        scratch_shapes=[pltpu.VMEM((seqs, 1, SHIFT_W), F32),
                        pltpu.VMEM((seqs, B_HEADS // 2, B_HEAD_DIM, 2 * B_HEAD_DIM), F32),
                        pltpu.VMEM((seqs * tb, B_WIDTH), F32)],
        compiler_params=pltpu.CompilerParams(
            dimension_semantics=("arbitrary", "arbitrary"), vmem_limit_bytes=VMEM_LIMIT),
        name="rwkv",
    )(pb3, shift_prev, s0, p['shift_mu'], p['w0'], p['a0'], p['k_k'], p['k_a'], p['r_k'], p['gn_w'], p['gn_b'],
      p['w2p'], p['a2p'], p['g2'], p['bd'], tri)


def _out_ffn_kernel(oa_ref, yb_ref, ga_ref, gb_ref, x_ref, n2_ref, nf_ref,
                    woa_ref, wob_ref, wout_ref, wg_ref, wu_ref, wd_ref, o_ref):
    merged = (_sigmoid(ga_ref[...].astype(F32)) * _dot(oa_ref[...], woa_ref[...])
              + _sigmoid(gb_ref[...].astype(F32)) * _dot(yb_ref[...], wob_ref[...]))
    h = x_ref[...] + _dot(merged.astype(BF16), wout_ref[...])
    hn = _rms(h, n2_ref[...]).astype(BF16)
    out = h
    for c, n in _col_chunks(D_FF, 4 * MXU_COLS):
        gate = _dot(hn, wg_ref[:, c:c + n])
        up = _dot(hn, wu_ref[:, c:c + n])
        act = (gate * _sigmoid(gate) * up).astype(BF16)
        out = out + _dot(act, wd_ref[c:c + n, :])
    o_ref[...] = _rms(out, nf_ref[...])


def _out_ffn(o_a, y_b, pa, x2, p, norm_f):
    n = x2.shape[0]
    tm = min(n, 512)
    row = lambda w, j=0: pl.BlockSpec((tm, w), lambda i: (i, j))
    vec = pl.BlockSpec((1, D_MODEL), lambda i: (0, 0))
    resident = lambda r, c: pl.BlockSpec((r, c), lambda i: (0, 0), pipeline_mode=pl.Buffered(1))
    return pl.pallas_call(
        _out_ffn_kernel,
        grid=(n // tm,),
        in_specs=[
            row(A_Q), row(B_WIDTH),
            row(D_MODEL, CA_GA // D_MODEL), row(D_MODEL, CA_GB // D_MODEL), row(D_MODEL),
            vec, vec,
            resident(A_Q, D_MODEL), resident(B_WIDTH, D_MODEL), resident(D_MODEL, D_MODEL),
            resident(D_MODEL, D_FF), resident(D_MODEL, D_FF), resident(D_FF, D_MODEL),
        ],
        out_specs=row(D_MODEL),
        out_shape=jax.ShapeDtypeStruct((n, D_MODEL), F32),
        compiler_params=pltpu.CompilerParams(
            dimension_semantics=("arbitrary",), vmem_limit_bytes=VMEM_LIMIT),
        name="out_proj_ffn",
    )(o_a, y_b, pa, pa, x2, p['norm2'], norm_f, p['w_oa'], p['w_ob'], p['w_out'],
      p['w_gate'], p['w_up'], p['w_down'])


def _prepare_params(l, norm1, w_in, idx_k_g, idx_k_b, shift_mu, w0, w2, a0, a2, g2, k_k, k_a, r_k,
                    gn_w, gn_b, w_oa, w_ob, w_out, norm2, w_gate, w_up, w_down):
    w = jnp.transpose(w_in[l])
    o = np.cumsum([0, A_Q, A_KV, A_KV, IDX_Q, IDX_DIM, IDX_HEADS, SHIFT_W, D_MODEL, D_MODEL])
    seg = lambda i: w[o[i]:o[i + 1]]
    pad = jnp.zeros((LANES - IDX_DIM - IDX_HEADS, D_MODEL), w.dtype)
    w_in_p = jnp.concatenate([seg(3), seg(7), seg(8), seg(0), seg(6), seg(1), seg(2), seg(4), seg(5), pad],
                             axis=0).astype(BF16)
    zeros = jnp.zeros((LANES - DECAY_LORA, B_WIDTH), F32)
    head = np.arange(B_WIDTH) // B_HEAD_DIM
    row = lambda x: x[l].reshape(1, -1)
    return {
        'norm1': row(norm1), 'w_in_p': w_in_p,
        'idx_k_g': row(idx_k_g), 'idx_k_b': row(idx_k_b),
        'shift_mu': row(shift_mu), 'w0': row(w0), 'a0': row(a0),
        'k_k': row(k_k), 'k_a': row(k_a), 'r_k': row(r_k), 'gn_w': row(gn_w), 'gn_b': row(gn_b),
        'w2p': jnp.concatenate([w2[l], zeros], axis=0).astype(BF16),
        'a2p': jnp.concatenate([zeros, a2[l]], axis=0).astype(BF16),
        'g2': g2[l].astype(BF16),
        'bd': jnp.asarray(head[:, None] == head[None, :], BF16),
        'w_oa': w_oa[l].astype(BF16), 'w_ob': w_ob[l].astype(BF16), 'w_out': w_out[l].astype(BF16),
        'norm2': row(norm2),
        'w_gate': w_gate[l].astype(BF16), 'w_up': w_up[l].astype(BF16), 'w_down': w_down[l].astype(BF16),
    }


def _layer(x, attend, shift_prev, wkv0, p, norm_f, eye, chunk):
    b, t, _ = x.shape
    x2 = x.reshape(b * t, D_MODEL)
    pa, pb, *transposed = _in_proj(x2, p['norm1'], p['w_in_p'], eye, t)
    pa3 = pa.reshape(b, t, PA_WIDTH)
    pb3 = pb.reshape(b, t, PB_WIDTH)
    o_a, ik_ln = attend(pa3, pb3, *transposed)
    y_b, wkv_new = _rwkv(pb3, shift_prev, wkv0, p, chunk)
    flat = lambda z: z.reshape(b * t, z.shape[-1])
    out = _out_ffn(flat(o_a), flat(y_b), pa, x2, p, norm_f).reshape(b, t, D_MODEL)
    if transposed:
        k_new, v_new = (jnp.transpose(z.reshape(b, A_KV_HEADS, A_HEAD_DIM, t), (0, 3, 1, 2)) for z in transposed[:2])
    else:
        k_new = pb3[:, :, CB_K:CB_K + A_KV].reshape(b, t, A_KV_HEADS, A_HEAD_DIM)
        v_new = pb3[:, :, CB_V:CB_V + A_KV].reshape(b, t, A_KV_HEADS, A_HEAD_DIM)
    shift_new = pb3[:, t - 1:t, CB_U:CB_U + SHIFT_W]
    return out, k_new, v_new, ik_ln, wkv_new, shift_new


def kernel(x_prompt, x_sample, cache_k, cache_v, cache_idx_k, state_wkv, state_shift,
           norm1, w_in, idx_k_g, idx_k_b, shift_mu, w0, w2, a0, a2, g2, k_k, k_a, r_k,
           gn_w, gn_b, w_oa, w_ob, w_out, norm2, w_gate, w_up, w_down, norm_f):
    assert w_in.shape[0] == 1, "single-layer kernel"
    l = 0
    p = _prepare_params(l, norm1, w_in, idx_k_g, idx_k_b, shift_mu, w0, w2, a0, a2, g2, k_k, k_a, r_k,
                        gn_w, gn_b, w_oa, w_ob, w_out, norm2, w_gate, w_up, w_down)
    nf = norm_f.reshape(1, -1)
    tril = jnp.asarray(np.tril(np.ones((KEY_TILE, KEY_TILE), np.float32)), BF16)
    eye = jnp.asarray(np.eye(KEY_TILE, dtype=np.float32), BF16)

    n_p = x_prompt.shape[0]
    shift0 = jnp.zeros((n_p, 1, SHIFT_W), F32)
    wkv_zero = jnp.zeros((n_p, B_HEADS, B_HEAD_DIM, B_HEAD_DIM), F32)
    attend_p = lambda pa3, pb3, k_t, v_t, q_t: _attn_prompt(q_t, pb3, v_t, p['idx_k_g'], p['idx_k_b'], tril, eye)
    y_p, k_p, v_p, ik_p, wkv_p, shift_p = _layer(x_prompt, attend_p, shift0, wkv_zero, p, nf, eye, 64)

    n_s, t_s = x_sample.shape[0], x_sample.shape[1]
    past = cache_k.shape[2]
    ck = jnp.transpose(cache_k[l], (0, 2, 3, 1))
    cv = jnp.transpose(cache_v[l], (0, 2, 3, 1))
    cik = jnp.transpose(cache_idx_k[l], (0, 2, 1))
    attend_s = lambda pa3, pb3: _attn_sample(pa3, pb3, ck, cv, cik, p['idx_k_g'], p['idx_k_b'], tril, eye)
    y_s, k_s, v_s, ik_s, wkv_s, shift_s = _layer(x_sample, attend_s, state_shift[l], state_wkv[l], p, nf, eye, t_s)

    lead = lambda z: z[None]
    return (y_p, y_s, lead(k_p), lead(v_p), lead(ik_p), lead(wkv_p), lead(shift_p),
            lead(k_s), lead(v_s), lead(ik_s), lead(wkv_s), lead(shift_s))
```

```python
import functools

import jax
import jax.numpy as jnp
import numpy as np
from jax import lax
from jax.experimental import pallas as pl
from jax.experimental.pallas import tpu as pltpu

F32 = jnp.float32
BF16 = jnp.bfloat16

D_MODEL = 1024
CHUNK = 64
A_HEADS = 8
A_KV_HEADS = 2
A_GROUP = A_HEADS // A_KV_HEADS
A_HEAD_DIM = 64
A_Q = A_HEADS * A_HEAD_DIM
A_KV = A_KV_HEADS * A_HEAD_DIM
IDX_HEADS = 16
IDX_DIM = 64
IDX_Q = IDX_HEADS * IDX_DIM
TOPK_MAX = 256
B_HEADS = 8
B_HEAD_DIM = 64
B_WIDTH = B_HEADS * B_HEAD_DIM
DECAY_LORA = 64
AAA_LORA = 64
GATE_LORA = 128
SHIFT_W = 3 * B_WIDTH + DECAY_LORA + AAA_LORA + GATE_LORA
D_FF = 2816
RMS_EPS = 1e-6
LN_EPS = 1e-6
GN_EPS = 64e-5

LANES = 128
SUB = 8
VMEM_LIMIT = 56 * 1024 * 1024
NEG = -1e30
KEY_TILE = 256
V_ROWS = A_HEAD_DIM + 16
KX_WIDTH = 256

CA_IQ = 0
CA_GA = CA_IQ + IDX_Q
CA_GB = CA_GA + D_MODEL
CA_Q = CA_GB + D_MODEL
PA_WIDTH = CA_Q + A_Q
CB_U = 0
CB_K = CB_U + SHIFT_W
CB_V = CB_K + A_KV
CB_IKW = CB_V + A_KV
PB_WIDTH = CB_IKW + LANES
MXU_COLS = 256


def _dot(a, b, trans_b=False):
    dn = (((1,), (1 if trans_b else 0,)), ((), ()))
    return lax.dot_general(a, b, dn, preferred_element_type=F32)


def _split2(x):
    hi = x.astype(BF16)
    lo = (x - hi.astype(F32)).astype(BF16)
    return hi, lo


def _dot3s(a_split, b_split, trans_b=False):
    ah, al = a_split
    bh, bl = b_split
    return _dot(ah, bh, trans_b) + (_dot(ah, bl, trans_b) + _dot(al, bh, trans_b))


def _dot3(a, b, trans_b=False):
    return _dot3s(_split2(a), _split2(b), trans_b)


def _dot_exact_rhs(a, b_bf16, terms=2):
    out = None
    rem = a
    for _ in range(terms):
        part = rem.astype(BF16)
        rem = rem - part.astype(F32)
        d = _dot(part, b_bf16)
        out = d if out is None else out + d
    return out


def _dot_exact_lhs(a_bf16, b, terms=3, trans_b=False):
    out = None
    rem = b
    for _ in range(terms):
        part = rem.astype(BF16)
        rem = rem - part.astype(F32)
        d = _dot(a_bf16, part, trans_b)
        out = d if out is None else out + d
    return out


def _sigmoid(x):
    return 1.0 / (1.0 + jnp.exp(-x))


def _rms(x, g):
    ms = jnp.mean(x * x, axis=-1, keepdims=True)
    return x * lax.rsqrt(ms + RMS_EPS) * g


def _col_chunks(width, chunk):
    return [(c, min(chunk, width - c)) for c in range(0, width, chunk)]


QT_WIDTH = IDX_Q + A_Q


def _in_proj_kernel(x_ref, g_ref, w_ref, eye_ref, pa_ref, pb_ref, *t_refs):
    xn = _rms(x_ref[...], g_ref[...]).astype(BF16)
    for c, n in _col_chunks(PA_WIDTH, 2 * MXU_COLS):
        pa_ref[:, c:c + n] = _dot(xn, w_ref[c:c + n, :], trans_b=True).astype(BF16)
    for c, n in _col_chunks(PB_WIDTH, 2 * MXU_COLS):
        pb_ref[:, c:c + n] = _dot(xn, w_ref[PA_WIDTH + c:PA_WIDTH + c + n, :], trans_b=True)
    if t_refs:
        kt_ref, vt_ref, qt_ref = t_refs
        kt_ref[0] = pb_ref[:, CB_K:CB_K + A_KV].T
        vt_ref[0] = pb_ref[:, CB_V:CB_V + A_KV].T
        for dst, src in ((0, CA_IQ), (IDX_Q, CA_Q)):
            for c, n in _col_chunks(IDX_Q if src == CA_IQ else A_Q, MXU_COLS):
                qt_ref[0, dst + c:dst + c + n, :] = _transpose_bf16(
                    pa_ref[:, src + c:src + c + n], eye_ref[0:n, 0:n]).astype(BF16)


def _in_proj(x2, norm1, w_in_p, eye, seq):
    n = x2.shape[0]
    tm = min(n, 512)
    out_specs = [pl.BlockSpec((tm, PA_WIDTH), lambda i: (i, 0)), pl.BlockSpec((tm, PB_WIDTH), lambda i: (i, 0))]
    out_shape = [jax.ShapeDtypeStruct((n, PA_WIDTH), BF16), jax.ShapeDtypeStruct((n, PB_WIDTH), F32)]
    if seq % tm == 0:
        per_seq = seq // tm
        t_spec = lambda w: pl.BlockSpec((1, w, tm), lambda i: (i // per_seq, 0, i % per_seq))
        out_specs += [t_spec(A_KV), t_spec(A_KV), t_spec(QT_WIDTH)]
        out_shape += [jax.ShapeDtypeStruct((n // seq, A_KV, seq), F32)] * 2
        out_shape += [jax.ShapeDtypeStruct((n // seq, QT_WIDTH, seq), BF16)]
    return pl.pallas_call(
        _in_proj_kernel,
        grid=(n // tm,),
        in_specs=[
            pl.BlockSpec((tm, D_MODEL), lambda i: (i, 0)),
            pl.BlockSpec((1, D_MODEL), lambda i: (0, 0)),
            pl.BlockSpec((PA_WIDTH + PB_WIDTH, D_MODEL), lambda i: (0, 0), pipeline_mode=pl.Buffered(1)),
            pl.BlockSpec((KEY_TILE, KEY_TILE), lambda i: (0, 0)),
        ],
        out_specs=out_specs,
        out_shape=out_shape,
        compiler_params=pltpu.CompilerParams(
            dimension_semantics=("arbitrary",), vmem_limit_bytes=VMEM_LIMIT),
        name="in_proj",
    )(x2, norm1, w_in_p, eye)


def _layer_norm(x, g, b):
    mu = jnp.mean(x, axis=-1, keepdims=True)
    d = x - mu
    var = jnp.mean(d * d, axis=-1, keepdims=True)
    return d * lax.rsqrt(var + LN_EPS) * g + b


def _transpose_bf16(x, eye):
    return _dot(eye, x, trans_b=True)


def _key_loop(nt, body, init):
    if isinstance(nt, int):
        carry = init
        for j in range(nt):
            carry = body(j, carry)
        return carry
    carry = lax.fori_loop(0, nt // 2, lambda k, c: body(2 * k + 1, body(2 * k, c)), init)
    return lax.cond(nt % 2 == 1, lambda c: body(nt - 1, c), lambda c: c, carry)


def _key_off(j):
    return j * KEY_TILE if isinstance(j, int) else pl.multiple_of(j * KEY_TILE, KEY_TILE)


def _fold(x, op):
    x = x.reshape(x.shape[0] // SUB, SUB, x.shape[1])
    while x.shape[0] > 1:
        half = x.shape[0] // 2
        x = op(x[:half], x[half:])
    return x[0]


def _stage_queries(iq, q, iw, eye_ref, iq_t, w_full, q_t, rows, transposed):
    eye = eye_ref[0:IDX_DIM, 0:IDX_DIM]

    def head_t(x, h, dim):
        if transposed:
            return x[h * dim:(h + 1) * dim, :] * (dim ** -0.5)
        return _transpose_bf16((x[:, h * dim:(h + 1) * dim] * (dim ** -0.5)).astype(BF16), eye)

    for h in range(IDX_HEADS):
        iq_t[:, h * rows:(h + 1) * rows] = head_t(iq, h, IDX_DIM).astype(BF16)
    w_t = _dot_exact_lhs(eye_ref[0:IDX_HEADS, 0:IDX_HEADS], iw, trans_b=True)
    for h in range(IDX_HEADS):
        w_full[:, h * rows:(h + 1) * rows] = w_t[h:h + 1, :] * (IDX_HEADS ** -0.5)
    lanes_g = A_GROUP * rows
    q_t[...] = jnp.zeros(q_t.shape, BF16)
    for g in range(A_KV_HEADS):
        for hh in range(A_GROUP):
            col = g * lanes_g + hh * rows
            q_t[g * A_HEAD_DIM:(g + 1) * A_HEAD_DIM, col:col + rows] = head_t(
                q, g * A_GROUP + hh, A_HEAD_DIM).astype(BF16)


PACK_ORDER = (0, 2, 1, 3)


def _pack_quarters(acc):
    n = acc.shape[0] // 4
    a, b, c, d = (acc[i * n:(i + 1) * n] for i in range(4))
    lane = lax.broadcasted_iota(jnp.int32, (n, LANES), 1)
    low64 = lane < LANES // 2
    low32 = (lane & (LANES // 4)) == 0
    ab = jnp.where(low64, a, b) + pltpu.roll(jnp.where(low64, b, a), LANES // 2, 1)
    cd = jnp.where(low64, c, d) + pltpu.roll(jnp.where(low64, d, c), LANES // 2, 1)
    t1 = ab + pltpu.roll(ab, 3 * LANES // 4, 1)
    t2 = cd + pltpu.roll(cd, LANES // 4, 1)
    return jnp.where(low32, t1, t2)


def _index_scores(iq_t, w_full, kx, sc, sc_pk, nt, adm_fn, adm_pk_fn, rows):
    width = sc.shape[1]
    inf = jnp.float32(jnp.inf)

    def body(j, carry):
        mn, mx, na = carry
        off = _key_off(j)
        lg = _dot(kx[pl.ds(off, KEY_TILE), A_KV:A_KV + IDX_DIM], iq_t[...])
        acc = None
        for c in range(IDX_HEADS * rows // width):
            x = jnp.maximum(lg[:, c * width:(c + 1) * width], 0.0) * w_full[:, c * width:(c + 1) * width]
            acc = x if acc is None else acc + x
        if sc_pk is not None:
            assert LANES // rows == 4
            pk = KEY_TILE // 4
            ok, val = adm_pk_fn(j), _pack_quarters(acc)
            sc_pk[pl.ds(j * pk, pk), :] = jnp.where(ok, val, -inf)
        shift = LANES // 2
        while shift >= rows:
            acc = acc + pltpu.roll(acc, shift, 1)
            shift //= 2
        sc[pl.ds(off, KEY_TILE), :] = jnp.where(adm_fn(j), acc, -inf)
        if sc_pk is None:
            ok, val = adm_fn(j), acc
        return (jnp.minimum(mn, _fold(jnp.where(ok, val, inf), jnp.minimum)),
                jnp.maximum(mx, _fold(jnp.where(ok, val, -inf), jnp.maximum)),
                na + _fold(jnp.where(ok, 1.0, 0.0), jnp.add))

    shape = (SUB, width)
    return _key_loop(nt, body, (jnp.full(shape, inf, F32), jnp.full(shape, -inf, F32), jnp.zeros(shape, F32)))


def _select_topk(sc, sc_pk, tril_ref, nt, stats):
    shape = (SUB, sc.shape[1])
    inf = jnp.float32(jnp.inf)
    src = sc if sc_pk is None else sc_pk
    rows_t = KEY_TILE if sc_pk is None else KEY_TILE // 4

    def tile(j):
        start = _key_off(j) if sc_pk is None else j * rows_t
        return src[pl.ds(start, rows_t), :]

    def allsub(x, red):
        op = {jnp.sum: jnp.add, jnp.min: jnp.minimum, jnp.max: jnp.maximum}[red]
        if sc_pk is not None:
            x = op(op(x, pltpu.roll(x, LANES // 4, 1)),
                   op(pltpu.roll(x, LANES // 2, 1), pltpu.roll(x, 3 * LANES // 4, 1)))
        return jnp.broadcast_to(red(x, axis=0, keepdims=True), shape)

    def count_ge(t):
        t1 = t[0:1, :]
        acc = _key_loop(
            nt, lambda j, a: a + _fold(jnp.where(tile(j) >= t1, 1.0, 0.0), jnp.add), jnp.zeros(shape, F32))
        return allsub(acc, jnp.sum)

    mn, mx, na = stats
    lo0 = allsub(mn, jnp.min)
    mx = allsub(mx, jnp.max)
    n_adm = allsub(na, jnp.sum)
    kq = jnp.minimum(n_adm, float(TOPK_MAX))
    hi0 = mx + (jnp.abs(mx) * 1e-6 + 1e-30)

    def cond(c):
        return jnp.logical_and(c[0] < 400, c[5] > 0.5)

    def probe(x, lo, hi, c_lo, c_hi):
        c_x = count_ge(x)
        ge = c_x >= kq
        return jnp.where(ge, x, lo), jnp.where(ge, hi, x), jnp.where(ge, c_x, c_lo), jnp.where(ge, c_hi, c_x)

    def body(c):
        it, lo, hi, c_lo, c_hi, _ = c
        for _ in range(2):
            lo, hi, c_lo, c_hi = probe(0.5 * (lo + hi), lo, hi, c_lo, c_hi)
        nxt = 0.5 * (lo + hi)
        active = jnp.where(c_lo - c_hi > 1.5, jnp.where(nxt > lo, jnp.where(nxt < hi, 1.0, 0.0), 0.0), 0.0)
        active = jnp.where(n_adm > kq, active, 0.0)
        return it + 1, lo, hi, c_lo, c_hi, jnp.max(active)

    _, lo, _, c_lo, c_hi, _ = lax.while_loop(
        cond, body, (jnp.int32(0), lo0, hi0, n_adm, jnp.zeros(shape, F32), jnp.float32(1.0)))

    lo1 = lo[0:1, :]
    thr = allsub(
        _key_loop(nt, lambda j, a: jnp.minimum(a, _fold(jnp.where(tile(j) >= lo1, tile(j), inf), jnp.minimum)),
                  jnp.full(shape, inf, F32)),
        jnp.min)[0:1, :]
    take = (kq - c_hi)[0:1, :]
    has_tie = jnp.max(c_lo - kq) > 0.5

    @pl.when(jnp.logical_not(has_tie))
    def _():
        def wr(j, carry):
            off = _key_off(j)
            sc[pl.ds(off, KEY_TILE), :] = jnp.where(sc[pl.ds(off, KEY_TILE), :] >= thr, 0.0, NEG)
            return carry
        _key_loop(nt, wr, 0)

    @pl.when(has_tie)
    def _():
        def wr(j, seen):
            off = _key_off(j)
            s = sc[pl.ds(off, KEY_TILE), :]
            tie = jnp.where(s == thr, 1.0, 0.0)
            rank = _dot(tril_ref[...], tie.astype(BF16)) + seen
            keep_tie = jnp.where(s == thr, jnp.where(rank <= take, 0.0, NEG), NEG)
            sc[pl.ds(off, KEY_TILE), :] = jnp.where(s > thr, 0.0, keep_tie)
            return seen + jnp.sum(tie, axis=0, keepdims=True)
        lax.fori_loop(0, nt, wr, jnp.zeros((1, sc.shape[1]), F32))


def _attend(sc, q_t, kx, v_t, s_sc, acc_sc, nt):
    lanes = q_t.shape[1]
    lanes_g = lanes // A_KV_HEADS
    rep = lanes // sc.shape[1]

    def scores(j, macc):
        off = _key_off(j)
        bias = jnp.concatenate([sc[pl.ds(off, KEY_TILE), :]] * rep, axis=1)
        s = _dot(kx[pl.ds(off, KEY_TILE), 0:A_KV], q_t[...]) + bias
        s_sc[pl.ds(off, KEY_TILE), :] = s
        return jnp.maximum(macc, _fold(s, jnp.maximum))

    macc = _key_loop(nt, scores, jnp.full((SUB, lanes), NEG, F32))
    m = jnp.max(macc, axis=0, keepdims=True)
    acc_sc[...] = jnp.zeros(acc_sc.shape, F32)

    def weighted(j, carry):
        off = _key_off(j)
        for g in range(A_KV_HEADS):
            cols = slice(g * lanes_g, (g + 1) * lanes_g)
            p = jnp.exp((s_sc[pl.ds(off, KEY_TILE), cols] - m[:, cols]).astype(BF16))
            acc_sc[g] += _dot(v_t[g, :, pl.ds(off, KEY_TILE)], p)
        return carry

    _key_loop(nt, weighted, 0)


def _ones_row(cols):
    first = lax.broadcasted_iota(jnp.int32, (V_ROWS - A_HEAD_DIM, cols), 0) == 0
    return jnp.where(first, 1.0, 0.0).astype(BF16)


def _write_heads(o_ref, acc_sc, eye_ref, rows):
    width = A_GROUP * A_HEAD_DIM
    for g in range(A_KV_HEADS):
        acc = acc_sc[g]
        o_t = (acc[0:A_HEAD_DIM] * (1.0 / acc[A_HEAD_DIM:A_HEAD_DIM + 1])).astype(BF16)
        stacked = jnp.concatenate([o_t[:, hh * rows:(hh + 1) * rows] for hh in range(A_GROUP)], axis=0)
        o_ref[0, :, g * width:(g + 1) * width] = _transpose_bf16(stacked, eye_ref[0:rows, 0:rows]).astype(o_ref.dtype)


_ATTN_SCRATCH = lambda rows, keys: [
    pltpu.VMEM((keys, KX_WIDTH), BF16),
    pltpu.VMEM((A_KV_HEADS, V_ROWS, keys), BF16),
    pltpu.VMEM((IDX_DIM, IDX_HEADS * rows), BF16),
    pltpu.VMEM((1, IDX_HEADS * rows), F32),
    pltpu.VMEM((A_KV, A_HEADS * rows), BF16),
    pltpu.VMEM((keys, max(rows, LANES)), F32),
    pltpu.VMEM((keys, A_HEADS * rows), F32),
    pltpu.VMEM((A_KV_HEADS, V_ROWS, A_GROUP * rows), F32),
] + ([pltpu.VMEM((keys // (LANES // rows), LANES), F32)] if rows < LANES else [])


def _attn_prompt_kernel(iq_ref, q_ref, k_ref, v_ref, ikw_ref, ikwq_ref, lng_ref, lnb_ref, tril_ref, eye_ref,
                        o_ref, ikln_ref, kx, v_t, iq_t, w_full, q_t, sc, s_sc, acc_sc, *, rows):
    i = pl.program_id(1)

    @pl.when(i == 0)
    def _():
        keys = kx.shape[0]
        for g in range(A_KV_HEADS):
            sl = slice(g * A_HEAD_DIM, (g + 1) * A_HEAD_DIM)
            v_t[g, 0:A_HEAD_DIM, :] = v_ref[0, sl, :].astype(BF16)
            v_t[g, A_HEAD_DIM:V_ROWS, :] = _ones_row(keys)
        ln = _layer_norm(ikw_ref[0][:, :IDX_DIM], lng_ref[...], lnb_ref[...])
        ikln_ref[0] = ln.T
        kx[:, 0:A_KV] = k_ref[0].astype(BF16)
        kx[:, A_KV:A_KV + IDX_DIM] = ln.astype(BF16)
        kx[:, A_KV + IDX_DIM:KX_WIDTH] = jnp.zeros((keys, KX_WIDTH - A_KV - IDX_DIM), BF16)

    _stage_queries(iq_ref[0], q_ref[0], ikwq_ref[0][:, IDX_DIM:IDX_DIM + IDX_HEADS], eye_ref, iq_t, w_full, q_t, rows,
                   transposed=True)

    nt = ((i + 1) * rows + KEY_TILE - 1) // KEY_TILE
    q_pos = i * rows + lax.broadcasted_iota(jnp.int32, (KEY_TILE, rows), 1)
    q_end = (q_pos // CHUNK + 1) * CHUNK

    def adm(j):
        return j * KEY_TILE + lax.broadcasted_iota(jnp.int32, (KEY_TILE, rows), 0) < q_end

    stats = _index_scores(iq_t, w_full, kx, sc, None, nt, adm, None, rows)
    _select_topk(sc, None, tril_ref, nt, stats)
    _attend(sc, q_t, kx, v_t, s_sc, acc_sc, nt)
    _write_heads(o_ref, acc_sc, eye_ref, rows)


def _attn_prompt(q_t, pb3, v_t, idx_k_g, idx_k_b, tril, eye):
    b, s, _ = pb3.shape
    rows = 2 * LANES
    kernel = functools.partial(_attn_prompt_kernel, rows=rows)
    o_a, ik_t = pl.pallas_call(
        kernel,
        grid=(b, s // rows),
        in_specs=[
            pl.BlockSpec((1, IDX_Q, rows), lambda bi, i: (bi, 0, i)),
            pl.BlockSpec((1, A_Q, rows), lambda bi, i: (bi, IDX_Q // A_Q, i)),
            pl.BlockSpec((1, s, A_KV), lambda bi, i: (bi, 0, CB_K // A_KV)),
            pl.BlockSpec((1, A_KV, s), lambda bi, i: (bi, 0, 0)),
            pl.BlockSpec((1, s, LANES), lambda bi, i: (bi, 0, CB_IKW // LANES)),
            pl.BlockSpec((1, rows, LANES), lambda bi, i: (bi, i, CB_IKW // LANES)),
            pl.BlockSpec((1, IDX_DIM), lambda bi, i: (0, 0)),
            pl.BlockSpec((1, IDX_DIM), lambda bi, i: (0, 0)),
            pl.BlockSpec((KEY_TILE, KEY_TILE), lambda bi, i: (0, 0)),
            pl.BlockSpec((KEY_TILE, KEY_TILE), lambda bi, i: (0, 0)),
        ],
        out_specs=[
            pl.BlockSpec((1, rows, A_Q), lambda bi, i: (bi, i, 0)),
            pl.BlockSpec((1, IDX_DIM, s), lambda bi, i: (bi, 0, 0)),
        ],
        out_shape=[
            jax.ShapeDtypeStruct((b, s, A_Q), BF16),
            jax.ShapeDtypeStruct((b, IDX_DIM, s), F32),
        ],
        scratch_shapes=_ATTN_SCRATCH(rows, s),
        compiler_params=pltpu.CompilerParams(
            dimension_semantics=("arbitrary", "arbitrary"), vmem_limit_bytes=VMEM_LIMIT),
        name="attn_prompt",
    )(q_t, q_t, pb3, v_t, pb3, pb3, idx_k_g, idx_k_b, tril, eye)
    return o_a, jnp.transpose(ik_t, (0, 2, 1))


def _attn_sample_kernel(iq_ref, q_ref, k_ref, v_ref, ikw_ref, ck_ref, cv_ref, cik_ref, lng_ref, lnb_ref, tril_ref,
                        eye_ref, o_ref, ikln_ref, kx, v_t, iq_t, w_full, q_t, sc, s_sc, acc_sc, sc_pk,
                        *, rows, past, keys, pieces):
    new = rows
    pad = keys - past - new
    piece = pl.program_id(1)
    per_piece = past // pieces
    eye = eye_ref[0:A_HEAD_DIM, 0:A_HEAD_DIM]
    spare = jnp.zeros((KX_WIDTH - A_KV - IDX_DIM, KEY_TILE), BF16)
    for c in range(0, per_piece, KEY_TILE):
        dst = pl.ds(pl.multiple_of(piece * per_piece + c, KEY_TILE), KEY_TILE)
        cols = slice(c, c + KEY_TILE)
        stack = jnp.concatenate([ck_ref[0, g, :, cols].astype(BF16) for g in range(A_KV_HEADS)]
                                + [cik_ref[0, :, cols].astype(BF16), spare], axis=0)
        kx[dst, :] = _transpose_bf16(stack, eye_ref[...]).astype(BF16)
        for g in range(A_KV_HEADS):
            v_t[g, 0:A_HEAD_DIM, dst] = cv_ref[0, g, :, cols].astype(BF16)

    @pl.when(piece == pieces - 1)
    def _():
        vf = v_ref[0]
        for g in range(A_KV_HEADS):
            sl = slice(g * A_HEAD_DIM, (g + 1) * A_HEAD_DIM)
            v_t[g, 0:A_HEAD_DIM, past:past + new] = _transpose_bf16(vf[:, sl].astype(BF16), eye).astype(BF16)
            v_t[g, 0:A_HEAD_DIM, past + new:keys] = jnp.zeros((A_HEAD_DIM, pad), BF16)
            v_t[g, A_HEAD_DIM:V_ROWS, :] = _ones_row(keys)
        ikw = ikw_ref[0]
        ln = _layer_norm(ikw[:, :IDX_DIM], lng_ref[...], lnb_ref[...])
        ikln_ref[0] = ln
        kx[past:past + new, 0:A_KV] = k_ref[0].astype(BF16)
        kx[past:past + new, A_KV:A_KV + IDX_DIM] = ln.astype(BF16)
        kx[past:past + new, A_KV + IDX_DIM:KX_WIDTH] = jnp.zeros((new, KX_WIDTH - A_KV - IDX_DIM), BF16)
        kx[past + new:keys, :] = jnp.zeros((pad, KX_WIDTH), BF16)

        _stage_queries(iq_ref[0], q_ref[0], ikw[:, IDX_DIM:IDX_DIM + IDX_HEADS], eye_ref, iq_t, w_full, q_t, rows,
                       transposed=False)

        nt = keys // KEY_TILE

        def adm(j):
            return j * KEY_TILE + lax.broadcasted_iota(jnp.int32, (KEY_TILE, LANES), 0) < past + new

        pk = KEY_TILE // 4
        group = lax.broadcasted_iota(jnp.int32, (pk, LANES), 1) // rows
        sub_block = jnp.zeros((pk, LANES), jnp.int32)
        for c, blk in enumerate(PACK_ORDER):
            sub_block = jnp.where(group == c, blk, sub_block)
        key_in_tile = sub_block * pk + lax.broadcasted_iota(jnp.int32, (pk, LANES), 0)

        def adm_pk(j):
            return j * KEY_TILE + key_in_tile < past + new

        stats = _index_scores(iq_t, w_full, kx, sc, sc_pk, nt, adm, adm_pk, rows)
        _select_topk(sc, sc_pk, tril_ref, nt, stats)
        _attend(sc, q_t, kx, v_t, s_sc, acc_sc, nt)
        _write_heads(o_ref, acc_sc, eye_ref, rows)


def _attn_sample(pa3, pb3, cache_k, cache_v, cache_ik, idx_k_g, idx_k_b, tril, eye):
    b, t, _ = pa3.shape
    past = cache_ik.shape[2]
    pieces = 1
    keys = -(-(past + t) // KEY_TILE) * KEY_TILE
    kernel = functools.partial(_attn_sample_kernel, rows=t, past=past, keys=keys, pieces=pieces)
    return pl.pallas_call(
        kernel,
        grid=(b, pieces),
        in_specs=[
            pl.BlockSpec((1, t, IDX_Q), lambda bi, pc: (bi, 0, CA_IQ // IDX_Q)),
            pl.BlockSpec((1, t, A_Q), lambda bi, pc: (bi, 0, CA_Q // A_Q)),
            pl.BlockSpec((1, t, A_KV), lambda bi, pc: (bi, 0, CB_K // A_KV)),
            pl.BlockSpec((1, t, A_KV), lambda bi, pc: (bi, 0, CB_V // A_KV)),
            pl.BlockSpec((1, t, LANES), lambda bi, pc: (bi, 0, CB_IKW // LANES)),
            pl.BlockSpec((1, A_KV_HEADS, A_HEAD_DIM, past // pieces), lambda bi, pc: (bi, 0, 0, pc)),
            pl.BlockSpec((1, A_KV_HEADS, A_HEAD_DIM, past // pieces), lambda bi, pc: (bi, 0, 0, pc)),
            pl.BlockSpec((1, IDX_DIM, past // pieces), lambda bi, pc: (bi, 0, pc)),
            pl.BlockSpec((1, IDX_DIM), lambda bi, pc: (0, 0)),
            pl.BlockSpec((1, IDX_DIM), lambda bi, pc: (0, 0)),
            pl.BlockSpec((KEY_TILE, KEY_TILE), lambda bi, pc: (0, 0)),
            pl.BlockSpec((KEY_TILE, KEY_TILE), lambda bi, pc: (0, 0)),
        ],
        out_specs=[
            pl.BlockSpec((1, t, A_Q), lambda bi, pc: (bi, 0, 0)),
            pl.BlockSpec((1, t, IDX_DIM), lambda bi, pc: (bi, 0, 0)),
        ],
        out_shape=[
            jax.ShapeDtypeStruct((b, t, A_Q), BF16),
            jax.ShapeDtypeStruct((b, t, IDX_DIM), F32),
        ],
        scratch_shapes=_ATTN_SCRATCH(t, keys),
        compiler_params=pltpu.CompilerParams(
            dimension_semantics=("arbitrary", "arbitrary"), vmem_limit_bytes=VMEM_LIMIT),
        name="attn_sample",
    )(pa3, pa3, pb3, pb3, pb3, cache_k, cache_v, cache_ik, idx_k_g, idx_k_b, tril, eye)


def _head_pair_diag(x):
    w = x.shape[1] // 2
    first = lax.broadcasted_iota(jnp.int32, x.shape, 1) < w
    zero = jnp.zeros_like(x)
    return jnp.concatenate([jnp.where(first, x, zero), jnp.where(first, zero, x)], axis=0)


def _rwkv_kernel(u_ref, shift_ref, s0_ref, mu_ref, w0_ref, a0_ref, kk_ref, ka_ref, rk_ref, gnw_ref, gnb_ref,
                 w2_ref, a2_ref, g2_ref, bd_ref, tri_ref, yb_ref, s_out_ref, carry, state, y_sc, *, tc):
    t = pl.program_id(1)
    seqs, tb = u_ref.shape[0], u_ref.shape[1]
    u = u_ref[...].reshape(seqs * tb, SHIFT_W)
    n = B_HEAD_DIM

    @pl.when(t == 0)
    def _():
        carry[...] = shift_ref[...]
        for s in range(seqs):
            for p in range(B_HEADS // 2):
                state[s, p] = jnp.concatenate([s0_ref[s, 2 * p], s0_ref[s, 2 * p + 1]], axis=1)

    row = lax.broadcasted_iota(jnp.int32, u.shape, 0)
    u_prev = pltpu.roll(u, 1, 0)
    for s in range(seqs):
        u_prev = jnp.where(row == s * tb, carry[s], u_prev)
        carry[s] = u[(s + 1) * tb - 1:(s + 1) * tb, :]
    m = u + (u_prev - u) * mu_ref[...]

    r = m[:, 0:B_WIDTH]
    k = m[:, B_WIDTH:2 * B_WIDTH]
    v = m[:, 2 * B_WIDTH:3 * B_WIDTH]
    lora = m[:, 3 * B_WIDTH:3 * B_WIDTH + LANES]
    gl = m[:, 3 * B_WIDTH + LANES:]
    lane = lax.broadcasted_iota(jnp.int32, lora.shape, 1)
    lora = jnp.where(lane < DECAY_LORA, jnp.tanh(lora), lora).astype(BF16)
    z = w0_ref[...] + _dot(lora, w2_ref[...])
    lw = -float(np.exp(-0.5)) * _sigmoid(z)
    a = _sigmoid(a0_ref[...] + _dot(lora, a2_ref[...]))
    g = _dot(_sigmoid(gl).astype(BF16), g2_ref[...])

    kk = k * kk_ref[...]
    ss = _dot_exact_rhs(kk * kk, bd_ref[...])
    kk = kk / jnp.maximum(jnp.sqrt(ss), 1e-12)
    k2 = k * (1.0 + (a - 1.0) * ka_ref[...])
    bonus = _dot_exact_rhs(r * k2 * rk_ref[...], bd_ref[...]) * v

    cum = _dot_exact_lhs(tri_ref[...], lw)
    g_in = jnp.exp(cum)
    g_ex = jnp.exp(cum - lw)
    g_inv = jnp.exp(-cum)
    a_t = -kk * g_ex
    b_t = kk * a * g_inv
    k_t = k2 * g_inv
    r_t = r * g_in

    ri = lax.broadcasted_iota(jnp.int32, (tc, 2 * tc), 0)
    ci = lax.broadcasted_iota(jnp.int32, (tc, 2 * tc), 1) % tc
    strict = ci < ri
    incl = ci <= ri
    eye = jnp.where(ci == ri, 1.0, 0.0)

    bd = _head_pair_diag
    bf = lambda x: x.astype(BF16)
    per_seq = tb // tc
    chunks = range(seqs * per_seq)
    items = [(c, p) for c in chunks for p in range(B_HEADS // 2)]
    lanes = [slice(p * 2 * n, (p + 1) * 2 * n) for p in range(B_HEADS // 2)]
    toks = [slice(c * tc, (c + 1) * tc) for c in chunks]
    g_end = [g_in[(c + 1) * tc - 1:(c + 1) * tc, :] for c in chunks]
    v2 = {(c, p): v[toks[c], lanes[p]] for c, p in items}
    v_bd = {i: bd(bf(v2[i])) for i in items}
    a2 = {(c, p): a_t[toks[c], lanes[p]] for c, p in items}
    r2 = {(c, p): r_t[toks[c], lanes[p]] for c, p in items}
    bk = {(c, p): jnp.concatenate([b_t[toks[c], lanes[p]], k_t[toks[c], lanes[p]]], axis=0) for c, p in items}
    bk_bd = {i: jnp.concatenate([bd(bf(bk[i][0:tc])), bd(bf(bk[i][tc:2 * tc]))], axis=0) for i in items}
    cross = {i: _dot(bf(jnp.concatenate([a2[i], r2[i]], axis=0)), bk_bd[i], trans_b=True)
             for i in items}
    l_ab = {i: jnp.where(strict, cross[i][0:tc, 0:2 * tc], 0.0) for i in items}
    l_akv = {i: _dot(bf(jnp.where(strict, cross[i][0:tc, 2 * tc:4 * tc], 0.0)), v_bd[i]) for i in items}
    m_rbk = {i: bf(jnp.concatenate([jnp.where(incl, cross[i][tc:2 * tc, 0:2 * tc], 0.0),
                                    jnp.where(incl, cross[i][tc:2 * tc, 2 * tc:4 * tc], 0.0)], axis=1))
             for i in items}

    inv = {i: eye + l_ab[i] for i in items}
    pwb = {i: bf(l_ab[i]) for i in items}
    span = 2
    while span < tc:
        pwb = {i: bf(_dot(pwb[i], bd(pwb[i]))) for i in items}
        inv = {i: inv[i] + _dot(bf(inv[i]), bd(pwb[i])) for i in items}
        span *= 2
    inv = {i: bf(inv[i]) for i in items}
    ar = {i: _split2(jnp.concatenate([_dot(inv[i], bd(bf(a2[i]))), r2[i]], axis=0)) for i in items}
    z_free = {i: _dot(inv[i], bd(bf(l_akv[i]))) for i in items}

    first_head = lax.broadcasted_iota(jnp.int32, (n, 2 * n), 1) < n
    s_cur = {(s, p): state[s, p] for s in range(seqs) for p in range(B_HEADS // 2)}
    for j in range(per_seq):
        here = [(s * per_seq + j, s, p) for s in range(seqs) for p in range(B_HEADS // 2)]
        from_state = {(c, p): _dot3s(ar[c, p], _split2(bd(s_cur[s, p])), trans_b=True) for c, s, p in here}
        z = {(c, p): from_state[c, p][0:tc] + z_free[c, p] for c, s, p in here}
        y = {(c, p): from_state[c, p][tc:2 * tc]
             + _dot(m_rbk[c, p], jnp.concatenate([bd(bf(z[c, p])), v_bd[c, p]], axis=0)) for c, s, p in here}
        zv = {(c, p): jnp.concatenate([z[c, p], v2[c, p]], axis=0) for c, s, p in here}
        full = {(c, p): _dot3(zv[c, p].T, bk[c, p] * g_end[c][:, lanes[p]]) for c, s, p in here}
        s_cur = {(s, p): s_cur[s, p] * g_end[c][:, lanes[p]]
                 + jnp.where(first_head, full[c, p][0:n], full[c, p][n:2 * n]) for c, s, p in here}
        for c, s, p in here:
            y_sc[toks[c], lanes[p]] = y[c, p]
    for (s, p), val in s_cur.items():
        state[s, p] = val

    y = y_sc[...]
    mean = _dot_exact_rhs(y, bd_ref[...]) * (1.0 / n)
    d = y - mean
    var = _dot_exact_rhs(d * d, bd_ref[...]) * (1.0 / n)
    yn = d * lax.rsqrt(var + GN_EPS) * gnw_ref[...] + gnb_ref[...]
    yb_ref[...] = ((yn + bonus) * g).astype(yb_ref.dtype).reshape(seqs, tb, B_WIDTH)

    @pl.when(t == pl.num_programs(1) - 1)
    def _():
        for s in range(seqs):
            for p in range(B_HEADS // 2):
                s_out_ref[s, 2 * p] = state[s, p][:, 0:n]
                s_out_ref[s, 2 * p + 1] = state[s, p][:, n:2 * n]


def _rwkv(pb3, shift_prev, s0, p, tc):
    b, s, _ = pb3.shape
    tb = min(s, 4 * tc)
    seqs = max(2, 4 * tc // tb)
    tri = jnp.asarray(np.kron(np.eye(seqs * tb // tc), np.tril(np.ones((tc, tc)))), BF16)
    vec = lambda w: pl.BlockSpec((1, w), lambda bi, t: (0, 0))
    mat = lambda r, c: pl.BlockSpec((r, c), lambda bi, t: (0, 0))
    st = pl.BlockSpec((seqs, B_HEADS, B_HEAD_DIM, B_HEAD_DIM), lambda bi, t: (bi, 0, 0, 0))
    return pl.pallas_call(
        functools.partial(_rwkv_kernel, tc=tc),
        grid=(b // seqs, s // tb),
        in_specs=[
            pl.BlockSpec((seqs, tb, SHIFT_W), lambda bi, t: (bi, t, CB_U // SHIFT_W)),
            pl.BlockSpec((seqs, 1, SHIFT_W), lambda bi, t: (bi, 0, 0)),
            st,
            vec(SHIFT_W), vec(B_WIDTH), vec(B_WIDTH), vec(B_WIDTH), vec(B_WIDTH), vec(B_WIDTH),
            vec(B_WIDTH), vec(B_WIDTH),
            mat(LANES, B_WIDTH), mat(LANES, B_WIDTH), mat(GATE_LORA, B_WIDTH), mat(B_WIDTH, B_WIDTH),
            mat(seqs * tb, seqs * tb),
        ],
        out_specs=[pl.BlockSpec((seqs, tb, B_WIDTH), lambda bi, t: (bi, t, 0)), st],
        out_shape=[jax.ShapeDtypeStruct((b, s, B_WIDTH), BF16),
                   jax.ShapeDtypeStruct((b, B_HEADS, B_HEAD_DIM, B_HEAD_DIM), F32)],
        scratch_shapes=[pltpu.VMEM((seqs, 1, SHIFT_W), F32),
                        pltpu.VMEM((seqs, B_HEADS // 2, B_HEAD_DIM, 2 * B_HEAD_DIM), F32),
                        pltpu.VMEM((seqs * tb, B_WIDTH), F32)],
        compiler_params=pltpu.CompilerParams(
            dimension_semantics=("arbitrary", "arbitrary"), vmem_limit_bytes=VMEM_LIMIT),
        name="rwkv",
    )(pb3, shift_prev, s0, p['shift_mu'], p['w0'], p['a0'], p['k_k'], p['k_a'], p['r_k'], p['gn_w'], p['gn_b'],
      p['w2p'], p['a2p'], p['g2'], p['bd'], tri)


def _out_ffn_kernel(oa_ref, yb_ref, ga_ref, gb_ref, x_ref, n2_ref, nf_ref,
                    woa_ref, wob_ref, wout_ref, wg_ref, wu_ref, wd_ref, o_ref):
    merged = (_sigmoid(ga_ref[...].astype(F32)) * _dot(oa_ref[...], woa_ref[...])
              + _sigmoid(gb_ref[...].astype(F32)) * _dot(yb_ref[...], wob_ref[...]))
    h = x_ref[...] + _dot(merged.astype(BF16), wout_ref[...])
    hn = _rms(h, n2_ref[...]).astype(BF16)
    out = h
    for c, n in _col_chunks(D_FF, 4 * MXU_COLS):
        gate = _dot(hn, wg_ref[:, c:c + n])
        up = _dot(hn, wu_ref[:, c:c + n])
        act = (gate * _sigmoid(gate) * up).astype(BF16)
        out = out + _dot(act, wd_ref[c:c + n, :])
    o_ref[...] = _rms(out, nf_ref[...])


def _out_ffn(o_a, y_b, pa, x2, p, norm_f):
    n = x2.shape[0]
    tm = min(n, 512)
    row = lambda w, j=0: pl.BlockSpec((tm, w), lambda i: (i, j))
    vec = pl.BlockSpec((1, D_MODEL), lambda i: (0, 0))
    resident = lambda r, c: pl.BlockSpec((r, c), lambda i: (0, 0), pipeline_mode=pl.Buffered(1))
    return pl.pallas_call(
        _out_ffn_kernel,
        grid=(n // tm,),
        in_specs=[
            row(A_Q), row(B_WIDTH),
            row(D_MODEL, CA_GA // D_MODEL), row(D_MODEL, CA_GB // D_MODEL), row(D_MODEL),
            vec, vec,
            resident(A_Q, D_MODEL), resident(B_WIDTH, D_MODEL), resident(D_MODEL, D_MODEL),
            resident(D_MODEL, D_FF), resident(D_MODEL, D_FF), resident(D_FF, D_MODEL),
        ],
        out_specs=row(D_MODEL),
        out_shape=jax.ShapeDtypeStruct((n, D_MODEL), F32),
        compiler_params=pltpu.CompilerParams(
            dimension_semantics=("arbitrary",), vmem_limit_bytes=VMEM_LIMIT),
        name="out_proj_ffn",
    )(o_a, y_b, pa, pa, x2, p['norm2'], norm_f, p['w_oa'], p['w_ob'], p['w_out'],
      p['w_gate'], p['w_up'], p['w_down'])


def _prepare_params(l, norm1, w_in, idx_k_g, idx_k_b, shift_mu, w0, w2, a0, a2, g2, k_k, k_a, r_k,
                    gn_w, gn_b, w_oa, w_ob, w_out, norm2, w_gate, w_up, w_down):
    w = jnp.transpose(w_in[l])
    o = np.cumsum([0, A_Q, A_KV, A_KV, IDX_Q, IDX_DIM, IDX_HEADS, SHIFT_W, D_MODEL, D_MODEL])
    seg = lambda i: w[o[i]:o[i + 1]]
    pad = jnp.zeros((LANES - IDX_DIM - IDX_HEADS, D_MODEL), w.dtype)
    w_in_p = jnp.concatenate([seg(3), seg(7), seg(8), seg(0), seg(6), seg(1), seg(2), seg(4), seg(5), pad],
                             axis=0).astype(BF16)
    zeros = jnp.zeros((LANES - DECAY_LORA, B_WIDTH), F32)
    head = np.arange(B_WIDTH) // B_HEAD_DIM
    row = lambda x: x[l].reshape(1, -1)
    return {
        'norm1': row(norm1), 'w_in_p': w_in_p,
        'idx_k_g': row(idx_k_g), 'idx_k_b': row(idx_k_b),
        'shift_mu': row(shift_mu), 'w0': row(w0), 'a0': row(a0),
        'k_k': row(k_k), 'k_a': row(k_a), 'r_k': row(r_k), 'gn_w': row(gn_w), 'gn_b': row(gn_b),
        'w2p': jnp.concatenate([w2[l], zeros], axis=0).astype(BF16),
        'a2p': jnp.concatenate([zeros, a2[l]], axis=0).astype(BF16),
        'g2': g2[l].astype(BF16),
        'bd': jnp.asarray(head[:, None] == head[None, :], BF16),
        'w_oa': w_oa[l].astype(BF16), 'w_ob': w_ob[l].astype(BF16), 'w_out': w_out[l].astype(BF16),
        'norm2': row(norm2),
        'w_gate': w_gate[l].astype(BF16), 'w_up': w_up[l].astype(BF16), 'w_down': w_down[l].astype(BF16),
    }


def _layer(x, attend, shift_prev, wkv0, p, norm_f, eye, chunk):
    b, t, _ = x.shape
    x2 = x.reshape(b * t, D_MODEL)
    pa, pb, *transposed = _in_proj(x2, p['norm1'], p['w_in_p'], eye, t)
    pa3 = pa.reshape(b, t, PA_WIDTH)
    pb3 = pb.reshape(b, t, PB_WIDTH)
    o_a, ik_ln = attend(pa3, pb3, *transposed)
    y_b, wkv_new = _rwkv(pb3, shift_prev, wkv0, p, chunk)
    flat = lambda z: z.reshape(b * t, z.shape[-1])
    out = _out_ffn(flat(o_a), flat(y_b), pa, x2, p, norm_f).reshape(b, t, D_MODEL)
    if transposed:
        k_new, v_new = (jnp.transpose(z.reshape(b, A_KV_HEADS, A_HEAD_DIM, t), (0, 3, 1, 2)) for z in transposed[:2])
    else:
        k_new = pb3[:, :, CB_K:CB_K + A_KV].reshape(b, t, A_KV_HEADS, A_HEAD_DIM)
        v_new = pb3[:, :, CB_V:CB_V + A_KV].reshape(b, t, A_KV_HEADS, A_HEAD_DIM)
    shift_new = pb3[:, t - 1:t, CB_U:CB_U + SHIFT_W]
    return out, k_new, v_new, ik_ln, wkv_new, shift_new


def kernel(x_prompt, x_sample, cache_k, cache_v, cache_idx_k, state_wkv, state_shift,
           norm1, w_in, idx_k_g, idx_k_b, shift_mu, w0, w2, a0, a2, g2, k_k, k_a, r_k,
           gn_w, gn_b, w_oa, w_ob, w_out, norm2, w_gate, w_up, w_down, norm_f):
    assert w_in.shape[0] == 1, "single-layer kernel"
    l = 0
    p = _prepare_params(l, norm1, w_in, idx_k_g, idx_k_b, shift_mu, w0, w2, a0, a2, g2, k_k, k_a, r_k,
                        gn_w, gn_b, w_oa, w_ob, w_out, norm2, w_gate, w_up, w_down)
    nf = norm_f.reshape(1, -1)
    tril = jnp.asarray(np.tril(np.ones((KEY_TILE, KEY_TILE), np.float32)), BF16)
    eye = jnp.asarray(np.eye(KEY_TILE, dtype=np.float32), BF16)

    n_p = x_prompt.shape[0]
    shift0 = jnp.zeros((n_p, 1, SHIFT_W), F32)
    wkv_zero = jnp.zeros((n_p, B_HEADS, B_HEAD_DIM, B_HEAD_DIM), F32)
    attend_p = lambda pa3, pb3, k_t, v_t, q_t: _attn_prompt(q_t, pb3, v_t, p['idx_k_g'], p['idx_k_b'], tril, eye)
    y_p, k_p, v_p, ik_p, wkv_p, shift_p = _layer(x_prompt, attend_p, shift0, wkv_zero, p, nf, eye, 64)

    n_s, t_s = x_sample.shape[0], x_sample.shape[1]
    past = cache_k.shape[2]
    ck = jnp.transpose(cache_k[l], (0, 2, 3, 1))
    cv = jnp.transpose(cache_v[l], (0, 2, 3, 1))
    cik = jnp.transpose(cache_idx_k[l], (0, 2, 1))
    attend_s = lambda pa3, pb3: _attn_sample(pa3, pb3, ck, cv, cik, p['idx_k_g'], p['idx_k_b'], tril, eye)
    y_s, k_s, v_s, ik_s, wkv_s, shift_s = _layer(x_sample, attend_s, state_shift[l], state_wkv[l], p, nf, eye, t_s)

    lead = lambda z: z[None]
    return (y_p, y_s, lead(k_p), lead(v_p), lead(ik_p), lead(wkv_p), lead(shift_p),
            lead(k_s), lead(v_s), lead(ik_s), lead(wkv_s), lead(shift_s))
```
